```python
import jax, jax.numpy as jnp
from jax import lax
import numpy as np

D_MODEL = 2048
BATCH = 2
SEQ = 4096
DEPTH = 2
DEC_BATCH = 32
DEC_SEQ = 8
PAST_LEN = 16384
PAGE_SIZE = 128

HEAD_DIM = 64
ROPE_THETA = 10000.0
NORM_EPS = 1e-6
BLOCK = 128
N_MIXERS = 2
A_WINDOW = 128
A_Q_HEADS = D_MODEL // HEAD_DIM
A_KV_HEADS = 8
A_GROUP = A_Q_HEADS // A_KV_HEADS
B_PATTERNS = ((128, 1), (512, 4), (2048, 16))
B_N_GROUPS = len(B_PATTERNS)
B_Q_HEADS = 16
B_KV_HEADS = 4
B_GROUP = B_Q_HEADS // B_KV_HEADS
D_FF = 5632
CONV_W = 3
N_LAYERS_A = (DEPTH + 1) // 2
N_LAYERS_B = DEPTH // 2
NEG_INF = -1e30

kernel_name = 'hybrid_swa_sink_dilated_convffn_step'


def rms_norm(x, g):
    xf = x.astype(jnp.float32)
    y = xf * lax.rsqrt(jnp.mean(xf * xf, axis=-1, keepdims=True) + NORM_EPS)
    return (y * g.astype(jnp.float32)).astype(x.dtype)


def rope(x, pos):
    half = HEAD_DIM // 2
    inv_freq = ROPE_THETA ** (-jnp.arange(half, dtype=jnp.float32) / half)
    ang = pos.astype(jnp.float32)[:, None] * inv_freq[None, :]
    cos = jnp.cos(ang)[:, None, :]
    sin = jnp.sin(ang)[:, None, :]
    xf = x.astype(jnp.float32)
    x1, x2 = xf[..., :half], xf[..., half:]
    return jnp.concatenate([x1 * cos - x2 * sin, x2 * cos + x1 * sin], axis=-1).astype(x.dtype)


def project_qkv(xn, w_qkv, q_gain, k_gain, n_q, n_kv, pos):
    b, s, _ = xn.shape
    qkv = xn @ w_qkv
    q = qkv[..., :n_q * HEAD_DIM].reshape(b, s, n_q, HEAD_DIM)
    k = qkv[..., n_q * HEAD_DIM:(n_q + n_kv) * HEAD_DIM].reshape(b, s, n_kv, HEAD_DIM)
    v = qkv[..., (n_q + n_kv) * HEAD_DIM:].reshape(b, s, n_kv, HEAD_DIM)
    q = rope(rms_norm(q, q_gain), pos)
    k = rope(rms_norm(k, k_gain), pos)
    return q, k, v


def attend(q, k, v, mask, sink=None):
    s = jnp.einsum('bnqhgd,bnkhd->bnhgqk', q, k).astype(jnp.float32) * (HEAD_DIM ** -0.5)
    s = jnp.where(mask, s, NEG_INF)
    m = jnp.max(s, axis=-1, keepdims=True)
    if sink is not None:
        sk = sink.astype(jnp.float32)[:, :, None, None]
        m = jnp.maximum(m, sk)
    p = jnp.exp(s - m)
    denom = jnp.sum(p, axis=-1, keepdims=True)
    if sink is not None:
        denom = denom + jnp.exp(sk - m)
    out = jnp.einsum('bnhgqk,bnkhd->bnqhgd', (p / denom).astype(v.dtype), v)
    lse = jnp.moveaxis((m + jnp.log(denom))[..., 0], -1, 2)
    return out, lse


def band_attend(q, k, v, span, sink=None):
    bt, L = q.shape[:2]
    nb = L // BLOCK
    qb = q.reshape(bt, nb, BLOCK, *q.shape[2:])

    def prev_cur(t):
        tb = t.reshape(bt, nb, BLOCK, *t.shape[2:])
        prev = jnp.concatenate([jnp.zeros_like(tb[:, :1]), tb[:, :-1]], axis=1)
        return jnp.concatenate([prev, tb], axis=2)

    kb, vb = prev_cur(k), prev_cur(v)
    qi = jnp.arange(BLOCK)[:, None]
    kj = jnp.arange(2 * BLOCK)[None, :]
    dist = qi + BLOCK - kj
    kpos = (jnp.arange(nb) * BLOCK - BLOCK)[:, None, None] + kj[None]
    mask = (dist >= 0) & (dist <= span) & (kpos >= 0)
    out, lse = attend(qb, kb, vb, mask[None, :, None, None], sink)
    return out.reshape(q.shape), lse.reshape(bt, L, *q.shape[2:4])


def to_residues(t, d, lsub):
    b, s = t.shape[:2]
    n = s // d
    t = jnp.moveaxis(t.reshape(b, n, d, *t.shape[2:]), 2, 1).reshape(b * d, n, *t.shape[2:])
    return jnp.pad(t, [(0, 0), (0, lsub - n)] + [(0, 0)] * (t.ndim - 2))


def dilated_band(q, k, v, w, d):
    b, s = q.shape[:2]
    n = s // d
    lsub = -(-n // BLOCK) * BLOCK
    qr, kr, vr = to_residues(q, d, lsub), to_residues(k, d, lsub), to_residues(v, d, lsub)
    o, l = band_attend(qr, kr, vr, w // d)

    def back(t):
        t = t[:, :n].reshape(b, d, n, *t.shape[2:])
        return jnp.moveaxis(t, 1, 2).reshape(b, s, *t.shape[3:])

    return back(o), back(l)


def combine_groups(outs, lses):
    o = jnp.stack(outs, axis=0)
    wts = jax.nn.softmax(jnp.stack(lses, axis=0), axis=0)
    return jnp.sum(wts[..., None].astype(o.dtype) * o, axis=0)


def mixer_a_prompt(xn, w_qkv, q_gain, k_gain, sinks, w_o):
    b, s, _ = xn.shape
    pos = jnp.arange(s, dtype=jnp.int32)
    q, k, v = project_qkv(xn, w_qkv, q_gain, k_gain, A_Q_HEADS, A_KV_HEADS, pos)
    q = q.reshape(b, s, A_KV_HEADS, A_GROUP, HEAD_DIM)
    out, _ = band_attend(q, k, v, A_WINDOW, sinks.reshape(A_KV_HEADS, A_GROUP))
    keep = min(A_WINDOW, s)
    return out.reshape(b, s, -1) @ w_o, k[:, s - keep:], v[:, s - keep:]


def mixer_a_sample(xn, cache_k, cache_v, w_qkv, q_gain, k_gain, sinks, w_o):
    b, s, _ = xn.shape
    lc = cache_k.shape[1]
    pos = PAST_LEN + jnp.arange(s, dtype=jnp.int32)
    q, k, v = project_qkv(xn, w_qkv, q_gain, k_gain, A_Q_HEADS, A_KV_HEADS, pos)
    q = q.reshape(b, s, A_KV_HEADS, A_GROUP, HEAD_DIM)
    k_all = jnp.concatenate([cache_k.astype(k.dtype), k], axis=1)
    v_all = jnp.concatenate([cache_v.astype(v.dtype), v], axis=1)
    kpos = PAST_LEN - lc + jnp.arange(lc + s, dtype=jnp.int32)
    dist = pos[:, None] - kpos[None, :]
    mask = (dist >= 0) & (dist <= A_WINDOW)
    out, _ = attend(q[:, None], k_all[:, None], v_all[:, None], mask[None, None, None, None],
                    sinks.reshape(A_KV_HEADS, A_GROUP))
    return out.reshape(b, s, -1) @ w_o, k_all[:, -lc:], v_all[:, -lc:]


def mixer_b_prompt(xn, w_qkv, q_gain, k_gain, w_o):
    b, s, _ = xn.shape
    pos = jnp.arange(s, dtype=jnp.int32)
    q, k, v = project_qkv(xn, w_qkv, q_gain, k_gain, B_N_GROUPS * B_Q_HEADS, B_N_GROUPS * B_KV_HEADS, pos)
    outs, lses, k_keep, v_keep = [], [], [], []
    for g, (w, d) in enumerate(B_PATTERNS):
        qg = q[:, :, g * B_Q_HEADS:(g + 1) * B_Q_HEADS].reshape(b, s, B_KV_HEADS, B_GROUP, HEAD_DIM)
        kg = k[:, :, g * B_KV_HEADS:(g + 1) * B_KV_HEADS]
        vg = v[:, :, g * B_KV_HEADS:(g + 1) * B_KV_HEADS]
        o, l = dilated_band(qg, kg, vg, w, d)
        outs.append(o)
        lses.append(l)
        keep = min(w, s)
        k_keep.append(kg[:, s - keep:])
        v_keep.append(vg[:, s - keep:])
    y = combine_groups(outs, lses).reshape(b, s, -1) @ w_o
    return y, jnp.concatenate(k_keep, axis=1), jnp.concatenate(v_keep, axis=1)


def mixer_b_sample(xn, cache_k, cache_v, w_qkv, q_gain, k_gain, w_o):
    b, s, _ = xn.shape
    pos = PAST_LEN + jnp.arange(s, dtype=jnp.int32)
    q, k, v = project_qkv(xn, w_qkv, q_gain, k_gain, B_N_GROUPS * B_Q_HEADS, B_N_GROUPS * B_KV_HEADS, pos)
    outs, lses, k_new, v_new = [], [], [], []
    off = 0
    for g, (w, d) in enumerate(B_PATTERNS):
        lc = min(w, PAST_LEN)
        qg = q[:, :, g * B_Q_HEADS:(g + 1) * B_Q_HEADS].reshape(b, s, B_KV_HEADS, B_GROUP, HEAD_DIM)
        k_all = jnp.concatenate([cache_k[:, off:off + lc].astype(k.dtype), k[:, :, g * B_KV_HEADS:(g + 1) * B_KV_HEADS]], axis=1)
        v_all = jnp.concatenate([cache_v[:, off:off + lc].astype(v.dtype), v[:, :, g * B_KV_HEADS:(g + 1) * B_KV_HEADS]], axis=1)
        off += lc
        idx = lc + jnp.arange(s, dtype=jnp.int32)[:, None] - d * jnp.arange(w // d + 1, dtype=jnp.int32)[None, :]
        valid = idx >= 0
        idx = jnp.maximum(idx, 0)
        kg = jnp.take(k_all, idx, axis=1)
        vg = jnp.take(v_all, idx, axis=1)
        o, l = attend(qg[:, :, None], kg, vg, valid[None, :, None, None, None, :])
        outs.append(o[:, :, 0])
        lses.append(l[:, :, 0])
        k_new.append(k_all[:, -lc:])
        v_new.append(v_all[:, -lc:])
    y = combine_groups(outs, lses).reshape(b, s, -1) @ w_o
    return y, jnp.concatenate(k_new, axis=1), jnp.concatenate(v_new, axis=1)


def conv_ffn(xn, prefix, w_gate, w_up, conv_w, conv_b, w_down):
    s = xn.shape[1]
    gate = xn @ w_gate
    up = xn @ w_up
    ext = jnp.concatenate([prefix.astype(gate.dtype), gate], axis=1)
    conv = conv_b
    for j in range(CONV_W):
        conv = conv + conv_w[j] * ext[:, j:j + s]
    h = jax.nn.silu(conv) * up
    return h @ w_down, ext[:, -(CONV_W - 1):]


def setup_inputs(seed: int = 0) -> dict:
    key = jax.random.key(seed)
    ks = jax.random.split(key, 24)
    f32 = jnp.float32

    def nrm(k, shape, scale):
        return jax.random.normal(k, shape, f32) * scale

    la = min(A_WINDOW, PAST_LEN)
    lb = sum(min(w, PAST_LEN) for w, _ in B_PATTERNS)
    a_cols = (A_Q_HEADS + 2 * A_KV_HEADS) * HEAD_DIM
    b_cols = B_N_GROUPS * (B_Q_HEADS + 2 * B_KV_HEADS) * HEAD_DIM
    return {
        'x_prompt': nrm(ks[0], (BATCH, SEQ, D_MODEL), 1.0),
        'x_sample': nrm(ks[1], (DEC_BATCH, DEC_SEQ, D_MODEL), 1.0),
        'cache_a_k': nrm(ks[2], (N_LAYERS_A, DEC_BATCH, la, A_KV_HEADS, HEAD_DIM), 1.0),
        'cache_a_v': nrm(ks[3], (N_LAYERS_A, DEC_BATCH, la, A_KV_HEADS, HEAD_DIM), 1.0),
        'cache_b_k': nrm(ks[4], (N_LAYERS_B, DEC_BATCH, lb, B_KV_HEADS, HEAD_DIM), 1.0),
        'cache_b_v': nrm(ks[5], (N_LAYERS_B, DEC_BATCH, lb, B_KV_HEADS, HEAD_DIM), 1.0),
        'state_ffn_conv': nrm(ks[6], (DEPTH, DEC_BATCH, CONV_W - 1, D_FF), 1.0),
        'attn_norm': 1.0 + nrm(ks[7], (DEPTH, D_MODEL), 0.02),
        'ffn_norm': 1.0 + nrm(ks[8], (DEPTH, D_MODEL), 0.02),
        'a_w_qkv': nrm(ks[9], (N_LAYERS_A, D_MODEL, a_cols), D_MODEL ** -0.5),
        'a_q_norm': 1.0 + nrm(ks[10], (N_LAYERS_A, HEAD_DIM), 0.02),
        'a_k_norm': 1.0 + nrm(ks[11], (N_LAYERS_A, HEAD_DIM), 0.02),
        'a_sinks': nrm(ks[12], (N_LAYERS_A, A_Q_HEADS), 0.5),
        'a_w_o': nrm(ks[13], (N_LAYERS_A, A_Q_HEADS * HEAD_DIM, D_MODEL), (A_Q_HEADS * HEAD_DIM) ** -0.5),
        'b_w_qkv': nrm(ks[14], (N_LAYERS_B, D_MODEL, b_cols), D_MODEL ** -0.5),
        'b_q_norm': 1.0 + nrm(ks[15], (N_LAYERS_B, HEAD_DIM), 0.02),
        'b_k_norm': 1.0 + nrm(ks[16], (N_LAYERS_B, HEAD_DIM), 0.02),
        'b_w_o': nrm(ks[17], (N_LAYERS_B, B_Q_HEADS * HEAD_DIM, D_MODEL), (B_Q_HEADS * HEAD_DIM) ** -0.5),
        'ffn_w_gate': nrm(ks[18], (DEPTH, D_MODEL, D_FF), D_MODEL ** -0.5),
        'ffn_w_up': nrm(ks[19], (DEPTH, D_MODEL, D_FF), D_MODEL ** -0.5),
        'ffn_conv_w': nrm(ks[20], (DEPTH, CONV_W, D_FF), CONV_W ** -0.5),
        'ffn_conv_b': nrm(ks[21], (DEPTH, D_FF), 0.02),
        'ffn_w_down': nrm(ks[22], (DEPTH, D_FF, D_MODEL), D_FF ** -0.5),
    }


def reference(x_prompt, x_sample, cache_a_k, cache_a_v, cache_b_k, cache_b_v, state_ffn_conv,
              attn_norm, ffn_norm, a_w_qkv, a_q_norm, a_k_norm, a_sinks, a_w_o,
              b_w_qkv, b_q_norm, b_k_norm, b_w_o,
              ffn_w_gate, ffn_w_up, ffn_conv_w, ffn_conv_b, ffn_w_down):
    yp, ys = x_prompt, x_sample
    a_kp, a_vp, a_ks, a_vs = [], [], [], []
    b_kp, b_vp, b_ks, b_vs = [], [], [], []
    conv_p, conv_s = [], []
    for i in range(DEPTH):
        j = i // N_MIXERS
        hp = rms_norm(yp, attn_norm[i])
        hs = rms_norm(ys, attn_norm[i])
        if i % N_MIXERS == 0:
            mp, kp, vp = mixer_a_prompt(hp, a_w_qkv[j], a_q_norm[j], a_k_norm[j], a_sinks[j], a_w_o[j])
            ms, kq, vq = mixer_a_sample(hs, cache_a_k[j], cache_a_v[j], a_w_qkv[j], a_q_norm[j], a_k_norm[j],
                                        a_sinks[j], a_w_o[j])
            a_kp.append(kp); a_vp.append(vp); a_ks.append(kq); a_vs.append(vq)
        else:
            mp, kp, vp = mixer_b_prompt(hp, b_w_qkv[j], b_q_norm[j], b_k_norm[j], b_w_o[j])
            ms, kq, vq = mixer_b_sample(hs, cache_b_k[j], cache_b_v[j], b_w_qkv[j], b_q_norm[j], b_k_norm[j],
                                        b_w_o[j])
            b_kp.append(kp); b_vp.append(vp); b_ks.append(kq); b_vs.append(vq)
        yp = yp + mp
        ys = ys + ms
        zero_prefix = jnp.zeros((yp.shape[0], CONV_W - 1, D_FF), yp.dtype)
        fp, cp = conv_ffn(rms_norm(yp, ffn_norm[i]), zero_prefix, ffn_w_gate[i], ffn_w_up[i],
                          ffn_conv_w[i], ffn_conv_b[i], ffn_w_down[i])
        fs, cs = conv_ffn(rms_norm(ys, ffn_norm[i]), state_ffn_conv[i], ffn_w_gate[i], ffn_w_up[i],
                          ffn_conv_w[i], ffn_conv_b[i], ffn_w_down[i])
        yp = yp + fp
        ys = ys + fs
        conv_p.append(cp)
        conv_s.append(cs)
    a_k_prompt = jnp.stack(a_kp)
    a_v_prompt = jnp.stack(a_vp)
    a_k_sample = jnp.stack(a_ks)
    a_v_sample = jnp.stack(a_vs)
    b_k_prompt = jnp.stack(b_kp)
    b_v_prompt = jnp.stack(b_vp)
    b_k_sample = jnp.stack(b_ks)
    b_v_sample = jnp.stack(b_vs)
    ffn_conv_prompt = jnp.stack(conv_p)
    ffn_conv_sample = jnp.stack(conv_s)
    return (yp, ys, a_k_prompt, a_v_prompt, a_k_sample, a_v_sample,
            b_k_prompt, b_v_prompt, b_k_sample, b_v_sample, ffn_conv_prompt, ffn_conv_sample)
```

```python
import functools

import jax
import jax.numpy as jnp
from jax import lax
from jax.experimental import pallas as pl
from jax.experimental.pallas import tpu as pltpu

F32 = jnp.float32
BF16 = jnp.bfloat16

D_MODEL = 2048
HEAD_DIM = 64
HALF = HEAD_DIM // 2
ROPE_THETA = 10000.0
NORM_EPS = 1e-6
BLOCK = 128
PAST_LEN = 16384
A_WINDOW = 128
A_Q_HEADS = 32
A_KV_HEADS = 8
B_PATTERNS = ((128, 1), (512, 4), (2048, 16))
B_Q_HEADS = 16
B_KV_HEADS = 4
D_FF = 5632
CONV_W = 3
NEG_INF = -1e30

V7X_MXU_DIM = 256
V7X_LANES = 128
V7X_SUBLANES = 8
VMEM_LIMIT = 56 * 1024 * 1024

TM_PROJ = 512
TM_FFN = 512
TF_FFN = 512
N_QKV_TILES = 6


def _params(*sem):
    return pltpu.CompilerParams(dimension_semantics=sem, vmem_limit_bytes=VMEM_LIMIT)


def _rms(x, g):
    ms = jnp.mean(x * x, axis=-1, keepdims=True)
    return x * lax.rsqrt(ms + NORM_EPS) * g


def _head_norm_rope(a, gain, ones_blk, cos, sin):
    tn = a.shape[1]
    x2 = a * a
    hi = x2.astype(BF16)
    lo = (x2 - hi.astype(F32)).astype(BF16)
    parts = []
    for c in range(tn // V7X_MXU_DIM):
        sl = slice(V7X_MXU_DIM * c, V7X_MXU_DIM * (c + 1))
        parts.append(jnp.dot(hi[:, sl], ones_blk, preferred_element_type=F32)
                     + jnp.dot(lo[:, sl], ones_blk, preferred_element_type=F32))
    ss = jnp.concatenate(parts, axis=1)
    y = a * lax.rsqrt(ss * (1.0 / HEAD_DIM) + NORM_EPS) * gain
    lane = lax.broadcasted_iota(jnp.int32, y.shape, 1)
    first_half = (lane & (HEAD_DIM - 1)) < HALF
    partner = jnp.where(first_half, pltpu.roll(y, tn - HALF, 1), pltpu.roll(y, HALF, 1))
    reps = tn // V7X_LANES
    return y * jnp.tile(cos, (1, reps)) + partner * jnp.tile(sin, (1, reps))


def _qkv_kernel(x_ref, g_ref, w_ref, cos_ref, sin_ref, qg_ref, kg_ref, ones_ref,
                q_ref, k_ref, v_ref, xn_ref, *, n_q_tiles):
    n = pl.program_id(1)

    @pl.when(n == 0)
    def _():
        xn_ref[...] = _rms(x_ref[...], g_ref[...]).astype(BF16)

    acc = jnp.dot(xn_ref[...], w_ref[...], preferred_element_type=F32)

    @pl.when(n < n_q_tiles)
    def _():
        r = _head_norm_rope(acc, qg_ref[...], ones_ref[...], cos_ref[...], sin_ref[...])
        q_ref[...] = (r * (HEAD_DIM ** -0.5)).astype(q_ref.dtype)

    @pl.when(n == n_q_tiles)
    def _():
        k_ref[...] = _head_norm_rope(acc, kg_ref[...], ones_ref[...], cos_ref[...], sin_ref[...])

    @pl.when(n == n_q_tiles + 1)
    def _():
        v_ref[...] = acc


def _qkv_call(x, g, w, cos, sin, qg, kg, ones_blk, *, n_q, n_kv, tm, q_dtype):
    m = x.shape[0]
    ncols = w.shape[1]
    tn = ncols // N_QKV_TILES
    n_q_tiles = (n_q * HEAD_DIM) // tn
    assert n_q_tiles * tn == n_q * HEAD_DIM and n_kv * HEAD_DIM == tn and tn % V7X_MXU_DIM == 0
    assert m % tm == 0
    last_q = n_q_tiles - 1
    return pl.pallas_call(
        functools.partial(_qkv_kernel, n_q_tiles=n_q_tiles),
        grid=(m // tm, N_QKV_TILES),
        in_specs=[
            pl.BlockSpec((tm, D_MODEL), lambda i, n: (i, 0)),
            pl.BlockSpec((1, D_MODEL), lambda i, n: (0, 0)),
            pl.BlockSpec((D_MODEL, tn), lambda i, n: (0, n)),
            pl.BlockSpec((tm, V7X_LANES), lambda i, n: (i, 0)),
            pl.BlockSpec((tm, V7X_LANES), lambda i, n: (i, 0)),
            pl.BlockSpec((1, tn), lambda i, n: (0, 0)),
            pl.BlockSpec((1, tn), lambda i, n: (0, 0)),
            pl.BlockSpec((V7X_MXU_DIM, V7X_MXU_DIM), lambda i, n: (0, 0)),
        ],
        out_specs=[
            pl.BlockSpec((tm, tn), lambda i, n: (i, jnp.minimum(n, last_q))),
            pl.BlockSpec((tm, tn), lambda i, n: (i, 0)),
            pl.BlockSpec((tm, tn), lambda i, n: (i, 0)),
        ],
        out_shape=[
            jax.ShapeDtypeStruct((m, n_q * HEAD_DIM), q_dtype),
            jax.ShapeDtypeStruct((m, tn), F32),
            jax.ShapeDtypeStruct((m, tn), F32),
        ],
        scratch_shapes=[pltpu.VMEM((tm, D_MODEL), BF16)],
        compiler_params=_params("parallel", "arbitrary"),
        name="qkv_proj",
    )(x, g, w, cos, sin, qg, kg, ones_blk)


def _band_kernel(*refs, hkv, grp, has_sink, want_lse):
    q_ref, kc_ref, kp_ref, vc_ref, vp_ref = refs[:5]
    pos = 5
    sink_ref = None
    if has_sink:
        sink_ref = refs[pos]
        pos += 1
    o_ref = refs[pos]
    lse_ref = refs[pos + 1] if want_lse else None

    i = pl.program_id(2)
    rows = grp * BLOCK
    qi = lax.broadcasted_iota(jnp.int32, (rows, 2 * BLOCK), 0) & (BLOCK - 1)
    kj = lax.broadcasted_iota(jnp.int32, (rows, 2 * BLOCK), 1)
    mask = (kj >= qi) & (kj <= qi + BLOCK) & ((kj >= BLOCK) | (i > 0))

    for h in range(hkv):
        hs = slice(HEAD_DIM * h, HEAD_DIM * (h + 1))
        kb = jnp.concatenate([kp_ref[0, :, hs], kc_ref[0, :, hs]], axis=0).astype(BF16)
        vb = jnp.concatenate([vp_ref[0, :, hs], vc_ref[0, :, hs]], axis=0).astype(BF16)
        q4 = jnp.concatenate(
            [q_ref[0, :, HEAD_DIM * (grp * h + j):HEAD_DIM * (grp * h + j + 1)] for j in range(grp)],
            axis=0)
        s = lax.dot_general(q4, kb, (((1,), (1,)), ((), ())), preferred_element_type=F32)
        s = jnp.where(mask, s, NEG_INF)
        m = jnp.max(s, axis=1, keepdims=True)
        if has_sink:
            sk = jnp.concatenate(
                [jnp.full((BLOCK, 1), sink_ref[grp * h + j], F32) for j in range(grp)], axis=0)
            m = jnp.maximum(m, sk)
        p = jnp.exp(s - m)
        l = jnp.sum(p, axis=1, keepdims=True)
        if has_sink:
            l = l + jnp.exp(sk - m)
        o = jnp.dot(p.astype(BF16), vb, preferred_element_type=F32) / l
        if want_lse:
            lse = jnp.broadcast_to(m + jnp.log(l), (rows, HEAD_DIM))
        for j in range(grp):
            cs = slice(HEAD_DIM * (grp * h + j), HEAD_DIM * (grp * h + j + 1))
            o_ref[0, :, cs] = o[BLOCK * j:BLOCK * (j + 1)].astype(o_ref.dtype)
            if want_lse:
                lse_ref[0, :, cs] = lse[BLOCK * j:BLOCK * (j + 1)]


def _band_call(q, k, v, sinks, *, hkv, grp, n_res, q_stride, kv_stride, q_off, kv_off, want_lse,
               o_dtype):
    b, n, _ = q.shape
    cq = hkv * grp * HEAD_DIM
    ck = hkv * HEAD_DIM
    nblk = n // BLOCK
    has_sink = sinks is not None
    cur = lambda bi, r, i: (bi, i, r * kv_stride + kv_off)
    prev = lambda bi, r, i: (bi, jnp.maximum(i - 1, 0), r * kv_stride + kv_off)
    in_specs = [
        pl.BlockSpec((1, BLOCK, cq), lambda bi, r, i: (bi, i, r * q_stride + q_off)),
        pl.BlockSpec((1, BLOCK, ck), cur),
        pl.BlockSpec((1, BLOCK, ck), prev),
        pl.BlockSpec((1, BLOCK, ck), cur),
        pl.BlockSpec((1, BLOCK, ck), prev),
    ]
    args = [q, k, k, v, v]
    if has_sink:
        in_specs.append(pl.BlockSpec(memory_space=pltpu.SMEM))
        args.append(sinks)
    o_spec = pl.BlockSpec((1, BLOCK, cq), lambda bi, r, i: (bi, i, r))
    out_specs = [o_spec]
    out_shape = [jax.ShapeDtypeStruct((b, n, n_res * cq), o_dtype)]
    if want_lse:
        out_specs.append(o_spec)
        out_shape.append(jax.ShapeDtypeStruct((b, n, n_res * cq), F32))
    return pl.pallas_call(
        functools.partial(_band_kernel, hkv=hkv, grp=grp, has_sink=has_sink, want_lse=want_lse),
        grid=(b, n_res, nblk),
        in_specs=in_specs,
        out_specs=out_specs,
        out_shape=out_shape,
        compiler_params=_params("parallel", "parallel", "arbitrary"),
        name="band_attn",
    )(*args)


def _wo_kernel(y_ref, o_ref, w_ref, out_ref):
    out_ref[...] = y_ref[...] + jnp.dot(o_ref[...].astype(BF16), w_ref[...],
                                        preferred_element_type=F32)


def _wo_comb_kernel(y_ref, o0_ref, o1_ref, o2_ref, l0_ref, l1_ref, l2_ref, w_ref, out_ref):
    l0, l1, l2 = l0_ref[...], l1_ref[...], l2_ref[...]
    mx = jnp.maximum(jnp.maximum(l0, l1), l2)
    e0, e1, e2 = jnp.exp(l0 - mx), jnp.exp(l1 - mx), jnp.exp(l2 - mx)
    den = e0 + e1 + e2
    comb = (e0 / den) * o0_ref[...] + (e1 / den) * o1_ref[...] + (e2 / den) * o2_ref[...]
    out_ref[...] = y_ref[...] + jnp.dot(comb.astype(BF16), w_ref[...], preferred_element_type=F32)


def _wo_call(y, o, w, *, tm):
    m = y.shape[0]
    c = o.shape[1]
    return pl.pallas_call(
        _wo_kernel,
        grid=(m // tm,),
        in_specs=[
            pl.BlockSpec((tm, D_MODEL), lambda i: (i, 0)),
            pl.BlockSpec((tm, c), lambda i: (i, 0)),
            pl.BlockSpec((c, D_MODEL), lambda i: (0, 0)),
        ],
        out_specs=pl.BlockSpec((tm, D_MODEL), lambda i: (i, 0)),
        out_shape=jax.ShapeDtypeStruct((m, D_MODEL), F32),
        compiler_params=_params("parallel"),
        name="wo_proj",
    )(y, o, w)


def _wo_comb_call(y, os_, ls_, w, *, tm):
    m = y.shape[0]
    c = w.shape[0]
    blk = pl.BlockSpec((tm, c), lambda i: (i, 0))
    return pl.pallas_call(
        _wo_comb_kernel,
        grid=(m // tm,),
        in_specs=[pl.BlockSpec((tm, D_MODEL), lambda i: (i, 0))] + [blk] * 6
                 + [pl.BlockSpec((c, D_MODEL), lambda i: (0, 0))],
        out_specs=pl.BlockSpec((tm, D_MODEL), lambda i: (i, 0)),
        out_shape=jax.ShapeDtypeStruct((m, D_MODEL), F32),
        compiler_params=_params("parallel"),
        name="wo_comb_proj",
    )(y, *os_, *ls_, w)


def _ffn_tail(gate, g1, g2, up, cw_ref, cb_ref, wd_ref):
    conv = cb_ref[...] + cw_ref[0:1, :] * g2 + cw_ref[1:2, :] * g1 + cw_ref[2:3, :] * gate
    h = conv * jax.nn.sigmoid(conv) * up
    return jnp.dot(h.astype(BF16), wd_ref[...], preferred_element_type=F32)


def _ffn_prompt_kernel(y_ref, g_ref, wg_ref, wu_ref, wd_ref, cw_ref, cb_ref,
                       out_ref, tail_ref, xn_ref, carry_ref, *, tiles_per_seq):
    m = pl.program_id(0)
    f = pl.program_id(1)

    @pl.when(f == 0)
    def _():
        x = y_ref[...]
        xn_ref[...] = _rms(x, g_ref[...]).astype(BF16)
        out_ref[...] = x

    xn = xn_ref[...]
    gate = jnp.dot(xn, wg_ref[...], preferred_element_type=F32)
    up = jnp.dot(xn, wu_ref[...], preferred_element_type=F32)
    tm = gate.shape[0]
    @pl.when(m % tiles_per_seq == 0)
    def _():
        carry_ref[f] = jnp.zeros(carry_ref.shape[1:], F32)

    c = carry_ref[f]
    row = lax.broadcasted_iota(jnp.int32, gate.shape, 0)
    g1 = jnp.where(row == 0, c[7:8, :], pltpu.roll(gate, 1, 0))
    g2 = jnp.where(row == 0, c[6:7, :], jnp.where(row == 1, c[7:8, :], pltpu.roll(gate, 2, 0)))
    out_ref[...] += _ffn_tail(gate, g1, g2, up, cw_ref, cb_ref, wd_ref)
    last = gate[tm - V7X_SUBLANES:, :]
    carry_ref[f] = last
    tail_ref[0] = last


def _ffn_prompt_call(y, g, wg, wu, wd, cw, cb, *, seq):
    m = y.shape[0]
    tm, tf = TM_FFN, TF_FFN
    nf = D_FF // tf
    assert m % tm == 0 and seq % tm == 0 and D_FF % tf == 0
    return pl.pallas_call(
        functools.partial(_ffn_prompt_kernel, tiles_per_seq=seq // tm),
        grid=(m // tm, nf),
        in_specs=[
            pl.BlockSpec((tm, D_MODEL), lambda i, f: (i, 0)),
            pl.BlockSpec((1, D_MODEL), lambda i, f: (0, 0)),
            pl.BlockSpec((D_MODEL, tf), lambda i, f: (0, f)),
            pl.BlockSpec((D_MODEL, tf), lambda i, f: (0, f)),
            pl.BlockSpec((tf, D_MODEL), lambda i, f: (f, 0)),
            pl.BlockSpec((CONV_W, tf), lambda i, f: (0, f)),
            pl.BlockSpec((1, tf), lambda i, f: (0, f)),
        ],
        out_specs=[
            pl.BlockSpec((tm, D_MODEL), lambda i, f: (i, 0)),
            pl.BlockSpec((1, V7X_SUBLANES, tf), lambda i, f: (i, 0, f)),
        ],
        out_shape=[
            jax.ShapeDtypeStruct((m, D_MODEL), F32),
            jax.ShapeDtypeStruct((m // tm, V7X_SUBLANES, D_FF), F32),
        ],
        scratch_shapes=[pltpu.VMEM((tm, D_MODEL), BF16),
                        pltpu.VMEM((nf, V7X_SUBLANES, tf), F32)],
        compiler_params=_params("arbitrary", "arbitrary"),
        name="conv_ffn_prompt",
    )(y, g, wg, wu, wd, cw, cb)


def _ffn_sample_kernel(y_ref, g_ref, wg_ref, wu_ref, wd_ref, cw_ref, cb_ref, s0_ref, s1_ref,
                       out_ref, gate_ref, xn_ref, *, seq):
    f = pl.program_id(0)

    @pl.when(f == 0)
    def _():
        x = y_ref[...]
        xn_ref[...] = _rms(x, g_ref[...]).astype(BF16)
        out_ref[...] = x

    xn = xn_ref[...]
    gate = jnp.dot(xn, wg_ref[...], preferred_element_type=F32)
    up = jnp.dot(xn, wu_ref[...], preferred_element_type=F32)
    t = lax.broadcasted_iota(jnp.int32, gate.shape, 0) & (seq - 1)
    s0, s1 = s0_ref[...], s1_ref[...]
    g1 = jnp.where(t == 0, s1, pltpu.roll(gate, 1, 0))
    g2 = jnp.where(t == 0, s0, jnp.where(t == 1, s1, pltpu.roll(gate, 2, 0)))
    out_ref[...] += _ffn_tail(gate, g1, g2, up, cw_ref, cb_ref, wd_ref)
    gate_ref[...] = gate


def _ffn_sample_call(y, g, wg, wu, wd, cw, cb, s0, s1, *, seq):
    m = y.shape[0]
    tf = TF_FFN
    nf = D_FF // tf
    full = pl.BlockSpec((m, D_MODEL), lambda f: (0, 0))
    col = pl.BlockSpec((m, tf), lambda f: (0, f))
    return pl.pallas_call(
        functools.partial(_ffn_sample_kernel, seq=seq),
        grid=(nf,),
        in_specs=[
            full,
            pl.BlockSpec((1, D_MODEL), lambda f: (0, 0)),
            pl.BlockSpec((D_MODEL, tf), lambda f: (0, f)),
            pl.BlockSpec((D_MODEL, tf), lambda f: (0, f)),
            pl.BlockSpec((tf, D_MODEL), lambda f: (f, 0)),
            pl.BlockSpec((CONV_W, tf), lambda f: (0, f)),
            pl.BlockSpec((1, tf), lambda f: (0, f)),
            col, col,
        ],
        out_specs=[full, col],
        out_shape=[jax.ShapeDtypeStruct((m, D_MODEL), F32),
                   jax.ShapeDtypeStruct((m, D_FF), F32)],
        scratch_shapes=[pltpu.VMEM((m, D_MODEL), BF16)],
        compiler_params=_params("arbitrary"),
        name="conv_ffn_sample",
    )(y, g, wg, wu, wd, cw, cb, s0, s1)


def _block_diag_q(q_ref, col0, hkv, grp, s):
    blocks = []
    for h in range(hkv):
        qh = jnp.concatenate(
            [q_ref[:, col0 + HEAD_DIM * (grp * h + j):col0 + HEAD_DIM * (grp * h + j + 1)]
             for j in range(grp)], axis=0)
        pieces = []
        if h > 0:
            pieces.append(jnp.zeros((grp * s, HEAD_DIM * h), F32))
        pieces.append(qh)
        if h < hkv - 1:
            pieces.append(jnp.zeros((grp * s, HEAD_DIM * (hkv - 1 - h)), F32))
        blocks.append(jnp.concatenate(pieces, axis=1) if len(pieces) > 1 else qh)
    return jnp.concatenate(blocks, axis=0).astype(BF16)


def _cached_attend(qbd, k_all, v_all, lc, d, s, sink_col):
    r_, nk = qbd.shape[0], k_all.shape[0]
    sc = lax.dot_general(qbd, k_all.astype(BF16), (((1,), (1,)), ((), ())),
                         preferred_element_type=F32)
    i = lax.broadcasted_iota(jnp.int32, (r_, nk), 0) & (s - 1)
    c = lax.broadcasted_iota(jnp.int32, (r_, nk), 1)
    valid = (c >= i) & (c <= lc + i)
    if d > 1:
        valid = valid & ((c & (d - 1)) == (i & (d - 1)))
    sc = jnp.where(valid, sc, NEG_INF)
    m = jnp.max(sc, axis=1, keepdims=True)
    if sink_col is not None:
        m = jnp.maximum(m, sink_col)
    p = jnp.exp(sc - m)
    l = jnp.sum(p, axis=1, keepdims=True)
    if sink_col is not None:
        l = l + jnp.exp(sink_col - m)
    o = jnp.dot(p.astype(BF16), v_all.astype(BF16), preferred_element_type=F32) / l
    return o, m + jnp.log(l)


def _diag_heads(o, hkv, grp, s):
    pieces = []
    for h in range(hkv):
        for j in range(grp):
            r0 = (h * grp + j) * s
            pieces.append(o[r0:r0 + s, HEAD_DIM * h:HEAD_DIM * (h + 1)])
    return jnp.concatenate(pieces, axis=1)


def _pad_rows(parts, total):
    n = sum(p.shape[0] for p in parts)
    if total > n:
        parts = parts + [jnp.zeros((total - n, parts[0].shape[1]), F32)]
    return jnp.concatenate(parts, axis=0)


def _sample_a_kernel(q_ref, kn_ref, vn_ref, ck_ref, cv_ref, sink_ref, o_ref, ko_ref, vo_ref, *, s):
    lc = ck_ref.shape[1]
    hkv, grp = A_KV_HEADS, A_Q_HEADS // A_KV_HEADS
    kn, vn = kn_ref[...], vn_ref[...]
    ck, cv = ck_ref[0], cv_ref[0]
    nk = lc + BLOCK
    k_all = _pad_rows([ck, kn], nk)
    v_all = _pad_rows([cv, vn], nk)
    qbd = _block_diag_q(q_ref, 0, hkv, grp, s)
    sink_col = jnp.concatenate(
        [jnp.full((s, 1), sink_ref[hq], F32) for hq in range(hkv * grp)], axis=0)
    o, _ = _cached_attend(qbd, k_all, v_all, lc, 1, s, sink_col)
    o_ref[...] = _diag_heads(o, hkv, grp, s)
    ko_ref[0, :lc - s, :] = ck[s:, :]
    ko_ref[0, lc - s:, :] = kn
    vo_ref[0, :lc - s, :] = cv[s:, :]
    vo_ref[0, lc - s:, :] = vn


def _sample_a_call(q, kn, vn, ck, cv, sinks, *, s):
    nb, lc, c = ck.shape
    row = lambda w: pl.BlockSpec((s, w), lambda b: (b, 0))
    cache = pl.BlockSpec((1, lc, c), lambda b: (b, 0, 0))
    return pl.pallas_call(
        functools.partial(_sample_a_kernel, s=s),
        grid=(nb,),
        in_specs=[row(q.shape[1]), row(c), row(c), cache, cache,
                  pl.BlockSpec(memory_space=pltpu.SMEM)],
        out_specs=[row(q.shape[1]), cache, cache],
        out_shape=[jax.ShapeDtypeStruct(q.shape, F32),
                   jax.ShapeDtypeStruct(ck.shape, F32),
                   jax.ShapeDtypeStruct(cv.shape, F32)],
        compiler_params=_params("parallel"),
        name="sample_attn_a",
    )(q, kn, vn, ck, cv, sinks)


def _sample_b_kernel(q_ref, kn_ref, vn_ref, ck_ref, cv_ref, o_ref, ko_ref, vo_ref, *, s):
    hkv, grp = B_KV_HEADS, B_Q_HEADS // B_KV_HEADS
    ckv = hkv * HEAD_DIM
    outs, lses = [], []
    off = 0
    for g, (w, d) in enumerate(B_PATTERNS):
        lc = w
        kn = kn_ref[:, ckv * g:ckv * (g + 1)]
        vn = vn_ref[:, ckv * g:ckv * (g + 1)]
        ck = ck_ref[0, off:off + lc, :]
        cv = cv_ref[0, off:off + lc, :]
        nk = lc + BLOCK
        qbd = _block_diag_q(q_ref, B_Q_HEADS * HEAD_DIM * g, hkv, grp, s)
        o, lse = _cached_attend(qbd, _pad_rows([ck, kn], nk), _pad_rows([cv, vn], nk), lc, d, s, None)
        outs.append(o)
        lses.append(lse)
        ko_ref[0, off:off + lc - s, :] = ck[s:, :]
        ko_ref[0, off + lc - s:off + lc, :] = kn
        vo_ref[0, off:off + lc - s, :] = cv[s:, :]
        vo_ref[0, off + lc - s:off + lc, :] = vn
        off += lc
    mx = jnp.maximum(jnp.maximum(lses[0], lses[1]), lses[2])
    es = [jnp.exp(l - mx) for l in lses]
    den = es[0] + es[1] + es[2]
    comb = (es[0] / den) * outs[0] + (es[1] / den) * outs[1] + (es[2] / den) * outs[2]
    o_ref[...] = _diag_heads(comb, hkv, grp, s)


def _sample_b_call(q, kn, vn, ck, cv, *, s):
    nb, lb, c = ck.shape
    row = lambda w: pl.BlockSpec((s, w), lambda b: (b, 0))
    cache = pl.BlockSpec((1, lb, c), lambda b: (b, 0, 0))
    co = B_Q_HEADS * HEAD_DIM
    return pl.pallas_call(
        functools.partial(_sample_b_kernel, s=s),
        grid=(nb,),
        in_specs=[row(q.shape[1]), row(kn.shape[1]), row(vn.shape[1]), cache, cache],
        out_specs=[row(co), cache, cache],
        out_shape=[jax.ShapeDtypeStruct((q.shape[0], co), F32),
                   jax.ShapeDtypeStruct(ck.shape, F32),
                   jax.ShapeDtypeStruct(cv.shape, F32)],
        compiler_params=_params("parallel"),
        name="sample_attn_b",
    )(q, kn, vn, ck, cv)


def _rope_tables(pos):
    inv_freq = ROPE_THETA ** (-jnp.arange(HALF, dtype=F32) / HALF)
    ang = pos.astype(F32)[:, None] * inv_freq[None, :]
    cos, sin = jnp.cos(ang), jnp.sin(ang)
    return jnp.tile(cos, (1, 4)), jnp.tile(jnp.concatenate([-sin, sin], axis=1), (1, 2))


def kernel(x_prompt, x_sample, cache_a_k, cache_a_v, cache_b_k, cache_b_v, state_ffn_conv,
           attn_norm, ffn_norm, a_w_qkv, a_q_norm, a_k_norm, a_sinks, a_w_o,
           b_w_qkv, b_q_norm, b_k_norm, b_w_o,
           ffn_w_gate, ffn_w_up, ffn_conv_w, ffn_conv_b, ffn_w_down):
    nb, seq, _ = x_prompt.shape
    ns, dec, _ = x_sample.shape
    mp, ms = nb * seq, ns * dec

    cos_p, sin_p = _rope_tables(jnp.tile(jnp.arange(seq, dtype=jnp.int32), nb))
    cos_s, sin_s = _rope_tables(jnp.tile(PAST_LEN + jnp.arange(dec, dtype=jnp.int32), ns))
    idx = jnp.arange(V7X_MXU_DIM, dtype=jnp.int32) // HEAD_DIM
    ones_blk = (idx[:, None] == idx[None, :]).astype(BF16)

    yp = x_prompt.reshape(mp, D_MODEL)
    ys = x_sample.reshape(ms, D_MODEL)
    row = lambda a: a.reshape(1, -1)

    def head_gain(gv, n_heads_per_tile):
        return jnp.tile(gv, n_heads_per_tile).reshape(1, -1)

    wqkv = a_w_qkv[0].astype(BF16)
    wo = a_w_o[0].astype(BF16)
    tn = wqkv.shape[1] // N_QKV_TILES
    qg, kg = head_gain(a_q_norm[0], tn // HEAD_DIM), head_gain(a_k_norm[0], tn // HEAD_DIM)
    g_attn = row(attn_norm[0])
    grp_a = A_Q_HEADS // A_KV_HEADS
    ca = A_KV_HEADS * HEAD_DIM

    q, k, v = _qkv_call(yp, g_attn, wqkv, cos_p, sin_p, qg, kg, ones_blk,
                        n_q=A_Q_HEADS, n_kv=A_KV_HEADS, tm=TM_PROJ, q_dtype=BF16)
    (o,) = _band_call(q.reshape(nb, seq, -1), k.reshape(nb, seq, -1), v.reshape(nb, seq, -1),
                      a_sinks[0], hkv=A_KV_HEADS, grp=grp_a, n_res=1, q_stride=1, kv_stride=1,
                      q_off=0, kv_off=0, want_lse=False, o_dtype=BF16)
    yp = _wo_call(yp, o.reshape(mp, -1), wo, tm=TM_PROJ)
    keep = min(A_WINDOW, seq)
    a_k_prompt = k.reshape(nb, seq, A_KV_HEADS, HEAD_DIM)[:, seq - keep:][None]
    a_v_prompt = v.reshape(nb, seq, A_KV_HEADS, HEAD_DIM)[:, seq - keep:][None]

    qs, ks, vs = _qkv_call(ys, g_attn, wqkv, cos_s, sin_s, qg, kg, ones_blk,
                           n_q=A_Q_HEADS, n_kv=A_KV_HEADS, tm=ms, q_dtype=F32)
    la = cache_a_k.shape[2]
    os_, ako, avo = _sample_a_call(qs, ks, vs, cache_a_k[0].reshape(ns, la, ca),
                                   cache_a_v[0].reshape(ns, la, ca), a_sinks[0], s=dec)
    ys = _wo_call(ys, os_, wo, tm=ms)
    a_k_sample = ako.reshape(1, ns, la, A_KV_HEADS, HEAD_DIM)
    a_v_sample = avo.reshape(1, ns, la, A_KV_HEADS, HEAD_DIM)

    conv_p, conv_s = [], []

    def ffn(layer, yp, ys):
        wg = ffn_w_gate[layer].astype(BF16)
        wu = ffn_w_up[layer].astype(BF16)
        wd = ffn_w_down[layer].astype(BF16)
        g_ffn = row(ffn_norm[layer])
        cw, cb = ffn_conv_w[layer], row(ffn_conv_b[layer])
        yp, tail = _ffn_prompt_call(yp, g_ffn, wg, wu, wd, cw, cb, seq=seq)
        tiles = seq // TM_FFN
        conv_p.append(tail[tiles - 1::tiles, V7X_SUBLANES - (CONV_W - 1):, :])
        st = state_ffn_conv[layer]
        s0 = jnp.repeat(st[:, 0, :], dec, axis=0)
        s1 = jnp.repeat(st[:, 1, :], dec, axis=0)
        ys, gate_s = _ffn_sample_call(ys, g_ffn, wg, wu, wd, cw, cb, s0, s1, seq=dec)
        conv_s.append(gate_s.reshape(ns, dec, D_FF)[:, dec - (CONV_W - 1):, :])
        return yp, ys

    yp, ys = ffn(0, yp, ys)

    wqkv = b_w_qkv[0].astype(BF16)
    wo = b_w_o[0].astype(BF16)
    tn = wqkv.shape[1] // N_QKV_TILES
    qg, kg = head_gain(b_q_norm[0], tn // HEAD_DIM), head_gain(b_k_norm[0], tn // HEAD_DIM)
    g_attn = row(attn_norm[1])
    n_grp = len(B_PATTERNS)
    grp_b = B_Q_HEADS // B_KV_HEADS
    nqb, nkvb = n_grp * B_Q_HEADS, n_grp * B_KV_HEADS
    cb_ = B_KV_HEADS * HEAD_DIM

    q, k, v = _qkv_call(yp, g_attn, wqkv, cos_p, sin_p, qg, kg, ones_blk,
                        n_q=nqb, n_kv=nkvb, tm=TM_PROJ, q_dtype=BF16)
    outs, lses = [], []
    for g, (w, d) in enumerate(B_PATTERNS):
        assert w // d == BLOCK and (seq // d) % BLOCK == 0
        n = seq // d
        og, lg = _band_call(q.reshape(nb, n, -1), k.reshape(nb, n, -1), v.reshape(nb, n, -1), None,
                            hkv=B_KV_HEADS, grp=grp_b, n_res=d, q_stride=n_grp, kv_stride=n_grp,
                            q_off=g, kv_off=g, want_lse=True, o_dtype=F32)
        outs.append(og.reshape(mp, -1))
        lses.append(lg.reshape(mp, -1))
    yp = _wo_comb_call(yp, outs, lses, wo, tm=TM_PROJ // 2)
    k4 = k.reshape(nb, seq, nkvb, HEAD_DIM)
    v4 = v.reshape(nb, seq, nkvb, HEAD_DIM)
    kk, vk = [], []
    for g, (w, d) in enumerate(B_PATTERNS):
        keep = min(w, seq)
        kk.append(k4[:, seq - keep:, B_KV_HEADS * g:B_KV_HEADS * (g + 1)])
        vk.append(v4[:, seq - keep:, B_KV_HEADS * g:B_KV_HEADS * (g + 1)])
    b_k_prompt = jnp.concatenate(kk, axis=1)[None]
    b_v_prompt = jnp.concatenate(vk, axis=1)[None]

    qs, ks, vs = _qkv_call(ys, g_attn, wqkv, cos_s, sin_s, qg, kg, ones_blk,
                           n_q=nqb, n_kv=nkvb, tm=ms, q_dtype=F32)
    lb = cache_b_k.shape[2]
    os_, bko, bvo = _sample_b_call(qs, ks, vs, cache_b_k[0].reshape(ns, lb, cb_),
                                   cache_b_v[0].reshape(ns, lb, cb_), s=dec)
    ys = _wo_call(ys, os_, wo, tm=ms)
    b_k_sample = bko.reshape(1, ns, lb, B_KV_HEADS, HEAD_DIM)
    b_v_sample = bvo.reshape(1, ns, lb, B_KV_HEADS, HEAD_DIM)

    yp, ys = ffn(1, yp, ys)

    return (yp.reshape(nb, seq, D_MODEL), ys.reshape(ns, dec, D_MODEL),
            a_k_prompt, a_v_prompt, a_k_sample, a_v_sample,
            b_k_prompt, b_v_prompt, b_k_sample, b_v_sample,
            jnp.stack(conv_p), jnp.stack(conv_s))
```

```python
import functools

import jax
import jax.numpy as jnp
from jax import lax
from jax.experimental import pallas as pl
from jax.experimental.pallas import tpu as pltpu

F32 = jnp.float32
BF16 = jnp.bfloat16

D_MODEL = 2048
HEAD_DIM = 64
HALF = HEAD_DIM // 2
ROPE_THETA = 10000.0
NORM_EPS = 1e-6
BLOCK = 128
PAST_LEN = 16384
A_WINDOW = 128
A_Q_HEADS = 32
A_KV_HEADS = 8
B_PATTERNS = ((128, 1), (512, 4), (2048, 16))
B_Q_HEADS = 16
B_KV_HEADS = 4
D_FF = 5632
CONV_W = 3
NEG_INF = -1e30

V7X_MXU_DIM = 256
V7X_LANES = 128
V7X_SUBLANES = 8
VMEM_LIMIT = 56 * 1024 * 1024

TM_PROJ = 512
TM_FFN = 512
TF_FFN = 512
TR_CAST = 256
N_QKV_TILES = 6
DIL_MAX_ROWS = 1024


def _params(*sem):
    return pltpu.CompilerParams(dimension_semantics=sem, vmem_limit_bytes=VMEM_LIMIT)


def _rms(x, g):
    ms = jnp.mean(x * x, axis=-1, keepdims=True)
    return x * lax.rsqrt(ms + NORM_EPS) * g


def _cast_kernel(w_ref, o_ref):
    o_ref[...] = w_ref[0].astype(BF16)


def _cast_call(w, layer):
    _, r, c = w.shape
    tr = TR_CAST
    assert r % tr == 0
    return pl.pallas_call(
        _cast_kernel,
        grid=(r // tr,),
        in_specs=[pl.BlockSpec((1, tr, c), lambda i: (layer, i, 0))],
        out_specs=pl.BlockSpec((tr, c), lambda i: (i, 0)),
        out_shape=jax.ShapeDtypeStruct((r, c), BF16),
        compiler_params=_params("parallel"),
        name="cast_bf16",
    )(w)


def _head_norm_rope(a, gain, ones_blk, cos, sin):
    tn = a.shape[1]
    x2 = a * a
    hi = x2.astype(BF16)
    lo = (x2 - hi.astype(F32)).astype(BF16)
    parts = []
    for c in range(tn // V7X_MXU_DIM):
        sl = slice(V7X_MXU_DIM * c, V7X_MXU_DIM * (c + 1))
        parts.append(jnp.dot(hi[:, sl], ones_blk, preferred_element_type=F32)
                     + jnp.dot(lo[:, sl], ones_blk, preferred_element_type=F32))
    ss = jnp.concatenate(parts, axis=1)
    y = a * lax.rsqrt(ss * (1.0 / HEAD_DIM) + NORM_EPS) * gain
    lane = lax.broadcasted_iota(jnp.int32, y.shape, 1)
    first_half = (lane & (HEAD_DIM - 1)) < HALF
    partner = jnp.where(first_half, pltpu.roll(y, tn - HALF, 1), pltpu.roll(y, HALF, 1))
    reps = tn // V7X_LANES
    return y * jnp.tile(cos, (1, reps)) + partner * jnp.tile(sin, (1, reps))


def _store_cols(ref, val, lane_major):
    if not lane_major:
        ref[...] = val.astype(ref.dtype)
        return
    for t in range(val.shape[1] // V7X_LANES):
        ref[t] = val[:, V7X_LANES * t:V7X_LANES * (t + 1)].astype(ref.dtype)


def _qkv_kernel(x_ref, g_ref, w_ref, cos_ref, sin_ref, qg_ref, kg_ref, ones_ref,
                q_ref, k_ref, v_ref, xn_ref, *, n_q_tiles, lane_major):
    n = pl.program_id(1)

    @pl.when(n == 0)
    def _():
        xn_ref[...] = _rms(x_ref[...], g_ref[...]).astype(BF16)

    acc = jnp.dot(xn_ref[...], w_ref[...], preferred_element_type=F32)

    @pl.when(n < n_q_tiles)
    def _():
        r = _head_norm_rope(acc, qg_ref[...], ones_ref[...], cos_ref[...], sin_ref[...])
        _store_cols(q_ref, r * (HEAD_DIM ** -0.5), lane_major)

    @pl.when(n == n_q_tiles)
    def _():
        _store_cols(k_ref, _head_norm_rope(acc, kg_ref[...], ones_ref[...], cos_ref[...],
                                           sin_ref[...]), lane_major)

    @pl.when(n == n_q_tiles + 1)
    def _():
        _store_cols(v_ref, acc, lane_major)


def _qkv_call(x, g, w, cos, sin, qg, kg, ones_blk, *, n_q, n_kv, tm, q_dtype, lane_major=False):
    m = x.shape[0]
    ncols = w.shape[1]
    tn = ncols // N_QKV_TILES
    n_q_tiles = (n_q * HEAD_DIM) // tn
    assert n_q_tiles * tn == n_q * HEAD_DIM and n_kv * HEAD_DIM == tn and tn % V7X_MXU_DIM == 0
    assert m % tm == 0
    last_q = n_q_tiles - 1
    if lane_major:
        lt = tn // V7X_LANES
        out_specs = [
            pl.BlockSpec((lt, tm, V7X_LANES), lambda i, n: (jnp.minimum(n, last_q), i, 0)),
            pl.BlockSpec((lt, tm, V7X_LANES), lambda i, n: (0, i, 0)),
            pl.BlockSpec((lt, tm, V7X_LANES), lambda i, n: (0, i, 0)),
        ]
        out_shape = [
            jax.ShapeDtypeStruct((n_q_tiles * lt, m, V7X_LANES), q_dtype),
            jax.ShapeDtypeStruct((lt, m, V7X_LANES), F32),
            jax.ShapeDtypeStruct((lt, m, V7X_LANES), F32),
        ]
    else:
        out_specs = [
            pl.BlockSpec((tm, tn), lambda i, n: (i, jnp.minimum(n, last_q))),
            pl.BlockSpec((tm, tn), lambda i, n: (i, 0)),
            pl.BlockSpec((tm, tn), lambda i, n: (i, 0)),
        ]
        out_shape = [
            jax.ShapeDtypeStruct((m, n_q * HEAD_DIM), q_dtype),
            jax.ShapeDtypeStruct((m, tn), F32),
            jax.ShapeDtypeStruct((m, tn), F32),
        ]
    return pl.pallas_call(
        functools.partial(_qkv_kernel, n_q_tiles=n_q_tiles, lane_major=lane_major),
        grid=(m // tm, N_QKV_TILES),
        in_specs=[
            pl.BlockSpec((tm, D_MODEL), lambda i, n: (i, 0)),
            pl.BlockSpec((1, D_MODEL), lambda i, n: (0, 0)),
            pl.BlockSpec((D_MODEL, tn), lambda i, n: (0, n)),
            pl.BlockSpec((tm, V7X_LANES), lambda i, n: (i, 0)),
            pl.BlockSpec((tm, V7X_LANES), lambda i, n: (i, 0)),
            pl.BlockSpec((1, tn), lambda i, n: (0, 0)),
            pl.BlockSpec((1, tn), lambda i, n: (0, 0)),
            pl.BlockSpec((V7X_MXU_DIM, V7X_MXU_DIM), lambda i, n: (0, 0)),
        ],
        out_specs=out_specs,
        out_shape=out_shape,
        scratch_shapes=[pltpu.VMEM((tm, D_MODEL), BF16)],
        compiler_params=_params("parallel", "arbitrary"),
        name="qkv_proj",
    )(x, g, w, cos, sin, qg, kg, ones_blk)


def _band_mask(rows, first):
    qi = lax.broadcasted_iota(jnp.int32, (rows, 2 * BLOCK), 0) & (BLOCK - 1)
    kj = lax.broadcasted_iota(jnp.int32, (rows, 2 * BLOCK), 1)
    return (kj >= qi) & (kj <= qi + BLOCK) & ((kj >= BLOCK) | jnp.logical_not(first))


def _softmax_pv(q4, kb, vb, mask, sk):
    s = lax.dot_general(q4, kb, (((1,), (1,)), ((), ())), preferred_element_type=F32)
    s = jnp.where(mask, s, NEG_INF)
    m = jnp.max(s, axis=1, keepdims=True)
    if sk is not None:
        m = jnp.maximum(m, sk)
    p = jnp.exp(s - m)
    l = jnp.sum(p, axis=1, keepdims=True)
    if sk is not None:
        l = l + jnp.exp(sk - m)
    o = jnp.dot(p.astype(BF16), vb, preferred_element_type=F32) / l
    return o, m + jnp.log(l)


def _band_kernel(q_ref, kc_ref, kp_ref, vc_ref, vp_ref, sink_ref, o_ref, *, hkv, grp):
    rows = grp * BLOCK
    mask = _band_mask(rows, pl.program_id(1) == 0)
    for h in range(hkv):
        hs = slice(HEAD_DIM * h, HEAD_DIM * (h + 1))
        kb = jnp.concatenate([kp_ref[0, :, hs], kc_ref[0, :, hs]], axis=0).astype(BF16)
        vb = jnp.concatenate([vp_ref[0, :, hs], vc_ref[0, :, hs]], axis=0).astype(BF16)
        q4 = jnp.concatenate(
            [q_ref[0, :, HEAD_DIM * (grp * h + j):HEAD_DIM * (grp * h + j + 1)]
             for j in range(grp)], axis=0)
        sk = jnp.concatenate(
            [jnp.full((BLOCK, 1), sink_ref[grp * h + j], F32) for j in range(grp)], axis=0)
        o, _ = _softmax_pv(q4, kb, vb, mask, sk)
        cs = slice(HEAD_DIM * grp * h, HEAD_DIM * grp * (h + 1))
        o_ref[0, :, cs] = jnp.concatenate(
            [o[BLOCK * j:BLOCK * (j + 1)] for j in range(grp)], axis=1).astype(o_ref.dtype)


def _band_call(q, k, v, sinks, *, hkv, grp):
    b, seq, cq = q.shape
    ck = hkv * HEAD_DIM
    cur = lambda bi, c: (bi, c, 0)
    prev = lambda bi, c: (bi, jnp.maximum(c - 1, 0), 0)
    return pl.pallas_call(
        functools.partial(_band_kernel, hkv=hkv, grp=grp),
        grid=(b, seq // BLOCK),
        in_specs=[
            pl.BlockSpec((1, BLOCK, cq), cur),
            pl.BlockSpec((1, BLOCK, ck), cur),
            pl.BlockSpec((1, BLOCK, ck), prev),
            pl.BlockSpec((1, BLOCK, ck), cur),
            pl.BlockSpec((1, BLOCK, ck), prev),
            pl.BlockSpec(memory_space=pltpu.SMEM),
        ],
        out_specs=pl.BlockSpec((1, BLOCK, cq), cur),
        out_shape=jax.ShapeDtypeStruct((b, seq, cq), BF16),
        compiler_params=_params("parallel", "arbitrary"),
        name="band_attn",
    )(q, k, k, v, v, sinks)


def _dil_kernel(q_ref, kc_ref, kp_ref, vc_ref, vp_ref, o_ref, lse_ref, *, kv_tiles, grp, d):
    rows = grp * BLOCK
    mask = _band_mask(rows, pl.program_id(1) == 0)

    def residue(r):
        rs = pl.ds(r, BLOCK, stride=d) if d > 1 else slice(None)
        for t in range(kv_tiles):
            kt = jnp.concatenate([kp_ref[t, rs, :], kc_ref[t, rs, :]], axis=0).astype(BF16)
            vt = jnp.concatenate([vp_ref[t, rs, :], vc_ref[t, rs, :]], axis=0).astype(BF16)
            qts = [q_ref[grp * t + u, rs, :].astype(BF16) for u in range(grp)]
            o_parts, l_parts = [], []
            for e in range(2):
                ls = slice(HEAD_DIM * e, HEAD_DIM * (e + 1))
                q4 = jnp.concatenate(
                    [qts[(grp * e + j) // 2][:, HEAD_DIM * ((grp * e + j) % 2):
                                             HEAD_DIM * ((grp * e + j) % 2 + 1)]
                     for j in range(grp)], axis=0)
                o, lse = _softmax_pv(q4, kt[:, ls], vt[:, ls], mask, None)
                lse = jnp.broadcast_to(lse, (rows, HEAD_DIM))
                for j in range(grp):
                    o_parts.append(o[BLOCK * j:BLOCK * (j + 1)])
                    l_parts.append(lse[BLOCK * j:BLOCK * (j + 1)])
            for u in range(grp):
                o_ref[grp * t + u, rs, :] = jnp.concatenate(o_parts[2 * u:2 * u + 2], axis=1)
                lse_ref[grp * t + u, rs, :] = jnp.concatenate(l_parts[2 * u:2 * u + 2], axis=1)

    if d > 1:
        def body(r, carry):
            residue(r)
            return carry
        lax.fori_loop(0, d, body, 0)
    else:
        residue(0)


def _dil_call(q, k, v, *, nb, hkv, grp, d, group):
    m = q.shape[1]
    seq = m // nb
    rows = BLOCK * d
    hsplit = max(1, rows // DIL_MAX_ROWS)
    kv_tiles = hkv // 2 // hsplit
    q_tiles = kv_tiles * grp
    nchunk = seq // rows
    assert seq % rows == 0 and kv_tiles * 2 * hsplit == hkv and grp % 2 == 0
    cur = lambda bi, c, hp: (group * hsplit + hp, bi * nchunk + c, 0)
    prev = lambda bi, c, hp: (group * hsplit + hp, bi * nchunk + jnp.maximum(c - 1, 0), 0)
    o_spec = pl.BlockSpec((q_tiles, rows, V7X_LANES), lambda bi, c, hp: (hp, bi * nchunk + c, 0))
    o_shape = jax.ShapeDtypeStruct((q_tiles * hsplit, m, V7X_LANES), F32)
    return pl.pallas_call(
        functools.partial(_dil_kernel, kv_tiles=kv_tiles, grp=grp, d=d),
        grid=(nb, nchunk, hsplit),
        in_specs=[
            pl.BlockSpec((q_tiles, rows, V7X_LANES), cur),
            pl.BlockSpec((kv_tiles, rows, V7X_LANES), cur),
            pl.BlockSpec((kv_tiles, rows, V7X_LANES), prev),
            pl.BlockSpec((kv_tiles, rows, V7X_LANES), cur),
            pl.BlockSpec((kv_tiles, rows, V7X_LANES), prev),
        ],
        out_specs=[o_spec, o_spec],
        out_shape=[o_shape, o_shape],
        compiler_params=_params("parallel", "arbitrary", "arbitrary"),
        name="dilated_attn",
    )(q, k, k, v, v)


def _keep_t_kernel(x_ref, o_ref):
    o_ref[0] = x_ref[0].T


def _keep_t_lane_major_kernel(x_ref, o_ref):
    o_ref[0] = x_ref[0, 0].T


def _keep_t_call(x, keeps, c, *, lane_major=False):
    b, seq = x.shape[1:3] if lane_major else x.shape[:2]
    starts, first_rb = [], []
    n = 0
    for keep in keeps:
        assert keep % BLOCK == 0 and seq % BLOCK == 0
        starts.append(n)
        first_rb.append((seq - keep) // BLOCK)
        n += keep // BLOCK

    def src_block(j):
        rb = jnp.int32(0)
        cb = jnp.int32(0)
        for g in range(len(keeps)):
            inside = j >= starts[g]
            rb = jnp.where(inside, first_rb[g] + j - starts[g], rb)
            cb = jnp.where(inside, g, cb)
        return rb, cb

    if lane_major:
        ct = c // V7X_LANES
        grid = (b, ct, n)

        def src(bi, t, j):
            rb, cb = src_block(j)
            return cb * ct + t, bi, rb, 0

        in_spec = pl.BlockSpec((1, 1, BLOCK, V7X_LANES), src)
        out_spec = pl.BlockSpec((1, V7X_LANES, BLOCK), lambda bi, t, j: (bi, t, j))
        kern = _keep_t_lane_major_kernel
        sem = ("parallel", "parallel", "parallel")
    else:
        grid = (b, n)

        def src(bi, j):
            rb, cb = src_block(j)
            return bi, rb, cb

        in_spec = pl.BlockSpec((1, BLOCK, c), src)
        out_spec = pl.BlockSpec((1, c, BLOCK), lambda bi, j: (bi, 0, j))
        kern = _keep_t_kernel
        sem = ("parallel", "parallel")
    return pl.pallas_call(
        kern,
        grid=grid,
        in_specs=[in_spec],
        out_specs=out_spec,
        out_shape=jax.ShapeDtypeStruct((b, c, n * BLOCK), F32),
        compiler_params=_params(*sem),
        name="keep_rows_t",
    )(x)


def _wo_kernel(y_ref, o_ref, w_ref, out_ref):
    out_ref[...] = y_ref[...] + jnp.dot(o_ref[...].astype(BF16), w_ref[...],
                                        preferred_element_type=F32)


def _wo_comb_kernel(y_ref, o0_ref, o1_ref, o2_ref, l0_ref, l1_ref, l2_ref, w_ref, out_ref):
    tiles = []
    for t in range(o0_ref.shape[0]):
        l0, l1, l2 = l0_ref[t], l1_ref[t], l2_ref[t]
        mx = jnp.maximum(jnp.maximum(l0, l1), l2)
        e0, e1, e2 = jnp.exp(l0 - mx), jnp.exp(l1 - mx), jnp.exp(l2 - mx)
        den = e0 + e1 + e2
        comb = (e0 / den) * o0_ref[t] + (e1 / den) * o1_ref[t] + (e2 / den) * o2_ref[t]
        tiles.append(comb.astype(BF16))
    comb = jnp.concatenate(tiles, axis=1)
    out_ref[...] = y_ref[...] + jnp.dot(comb, w_ref[...], preferred_element_type=F32)


def _wo_call(y, o, w, *, tm):
    m = y.shape[0]
    c = o.shape[1]
    return pl.pallas_call(
        _wo_kernel,
        grid=(m // tm,),
        in_specs=[
            pl.BlockSpec((tm, D_MODEL), lambda i: (i, 0)),
            pl.BlockSpec((tm, c), lambda i: (i, 0)),
            pl.BlockSpec((c, D_MODEL), lambda i: (0, 0)),
        ],
        out_specs=pl.BlockSpec((tm, D_MODEL), lambda i: (i, 0)),
        out_shape=jax.ShapeDtypeStruct((m, D_MODEL), F32),
        compiler_params=_params("parallel"),
        name="wo_proj",
    )(y, o, w)


def _wo_comb_call(y, os_, ls_, w, *, tm):
    m = y.shape[0]
    c = w.shape[0]
    blk = pl.BlockSpec((c // V7X_LANES, tm, V7X_LANES), lambda i: (0, i, 0))
    return pl.pallas_call(
        _wo_comb_kernel,
        grid=(m // tm,),
        in_specs=[pl.BlockSpec((tm, D_MODEL), lambda i: (i, 0))] + [blk] * 6
                 + [pl.BlockSpec((c, D_MODEL), lambda i: (0, 0))],
        out_specs=pl.BlockSpec((tm, D_MODEL), lambda i: (i, 0)),
        out_shape=jax.ShapeDtypeStruct((m, D_MODEL), F32),
        compiler_params=_params("parallel"),
        name="wo_comb_proj",
    )(y, *os_, *ls_, w)


def _ffn_tail(gate, g1, g2, up, cw_ref, cb_ref, wd_ref):
    conv = cb_ref[...] + cw_ref[0:1, :] * g2 + cw_ref[1:2, :] * g1 + cw_ref[2:3, :] * gate
    h = conv * jax.nn.sigmoid(conv) * up
    return jnp.dot(h.astype(BF16), wd_ref[...], preferred_element_type=F32)


def _ffn_prompt_kernel(y_ref, g_ref, wg_ref, wu_ref, wd_ref, cw_ref, cb_ref,
                       out_ref, tail_ref, xn_ref, carry_ref, *, tiles_per_seq):
    m = pl.program_id(0)
    f = pl.program_id(1)

    @pl.when(f == 0)
    def _():
        x = y_ref[...]
        xn_ref[...] = _rms(x, g_ref[...]).astype(BF16)
        out_ref[...] = x

    @pl.when(m % tiles_per_seq == 0)
    def _():
        carry_ref[f] = jnp.zeros(carry_ref.shape[1:], F32)

    xn = xn_ref[...]
    gate = jnp.dot(xn, wg_ref[...], preferred_element_type=F32)
    up = jnp.dot(xn, wu_ref[...], preferred_element_type=F32)
    tm = gate.shape[0]
    c = carry_ref[f]
    row = lax.broadcasted_iota(jnp.int32, gate.shape, 0)
    g1 = jnp.where(row == 0, c[7:8, :], pltpu.roll(gate, 1, 0))
    g2 = jnp.where(row == 0, c[6:7, :], jnp.where(row == 1, c[7:8, :], pltpu.roll(gate, 2, 0)))
    out_ref[...] += _ffn_tail(gate, g1, g2, up, cw_ref, cb_ref, wd_ref)
    last = gate[tm - V7X_SUBLANES:, :]
    carry_ref[f] = last
    tail_ref[0] = last


def _ffn_prompt_call(y, g, wg, wu, wd, cw, cb, *, seq):
    m = y.shape[0]
    tm, tf = TM_FFN, TF_FFN
    nf = D_FF // tf
    assert m % tm == 0 and seq % tm == 0 and D_FF % tf == 0
    return pl.pallas_call(
        functools.partial(_ffn_prompt_kernel, tiles_per_seq=seq // tm),
        grid=(m // tm, nf),
        in_specs=[
            pl.BlockSpec((tm, D_MODEL), lambda i, f: (i, 0)),
            pl.BlockSpec((1, D_MODEL), lambda i, f: (0, 0)),
            pl.BlockSpec((D_MODEL, tf), lambda i, f: (0, f)),
            pl.BlockSpec((D_MODEL, tf), lambda i, f: (0, f)),
            pl.BlockSpec((tf, D_MODEL), lambda i, f: (f, 0)),
            pl.BlockSpec((CONV_W, tf), lambda i, f: (0, f)),
            pl.BlockSpec((1, tf), lambda i, f: (0, f)),
        ],
        out_specs=[
            pl.BlockSpec((tm, D_MODEL), lambda i, f: (i, 0)),
            pl.BlockSpec((1, V7X_SUBLANES, tf), lambda i, f: (i, 0, f)),
        ],
        out_shape=[
            jax.ShapeDtypeStruct((m, D_MODEL), F32),
            jax.ShapeDtypeStruct((m // tm, V7X_SUBLANES, D_FF), F32),
        ],
        scratch_shapes=[pltpu.VMEM((tm, D_MODEL), BF16),
                        pltpu.VMEM((nf, V7X_SUBLANES, tf), F32)],
        compiler_params=_params("arbitrary", "arbitrary"),
        name="conv_ffn_prompt",
    )(y, g, wg, wu, wd, cw, cb)


def _ffn_sample_kernel(y_ref, g_ref, wg_ref, wu_ref, wd_ref, cw_ref, cb_ref, s0_ref, s1_ref,
                       out_ref, gate_ref, xn_ref, *, seq):
    f = pl.program_id(0)

    @pl.when(f == 0)
    def _():
        x = y_ref[...]
        xn_ref[...] = _rms(x, g_ref[...]).astype(BF16)
        out_ref[...] = x

    xn = xn_ref[...]
    gate = jnp.dot(xn, wg_ref[...], preferred_element_type=F32)
    up = jnp.dot(xn, wu_ref[...], preferred_element_type=F32)
    t = lax.broadcasted_iota(jnp.int32, gate.shape, 0) & (seq - 1)
    s0, s1 = s0_ref[...], s1_ref[...]
    g1 = jnp.where(t == 0, s1, pltpu.roll(gate, 1, 0))
    g2 = jnp.where(t == 0, s0, jnp.where(t == 1, s1, pltpu.roll(gate, 2, 0)))
    out_ref[...] += _ffn_tail(gate, g1, g2, up, cw_ref, cb_ref, wd_ref)
    gate_ref[...] = gate


def _ffn_sample_call(y, g, wg, wu, wd, cw, cb, s0, s1, *, seq):
    m = y.shape[0]
    tf = TF_FFN
    nf = D_FF // tf
    full = pl.BlockSpec((m, D_MODEL), lambda f: (0, 0))
    col = pl.BlockSpec((m, tf), lambda f: (0, f))
    return pl.pallas_call(
        functools.partial(_ffn_sample_kernel, seq=seq),
        grid=(nf,),
        in_specs=[
            full,
            pl.BlockSpec((1, D_MODEL), lambda f: (0, 0)),
            pl.BlockSpec((D_MODEL, tf), lambda f: (0, f)),
            pl.BlockSpec((D_MODEL, tf), lambda f: (0, f)),
            pl.BlockSpec((tf, D_MODEL), lambda f: (f, 0)),
            pl.BlockSpec((CONV_W, tf), lambda f: (0, f)),
            pl.BlockSpec((1, tf), lambda f: (0, f)),
            col, col,
        ],
        out_specs=[full, col],
        out_shape=[jax.ShapeDtypeStruct((m, D_MODEL), F32),
                   jax.ShapeDtypeStruct((m, D_FF), F32)],
        scratch_shapes=[pltpu.VMEM((m, D_MODEL), BF16)],
        compiler_params=_params("arbitrary"),
        name="conv_ffn_sample",
    )(y, g, wg, wu, wd, cw, cb, s0, s1)


def _block_diag_q(q_ref, col0, hkv, grp, s):
    blocks = []
    for h in range(hkv):
        qh = jnp.concatenate(
            [q_ref[:, col0 + HEAD_DIM * (grp * h + j):col0 + HEAD_DIM * (grp * h + j + 1)]
             for j in range(grp)], axis=0)
        pieces = []
        if h > 0:
            pieces.append(jnp.zeros((grp * s, HEAD_DIM * h), F32))
        pieces.append(qh)
        if h < hkv - 1:
            pieces.append(jnp.zeros((grp * s, HEAD_DIM * (hkv - 1 - h)), F32))
        blocks.append(jnp.concatenate(pieces, axis=1) if len(pieces) > 1 else qh)
    return jnp.concatenate(blocks, axis=0).astype(BF16)


def _new_rows_t(x, s):
    pad = jnp.zeros((V7X_LANES - s, x.shape[1]), F32)
    return jnp.concatenate([pad, x], axis=0).T


def _cached_attend(qbd, ck_t, kn_t, cv_t, vn_t, d, s, sink_col):
    r_, lc = qbd.shape[0], ck_t.shape[1]
    sc = jnp.dot(qbd, ck_t.astype(BF16), preferred_element_type=F32)
    sn = jnp.dot(qbd, kn_t.astype(BF16), preferred_element_type=F32)
    ic = lax.broadcasted_iota(jnp.int32, (r_, lc), 0) & (s - 1)
    c = lax.broadcasted_iota(jnp.int32, (r_, lc), 1)
    i_n = lax.broadcasted_iota(jnp.int32, (r_, V7X_LANES), 0) & (s - 1)
    j = lax.broadcasted_iota(jnp.int32, (r_, V7X_LANES), 1) - (V7X_LANES - s)
    valid_c = c >= ic
    valid_n = (j >= 0) & (j <= i_n)
    if d > 1:
        valid_c = valid_c & ((c & (d - 1)) == (ic & (d - 1)))
        valid_n = valid_n & ((j & (d - 1)) == (i_n & (d - 1)))
    sc = jnp.where(valid_c, sc, NEG_INF)
    sn = jnp.where(valid_n, sn, NEG_INF)
    m = jnp.maximum(jnp.max(sc, axis=1, keepdims=True), jnp.max(sn, axis=1, keepdims=True))
    if sink_col is not None:
        m = jnp.maximum(m, sink_col)
    pc = jnp.exp(sc - m)
    pn = jnp.exp(sn - m)
    l = jnp.sum(pc, axis=1, keepdims=True) + jnp.sum(pn, axis=1, keepdims=True)
    if sink_col is not None:
        l = l + jnp.exp(sink_col - m)
    nt = (((1,), (1,)), ((), ()))
    o = (lax.dot_general(pc.astype(BF16), cv_t.astype(BF16), nt, preferred_element_type=F32)
         + lax.dot_general(pn.astype(BF16), vn_t.astype(BF16), nt, preferred_element_type=F32)) / l
    return o, m + jnp.log(l)


def _diag_heads(o, hkv, grp, s):
    pieces = []
    for h in range(hkv):
        for j in range(grp):
            r0 = (h * grp + j) * s
            pieces.append(o[r0:r0 + s, HEAD_DIM * h:HEAD_DIM * (h + 1)])
    return jnp.concatenate(pieces, axis=1)


def _store_shifted(out_ref, off, c_t, n_t, s):
    lc = c_t.shape[1]
    rolled = pltpu.roll(c_t, lc - s, 1)
    lane = lax.broadcasted_iota(jnp.int32, n_t.shape, 1)
    if lc > V7X_LANES:
        out_ref[0, :, off:off + lc - V7X_LANES] = rolled[:, :lc - V7X_LANES]
    out_ref[0, :, off + lc - V7X_LANES:off + lc] = jnp.where(
        lane < V7X_LANES - s, rolled[:, lc - V7X_LANES:], n_t)


def _sample_a_kernel(q_ref, kn_ref, vn_ref, ck_ref, cv_ref, sink_ref, o_ref, ko_ref, vo_ref, *, s):
    hkv, grp = A_KV_HEADS, A_Q_HEADS // A_KV_HEADS
    kn_t, vn_t = _new_rows_t(kn_ref[...], s), _new_rows_t(vn_ref[...], s)
    ck_t, cv_t = ck_ref[0], cv_ref[0]
    qbd = _block_diag_q(q_ref, 0, hkv, grp, s)
    sink_col = jnp.concatenate(
        [jnp.full((s, 1), sink_ref[hq], F32) for hq in range(hkv * grp)], axis=0)
    o, _ = _cached_attend(qbd, ck_t, kn_t, cv_t, vn_t, 1, s, sink_col)
    o_ref[...] = _diag_heads(o, hkv, grp, s)
    _store_shifted(ko_ref, 0, ck_t, kn_t, s)
    _store_shifted(vo_ref, 0, cv_t, vn_t, s)


def _sample_a_call(q, kn, vn, ck_t, cv_t, sinks, *, s):
    nb, c, lc = ck_t.shape
    row = lambda w: pl.BlockSpec((s, w), lambda b: (b, 0))
    cache = pl.BlockSpec((1, c, lc), lambda b: (b, 0, 0))
    return pl.pallas_call(
        functools.partial(_sample_a_kernel, s=s),
        grid=(nb,),
        in_specs=[row(q.shape[1]), row(c), row(c), cache, cache,
                  pl.BlockSpec(memory_space=pltpu.SMEM)],
        out_specs=[row(q.shape[1]), cache, cache],
        out_shape=[jax.ShapeDtypeStruct(q.shape, F32),
                   jax.ShapeDtypeStruct(ck_t.shape, F32),
                   jax.ShapeDtypeStruct(cv_t.shape, F32)],
        compiler_params=_params("parallel"),
        name="sample_attn_a",
    )(q, kn, vn, ck_t, cv_t, sinks)


def _sample_b_kernel(q_ref, kn_ref, vn_ref, ck_ref, cv_ref, o_ref, ko_ref, vo_ref, *, s):
    hkv, grp = B_KV_HEADS, B_Q_HEADS // B_KV_HEADS
    ckv = hkv * HEAD_DIM
    outs, lses = [], []
    off = 0
    for g, (w, d) in enumerate(B_PATTERNS):
        lc = w
        kn_t = _new_rows_t(kn_ref[:, ckv * g:ckv * (g + 1)], s)
        vn_t = _new_rows_t(vn_ref[:, ckv * g:ckv * (g + 1)], s)
        ck_t = ck_ref[0, :, off:off + lc]
        cv_t = cv_ref[0, :, off:off + lc]
        qbd = _block_diag_q(q_ref, B_Q_HEADS * HEAD_DIM * g, hkv, grp, s)
        o, lse = _cached_attend(qbd, ck_t, kn_t, cv_t, vn_t, d, s, None)
        outs.append(o)
        lses.append(lse)
        _store_shifted(ko_ref, off, ck_t, kn_t, s)
        _store_shifted(vo_ref, off, cv_t, vn_t, s)
        off += lc
    mx = jnp.maximum(jnp.maximum(lses[0], lses[1]), lses[2])
    es = [jnp.exp(l - mx) for l in lses]
    den = es[0] + es[1] + es[2]
    comb = (es[0] / den) * outs[0] + (es[1] / den) * outs[1] + (es[2] / den) * outs[2]
    o_ref[...] = _diag_heads(comb, hkv, grp, s)


def _sample_b_call(q, kn, vn, ck_t, cv_t, *, s):
    nb, c, lb = ck_t.shape
    row = lambda w: pl.BlockSpec((s, w), lambda b: (b, 0))
    cache = pl.BlockSpec((1, c, lb), lambda b: (b, 0, 0))
    co = B_Q_HEADS * HEAD_DIM
    return pl.pallas_call(
        functools.partial(_sample_b_kernel, s=s),
        grid=(nb,),
        in_specs=[row(q.shape[1]), row(kn.shape[1]), row(vn.shape[1]), cache, cache],
        out_specs=[row(co), cache, cache],
        out_shape=[jax.ShapeDtypeStruct((q.shape[0], co), F32),
                   jax.ShapeDtypeStruct(ck_t.shape, F32),
                   jax.ShapeDtypeStruct(cv_t.shape, F32)],
        compiler_params=_params("parallel"),
        name="sample_attn_b",
    )(q, kn, vn, ck_t, cv_t)


def _rope_tables(pos):
    inv_freq = ROPE_THETA ** (-jnp.arange(HALF, dtype=F32) / HALF)
    ang = pos.astype(F32)[:, None] * inv_freq[None, :]
    cos, sin = jnp.cos(ang), jnp.sin(ang)
    return jnp.tile(cos, (1, 4)), jnp.tile(jnp.concatenate([-sin, sin], axis=1), (1, 2))


def _cache_t(cache):
    b, l, h, dh = cache.shape
    return jnp.transpose(cache, (0, 2, 3, 1)).reshape(b, h * dh, l)


def _cache_from_t(x, h):
    b, c, l = x.shape
    return jnp.transpose(x.reshape(b, h, c // h, l), (0, 3, 1, 2))[None]


def kernel(x_prompt, x_sample, cache_a_k, cache_a_v, cache_b_k, cache_b_v, state_ffn_conv,
           attn_norm, ffn_norm, a_w_qkv, a_q_norm, a_k_norm, a_sinks, a_w_o,
           b_w_qkv, b_q_norm, b_k_norm, b_w_o,
           ffn_w_gate, ffn_w_up, ffn_conv_w, ffn_conv_b, ffn_w_down):
    nb, seq, _ = x_prompt.shape
    ns, dec, _ = x_sample.shape
    mp, ms = nb * seq, ns * dec

    cos_p, sin_p = _rope_tables(jnp.tile(jnp.arange(seq, dtype=jnp.int32), nb))
    cos_s, sin_s = _rope_tables(jnp.tile(PAST_LEN + jnp.arange(dec, dtype=jnp.int32), ns))
    idx = jnp.arange(V7X_MXU_DIM, dtype=jnp.int32) // HEAD_DIM
    ones_blk = (idx[:, None] == idx[None, :]).astype(BF16)

    yp = x_prompt.reshape(mp, D_MODEL)
    ys = x_sample.reshape(ms, D_MODEL)
    row = lambda a: a.reshape(1, -1)

    def head_gain(gv, n_heads_per_tile):
        return jnp.tile(gv, n_heads_per_tile).reshape(1, -1)

    wqkv = _cast_call(a_w_qkv, 0)
    wo = _cast_call(a_w_o, 0)
    tn = wqkv.shape[1] // N_QKV_TILES
    qg, kg = head_gain(a_q_norm[0], tn // HEAD_DIM), head_gain(a_k_norm[0], tn // HEAD_DIM)
    g_attn = row(attn_norm[0])
    grp_a = A_Q_HEADS // A_KV_HEADS

    q, k, v = _qkv_call(yp, g_attn, wqkv, cos_p, sin_p, qg, kg, ones_blk,
                        n_q=A_Q_HEADS, n_kv=A_KV_HEADS, tm=TM_PROJ, q_dtype=BF16)
    k3, v3 = k.reshape(nb, seq, -1), v.reshape(nb, seq, -1)
    o = _band_call(q.reshape(nb, seq, -1), k3, v3, a_sinks[0], hkv=A_KV_HEADS, grp=grp_a)
    yp = _wo_call(yp, o.reshape(mp, -1), wo, tm=TM_PROJ)
    keep = min(A_WINDOW, seq)
    ca = A_KV_HEADS * HEAD_DIM
    a_k_prompt = _cache_from_t(_keep_t_call(k3, [keep], ca), A_KV_HEADS)
    a_v_prompt = _cache_from_t(_keep_t_call(v3, [keep], ca), A_KV_HEADS)

    qs, ks, vs = _qkv_call(ys, g_attn, wqkv, cos_s, sin_s, qg, kg, ones_blk,
                           n_q=A_Q_HEADS, n_kv=A_KV_HEADS, tm=ms, q_dtype=F32)
    os_, ako, avo = _sample_a_call(qs, ks, vs, _cache_t(cache_a_k[0]), _cache_t(cache_a_v[0]),
                                   a_sinks[0], s=dec)
    ys = _wo_call(ys, os_, wo, tm=ms)
    a_k_sample = _cache_from_t(ako, A_KV_HEADS)
    a_v_sample = _cache_from_t(avo, A_KV_HEADS)

    conv_p, conv_s = [], []

    def ffn(layer, yp, ys):
        wg = _cast_call(ffn_w_gate, layer)
        wu = _cast_call(ffn_w_up, layer)
        wd = _cast_call(ffn_w_down, layer)
        g_ffn = row(ffn_norm[layer])
        cw, cb = ffn_conv_w[layer], row(ffn_conv_b[layer])
        yp, tail = _ffn_prompt_call(yp, g_ffn, wg, wu, wd, cw, cb, seq=seq)
        tiles = seq // TM_FFN
        conv_p.append(tail[tiles - 1::tiles, V7X_SUBLANES - (CONV_W - 1):, :])
        st = state_ffn_conv[layer]
        s0 = jnp.repeat(st[:, 0, :], dec, axis=0)
        s1 = jnp.repeat(st[:, 1, :], dec, axis=0)
        ys, gate_s = _ffn_sample_call(ys, g_ffn, wg, wu, wd, cw, cb, s0, s1, seq=dec)
        conv_s.append(gate_s.reshape(ns, dec, D_FF)[:, dec - (CONV_W - 1):, :])
        return yp, ys

    yp, ys = ffn(0, yp, ys)

    wqkv = _cast_call(b_w_qkv, 0)
    wo = _cast_call(b_w_o, 0)
    tn = wqkv.shape[1] // N_QKV_TILES
    qg, kg = head_gain(b_q_norm[0], tn // HEAD_DIM), head_gain(b_k_norm[0], tn // HEAD_DIM)
    g_attn = row(attn_norm[1])
    n_grp = len(B_PATTERNS)
    grp_b = B_Q_HEADS // B_KV_HEADS
    nqb, nkvb = n_grp * B_Q_HEADS, n_grp * B_KV_HEADS
    cb_ = B_KV_HEADS * HEAD_DIM

    q, k, v = _qkv_call(yp, g_attn, wqkv, cos_p, sin_p, qg, kg, ones_blk,
                        n_q=nqb, n_kv=nkvb, tm=TM_PROJ, q_dtype=F32, lane_major=True)
    outs, lses = [], []
    for g, (w, d) in enumerate(B_PATTERNS):
        assert w // d == BLOCK
        og, lg = _dil_call(q, k, v, nb=nb, hkv=B_KV_HEADS, grp=grp_b, d=d, group=g)
        outs.append(og)
        lses.append(lg)
    yp = _wo_comb_call(yp, outs, lses, wo, tm=TM_PROJ // 2)
    keeps = [min(w, seq) for w, _ in B_PATTERNS]
    k4 = k.reshape(k.shape[0], nb, seq, V7X_LANES)
    v4 = v.reshape(v.shape[0], nb, seq, V7X_LANES)
    b_k_prompt = _cache_from_t(_keep_t_call(k4, keeps, cb_, lane_major=True), B_KV_HEADS)
    b_v_prompt = _cache_from_t(_keep_t_call(v4, keeps, cb_, lane_major=True), B_KV_HEADS)

    qs, ks, vs = _qkv_call(ys, g_attn, wqkv, cos_s, sin_s, qg, kg, ones_blk,
                           n_q=nqb, n_kv=nkvb, tm=ms, q_dtype=F32)
    os_, bko, bvo = _sample_b_call(qs, ks, vs, _cache_t(cache_b_k[0]), _cache_t(cache_b_v[0]),
                                   s=dec)
    ys = _wo_call(ys, os_, wo, tm=ms)
    b_k_sample = _cache_from_t(bko, B_KV_HEADS)
    b_v_sample = _cache_from_t(bvo, B_KV_HEADS)

    yp, ys = ffn(1, yp, ys)

    return (yp.reshape(nb, seq, D_MODEL), ys.reshape(ns, dec, D_MODEL),
            a_k_prompt, a_v_prompt, a_k_sample, a_v_sample,
            b_k_prompt, b_v_prompt, b_k_sample, b_v_sample,
            jnp.stack(conv_p), jnp.stack(conv_s))
```

```python
import functools

import jax
import jax.numpy as jnp
from jax import lax
from jax.experimental import pallas as pl
from jax.experimental.pallas import tpu as pltpu

F32 = jnp.float32
BF16 = jnp.bfloat16

D_MODEL = 2048
HEAD_DIM = 64
HALF = HEAD_DIM // 2
ROPE_THETA = 10000.0
NORM_EPS = 1e-6
BLOCK = 128
PAST_LEN = 16384
A_WINDOW = 128
A_Q_HEADS = 32
A_KV_HEADS = 8
B_PATTERNS = ((128, 1), (512, 4), (2048, 16))
B_Q_HEADS = 16
B_KV_HEADS = 4
D_FF = 5632
CONV_W = 3
NEG_INF = -1e30

V7X_MXU_DIM = 256
V7X_LANES = 128
V7X_SUBLANES = 8
VMEM_LIMIT = 56 * 1024 * 1024

TM_PROJ = 512
TM_FFN = 512
TF_FFN = 512
TR_CAST = 256
N_QKV_TILES = 6
DIL_MAX_ROWS = 1024


def _params(*sem):
    return pltpu.CompilerParams(dimension_semantics=sem, vmem_limit_bytes=VMEM_LIMIT)


def _rms(x, g):
    ms = jnp.mean(x * x, axis=-1, keepdims=True)
    return x * lax.rsqrt(ms + NORM_EPS) * g


def _cast_kernel(w_ref, o_ref):
    o_ref[...] = w_ref[0].astype(BF16)


def _cast_call(w, layer):
    _, r, c = w.shape
    tr = TR_CAST
    assert r % tr == 0
    return pl.pallas_call(
        _cast_kernel,
        grid=(r // tr,),
        in_specs=[pl.BlockSpec((1, tr, c), lambda i: (layer, i, 0))],
        out_specs=pl.BlockSpec((tr, c), lambda i: (i, 0)),
        out_shape=jax.ShapeDtypeStruct((r, c), BF16),
        compiler_params=_params("parallel"),
        name="cast_bf16",
    )(w)


def _head_norm_rope(a, gain, ones_blk, cos, sin):
    tn = a.shape[1]
    x2 = a * a
    hi = x2.astype(BF16)
    lo = (x2 - hi.astype(F32)).astype(BF16)
    parts = []
    for c in range(tn // V7X_MXU_DIM):
        sl = slice(V7X_MXU_DIM * c, V7X_MXU_DIM * (c + 1))
        parts.append(jnp.dot(hi[:, sl], ones_blk, preferred_element_type=F32)
                     + jnp.dot(lo[:, sl], ones_blk, preferred_element_type=F32))
    ss = jnp.concatenate(parts, axis=1)
    y = a * lax.rsqrt(ss * (1.0 / HEAD_DIM) + NORM_EPS) * gain
    lane = lax.broadcasted_iota(jnp.int32, y.shape, 1)
    first_half = (lane & (HEAD_DIM - 1)) < HALF
    partner = jnp.where(first_half, pltpu.roll(y, tn - HALF, 1), pltpu.roll(y, HALF, 1))
    reps = tn // V7X_LANES
    return y * jnp.tile(cos, (1, reps)) + partner * jnp.tile(sin, (1, reps))


def _store_cols(ref, val, lane_major):
    if not lane_major:
        ref[...] = val.astype(ref.dtype)
        return
    for t in range(val.shape[1] // V7X_LANES):
        ref[t] = val[:, V7X_LANES * t:V7X_LANES * (t + 1)].astype(ref.dtype)


def _qkv_kernel(x_ref, g_ref, w_ref, cos_ref, sin_ref, qg_ref, kg_ref, ones_ref,
                q_ref, k_ref, v_ref, xn_ref, *, n_q_tiles, lane_major):
    n = pl.program_id(1)

    @pl.when(n == 0)
    def _():
        xn_ref[...] = _rms(x_ref[...], g_ref[...]).astype(BF16)

    acc = jnp.dot(xn_ref[...], w_ref[...], preferred_element_type=F32)

    @pl.when(n < n_q_tiles)
    def _():
        r = _head_norm_rope(acc, qg_ref[...], ones_ref[...], cos_ref[...], sin_ref[...])
        _store_cols(q_ref, r * (HEAD_DIM ** -0.5), lane_major)

    @pl.when(n == n_q_tiles)
    def _():
        _store_cols(k_ref, _head_norm_rope(acc, kg_ref[...], ones_ref[...], cos_ref[...],
                                           sin_ref[...]), lane_major)

    @pl.when(n == n_q_tiles + 1)
    def _():
        _store_cols(v_ref, acc, lane_major)


def _qkv_call(x, g, w, cos, sin, qg, kg, ones_blk, *, n_q, n_kv, tm, q_dtype, lane_major=False):
    m = x.shape[0]
    ncols = w.shape[1]
    tn = ncols // N_QKV_TILES
    n_q_tiles = (n_q * HEAD_DIM) // tn
    assert n_q_tiles * tn == n_q * HEAD_DIM and n_kv * HEAD_DIM == tn and tn % V7X_MXU_DIM == 0
    assert m % tm == 0
    last_q = n_q_tiles - 1
    if lane_major:
        lt = tn // V7X_LANES
        out_specs = [
            pl.BlockSpec((lt, tm, V7X_LANES), lambda i, n: (jnp.minimum(n, last_q), i, 0)),
            pl.BlockSpec((lt, tm, V7X_LANES), lambda i, n: (0, i, 0)),
            pl.BlockSpec((lt, tm, V7X_LANES), lambda i, n: (0, i, 0)),
        ]
        out_shape = [
            jax.ShapeDtypeStruct((n_q_tiles * lt, m, V7X_LANES), q_dtype),
            jax.ShapeDtypeStruct((lt, m, V7X_LANES), F32),
            jax.ShapeDtypeStruct((lt, m, V7X_LANES), F32),
        ]
    else:
        out_specs = [
            pl.BlockSpec((tm, tn), lambda i, n: (i, jnp.minimum(n, last_q))),
            pl.BlockSpec((tm, tn), lambda i, n: (i, 0)),
            pl.BlockSpec((tm, tn), lambda i, n: (i, 0)),
        ]
        out_shape = [
            jax.ShapeDtypeStruct((m, n_q * HEAD_DIM), q_dtype),
            jax.ShapeDtypeStruct((m, tn), F32),
            jax.ShapeDtypeStruct((m, tn), F32),
        ]
    return pl.pallas_call(
        functools.partial(_qkv_kernel, n_q_tiles=n_q_tiles, lane_major=lane_major),
        grid=(m // tm, N_QKV_TILES),
        in_specs=[
            pl.BlockSpec((tm, D_MODEL), lambda i, n: (i, 0)),
            pl.BlockSpec((1, D_MODEL), lambda i, n: (0, 0)),
            pl.BlockSpec((D_MODEL, tn), lambda i, n: (0, n)),
            pl.BlockSpec((tm, V7X_LANES), lambda i, n: (i, 0)),
            pl.BlockSpec((tm, V7X_LANES), lambda i, n: (i, 0)),
            pl.BlockSpec((1, tn), lambda i, n: (0, 0)),
            pl.BlockSpec((1, tn), lambda i, n: (0, 0)),
            pl.BlockSpec((V7X_MXU_DIM, V7X_MXU_DIM), lambda i, n: (0, 0)),
        ],
        out_specs=out_specs,
        out_shape=out_shape,
        scratch_shapes=[pltpu.VMEM((tm, D_MODEL), BF16)],
        compiler_params=_params("parallel", "arbitrary"),
        name="qkv_proj",
    )(x, g, w, cos, sin, qg, kg, ones_blk)


GRP = 4


def _band_mask_t(first):
    shape = (2 * BLOCK, GRP * BLOCK)
    kj = lax.broadcasted_iota(jnp.int32, shape, 0)
    qi = lax.broadcasted_iota(jnp.int32, shape, 1) & (BLOCK - 1)
    return (kj >= qi) & (kj <= qi + BLOCK) & ((kj >= BLOCK) | jnp.logical_not(first))


def _attend_kv_tile(kt, vt, q_tiles, mask, sinks, want_lse):
    nt = (((1,), (1,)), ((), ()))
    lo = lax.broadcasted_iota(jnp.int32, kt.shape, 1) < HEAD_DIM
    kt_sw = pltpu.roll(kt, HEAD_DIM, 1)
    k_forms = [(jnp.where(lo, kt, 0.0).astype(BF16), jnp.where(lo, 0.0, kt_sw).astype(BF16)),
               (jnp.where(lo, kt_sw, 0.0).astype(BF16), jnp.where(lo, 0.0, kt).astype(BF16))]
    vt_t = vt.T.astype(BF16)
    o_tiles, l_tiles = [], []
    for e in range(2):
        k_lo, k_hi = k_forms[e]
        qpair = jnp.concatenate([q_tiles[2 * e], q_tiles[2 * e + 1]], axis=0)
        s = jnp.concatenate([lax.dot_general(k_lo, qpair, nt, preferred_element_type=F32),
                             lax.dot_general(k_hi, qpair, nt, preferred_element_type=F32)], axis=1)
        s = jnp.where(mask, s, NEG_INF)
        m = jnp.max(s, axis=0, keepdims=True)
        if sinks is not None:
            sk = jnp.concatenate([jnp.full((1, BLOCK), sinks[GRP * e + j], F32)
                                  for j in (0, 2, 1, 3)], axis=1)
            m = jnp.maximum(m, sk)
        p = jnp.exp(s - m)
        l = jnp.sum(p, axis=0, keepdims=True)
        if sinks is not None:
            l = l + jnp.exp(sk - m)
        o_t = jnp.dot(vt_t[HEAD_DIM * e:HEAD_DIM * (e + 1)], p.astype(BF16),
                      preferred_element_type=F32) / l
        lse_t = jnp.broadcast_to(m + jnp.log(l), o_t.shape) if want_lse else None
        for u in range(2):
            c0, c1 = slice(BLOCK * u, BLOCK * (u + 1)), slice(BLOCK * (2 + u), BLOCK * (3 + u))
            o_tiles.append(jnp.concatenate([o_t[:, c0], o_t[:, c1]], axis=0).T)
            if want_lse:
                l_tiles.append(jnp.concatenate([lse_t[:, c0], lse_t[:, c1]], axis=0).T)
    return o_tiles, l_tiles


def _band_kernel(q_ref, kc_ref, kp_ref, vc_ref, vp_ref, sink_ref, o_ref, *, kv_tiles):
    mask = _band_mask_t(pl.program_id(1) == 0)
    for t in range(kv_tiles):
        ls = slice(V7X_LANES * t, V7X_LANES * (t + 1))
        kt = jnp.concatenate([kp_ref[0, :, ls], kc_ref[0, :, ls]], axis=0)
        vt = jnp.concatenate([vp_ref[0, :, ls], vc_ref[0, :, ls]], axis=0)
        q_tiles = [q_ref[0, :, V7X_LANES * (GRP * t + u):V7X_LANES * (GRP * t + u + 1)]
                   for u in range(GRP)]
        sinks = [sink_ref[2 * GRP * t + i] for i in range(2 * GRP)]
        o_tiles, _ = _attend_kv_tile(kt, vt, q_tiles, mask, sinks, False)
        for u in range(GRP):
            o_ref[0, :, V7X_LANES * (GRP * t + u):V7X_LANES * (GRP * t + u + 1)] = (
                o_tiles[u].astype(o_ref.dtype))


def _band_call(q, k, v, sinks, *, hkv, grp):
    b, seq, cq = q.shape
    ck = hkv * HEAD_DIM
    assert grp == GRP and hkv % 2 == 0
    cur = lambda bi, c: (bi, c, 0)
    prev = lambda bi, c: (bi, jnp.maximum(c - 1, 0), 0)
    return pl.pallas_call(
        functools.partial(_band_kernel, kv_tiles=hkv // 2),
        grid=(b, seq // BLOCK),
        in_specs=[
            pl.BlockSpec((1, BLOCK, cq), cur),
            pl.BlockSpec((1, BLOCK, ck), cur),
            pl.BlockSpec((1, BLOCK, ck), prev),
            pl.BlockSpec((1, BLOCK, ck), cur),
            pl.BlockSpec((1, BLOCK, ck), prev),
            pl.BlockSpec(memory_space=pltpu.SMEM),
        ],
        out_specs=pl.BlockSpec((1, BLOCK, cq), cur),
        out_shape=jax.ShapeDtypeStruct((b, seq, cq), BF16),
        compiler_params=_params("parallel", "arbitrary"),
        name="band_attn",
    )(q, k, k, v, v, sinks)


def _dil_kernel(q_ref, kc_ref, kp_ref, vc_ref, vp_ref, o_ref, lse_ref, *, kv_tiles, grp, d):
    mask = _band_mask_t(pl.program_id(1) == 0)

    def residue(r):
        rs = pl.ds(r, BLOCK, stride=d) if d > 1 else slice(None)
        for t in range(kv_tiles):
            kt = jnp.concatenate([kp_ref[t, rs, :], kc_ref[t, rs, :]], axis=0)
            vt = jnp.concatenate([vp_ref[t, rs, :], vc_ref[t, rs, :]], axis=0)
            q_tiles = [q_ref[grp * t + u, rs, :].astype(BF16) for u in range(grp)]
            o_tiles, l_tiles = _attend_kv_tile(kt, vt, q_tiles, mask, None, True)
            for u in range(grp):
                o_ref[grp * t + u, rs, :] = o_tiles[u]
                lse_ref[grp * t + u, rs, :] = l_tiles[u]

    if d > 1:
        def body(r, carry):
            residue(r)
            return carry
        lax.fori_loop(0, d, body, 0)
    else:
        residue(0)


def _dil_call(q, k, v, *, nb, hkv, grp, d, group):
    m = q.shape[1]
    seq = m // nb
    rows = BLOCK * d
    hsplit = max(1, rows // DIL_MAX_ROWS)
    kv_tiles = hkv // 2 // hsplit
    q_tiles = kv_tiles * grp
    nchunk = seq // rows
    assert seq % rows == 0 and kv_tiles * 2 * hsplit == hkv and grp % 2 == 0
    cur = lambda bi, c, hp: (group * hsplit + hp, bi * nchunk + c, 0)
    prev = lambda bi, c, hp: (group * hsplit + hp, bi * nchunk + jnp.maximum(c - 1, 0), 0)
    o_spec = pl.BlockSpec((q_tiles, rows, V7X_LANES), lambda bi, c, hp: (hp, bi * nchunk + c, 0))
    o_shape = jax.ShapeDtypeStruct((q_tiles * hsplit, m, V7X_LANES), F32)
    return pl.pallas_call(
        functools.partial(_dil_kernel, kv_tiles=kv_tiles, grp=grp, d=d),
        grid=(nb, nchunk, hsplit),
        in_specs=[
            pl.BlockSpec((q_tiles, rows, V7X_LANES), cur),
            pl.BlockSpec((kv_tiles, rows, V7X_LANES), cur),
            pl.BlockSpec((kv_tiles, rows, V7X_LANES), prev),
            pl.BlockSpec((kv_tiles, rows, V7X_LANES), cur),
            pl.BlockSpec((kv_tiles, rows, V7X_LANES), prev),
        ],
        out_specs=[o_spec, o_spec],
        out_shape=[o_shape, o_shape],
        compiler_params=_params("parallel", "arbitrary", "arbitrary"),
        name="dilated_attn",
    )(q, k, k, v, v)


def _keep_t_kernel(x_ref, o_ref):
    o_ref[0] = x_ref[0].T


def _keep_t_lane_major_kernel(x_ref, o_ref):
    o_ref[0] = x_ref[0, 0].T


def _keep_t_call(x, keeps, c, *, lane_major=False):
    b, seq = x.shape[1:3] if lane_major else x.shape[:2]
    starts, first_rb = [], []
    n = 0
    for keep in keeps:
        assert keep % BLOCK == 0 and seq % BLOCK == 0
        starts.append(n)
        first_rb.append((seq - keep) // BLOCK)
        n += keep // BLOCK

    def src_block(j):
        rb = jnp.int32(0)
        cb = jnp.int32(0)
        for g in range(len(keeps)):
            inside = j >= starts[g]
            rb = jnp.where(inside, first_rb[g] + j - starts[g], rb)
            cb = jnp.where(inside, g, cb)
        return rb, cb

    if lane_major:
        ct = c // V7X_LANES
        grid = (b, ct, n)

        def src(bi, t, j):
            rb, cb = src_block(j)
            return cb * ct + t, bi, rb, 0

        in_spec = pl.BlockSpec((1, 1, BLOCK, V7X_LANES), src)
        out_spec = pl.BlockSpec((1, V7X_LANES, BLOCK), lambda bi, t, j: (bi, t, j))
        kern = _keep_t_lane_major_kernel
        sem = ("parallel", "parallel", "parallel")
    else:
        grid = (b, n)

        def src(bi, j):
            rb, cb = src_block(j)
            return bi, rb, cb

        in_spec = pl.BlockSpec((1, BLOCK, c), src)
        out_spec = pl.BlockSpec((1, c, BLOCK), lambda bi, j: (bi, 0, j))
        kern = _keep_t_kernel
        sem = ("parallel", "parallel")
    return pl.pallas_call(
        kern,
        grid=grid,
        in_specs=[in_spec],
        out_specs=out_spec,
        out_shape=jax.ShapeDtypeStruct((b, c, n * BLOCK), F32),
        compiler_params=_params(*sem),
        name="keep_rows_t",
    )(x)


def _wo_kernel(y_ref, o_ref, w_ref, out_ref):
    out_ref[...] = y_ref[...] + jnp.dot(o_ref[...].astype(BF16), w_ref[...],
                                        preferred_element_type=F32)


def _wo_comb_kernel(y_ref, o0_ref, o1_ref, o2_ref, l0_ref, l1_ref, l2_ref, w_ref, out_ref):
    tiles = []
    for t in range(o0_ref.shape[0]):
        l0, l1, l2 = l0_ref[t], l1_ref[t], l2_ref[t]
        mx = jnp.maximum(jnp.maximum(l0, l1), l2)
        e0, e1, e2 = jnp.exp(l0 - mx), jnp.exp(l1 - mx), jnp.exp(l2 - mx)
        den = e0 + e1 + e2
        comb = (e0 / den) * o0_ref[t] + (e1 / den) * o1_ref[t] + (e2 / den) * o2_ref[t]
        tiles.append(comb.astype(BF16))
    comb = jnp.concatenate(tiles, axis=1)
    out_ref[...] = y_ref[...] + jnp.dot(comb, w_ref[...], preferred_element_type=F32)


def _wo_call(y, o, w, *, tm):
    m = y.shape[0]
    c = o.shape[1]
    return pl.pallas_call(
        _wo_kernel,
        grid=(m // tm,),
        in_specs=[
            pl.BlockSpec((tm, D_MODEL), lambda i: (i, 0)),
            pl.BlockSpec((tm, c), lambda i: (i, 0)),
            pl.BlockSpec((c, D_MODEL), lambda i: (0, 0)),
        ],
        out_specs=pl.BlockSpec((tm, D_MODEL), lambda i: (i, 0)),
        out_shape=jax.ShapeDtypeStruct((m, D_MODEL), F32),
        compiler_params=_params("parallel"),
        name="wo_proj",
    )(y, o, w)


def _wo_comb_call(y, os_, ls_, w, *, tm):
    m = y.shape[0]
    c = w.shape[0]
    blk = pl.BlockSpec((c // V7X_LANES, tm, V7X_LANES), lambda i: (0, i, 0))
    return pl.pallas_call(
        _wo_comb_kernel,
        grid=(m // tm,),
        in_specs=[pl.BlockSpec((tm, D_MODEL), lambda i: (i, 0))] + [blk] * 6
                 + [pl.BlockSpec((c, D_MODEL), lambda i: (0, 0))],
        out_specs=pl.BlockSpec((tm, D_MODEL), lambda i: (i, 0)),
        out_shape=jax.ShapeDtypeStruct((m, D_MODEL), F32),
        compiler_params=_params("parallel"),
        name="wo_comb_proj",
    )(y, *os_, *ls_, w)


def _ffn_tail(gate, g1, g2, up, cw_ref, cb_ref, wd_ref):
    conv = cb_ref[...] + cw_ref[0:1, :] * g2 + cw_ref[1:2, :] * g1 + cw_ref[2:3, :] * gate
    h = conv * jax.nn.sigmoid(conv) * up
    return jnp.dot(h.astype(BF16), wd_ref[...], preferred_element_type=F32)


def _ffn_prompt_kernel(y_ref, g_ref, wg_ref, wu_ref, wd_ref, cw_ref, cb_ref,
                       out_ref, tail_ref, xn_ref, carry_ref, *, tiles_per_seq):
    m = pl.program_id(0)
    f = pl.program_id(1)

    @pl.when(f == 0)
    def _():
        x = y_ref[...]
        xn_ref[...] = _rms(x, g_ref[...]).astype(BF16)
        out_ref[...] = x

    @pl.when(m % tiles_per_seq == 0)
    def _():
        carry_ref[f] = jnp.zeros(carry_ref.shape[1:], F32)

    xn = xn_ref[...]
    gate = jnp.dot(xn, wg_ref[...], preferred_element_type=F32)
    up = jnp.dot(xn, wu_ref[...], preferred_element_type=F32)
    tm = gate.shape[0]
    c = carry_ref[f]
    row = lax.broadcasted_iota(jnp.int32, gate.shape, 0)
    g1 = jnp.where(row == 0, c[7:8, :], pltpu.roll(gate, 1, 0))
    g2 = jnp.where(row == 0, c[6:7, :], jnp.where(row == 1, c[7:8, :], pltpu.roll(gate, 2, 0)))
    out_ref[...] += _ffn_tail(gate, g1, g2, up, cw_ref, cb_ref, wd_ref)
    last = gate[tm - V7X_SUBLANES:, :]
    carry_ref[f] = last
    tail_ref[0] = last


def _ffn_prompt_call(y, g, wg, wu, wd, cw, cb, *, seq):
    m = y.shape[0]
    tm, tf = TM_FFN, TF_FFN
    nf = D_FF // tf
    assert m % tm == 0 and seq % tm == 0 and D_FF % tf == 0
    return pl.pallas_call(
        functools.partial(_ffn_prompt_kernel, tiles_per_seq=seq // tm),
        grid=(m // tm, nf),
        in_specs=[
            pl.BlockSpec((tm, D_MODEL), lambda i, f: (i, 0)),
            pl.BlockSpec((1, D_MODEL), lambda i, f: (0, 0)),
            pl.BlockSpec((D_MODEL, tf), lambda i, f: (0, f)),
            pl.BlockSpec((D_MODEL, tf), lambda i, f: (0, f)),
            pl.BlockSpec((tf, D_MODEL), lambda i, f: (f, 0)),
            pl.BlockSpec((CONV_W, tf), lambda i, f: (0, f)),
            pl.BlockSpec((1, tf), lambda i, f: (0, f)),
        ],
        out_specs=[
            pl.BlockSpec((tm, D_MODEL), lambda i, f: (i, 0)),
            pl.BlockSpec((1, V7X_SUBLANES, tf), lambda i, f: (i, 0, f)),
        ],
        out_shape=[
            jax.ShapeDtypeStruct((m, D_MODEL), F32),
            jax.ShapeDtypeStruct((m // tm, V7X_SUBLANES, D_FF), F32),
        ],
        scratch_shapes=[pltpu.VMEM((tm, D_MODEL), BF16),
                        pltpu.VMEM((nf, V7X_SUBLANES, tf), F32)],
        compiler_params=_params("arbitrary", "arbitrary"),
        name="conv_ffn_prompt",
    )(y, g, wg, wu, wd, cw, cb)


def _ffn_sample_kernel(y_ref, g_ref, wg_ref, wu_ref, wd_ref, cw_ref, cb_ref, s0_ref, s1_ref,
                       out_ref, gate_ref, xn_ref, *, seq):
    f = pl.program_id(0)

    @pl.when(f == 0)
    def _():
        x = y_ref[...]
        xn_ref[...] = _rms(x, g_ref[...]).astype(BF16)
        out_ref[...] = x

    xn = xn_ref[...]
    gate = jnp.dot(xn, wg_ref[...], preferred_element_type=F32)
    up = jnp.dot(xn, wu_ref[...], preferred_element_type=F32)
    t = lax.broadcasted_iota(jnp.int32, gate.shape, 0) & (seq - 1)
    s0, s1 = s0_ref[...], s1_ref[...]
    g1 = jnp.where(t == 0, s1, pltpu.roll(gate, 1, 0))
    g2 = jnp.where(t == 0, s0, jnp.where(t == 1, s1, pltpu.roll(gate, 2, 0)))
    out_ref[...] += _ffn_tail(gate, g1, g2, up, cw_ref, cb_ref, wd_ref)
    gate_ref[...] = gate


def _ffn_sample_call(y, g, wg, wu, wd, cw, cb, s0, s1, *, seq):
    m = y.shape[0]
    tf = TF_FFN
    nf = D_FF // tf
    full = pl.BlockSpec((m, D_MODEL), lambda f: (0, 0))
    col = pl.BlockSpec((m, tf), lambda f: (0, f))
    return pl.pallas_call(
        functools.partial(_ffn_sample_kernel, seq=seq),
        grid=(nf,),
        in_specs=[
            full,
            pl.BlockSpec((1, D_MODEL), lambda f: (0, 0)),
            pl.BlockSpec((D_MODEL, tf), lambda f: (0, f)),
            pl.BlockSpec((D_MODEL, tf), lambda f: (0, f)),
            pl.BlockSpec((tf, D_MODEL), lambda f: (f, 0)),
            pl.BlockSpec((CONV_W, tf), lambda f: (0, f)),
            pl.BlockSpec((1, tf), lambda f: (0, f)),
            col, col,
        ],
        out_specs=[full, col],
        out_shape=[jax.ShapeDtypeStruct((m, D_MODEL), F32),
                   jax.ShapeDtypeStruct((m, D_FF), F32)],
        scratch_shapes=[pltpu.VMEM((m, D_MODEL), BF16)],
        compiler_params=_params("arbitrary"),
        name="conv_ffn_sample",
    )(y, g, wg, wu, wd, cw, cb, s0, s1)


def _block_diag_q(q_ref, col0, hkv, grp, s):
    blocks = []
    for h in range(hkv):
        qh = jnp.concatenate(
            [q_ref[:, col0 + HEAD_DIM * (grp * h + j):col0 + HEAD_DIM * (grp * h + j + 1)]
             for j in range(grp)], axis=0)
        pieces = []
        if h > 0:
            pieces.append(jnp.zeros((grp * s, HEAD_DIM * h), F32))
        pieces.append(qh)
        if h < hkv - 1:
            pieces.append(jnp.zeros((grp * s, HEAD_DIM * (hkv - 1 - h)), F32))
        blocks.append(jnp.concatenate(pieces, axis=1) if len(pieces) > 1 else qh)
    return jnp.concatenate(blocks, axis=0).astype(BF16)


def _new_rows_t(x, s):
    pad = jnp.zeros((V7X_LANES - s, x.shape[1]), F32)
    return jnp.concatenate([pad, x], axis=0).T


def _cached_attend(qbd, ck_t, kn_t, cv_t, vn_t, d, s, sink_col):
    r_, lc = qbd.shape[0], ck_t.shape[1]
    sc = jnp.dot(qbd, ck_t.astype(BF16), preferred_element_type=F32)
    sn = jnp.dot(qbd, kn_t.astype(BF16), preferred_element_type=F32)
    ic = lax.broadcasted_iota(jnp.int32, (r_, lc), 0) & (s - 1)
    c = lax.broadcasted_iota(jnp.int32, (r_, lc), 1)
    i_n = lax.broadcasted_iota(jnp.int32, (r_, V7X_LANES), 0) & (s - 1)
    j = lax.broadcasted_iota(jnp.int32, (r_, V7X_LANES), 1) - (V7X_LANES - s)
    valid_c = c >= ic
    valid_n = (j >= 0) & (j <= i_n)
    if d > 1:
        valid_c = valid_c & ((c & (d - 1)) == (ic & (d - 1)))
        valid_n = valid_n & ((j & (d - 1)) == (i_n & (d - 1)))
    sc = jnp.where(valid_c, sc, NEG_INF)
    sn = jnp.where(valid_n, sn, NEG_INF)
    m = jnp.maximum(jnp.max(sc, axis=1, keepdims=True), jnp.max(sn, axis=1, keepdims=True))
    if sink_col is not None:
        m = jnp.maximum(m, sink_col)
    pc = jnp.exp(sc - m)
    pn = jnp.exp(sn - m)
    l = jnp.sum(pc, axis=1, keepdims=True) + jnp.sum(pn, axis=1, keepdims=True)
    if sink_col is not None:
        l = l + jnp.exp(sink_col - m)
    nt = (((1,), (1,)), ((), ()))
    o = (lax.dot_general(pc.astype(BF16), cv_t.astype(BF16), nt, preferred_element_type=F32)
         + lax.dot_general(pn.astype(BF16), vn_t.astype(BF16), nt, preferred_element_type=F32)) / l
    return o, m + jnp.log(l)


def _diag_heads(o, hkv, grp, s):
    pieces = []
    for h in range(hkv):
        for j in range(grp):
            r0 = (h * grp + j) * s
            pieces.append(o[r0:r0 + s, HEAD_DIM * h:HEAD_DIM * (h + 1)])
    return jnp.concatenate(pieces, axis=1)


def _store_shifted(out_ref, off, c_t, n_t, s):
    lc = c_t.shape[1]
    rolled = pltpu.roll(c_t, lc - s, 1)
    lane = lax.broadcasted_iota(jnp.int32, n_t.shape, 1)
    if lc > V7X_LANES:
        out_ref[0, :, off:off + lc - V7X_LANES] = rolled[:, :lc - V7X_LANES]
    out_ref[0, :, off + lc - V7X_LANES:off + lc] = jnp.where(
        lane < V7X_LANES - s, rolled[:, lc - V7X_LANES:], n_t)


def _sample_a_kernel(q_ref, kn_ref, vn_ref, ck_ref, cv_ref, sink_ref, o_ref, ko_ref, vo_ref, *, s):
    hkv, grp = A_KV_HEADS, A_Q_HEADS // A_KV_HEADS
    kn_t, vn_t = _new_rows_t(kn_ref[...], s), _new_rows_t(vn_ref[...], s)
    ck_t, cv_t = ck_ref[0], cv_ref[0]
    qbd = _block_diag_q(q_ref, 0, hkv, grp, s)
    sink_col = jnp.concatenate(
        [jnp.full((s, 1), sink_ref[hq], F32) for hq in range(hkv * grp)], axis=0)
    o, _ = _cached_attend(qbd, ck_t, kn_t, cv_t, vn_t, 1, s, sink_col)
    o_ref[...] = _diag_heads(o, hkv, grp, s)
    _store_shifted(ko_ref, 0, ck_t, kn_t, s)
    _store_shifted(vo_ref, 0, cv_t, vn_t, s)


def _sample_a_call(q, kn, vn, ck_t, cv_t, sinks, *, s):
    nb, c, lc = ck_t.shape
    row = lambda w: pl.BlockSpec((s, w), lambda b: (b, 0))
    cache = pl.BlockSpec((1, c, lc), lambda b: (b, 0, 0))
    return pl.pallas_call(
        functools.partial(_sample_a_kernel, s=s),
        grid=(nb,),
        in_specs=[row(q.shape[1]), row(c), row(c), cache, cache,
                  pl.BlockSpec(memory_space=pltpu.SMEM)],
        out_specs=[row(q.shape[1]), cache, cache],
        out_shape=[jax.ShapeDtypeStruct(q.shape, F32),
                   jax.ShapeDtypeStruct(ck_t.shape, F32),
                   jax.ShapeDtypeStruct(cv_t.shape, F32)],
        compiler_params=_params("parallel"),
        name="sample_attn_a",
    )(q, kn, vn, ck_t, cv_t, sinks)


def _sample_b_kernel(q_ref, kn_ref, vn_ref, ck_ref, cv_ref, o_ref, ko_ref, vo_ref, *, s):
    hkv, grp = B_KV_HEADS, B_Q_HEADS // B_KV_HEADS
    ckv = hkv * HEAD_DIM
    outs, lses = [], []
    off = 0
    for g, (w, d) in enumerate(B_PATTERNS):
        lc = w
        kn_t = _new_rows_t(kn_ref[:, ckv * g:ckv * (g + 1)], s)
        vn_t = _new_rows_t(vn_ref[:, ckv * g:ckv * (g + 1)], s)
        ck_t = ck_ref[0, :, off:off + lc]
        cv_t = cv_ref[0, :, off:off + lc]
        qbd = _block_diag_q(q_ref, B_Q_HEADS * HEAD_DIM * g, hkv, grp, s)
        o, lse = _cached_attend(qbd, ck_t, kn_t, cv_t, vn_t, d, s, None)
        outs.append(o)
        lses.append(lse)
        _store_shifted(ko_ref, off, ck_t, kn_t, s)
        _store_shifted(vo_ref, off, cv_t, vn_t, s)
        off += lc
    mx = jnp.maximum(jnp.maximum(lses[0], lses[1]), lses[2])
    es = [jnp.exp(l - mx) for l in lses]
    den = es[0] + es[1] + es[2]
    comb = (es[0] / den) * outs[0] + (es[1] / den) * outs[1] + (es[2] / den) * outs[2]
    o_ref[...] = _diag_heads(comb, hkv, grp, s)


def _sample_b_call(q, kn, vn, ck_t, cv_t, *, s):
    nb, c, lb = ck_t.shape
    row = lambda w: pl.BlockSpec((s, w), lambda b: (b, 0))
    cache = pl.BlockSpec((1, c, lb), lambda b: (b, 0, 0))
    co = B_Q_HEADS * HEAD_DIM
    return pl.pallas_call(
        functools.partial(_sample_b_kernel, s=s),
        grid=(nb,),
        in_specs=[row(q.shape[1]), row(kn.shape[1]), row(vn.shape[1]), cache, cache],
        out_specs=[row(co), cache, cache],
        out_shape=[jax.ShapeDtypeStruct((q.shape[0], co), F32),
                   jax.ShapeDtypeStruct(ck_t.shape, F32),
                   jax.ShapeDtypeStruct(cv_t.shape, F32)],
        compiler_params=_params("parallel"),
        name="sample_attn_b",
    )(q, kn, vn, ck_t, cv_t)


def _rope_tables(pos):
    inv_freq = ROPE_THETA ** (-jnp.arange(HALF, dtype=F32) / HALF)
    ang = pos.astype(F32)[:, None] * inv_freq[None, :]
    cos, sin = jnp.cos(ang), jnp.sin(ang)
    return jnp.tile(cos, (1, 4)), jnp.tile(jnp.concatenate([-sin, sin], axis=1), (1, 2))


def _cache_t(cache):
    b, l, h, dh = cache.shape
    return jnp.transpose(cache, (0, 2, 3, 1)).reshape(b, h * dh, l)


def _cache_from_t(x, h):
    b, c, l = x.shape
    return jnp.transpose(x.reshape(b, h, c // h, l), (0, 3, 1, 2))[None]


def kernel(x_prompt, x_sample, cache_a_k, cache_a_v, cache_b_k, cache_b_v, state_ffn_conv,
           attn_norm, ffn_norm, a_w_qkv, a_q_norm, a_k_norm, a_sinks, a_w_o,
           b_w_qkv, b_q_norm, b_k_norm, b_w_o,
           ffn_w_gate, ffn_w_up, ffn_conv_w, ffn_conv_b, ffn_w_down):
    nb, seq, _ = x_prompt.shape
    ns, dec, _ = x_sample.shape
    mp, ms = nb * seq, ns * dec

    cos_p, sin_p = _rope_tables(jnp.tile(jnp.arange(seq, dtype=jnp.int32), nb))
    cos_s, sin_s = _rope_tables(jnp.tile(PAST_LEN + jnp.arange(dec, dtype=jnp.int32), ns))
    idx = jnp.arange(V7X_MXU_DIM, dtype=jnp.int32) // HEAD_DIM
    ones_blk = (idx[:, None] == idx[None, :]).astype(BF16)

    yp = x_prompt.reshape(mp, D_MODEL)
    ys = x_sample.reshape(ms, D_MODEL)
    row = lambda a: a.reshape(1, -1)

    def head_gain(gv, n_heads_per_tile):
        return jnp.tile(gv, n_heads_per_tile).reshape(1, -1)

    wqkv = _cast_call(a_w_qkv, 0)
    wo = _cast_call(a_w_o, 0)
    tn = wqkv.shape[1] // N_QKV_TILES
    qg, kg = head_gain(a_q_norm[0], tn // HEAD_DIM), head_gain(a_k_norm[0], tn // HEAD_DIM)
    g_attn = row(attn_norm[0])
    grp_a = A_Q_HEADS // A_KV_HEADS

    q, k, v = _qkv_call(yp, g_attn, wqkv, cos_p, sin_p, qg, kg, ones_blk,
                        n_q=A_Q_HEADS, n_kv=A_KV_HEADS, tm=TM_PROJ, q_dtype=BF16)
    k3, v3 = k.reshape(nb, seq, -1), v.reshape(nb, seq, -1)
    o = _band_call(q.reshape(nb, seq, -1), k3, v3, a_sinks[0], hkv=A_KV_HEADS, grp=grp_a)
    yp = _wo_call(yp, o.reshape(mp, -1), wo, tm=TM_PROJ)
    keep = min(A_WINDOW, seq)
    ca = A_KV_HEADS * HEAD_DIM
    a_k_prompt = _cache_from_t(_keep_t_call(k3, [keep], ca), A_KV_HEADS)
    a_v_prompt = _cache_from_t(_keep_t_call(v3, [keep], ca), A_KV_HEADS)

    qs, ks, vs = _qkv_call(ys, g_attn, wqkv, cos_s, sin_s, qg, kg, ones_blk,
                           n_q=A_Q_HEADS, n_kv=A_KV_HEADS, tm=ms, q_dtype=F32)
    os_, ako, avo = _sample_a_call(qs, ks, vs, _cache_t(cache_a_k[0]), _cache_t(cache_a_v[0]),
                                   a_sinks[0], s=dec)
    ys = _wo_call(ys, os_, wo, tm=ms)
    a_k_sample = _cache_from_t(ako, A_KV_HEADS)
    a_v_sample = _cache_from_t(avo, A_KV_HEADS)

    conv_p, conv_s = [], []

    def ffn(layer, yp, ys):
        wg = _cast_call(ffn_w_gate, layer)
        wu = _cast_call(ffn_w_up, layer)
        wd = _cast_call(ffn_w_down, layer)
        g_ffn = row(ffn_norm[layer])
        cw, cb = ffn_conv_w[layer], row(ffn_conv_b[layer])
        yp, tail = _ffn_prompt_call(yp, g_ffn, wg, wu, wd, cw, cb, seq=seq)
        tiles = seq // TM_FFN
        conv_p.append(tail[tiles - 1::tiles, V7X_SUBLANES - (CONV_W - 1):, :])
        st = state_ffn_conv[layer]
        s0 = jnp.repeat(st[:, 0, :], dec, axis=0)
        s1 = jnp.repeat(st[:, 1, :], dec, axis=0)
        ys, gate_s = _ffn_sample_call(ys, g_ffn, wg, wu, wd, cw, cb, s0, s1, seq=dec)
        conv_s.append(gate_s.reshape(ns, dec, D_FF)[:, dec - (CONV_W - 1):, :])
        return yp, ys

    yp, ys = ffn(0, yp, ys)

    wqkv = _cast_call(b_w_qkv, 0)
    wo = _cast_call(b_w_o, 0)
    tn = wqkv.shape[1] // N_QKV_TILES
    qg, kg = head_gain(b_q_norm[0], tn // HEAD_DIM), head_gain(b_k_norm[0], tn // HEAD_DIM)
    g_attn = row(attn_norm[1])
    n_grp = len(B_PATTERNS)
    grp_b = B_Q_HEADS // B_KV_HEADS
    nqb, nkvb = n_grp * B_Q_HEADS, n_grp * B_KV_HEADS
    cb_ = B_KV_HEADS * HEAD_DIM

    q, k, v = _qkv_call(yp, g_attn, wqkv, cos_p, sin_p, qg, kg, ones_blk,
                        n_q=nqb, n_kv=nkvb, tm=TM_PROJ, q_dtype=F32, lane_major=True)
    outs, lses = [], []
    for g, (w, d) in enumerate(B_PATTERNS):
        assert w // d == BLOCK
        og, lg = _dil_call(q, k, v, nb=nb, hkv=B_KV_HEADS, grp=grp_b, d=d, group=g)
        outs.append(og)
        lses.append(lg)
    yp = _wo_comb_call(yp, outs, lses, wo, tm=TM_PROJ // 2)
    keeps = [min(w, seq) for w, _ in B_PATTERNS]
    k4 = k.reshape(k.shape[0], nb, seq, V7X_LANES)
    v4 = v.reshape(v.shape[0], nb, seq, V7X_LANES)
    b_k_prompt = _cache_from_t(_keep_t_call(k4, keeps, cb_, lane_major=True), B_KV_HEADS)
    b_v_prompt = _cache_from_t(_keep_t_call(v4, keeps, cb_, lane_major=True), B_KV_HEADS)

    qs, ks, vs = _qkv_call(ys, g_attn, wqkv, cos_s, sin_s, qg, kg, ones_blk,
                           n_q=nqb, n_kv=nkvb, tm=ms, q_dtype=F32)
    os_, bko, bvo = _sample_b_call(qs, ks, vs, _cache_t(cache_b_k[0]), _cache_t(cache_b_v[0]),
                                   s=dec)
    ys = _wo_call(ys, os_, wo, tm=ms)
    b_k_sample = _cache_from_t(bko, B_KV_HEADS)
    b_v_sample = _cache_from_t(bvo, B_KV_HEADS)

    yp, ys = ffn(1, yp, ys)

    return (yp.reshape(nb, seq, D_MODEL), ys.reshape(ns, dec, D_MODEL),
            a_k_prompt, a_v_prompt, a_k_sample, a_v_sample,
            b_k_prompt, b_v_prompt, b_k_sample, b_v_sample,
            jnp.stack(conv_p), jnp.stack(conv_s))
```

```python
import functools

import jax
import jax.numpy as jnp
from jax import lax
from jax.experimental import pallas as pl
from jax.experimental.pallas import tpu as pltpu

F32 = jnp.float32
BF16 = jnp.bfloat16

D_MODEL = 2048
HEAD_DIM = 64
HALF = HEAD_DIM // 2
ROPE_THETA = 10000.0
NORM_EPS = 1e-6
BLOCK = 128
PAST_LEN = 16384
A_WINDOW = 128
A_Q_HEADS = 32
A_KV_HEADS = 8
B_PATTERNS = ((128, 1), (512, 4), (2048, 16))
B_Q_HEADS = 16
B_KV_HEADS = 4
D_FF = 5632
CONV_W = 3
NEG_INF = -1e30
LOG2E = 1.4426950408889634
LN2 = 0.6931471805599453
Q_SCALE = HEAD_DIM ** -0.5 * LOG2E

V7X_MXU_DIM = 256
V7X_LANES = 128
V7X_SUBLANES = 8
VMEM_LIMIT = 56 * 1024 * 1024

TM_PROJ = 512
TM_FFN = 512
TF_FFN = 512
TR_CAST = 256
PROJ_CHUNK_ROWS = 256
FFN_ROW_CHUNKS = 1
N_QKV_TILES = 6
DIL_MAX_ROWS = 1024


def _params(*sem):
    return pltpu.CompilerParams(dimension_semantics=sem, vmem_limit_bytes=VMEM_LIMIT)


def _rms(x, g):
    ms = jnp.mean(x * x, axis=-1, keepdims=True)
    return x * lax.rsqrt(ms + NORM_EPS) * g


def _cast_kernel(w_ref, o_ref):
    o_ref[...] = w_ref[0].astype(BF16)


def _cast_call(w, layer):
    _, r, c = w.shape
    tr = TR_CAST
    assert r % tr == 0
    return pl.pallas_call(
        _cast_kernel,
        grid=(r // tr,),
        in_specs=[pl.BlockSpec((1, tr, c), lambda i: (layer, i, 0))],
        out_specs=pl.BlockSpec((tr, c), lambda i: (i, 0)),
        out_shape=jax.ShapeDtypeStruct((r, c), BF16),
        compiler_params=_params("parallel"),
        name="cast_bf16",
    )(w)


def _head_norm_rope(a, gain, ones_blk, cos, sin):
    tn = a.shape[1]
    x2 = a * a
    hi = x2.astype(BF16)
    lo = (x2 - hi.astype(F32)).astype(BF16)
    parts = []
    for c in range(tn // V7X_MXU_DIM):
        sl = slice(V7X_MXU_DIM * c, V7X_MXU_DIM * (c + 1))
        parts.append(jnp.dot(hi[:, sl], ones_blk, preferred_element_type=F32)
                     + jnp.dot(lo[:, sl], ones_blk, preferred_element_type=F32))
    ss = jnp.concatenate(parts, axis=1)
    y = a * lax.rsqrt(ss * (1.0 / HEAD_DIM) + NORM_EPS) * gain
    lane = lax.broadcasted_iota(jnp.int32, y.shape, 1)
    first_half = (lane & (HEAD_DIM - 1)) < HALF
    partner = jnp.where(first_half, pltpu.roll(y, tn - HALF, 1), pltpu.roll(y, HALF, 1))
    reps = tn // V7X_LANES
    return y * jnp.tile(cos, (1, reps)) + partner * jnp.tile(sin, (1, reps))


def _store_cols(ref, rows, val, lane_major):
    if not lane_major:
        ref[rows, :] = val.astype(ref.dtype)
        return
    for t in range(val.shape[1] // V7X_LANES):
        ref[t, rows, :] = val[:, V7X_LANES * t:V7X_LANES * (t + 1)].astype(ref.dtype)


def _qkv_kernel(x_ref, g_ref, w_ref, cos_ref, sin_ref, qg_ref, kg_ref, ones_ref,
                q_ref, k_ref, v_ref, xn_ref, *, n_q_tiles, lane_major, row_chunks):
    n = pl.program_id(1)

    @pl.when(n == 0)
    def _():
        xn_ref[...] = _rms(x_ref[...], g_ref[...]).astype(BF16)

    rc = xn_ref.shape[0] // row_chunks

    def project(out_ref, epilogue):
        for c in range(row_chunks):
            rows = slice(rc * c, rc * (c + 1))
            acc = jnp.dot(xn_ref[rows, :], w_ref[...], preferred_element_type=F32)
            _store_cols(out_ref, rows, epilogue(acc, rows), lane_major)

    def norm_rope(gain_ref, scale):
        def fn(acc, rows):
            r = _head_norm_rope(acc, gain_ref[...], ones_ref[...], cos_ref[rows, :],
                                sin_ref[rows, :])
            return r * scale if scale != 1.0 else r
        return fn

    @pl.when(n < n_q_tiles)
    def _():
        project(q_ref, norm_rope(qg_ref, Q_SCALE))

    @pl.when(n == n_q_tiles)
    def _():
        project(k_ref, norm_rope(kg_ref, 1.0))

    @pl.when(n == n_q_tiles + 1)
    def _():
        project(v_ref, lambda acc, rows: acc)


def _qkv_call(x, g, w, cos, sin, qg, kg, ones_blk, *, n_q, n_kv, tm, q_dtype, lane_major=False):
    m = x.shape[0]
    ncols = w.shape[1]
    tn = ncols // N_QKV_TILES
    n_q_tiles = (n_q * HEAD_DIM) // tn
    assert n_q_tiles * tn == n_q * HEAD_DIM and n_kv * HEAD_DIM == tn and tn % V7X_MXU_DIM == 0
    assert m % tm == 0
    last_q = n_q_tiles - 1
    if lane_major:
        lt = tn // V7X_LANES
        out_specs = [
            pl.BlockSpec((lt, tm, V7X_LANES), lambda i, n: (jnp.minimum(n, last_q), i, 0)),
            pl.BlockSpec((lt, tm, V7X_LANES), lambda i, n: (0, i, 0)),
            pl.BlockSpec((lt, tm, V7X_LANES), lambda i, n: (0, i, 0)),
        ]
        out_shape = [
            jax.ShapeDtypeStruct((n_q_tiles * lt, m, V7X_LANES), q_dtype),
            jax.ShapeDtypeStruct((lt, m, V7X_LANES), F32),
            jax.ShapeDtypeStruct((lt, m, V7X_LANES), F32),
        ]
    else:
        out_specs = [
            pl.BlockSpec((tm, tn), lambda i, n: (i, jnp.minimum(n, last_q))),
            pl.BlockSpec((tm, tn), lambda i, n: (i, 0)),
            pl.BlockSpec((tm, tn), lambda i, n: (i, 0)),
        ]
        out_shape = [
            jax.ShapeDtypeStruct((m, n_q * HEAD_DIM), q_dtype),
            jax.ShapeDtypeStruct((m, tn), F32),
            jax.ShapeDtypeStruct((m, tn), F32),
        ]
    return pl.pallas_call(
        functools.partial(_qkv_kernel, n_q_tiles=n_q_tiles, lane_major=lane_major,
                          row_chunks=max(1, tm // PROJ_CHUNK_ROWS)),
        grid=(m // tm, N_QKV_TILES),
        in_specs=[
            pl.BlockSpec((tm, D_MODEL), lambda i, n: (i, 0)),
            pl.BlockSpec((1, D_MODEL), lambda i, n: (0, 0)),
            pl.BlockSpec((D_MODEL, tn), lambda i, n: (0, n)),
            pl.BlockSpec((tm, V7X_LANES), lambda i, n: (i, 0)),
            pl.BlockSpec((tm, V7X_LANES), lambda i, n: (i, 0)),
            pl.BlockSpec((1, tn), lambda i, n: (0, 0)),
            pl.BlockSpec((1, tn), lambda i, n: (0, 0)),
            pl.BlockSpec((V7X_MXU_DIM, V7X_MXU_DIM), lambda i, n: (0, 0)),
        ],
        out_specs=out_specs,
        out_shape=out_shape,
        scratch_shapes=[pltpu.VMEM((tm, D_MODEL), BF16)],
        compiler_params=_params("parallel", "arbitrary"),
        name="qkv_proj",
    )(x, g, w, cos, sin, qg, kg, ones_blk)


GRP = 4


def _band_mask_t(first):
    shape = (2 * BLOCK, BLOCK)
    kj = lax.broadcasted_iota(jnp.int32, shape, 0)
    qi = lax.broadcasted_iota(jnp.int32, shape, 1)
    seen = (kj >= qi) & (kj <= qi + BLOCK) & ((kj >= BLOCK) | jnp.logical_not(first))
    return jnp.where(seen, 0.0, NEG_INF)


def _attend_kv_tile(kt, vt, q_tiles, mask, sinks, want_lse):
    nt = (((1,), (1,)), ((), ()))
    lo = lax.broadcasted_iota(jnp.int32, kt.shape, 1) < HEAD_DIM
    kt_sw = pltpu.roll(kt, HEAD_DIM, 1)
    k_forms = [(jnp.where(lo, kt, 0.0).astype(BF16), jnp.where(lo, 0.0, kt_sw).astype(BF16)),
               (jnp.where(lo, kt_sw, 0.0).astype(BF16), jnp.where(lo, 0.0, kt).astype(BF16))]
    vt_t = vt.T.astype(BF16)
    mask4 = jnp.tile(mask, (1, GRP))
    o_tiles, l_tiles = [], []
    for e in range(2):
        k_lo, k_hi = k_forms[e]
        qpair = jnp.concatenate([q_tiles[2 * e], q_tiles[2 * e + 1]], axis=0)
        s = jnp.concatenate([lax.dot_general(k_lo, qpair, nt, preferred_element_type=F32),
                             lax.dot_general(k_hi, qpair, nt, preferred_element_type=F32)], axis=1)
        s = s + mask4
        m = jnp.max(s, axis=0, keepdims=True)
        if sinks is not None:
            sk = jnp.concatenate([jnp.full((1, BLOCK), sinks[GRP * e + j] * LOG2E, F32)
                                  for j in (0, 2, 1, 3)], axis=1)
            m = jnp.maximum(m, sk)
        p = jnp.exp2(s - m)
        l = jnp.sum(p, axis=0, keepdims=True)
        if sinks is not None:
            l = l + jnp.exp2(sk - m)
        o_t = jnp.dot(vt_t[HEAD_DIM * e:HEAD_DIM * (e + 1)], p.astype(BF16),
                      preferred_element_type=F32) / l
        lse_t = (jnp.broadcast_to((m + jnp.log2(l)) * LN2, o_t.shape) if want_lse else None)
        for u in range(2):
            c0, c1 = slice(BLOCK * u, BLOCK * (u + 1)), slice(BLOCK * (2 + u), BLOCK * (3 + u))
            o_tiles.append(jnp.concatenate([o_t[:, c0], o_t[:, c1]], axis=0).T)
            if want_lse:
                l_tiles.append(jnp.concatenate([lse_t[:, c0], lse_t[:, c1]], axis=0).T)
    return o_tiles, l_tiles


def _band_kernel(q_ref, kc_ref, kp_ref, vc_ref, vp_ref, sink_ref, o_ref, *, kv_tiles):
    mask = _band_mask_t(pl.program_id(1) == 0)
    for t in range(kv_tiles):
        ls = slice(V7X_LANES * t, V7X_LANES * (t + 1))
        kt = jnp.concatenate([kp_ref[0, :, ls], kc_ref[0, :, ls]], axis=0)
        vt = jnp.concatenate([vp_ref[0, :, ls], vc_ref[0, :, ls]], axis=0)
        q_tiles = [q_ref[0, :, V7X_LANES * (GRP * t + u):V7X_LANES * (GRP * t + u + 1)]
                   for u in range(GRP)]
        sinks = [sink_ref[2 * GRP * t + i] for i in range(2 * GRP)]
        o_tiles, _ = _attend_kv_tile(kt, vt, q_tiles, mask, sinks, False)
        for u in range(GRP):
            o_ref[0, :, V7X_LANES * (GRP * t + u):V7X_LANES * (GRP * t + u + 1)] = (
                o_tiles[u].astype(o_ref.dtype))


def _band_call(q, k, v, sinks, *, hkv, grp):
    b, seq, cq = q.shape
    ck = hkv * HEAD_DIM
    assert grp == GRP and hkv % 2 == 0
    cur = lambda bi, c: (bi, c, 0)
    prev = lambda bi, c: (bi, jnp.maximum(c - 1, 0), 0)
    return pl.pallas_call(
        functools.partial(_band_kernel, kv_tiles=hkv // 2),
        grid=(b, seq // BLOCK),
        in_specs=[
            pl.BlockSpec((1, BLOCK, cq), cur),
            pl.BlockSpec((1, BLOCK, ck), cur),
            pl.BlockSpec((1, BLOCK, ck), prev),
            pl.BlockSpec((1, BLOCK, ck), cur),
            pl.BlockSpec((1, BLOCK, ck), prev),
            pl.BlockSpec(memory_space=pltpu.SMEM),
        ],
        out_specs=pl.BlockSpec((1, BLOCK, cq), cur),
        out_shape=jax.ShapeDtypeStruct((b, seq, cq), BF16),
        compiler_params=_params("parallel", "arbitrary"),
        name="band_attn",
    )(q, k, k, v, v, sinks)


def _dil_kernel(q_ref, kc_ref, kp_ref, vc_ref, vp_ref, o_ref, lse_ref, *, kv_tiles, grp, d):
    mask = _band_mask_t(pl.program_id(1) == 0)

    def residue(r):
        rs = pl.ds(r, BLOCK, stride=d) if d > 1 else slice(None)
        for t in range(kv_tiles):
            kt = jnp.concatenate([kp_ref[t, rs, :], kc_ref[t, rs, :]], axis=0)
            vt = jnp.concatenate([vp_ref[t, rs, :], vc_ref[t, rs, :]], axis=0)
            q_tiles = [q_ref[grp * t + u, rs, :].astype(BF16) for u in range(grp)]
            o_tiles, l_tiles = _attend_kv_tile(kt, vt, q_tiles, mask, None, True)
            for u in range(grp):
                o_ref[grp * t + u, rs, :] = o_tiles[u]
                lse_ref[grp * t + u, rs, :] = l_tiles[u]

    if d > 1:
        def body(r, carry):
            residue(r)
            return carry
        lax.fori_loop(0, d, body, 0)
    else:
        residue(0)


def _dil_call(q, k, v, *, nb, hkv, grp, d, group):
    m = q.shape[1]
    seq = m // nb
    rows = BLOCK * d
    hsplit = max(1, rows // DIL_MAX_ROWS)
    kv_tiles = hkv // 2 // hsplit
    q_tiles = kv_tiles * grp
    nchunk = seq // rows
    assert seq % rows == 0 and kv_tiles * 2 * hsplit == hkv and grp % 2 == 0
    cur = lambda bi, c, hp: (group * hsplit + hp, bi * nchunk + c, 0)
    prev = lambda bi, c, hp: (group * hsplit + hp, bi * nchunk + jnp.maximum(c - 1, 0), 0)
    o_spec = pl.BlockSpec((q_tiles, rows, V7X_LANES), lambda bi, c, hp: (hp, bi * nchunk + c, 0))
    o_shape = jax.ShapeDtypeStruct((q_tiles * hsplit, m, V7X_LANES), F32)
    return pl.pallas_call(
        functools.partial(_dil_kernel, kv_tiles=kv_tiles, grp=grp, d=d),
        grid=(nb, nchunk, hsplit),
        in_specs=[
            pl.BlockSpec((q_tiles, rows, V7X_LANES), cur),
            pl.BlockSpec((kv_tiles, rows, V7X_LANES), cur),
            pl.BlockSpec((kv_tiles, rows, V7X_LANES), prev),
            pl.BlockSpec((kv_tiles, rows, V7X_LANES), cur),
            pl.BlockSpec((kv_tiles, rows, V7X_LANES), prev),
        ],
        out_specs=[o_spec, o_spec],
        out_shape=[o_shape, o_shape],
        compiler_params=_params("parallel", "arbitrary", "arbitrary"),
        name="dilated_attn",
    )(q, k, k, v, v)


def _keep_t_kernel(x_ref, o_ref):
    o_ref[0] = x_ref[0].T


def _keep_t_lane_major_kernel(x_ref, o_ref, *, keeps):
    g = pl.program_id(2)
    seq = x_ref.shape[2]
    off = 0
    for k, keep in enumerate(keeps):
        @pl.when(g == k)
        def _(keep=keep, off=off):
            o_ref[0, :, off:off + keep] = x_ref[0, 0, seq - keep:, :].T
        off += keep


def _keep_t_call(x, keeps, c, *, lane_major=False):
    b, seq = x.shape[1:3] if lane_major else x.shape[:2]
    starts, first_rb = [], []
    n = 0
    for keep in keeps:
        assert keep % BLOCK == 0 and seq % BLOCK == 0
        starts.append(n)
        first_rb.append((seq - keep) // BLOCK)
        n += keep // BLOCK

    def src_block(j):
        rb = jnp.int32(0)
        cb = jnp.int32(0)
        for g in range(len(keeps)):
            inside = j >= starts[g]
            rb = jnp.where(inside, first_rb[g] + j - starts[g], rb)
            cb = jnp.where(inside, g, cb)
        return rb, cb

    if lane_major:
        ct = c // V7X_LANES
        grid = (b, ct, len(keeps))
        in_spec = pl.BlockSpec((1, 1, seq, V7X_LANES), lambda bi, t, g: (g * ct + t, bi, 0, 0))
        out_spec = pl.BlockSpec((1, V7X_LANES, n * BLOCK), lambda bi, t, g: (bi, t, 0))
        kern = functools.partial(_keep_t_lane_major_kernel, keeps=tuple(keeps))
        sem = ("parallel", "parallel", "arbitrary")
    else:
        grid = (b, n)

        def src(bi, j):
            rb, cb = src_block(j)
            return bi, rb, cb

        in_spec = pl.BlockSpec((1, BLOCK, c), src)
        out_spec = pl.BlockSpec((1, c, BLOCK), lambda bi, j: (bi, 0, j))
        kern = _keep_t_kernel
        sem = ("parallel", "parallel")
    return pl.pallas_call(
        kern,
        grid=grid,
        in_specs=[in_spec],
        out_specs=out_spec,
        out_shape=jax.ShapeDtypeStruct((b, c, n * BLOCK), F32),
        compiler_params=_params(*sem),
        name="keep_rows_t",
    )(x)


def _wo_kernel(y_ref, o_ref, w_ref, out_ref):
    out_ref[...] = y_ref[...] + jnp.dot(o_ref[...].astype(BF16), w_ref[...],
                                        preferred_element_type=F32)


def _wo_comb_kernel(y_ref, o0_ref, o1_ref, o2_ref, l0_ref, l1_ref, l2_ref, w_ref, out_ref):
    tiles = []
    for t in range(o0_ref.shape[0]):
        l0, l1, l2 = l0_ref[t], l1_ref[t], l2_ref[t]
        mx = jnp.maximum(jnp.maximum(l0, l1), l2)
        e0, e1, e2 = jnp.exp(l0 - mx), jnp.exp(l1 - mx), jnp.exp(l2 - mx)
        den = e0 + e1 + e2
        comb = (e0 / den) * o0_ref[t] + (e1 / den) * o1_ref[t] + (e2 / den) * o2_ref[t]
        tiles.append(comb.astype(BF16))
    comb = jnp.concatenate(tiles, axis=1)
    out_ref[...] = y_ref[...] + jnp.dot(comb, w_ref[...], preferred_element_type=F32)


def _wo_call(y, o, w, *, tm):
    m = y.shape[0]
    c = o.shape[1]
    return pl.pallas_call(
        _wo_kernel,
        grid=(m // tm,),
        in_specs=[
            pl.BlockSpec((tm, D_MODEL), lambda i: (i, 0)),
            pl.BlockSpec((tm, c), lambda i: (i, 0)),
            pl.BlockSpec((c, D_MODEL), lambda i: (0, 0)),
        ],
        out_specs=pl.BlockSpec((tm, D_MODEL), lambda i: (i, 0)),
        out_shape=jax.ShapeDtypeStruct((m, D_MODEL), F32),
        compiler_params=_params("parallel"),
        name="wo_proj",
    )(y, o, w)


def _wo_comb_call(y, os_, ls_, w, *, tm):
    m = y.shape[0]
    c = w.shape[0]
    blk = pl.BlockSpec((c // V7X_LANES, tm, V7X_LANES), lambda i: (0, i, 0))
    return pl.pallas_call(
        _wo_comb_kernel,
        grid=(m // tm,),
        in_specs=[pl.BlockSpec((tm, D_MODEL), lambda i: (i, 0))] + [blk] * 6
                 + [pl.BlockSpec((c, D_MODEL), lambda i: (0, 0))],
        out_specs=pl.BlockSpec((tm, D_MODEL), lambda i: (i, 0)),
        out_shape=jax.ShapeDtypeStruct((m, D_MODEL), F32),
        compiler_params=_params("parallel"),
        name="wo_comb_proj",
    )(y, *os_, *ls_, w)


def _ffn_tail(gate, g1, g2, up, cw_ref, cb_ref, wd_ref):
    conv = cb_ref[...] + cw_ref[0:1, :] * g2 + cw_ref[1:2, :] * g1 + cw_ref[2:3, :] * gate
    h = conv * jax.nn.sigmoid(conv) * up
    return jnp.dot(h.astype(BF16), wd_ref[...], preferred_element_type=F32)


def _ffn_prompt_kernel(y_ref, g_ref, wg_ref, wu_ref, wd_ref, cw_ref, cb_ref,
                       out_ref, tail_ref, xn_ref, carry_ref, *, tiles_per_seq):
    m = pl.program_id(0)
    f = pl.program_id(1)

    @pl.when(f == 0)
    def _():
        x = y_ref[...]
        xn_ref[...] = _rms(x, g_ref[...]).astype(BF16)
        out_ref[...] = x

    @pl.when(m % tiles_per_seq == 0)
    def _():
        carry_ref[f] = jnp.zeros(carry_ref.shape[1:], F32)

    rc = xn_ref.shape[0] // FFN_ROW_CHUNKS
    last = carry_ref[f]
    for c in range(FFN_ROW_CHUNKS):
        rows = slice(rc * c, rc * (c + 1))
        xn = xn_ref[rows, :]
        gate = jnp.dot(xn, wg_ref[...], preferred_element_type=F32)
        up = jnp.dot(xn, wu_ref[...], preferred_element_type=F32)
        row = lax.broadcasted_iota(jnp.int32, gate.shape, 0)
        g1 = jnp.where(row == 0, last[7:8, :], pltpu.roll(gate, 1, 0))
        g2 = jnp.where(row == 0, last[6:7, :],
                       jnp.where(row == 1, last[7:8, :], pltpu.roll(gate, 2, 0)))
        out_ref[rows, :] += _ffn_tail(gate, g1, g2, up, cw_ref, cb_ref, wd_ref)
        last = gate[rc - V7X_SUBLANES:, :]
    carry_ref[f] = last
    tail_ref[0] = last


def _ffn_prompt_call(y, g, wg, wu, wd, cw, cb, *, seq):
    m = y.shape[0]
    tm, tf = TM_FFN, TF_FFN
    nf = D_FF // tf
    assert m % tm == 0 and seq % tm == 0 and D_FF % tf == 0
    return pl.pallas_call(
        functools.partial(_ffn_prompt_kernel, tiles_per_seq=seq // tm),
        grid=(m // tm, nf),
        in_specs=[
            pl.BlockSpec((tm, D_MODEL), lambda i, f: (i, 0)),
            pl.BlockSpec((1, D_MODEL), lambda i, f: (0, 0)),
            pl.BlockSpec((D_MODEL, tf), lambda i, f: (0, f)),
            pl.BlockSpec((D_MODEL, tf), lambda i, f: (0, f)),
            pl.BlockSpec((tf, D_MODEL), lambda i, f: (f, 0)),
            pl.BlockSpec((CONV_W, tf), lambda i, f: (0, f)),
            pl.BlockSpec((1, tf), lambda i, f: (0, f)),
        ],
        out_specs=[
            pl.BlockSpec((tm, D_MODEL), lambda i, f: (i, 0)),
            pl.BlockSpec((1, V7X_SUBLANES, tf), lambda i, f: (i, 0, f)),
        ],
        out_shape=[
            jax.ShapeDtypeStruct((m, D_MODEL), F32),
            jax.ShapeDtypeStruct((m // tm, V7X_SUBLANES, D_FF), F32),
        ],
        scratch_shapes=[pltpu.VMEM((tm, D_MODEL), BF16),
                        pltpu.VMEM((nf, V7X_SUBLANES, tf), F32)],
        compiler_params=_params("arbitrary", "arbitrary"),
        name="conv_ffn_prompt",
    )(y, g, wg, wu, wd, cw, cb)


def _ffn_sample_kernel(y_ref, g_ref, wg_ref, wu_ref, wd_ref, cw_ref, cb_ref, s0_ref, s1_ref,
                       out_ref, gate_ref, xn_ref, *, seq):
    f = pl.program_id(0)

    @pl.when(f == 0)
    def _():
        x = y_ref[...]
        xn_ref[...] = _rms(x, g_ref[...]).astype(BF16)
        out_ref[...] = x

    xn = xn_ref[...]
    gate = jnp.dot(xn, wg_ref[...], preferred_element_type=F32)
    up = jnp.dot(xn, wu_ref[...], preferred_element_type=F32)
    t = lax.broadcasted_iota(jnp.int32, gate.shape, 0) & (seq - 1)
    s0, s1 = s0_ref[...], s1_ref[...]
    g1 = jnp.where(t == 0, s1, pltpu.roll(gate, 1, 0))
    g2 = jnp.where(t == 0, s0, jnp.where(t == 1, s1, pltpu.roll(gate, 2, 0)))
    out_ref[...] += _ffn_tail(gate, g1, g2, up, cw_ref, cb_ref, wd_ref)
    gate_ref[...] = gate


def _ffn_sample_call(y, g, wg, wu, wd, cw, cb, s0, s1, *, seq):
    m = y.shape[0]
    tf = TF_FFN
    nf = D_FF // tf
    full = pl.BlockSpec((m, D_MODEL), lambda f: (0, 0))
    col = pl.BlockSpec((m, tf), lambda f: (0, f))
    return pl.pallas_call(
        functools.partial(_ffn_sample_kernel, seq=seq),
        grid=(nf,),
        in_specs=[
            full,
            pl.BlockSpec((1, D_MODEL), lambda f: (0, 0)),
            pl.BlockSpec((D_MODEL, tf), lambda f: (0, f)),
            pl.BlockSpec((D_MODEL, tf), lambda f: (0, f)),
            pl.BlockSpec((tf, D_MODEL), lambda f: (f, 0)),
            pl.BlockSpec((CONV_W, tf), lambda f: (0, f)),
            pl.BlockSpec((1, tf), lambda f: (0, f)),
            col, col,
        ],
        out_specs=[full, col],
        out_shape=[jax.ShapeDtypeStruct((m, D_MODEL), F32),
                   jax.ShapeDtypeStruct((m, D_FF), F32)],
        scratch_shapes=[pltpu.VMEM((m, D_MODEL), BF16)],
        compiler_params=_params("arbitrary"),
        name="conv_ffn_sample",
    )(y, g, wg, wu, wd, cw, cb, s0, s1)


def _block_diag_q(q_ref, col0, hkv, grp, s):
    blocks = []
    for h in range(hkv):
        qh = jnp.concatenate(
            [q_ref[:, col0 + HEAD_DIM * (grp * h + j):col0 + HEAD_DIM * (grp * h + j + 1)]
             for j in range(grp)], axis=0)
        pieces = []
        if h > 0:
            pieces.append(jnp.zeros((grp * s, HEAD_DIM * h), F32))
        pieces.append(qh)
        if h < hkv - 1:
            pieces.append(jnp.zeros((grp * s, HEAD_DIM * (hkv - 1 - h)), F32))
        blocks.append(jnp.concatenate(pieces, axis=1) if len(pieces) > 1 else qh)
    return jnp.concatenate(blocks, axis=0).astype(BF16)


def _new_rows_t(x, s):
    pad = jnp.zeros((V7X_LANES - s, x.shape[1]), F32)
    return jnp.concatenate([pad, x], axis=0).T


def _cached_attend(qbd, ck_t, kn_t, cv_t, vn_t, d, s, sink_col):
    r_, lc = qbd.shape[0], ck_t.shape[1]
    sc = jnp.dot(qbd, ck_t.astype(BF16), preferred_element_type=F32)
    sn = jnp.dot(qbd, kn_t.astype(BF16), preferred_element_type=F32)
    ic = lax.broadcasted_iota(jnp.int32, (r_, lc), 0) & (s - 1)
    c = lax.broadcasted_iota(jnp.int32, (r_, lc), 1)
    i_n = lax.broadcasted_iota(jnp.int32, (r_, V7X_LANES), 0) & (s - 1)
    j = lax.broadcasted_iota(jnp.int32, (r_, V7X_LANES), 1) - (V7X_LANES - s)
    valid_c = c >= ic
    valid_n = (j >= 0) & (j <= i_n)
    if d > 1:
        valid_c = valid_c & ((c & (d - 1)) == (ic & (d - 1)))
        valid_n = valid_n & ((j & (d - 1)) == (i_n & (d - 1)))
    sc = jnp.where(valid_c, sc, NEG_INF)
    sn = jnp.where(valid_n, sn, NEG_INF)
    m = jnp.maximum(jnp.max(sc, axis=1, keepdims=True), jnp.max(sn, axis=1, keepdims=True))
    if sink_col is not None:
        m = jnp.maximum(m, sink_col)
    pc = jnp.exp2(sc - m)
    pn = jnp.exp2(sn - m)
    l = jnp.sum(pc, axis=1, keepdims=True) + jnp.sum(pn, axis=1, keepdims=True)
    if sink_col is not None:
        l = l + jnp.exp2(sink_col - m)
    nt = (((1,), (1,)), ((), ()))
    o = (lax.dot_general(pc.astype(BF16), cv_t.astype(BF16), nt, preferred_element_type=F32)
         + lax.dot_general(pn.astype(BF16), vn_t.astype(BF16), nt, preferred_element_type=F32)) / l
    return o, (m + jnp.log2(l)) * LN2


def _diag_heads(o, hkv, grp, s):
    pieces = []
    for h in range(hkv):
        for j in range(grp):
            r0 = (h * grp + j) * s
            pieces.append(o[r0:r0 + s, HEAD_DIM * h:HEAD_DIM * (h + 1)])
    return jnp.concatenate(pieces, axis=1)


def _store_shifted(out_ref, off, c_t, n_t, s):
    lc = c_t.shape[1]
    rolled = pltpu.roll(c_t, lc - s, 1)
    lane = lax.broadcasted_iota(jnp.int32, n_t.shape, 1)
    if lc > V7X_LANES:
        out_ref[0, :, off:off + lc - V7X_LANES] = rolled[:, :lc - V7X_LANES]
    out_ref[0, :, off + lc - V7X_LANES:off + lc] = jnp.where(
        lane < V7X_LANES - s, rolled[:, lc - V7X_LANES:], n_t)


def _sample_a_kernel(q_ref, kn_ref, vn_ref, ck_ref, cv_ref, sink_ref, o_ref, ko_ref, vo_ref, *, s):
    hkv, grp = A_KV_HEADS, A_Q_HEADS // A_KV_HEADS
    kn_t, vn_t = _new_rows_t(kn_ref[...], s), _new_rows_t(vn_ref[...], s)
    ck_t, cv_t = ck_ref[0], cv_ref[0]
    qbd = _block_diag_q(q_ref, 0, hkv, grp, s)
    sink_col = jnp.concatenate(
        [jnp.full((s, 1), sink_ref[hq] * LOG2E, F32) for hq in range(hkv * grp)], axis=0)
    o, _ = _cached_attend(qbd, ck_t, kn_t, cv_t, vn_t, 1, s, sink_col)
    o_ref[...] = _diag_heads(o, hkv, grp, s)
    _store_shifted(ko_ref, 0, ck_t, kn_t, s)
    _store_shifted(vo_ref, 0, cv_t, vn_t, s)


def _sample_a_call(q, kn, vn, ck_t, cv_t, sinks, *, s):
    nb, c, lc = ck_t.shape
    row = lambda w: pl.BlockSpec((s, w), lambda b: (b, 0))
    cache = pl.BlockSpec((1, c, lc), lambda b: (b, 0, 0))
    return pl.pallas_call(
        functools.partial(_sample_a_kernel, s=s),
        grid=(nb,),
        in_specs=[row(q.shape[1]), row(c), row(c), cache, cache,
                  pl.BlockSpec(memory_space=pltpu.SMEM)],
        out_specs=[row(q.shape[1]), cache, cache],
        out_shape=[jax.ShapeDtypeStruct(q.shape, F32),
                   jax.ShapeDtypeStruct(ck_t.shape, F32),
                   jax.ShapeDtypeStruct(cv_t.shape, F32)],
        compiler_params=_params("parallel"),
        name="sample_attn_a",
    )(q, kn, vn, ck_t, cv_t, sinks)


def _sample_b_kernel(q_ref, kn_ref, vn_ref, ck_ref, cv_ref, o_ref, ko_ref, vo_ref, *, s):
    hkv, grp = B_KV_HEADS, B_Q_HEADS // B_KV_HEADS
    ckv = hkv * HEAD_DIM
    outs, lses = [], []
    off = 0
    for g, (w, d) in enumerate(B_PATTERNS):
        lc = w
        kn_t = _new_rows_t(kn_ref[:, ckv * g:ckv * (g + 1)], s)
        vn_t = _new_rows_t(vn_ref[:, ckv * g:ckv * (g + 1)], s)
        ck_t = ck_ref[0, :, off:off + lc]
        cv_t = cv_ref[0, :, off:off + lc]
        qbd = _block_diag_q(q_ref, B_Q_HEADS * HEAD_DIM * g, hkv, grp, s)
        o, lse = _cached_attend(qbd, ck_t, kn_t, cv_t, vn_t, d, s, None)
        outs.append(o)
        lses.append(lse)
        _store_shifted(ko_ref, off, ck_t, kn_t, s)
        _store_shifted(vo_ref, off, cv_t, vn_t, s)
        off += lc
    mx = jnp.maximum(jnp.maximum(lses[0], lses[1]), lses[2])
    es = [jnp.exp(l - mx) for l in lses]
    den = es[0] + es[1] + es[2]
    comb = (es[0] / den) * outs[0] + (es[1] / den) * outs[1] + (es[2] / den) * outs[2]
    o_ref[...] = _diag_heads(comb, hkv, grp, s)


def _sample_b_call(q, kn, vn, ck_t, cv_t, *, s):
    nb, c, lb = ck_t.shape
    row = lambda w: pl.BlockSpec((s, w), lambda b: (b, 0))
    cache = pl.BlockSpec((1, c, lb), lambda b: (b, 0, 0))
    co = B_Q_HEADS * HEAD_DIM
    return pl.pallas_call(
        functools.partial(_sample_b_kernel, s=s),
        grid=(nb,),
        in_specs=[row(q.shape[1]), row(kn.shape[1]), row(vn.shape[1]), cache, cache],
        out_specs=[row(co), cache, cache],
        out_shape=[jax.ShapeDtypeStruct((q.shape[0], co), F32),
                   jax.ShapeDtypeStruct(ck_t.shape, F32),
                   jax.ShapeDtypeStruct(cv_t.shape, F32)],
        compiler_params=_params("parallel"),
        name="sample_attn_b",
    )(q, kn, vn, ck_t, cv_t)


def _rope_tables(pos):
    inv_freq = ROPE_THETA ** (-jnp.arange(HALF, dtype=F32) / HALF)
    ang = pos.astype(F32)[:, None] * inv_freq[None, :]
    cos, sin = jnp.cos(ang), jnp.sin(ang)
    return jnp.tile(cos, (1, 4)), jnp.tile(jnp.concatenate([-sin, sin], axis=1), (1, 2))


def _cache_t(cache):
    b, l, h, dh = cache.shape
    return jnp.transpose(cache, (0, 2, 3, 1)).reshape(b, h * dh, l)


def _cache_from_t(x, h):
    b, c, l = x.shape
    return jnp.transpose(x.reshape(b, h, c // h, l), (0, 3, 1, 2))[None]


def kernel(x_prompt, x_sample, cache_a_k, cache_a_v, cache_b_k, cache_b_v, state_ffn_conv,
           attn_norm, ffn_norm, a_w_qkv, a_q_norm, a_k_norm, a_sinks, a_w_o,
           b_w_qkv, b_q_norm, b_k_norm, b_w_o,
           ffn_w_gate, ffn_w_up, ffn_conv_w, ffn_conv_b, ffn_w_down):
    nb, seq, _ = x_prompt.shape
    ns, dec, _ = x_sample.shape
    mp, ms = nb * seq, ns * dec

    cos_p, sin_p = _rope_tables(jnp.tile(jnp.arange(seq, dtype=jnp.int32), nb))
    cos_s, sin_s = _rope_tables(jnp.tile(PAST_LEN + jnp.arange(dec, dtype=jnp.int32), ns))
    idx = jnp.arange(V7X_MXU_DIM, dtype=jnp.int32) // HEAD_DIM
    ones_blk = (idx[:, None] == idx[None, :]).astype(BF16)

    yp = x_prompt.reshape(mp, D_MODEL)
    ys = x_sample.reshape(ms, D_MODEL)
    row = lambda a: a.reshape(1, -1)

    def head_gain(gv, n_heads_per_tile):
        return jnp.tile(gv, n_heads_per_tile).reshape(1, -1)

    wqkv = _cast_call(a_w_qkv, 0)
    wo = _cast_call(a_w_o, 0)
    tn = wqkv.shape[1] // N_QKV_TILES
    qg, kg = head_gain(a_q_norm[0], tn // HEAD_DIM), head_gain(a_k_norm[0], tn // HEAD_DIM)
    g_attn = row(attn_norm[0])
    grp_a = A_Q_HEADS // A_KV_HEADS

    q, k, v = _qkv_call(yp, g_attn, wqkv, cos_p, sin_p, qg, kg, ones_blk,
                        n_q=A_Q_HEADS, n_kv=A_KV_HEADS, tm=TM_PROJ, q_dtype=BF16)
    k3, v3 = k.reshape(nb, seq, -1), v.reshape(nb, seq, -1)
    o = _band_call(q.reshape(nb, seq, -1), k3, v3, a_sinks[0], hkv=A_KV_HEADS, grp=grp_a)
    yp = _wo_call(yp, o.reshape(mp, -1), wo, tm=TM_PROJ)
    keep = min(A_WINDOW, seq)
    ca = A_KV_HEADS * HEAD_DIM
    a_k_prompt = _cache_from_t(_keep_t_call(k3, [keep], ca), A_KV_HEADS)
    a_v_prompt = _cache_from_t(_keep_t_call(v3, [keep], ca), A_KV_HEADS)

    qs, ks, vs = _qkv_call(ys, g_attn, wqkv, cos_s, sin_s, qg, kg, ones_blk,
                           n_q=A_Q_HEADS, n_kv=A_KV_HEADS, tm=ms, q_dtype=F32)
    os_, ako, avo = _sample_a_call(qs, ks, vs, _cache_t(cache_a_k[0]), _cache_t(cache_a_v[0]),
                                   a_sinks[0], s=dec)
    ys = _wo_call(ys, os_, wo, tm=ms)
    a_k_sample = _cache_from_t(ako, A_KV_HEADS)
    a_v_sample = _cache_from_t(avo, A_KV_HEADS)

    conv_p, conv_s = [], []

    def ffn(layer, yp, ys):
        wg = _cast_call(ffn_w_gate, layer)
        wu = _cast_call(ffn_w_up, layer)
        wd = _cast_call(ffn_w_down, layer)
        g_ffn = row(ffn_norm[layer])
        cw, cb = ffn_conv_w[layer], row(ffn_conv_b[layer])
        yp, tail = _ffn_prompt_call(yp, g_ffn, wg, wu, wd, cw, cb, seq=seq)
        tiles = seq // TM_FFN
        conv_p.append(tail[tiles - 1::tiles, V7X_SUBLANES - (CONV_W - 1):, :])
        st = state_ffn_conv[layer]
        s0 = jnp.repeat(st[:, 0, :], dec, axis=0)
        s1 = jnp.repeat(st[:, 1, :], dec, axis=0)
        ys, gate_s = _ffn_sample_call(ys, g_ffn, wg, wu, wd, cw, cb, s0, s1, seq=dec)
        conv_s.append(gate_s.reshape(ns, dec, D_FF)[:, dec - (CONV_W - 1):, :])
        return yp, ys

    yp, ys = ffn(0, yp, ys)

    wqkv = _cast_call(b_w_qkv, 0)
    wo = _cast_call(b_w_o, 0)
    tn = wqkv.shape[1] // N_QKV_TILES
    qg, kg = head_gain(b_q_norm[0], tn // HEAD_DIM), head_gain(b_k_norm[0], tn // HEAD_DIM)
    g_attn = row(attn_norm[1])
    n_grp = len(B_PATTERNS)
    grp_b = B_Q_HEADS // B_KV_HEADS
    nqb, nkvb = n_grp * B_Q_HEADS, n_grp * B_KV_HEADS
    cb_ = B_KV_HEADS * HEAD_DIM

    q, k, v = _qkv_call(yp, g_attn, wqkv, cos_p, sin_p, qg, kg, ones_blk,
                        n_q=nqb, n_kv=nkvb, tm=TM_PROJ, q_dtype=F32, lane_major=True)
    outs, lses = [], []
    for g, (w, d) in enumerate(B_PATTERNS):
        assert w // d == BLOCK
        og, lg = _dil_call(q, k, v, nb=nb, hkv=B_KV_HEADS, grp=grp_b, d=d, group=g)
        outs.append(og)
        lses.append(lg)
    yp = _wo_comb_call(yp, outs, lses, wo, tm=TM_PROJ // 2)
    keeps = [min(w, seq) for w, _ in B_PATTERNS]
    k4 = k.reshape(k.shape[0], nb, seq, V7X_LANES)
    v4 = v.reshape(v.shape[0], nb, seq, V7X_LANES)
    b_k_prompt = _cache_from_t(_keep_t_call(k4, keeps, cb_, lane_major=True), B_KV_HEADS)
    b_v_prompt = _cache_from_t(_keep_t_call(v4, keeps, cb_, lane_major=True), B_KV_HEADS)

    qs, ks, vs = _qkv_call(ys, g_attn, wqkv, cos_s, sin_s, qg, kg, ones_blk,
                           n_q=nqb, n_kv=nkvb, tm=ms, q_dtype=F32)
    os_, bko, bvo = _sample_b_call(qs, ks, vs, _cache_t(cache_b_k[0]), _cache_t(cache_b_v[0]),
                                   s=dec)
    ys = _wo_call(ys, os_, wo, tm=ms)
    b_k_sample = _cache_from_t(bko, B_KV_HEADS)
    b_v_sample = _cache_from_t(bvo, B_KV_HEADS)

    yp, ys = ffn(1, yp, ys)

    return (yp.reshape(nb, seq, D_MODEL), ys.reshape(ns, dec, D_MODEL),
            a_k_prompt, a_v_prompt, a_k_sample, a_v_sample,
            b_k_prompt, b_v_prompt, b_k_sample, b_v_sample,
            jnp.stack(conv_p), jnp.stack(conv_s))
```

```python
import functools

import jax
import jax.numpy as jnp
from jax import lax
from jax.experimental import pallas as pl
from jax.experimental.pallas import tpu as pltpu

F32 = jnp.float32
BF16 = jnp.bfloat16

D_MODEL = 2048
HEAD_DIM = 64
HALF = HEAD_DIM // 2
ROPE_THETA = 10000.0
NORM_EPS = 1e-6
BLOCK = 128
PAST_LEN = 16384
A_WINDOW = 128
A_Q_HEADS = 32
A_KV_HEADS = 8
B_PATTERNS = ((128, 1), (512, 4), (2048, 16))
B_Q_HEADS = 16
B_KV_HEADS = 4
D_FF = 5632
CONV_W = 3
NEG_INF = -1e30
LOG2E = 1.4426950408889634
LN2 = 0.6931471805599453
Q_SCALE = HEAD_DIM ** -0.5 * LOG2E

V7X_MXU_DIM = 256
V7X_LANES = 128
V7X_SUBLANES = 8
VMEM_LIMIT = 56 * 1024 * 1024

TM_PROJ = 512
TM_QKV = 1024
TM_FFN = 512
TF_FFN = 512
TR_CAST = 256
PROJ_CHUNK_ROWS = 256
FFN_ROW_CHUNKS = 1
N_QKV_TILES = 6
DIL_MAX_ROWS = 1024
ATTN_DEPTH = 2
DIL_TILES_PER_TRIP = 4


def _params(*sem):
    return pltpu.CompilerParams(dimension_semantics=sem, vmem_limit_bytes=VMEM_LIMIT)


def _rms(x, g):
    ms = jnp.mean(x * x, axis=-1, keepdims=True)
    return x * lax.rsqrt(ms + NORM_EPS) * g


def _cast_kernel(w_ref, o_ref):
    o_ref[...] = w_ref[0].astype(BF16)


def _cast_call(w, layer):
    _, r, c = w.shape
    tr = TR_CAST
    assert r % tr == 0
    return pl.pallas_call(
        _cast_kernel,
        grid=(r // tr,),
        in_specs=[pl.BlockSpec((1, tr, c), lambda i: (layer, i, 0))],
        out_specs=pl.BlockSpec((tr, c), lambda i: (i, 0)),
        out_shape=jax.ShapeDtypeStruct((r, c), BF16),
        compiler_params=_params("parallel"),
        name="cast_bf16",
    )(w)


def _head_norm_rope(a, gain, ones_blk, cos, sin):
    tn = a.shape[1]
    x2 = a * a
    hi = x2.astype(BF16)
    lo = (x2 - hi.astype(F32)).astype(BF16)
    parts = []
    for c in range(tn // V7X_MXU_DIM):
        sl = slice(V7X_MXU_DIM * c, V7X_MXU_DIM * (c + 1))
        parts.append(jnp.dot(hi[:, sl], ones_blk, preferred_element_type=F32)
                     + jnp.dot(lo[:, sl], ones_blk, preferred_element_type=F32))
    ss = jnp.concatenate(parts, axis=1)
    y = a * lax.rsqrt(ss * (1.0 / HEAD_DIM) + NORM_EPS) * gain
    lane = lax.broadcasted_iota(jnp.int32, y.shape, 1)
    first_half = (lane & (HEAD_DIM - 1)) < HALF
    partner = jnp.where(first_half, pltpu.roll(y, tn - HALF, 1), pltpu.roll(y, HALF, 1))
    reps = tn // V7X_LANES
    return y * jnp.tile(cos, (1, reps)) + partner * jnp.tile(sin, (1, reps))


def _store_cols(ref, rows, val, lane_major):
    if not lane_major:
        ref[rows, :] = val.astype(ref.dtype)
        return
    for t in range(val.shape[1] // V7X_LANES):
        ref[t, rows, :] = val[:, V7X_LANES * t:V7X_LANES * (t + 1)].astype(ref.dtype)


def _qkv_kernel(x_ref, g_ref, w_ref, cos_ref, sin_ref, qg_ref, kg_ref, ones_ref,
                q_ref, k_ref, v_ref, xn_ref, *, n_q_tiles, lane_major, row_chunks):
    n = pl.program_id(1)

    @pl.when(n == 0)
    def _():
        xn_ref[...] = _rms(x_ref[...], g_ref[...]).astype(BF16)

    rc = xn_ref.shape[0] // row_chunks

    def project(out_ref, epilogue):
        for c in range(row_chunks):
            rows = slice(rc * c, rc * (c + 1))
            acc = jnp.dot(xn_ref[rows, :], w_ref[...], preferred_element_type=F32)
            _store_cols(out_ref, rows, epilogue(acc, rows), lane_major)

    def norm_rope(gain_ref, scale):
        def fn(acc, rows):
            r = _head_norm_rope(acc, gain_ref[...], ones_ref[...], cos_ref[rows, :],
                                sin_ref[rows, :])
            return r * scale if scale != 1.0 else r
        return fn

    @pl.when(n < n_q_tiles)
    def _():
        project(q_ref, norm_rope(qg_ref, Q_SCALE))

    @pl.when(n == n_q_tiles)
    def _():
        project(k_ref, norm_rope(kg_ref, 1.0))

    @pl.when(n == n_q_tiles + 1)
    def _():
        project(v_ref, lambda acc, rows: acc)


def _qkv_call(x, g, w, cos, sin, qg, kg, ones_blk, *, n_q, n_kv, tm, q_dtype, lane_major=False):
    m = x.shape[0]
    ncols = w.shape[1]
    tn = ncols // N_QKV_TILES
    n_q_tiles = (n_q * HEAD_DIM) // tn
    assert n_q_tiles * tn == n_q * HEAD_DIM and n_kv * HEAD_DIM == tn and tn % V7X_MXU_DIM == 0
    assert m % tm == 0
    last_q = n_q_tiles - 1
    if lane_major:
        lt = tn // V7X_LANES
        out_specs = [
            pl.BlockSpec((lt, tm, V7X_LANES), lambda i, n: (jnp.minimum(n, last_q), i, 0)),
            pl.BlockSpec((lt, tm, V7X_LANES), lambda i, n: (0, i, 0)),
            pl.BlockSpec((lt, tm, V7X_LANES), lambda i, n: (0, i, 0)),
        ]
        out_shape = [
            jax.ShapeDtypeStruct((n_q_tiles * lt, m, V7X_LANES), q_dtype),
            jax.ShapeDtypeStruct((lt, m, V7X_LANES), F32),
            jax.ShapeDtypeStruct((lt, m, V7X_LANES), F32),
        ]
    else:
        out_specs = [
            pl.BlockSpec((tm, tn), lambda i, n: (i, jnp.minimum(n, last_q))),
            pl.BlockSpec((tm, tn), lambda i, n: (i, 0)),
            pl.BlockSpec((tm, tn), lambda i, n: (i, 0)),
        ]
        out_shape = [
            jax.ShapeDtypeStruct((m, n_q * HEAD_DIM), q_dtype),
            jax.ShapeDtypeStruct((m, tn), F32),
            jax.ShapeDtypeStruct((m, tn), F32),
        ]
    return pl.pallas_call(
        functools.partial(_qkv_kernel, n_q_tiles=n_q_tiles, lane_major=lane_major,
                          row_chunks=max(1, tm // PROJ_CHUNK_ROWS)),
        grid=(m // tm, N_QKV_TILES),
        in_specs=[
            pl.BlockSpec((tm, D_MODEL), lambda i, n: (i, 0)),
            pl.BlockSpec((1, D_MODEL), lambda i, n: (0, 0)),
            pl.BlockSpec((D_MODEL, tn), lambda i, n: (0, n)),
            pl.BlockSpec((tm, V7X_LANES), lambda i, n: (i, 0)),
            pl.BlockSpec((tm, V7X_LANES), lambda i, n: (i, 0)),
            pl.BlockSpec((1, tn), lambda i, n: (0, 0)),
            pl.BlockSpec((1, tn), lambda i, n: (0, 0)),
            pl.BlockSpec((V7X_MXU_DIM, V7X_MXU_DIM), lambda i, n: (0, 0)),
        ],
        out_specs=out_specs,
        out_shape=out_shape,
        scratch_shapes=[pltpu.VMEM((tm, D_MODEL), BF16)],
        compiler_params=_params("parallel", "arbitrary"),
        name="qkv_proj",
    )(x, g, w, cos, sin, qg, kg, ones_blk)


GRP = 4


def _band_mask_t(first):
    shape = (2 * BLOCK, BLOCK)
    kj = lax.broadcasted_iota(jnp.int32, shape, 0)
    qi = lax.broadcasted_iota(jnp.int32, shape, 1)
    seen = (kj >= qi) & (kj <= qi + BLOCK) & ((kj >= BLOCK) | jnp.logical_not(first))
    return jnp.where(seen, 0.0, NEG_INF)


def _kv_tile_forms(kt, vt):
    lo = lax.broadcasted_iota(jnp.int32, kt.shape, 1) < HEAD_DIM
    kt_sw = pltpu.roll(kt, HEAD_DIM, 1)
    k_forms = [(jnp.where(lo, kt, 0.0).astype(BF16), jnp.where(lo, 0.0, kt_sw).astype(BF16)),
               (jnp.where(lo, kt_sw, 0.0).astype(BF16), jnp.where(lo, 0.0, kt).astype(BF16))]
    return k_forms, vt.T.astype(BF16)


def _head_scores(k_lo, k_hi, q_a, q_b):
    nt = (((1,), (1,)), ((), ()))
    qpair = jnp.concatenate([q_a, q_b], axis=0)
    return jnp.concatenate([lax.dot_general(k_lo, qpair, nt, preferred_element_type=F32),
                            lax.dot_general(k_hi, qpair, nt, preferred_element_type=F32)], axis=1)


def _head_softmax_pv(s, mask4, v_t, sinks, want_lse):
    s = s + mask4
    m = jnp.max(s, axis=0, keepdims=True)
    if sinks is not None:
        sk = jnp.concatenate([jnp.full((1, BLOCK), sinks[j] * LOG2E, F32)
                              for j in (0, 2, 1, 3)], axis=1)
        m = jnp.maximum(m, sk)
    p = jnp.exp2(s - m)
    l = jnp.sum(p, axis=0, keepdims=True)
    if sinks is not None:
        l = l + jnp.exp2(sk - m)
    o_t = jnp.dot(v_t, p.astype(BF16), preferred_element_type=F32) / l
    lse_t = jnp.broadcast_to((m + jnp.log2(l)) * LN2, o_t.shape) if want_lse else None
    o_tiles, l_tiles = [], []
    for u in range(2):
        c0, c1 = slice(BLOCK * u, BLOCK * (u + 1)), slice(BLOCK * (2 + u), BLOCK * (3 + u))
        o_tiles.append(jnp.concatenate([o_t[:, c0], o_t[:, c1]], axis=0).T)
        if want_lse:
            l_tiles.append(jnp.concatenate([lse_t[:, c0], lse_t[:, c1]], axis=0).T)
    return o_tiles, l_tiles


def _attend_tiles(tiles, mask, want_lse):
    mask4 = jnp.tile(mask, (1, GRP))
    jobs = [(ti, e) for ti in range(len(tiles)) for e in range(2)]
    forms = {}

    def scores(job):
        ti, e = job
        load_kv, load_q = tiles[ti][:2]
        if ti not in forms:
            forms[ti] = _kv_tile_forms(*load_kv())
        k_lo, k_hi = forms[ti][0][e]
        return _head_scores(k_lo, k_hi, load_q(2 * e), load_q(2 * e + 1))

    pending = [scores(job) for job in jobs[:ATTN_DEPTH]]
    for idx, (ti, e) in enumerate(jobs):
        s = pending.pop(0)
        if idx + ATTN_DEPTH < len(jobs):
            pending.append(scores(jobs[idx + ATTN_DEPTH]))
        sink_of, store = tiles[ti][2:]
        v_t = forms[ti][1][HEAD_DIM * e:HEAD_DIM * (e + 1)]
        sinks = None if sink_of is None else [sink_of(GRP * e + j) for j in range(GRP)]
        o_tiles, l_tiles = _head_softmax_pv(s, mask4, v_t, sinks, want_lse)
        for u in range(2):
            store(2 * e + u, o_tiles[u], l_tiles[u] if want_lse else None)


def _band_kernel(q_ref, kc_ref, kp_ref, vc_ref, vp_ref, sink_ref, o_ref, *, kv_tiles):
    mask = _band_mask_t(pl.program_id(1) == 0)
    lanes = lambda t: slice(V7X_LANES * t, V7X_LANES * (t + 1))

    def tile(t):
        def load_kv():
            return (jnp.concatenate([kp_ref[0, :, lanes(t)], kc_ref[0, :, lanes(t)]], axis=0),
                    jnp.concatenate([vp_ref[0, :, lanes(t)], vc_ref[0, :, lanes(t)]], axis=0))

        def store(u, o_tile, _):
            o_ref[0, :, lanes(GRP * t + u)] = o_tile.astype(o_ref.dtype)

        return (load_kv, lambda u: q_ref[0, :, lanes(GRP * t + u)],
                lambda h: sink_ref[2 * GRP * t + h], store)

    _attend_tiles([tile(t) for t in range(kv_tiles)], mask, False)


def _band_call(q, k, v, sinks, *, hkv, grp):
    b, seq, cq = q.shape
    ck = hkv * HEAD_DIM
    assert grp == GRP and hkv % 2 == 0
    cur = lambda bi, c: (bi, c, 0)
    prev = lambda bi, c: (bi, jnp.maximum(c - 1, 0), 0)
    return pl.pallas_call(
        functools.partial(_band_kernel, kv_tiles=hkv // 2),
        grid=(b, seq // BLOCK),
        in_specs=[
            pl.BlockSpec((1, BLOCK, cq), cur),
            pl.BlockSpec((1, BLOCK, ck), cur),
            pl.BlockSpec((1, BLOCK, ck), prev),
            pl.BlockSpec((1, BLOCK, ck), cur),
            pl.BlockSpec((1, BLOCK, ck), prev),
            pl.BlockSpec(memory_space=pltpu.SMEM),
        ],
        out_specs=pl.BlockSpec((1, BLOCK, cq), cur),
        out_shape=jax.ShapeDtypeStruct((b, seq, cq), BF16),
        compiler_params=_params("parallel", "arbitrary"),
        name="band_attn",
    )(q, k, k, v, v, sinks)


def _dil_kernel(q_ref, kc_ref, kp_ref, vc_ref, vp_ref, o_ref, lse_ref, *, kv_tiles, grp, d):
    mask = _band_mask_t(pl.program_id(1) == 0)

    def tile(r, t):
        rs = pl.ds(r, BLOCK, stride=d) if d > 1 else slice(None)

        def load_kv():
            return (jnp.concatenate([kp_ref[t, rs, :], kc_ref[t, rs, :]], axis=0),
                    jnp.concatenate([vp_ref[t, rs, :], vc_ref[t, rs, :]], axis=0))

        def store(u, o_tile, l_tile):
            o_ref[grp * t + u, rs, :] = o_tile
            lse_ref[grp * t + u, rs, :] = l_tile

        return load_kv, lambda u: q_ref[grp * t + u, rs, :].astype(BF16), None, store

    unroll = min(d, max(1, DIL_TILES_PER_TRIP // kv_tiles))

    def body(i, carry):
        _attend_tiles([tile(i * unroll + rr, t) for rr in range(unroll) for t in range(kv_tiles)],
                      mask, True)
        return carry

    if d > unroll:
        lax.fori_loop(0, d // unroll, body, 0)
    else:
        body(0, 0)


def _dil_call(q, k, v, *, nb, hkv, grp, d, group):
    m = q.shape[1]
    seq = m // nb
    rows = BLOCK * d
    hsplit = max(1, rows // DIL_MAX_ROWS)
    kv_tiles = hkv // 2 // hsplit
    q_tiles = kv_tiles * grp
    nchunk = seq // rows
    assert seq % rows == 0 and kv_tiles * 2 * hsplit == hkv and grp % 2 == 0
    cur = lambda bi, c, hp: (group * hsplit + hp, bi * nchunk + c, 0)
    prev = lambda bi, c, hp: (group * hsplit + hp, bi * nchunk + jnp.maximum(c - 1, 0), 0)
    o_spec = pl.BlockSpec((q_tiles, rows, V7X_LANES), lambda bi, c, hp: (hp, bi * nchunk + c, 0))
    o_shape = jax.ShapeDtypeStruct((q_tiles * hsplit, m, V7X_LANES), F32)
    return pl.pallas_call(
        functools.partial(_dil_kernel, kv_tiles=kv_tiles, grp=grp, d=d),
        grid=(nb, nchunk, hsplit),
        in_specs=[
            pl.BlockSpec((q_tiles, rows, V7X_LANES), cur),
            pl.BlockSpec((kv_tiles, rows, V7X_LANES), cur),
            pl.BlockSpec((kv_tiles, rows, V7X_LANES), prev),
            pl.BlockSpec((kv_tiles, rows, V7X_LANES), cur),
            pl.BlockSpec((kv_tiles, rows, V7X_LANES), prev),
        ],
        out_specs=[o_spec, o_spec],
        out_shape=[o_shape, o_shape],
        compiler_params=_params("parallel", "arbitrary", "arbitrary"),
        name="dilated_attn",
    )(q, k, k, v, v)


def _keep_t_kernel(x_ref, o_ref):
    o_ref[0] = x_ref[0].T


def _keep_t_lane_major_kernel(x_ref, o_ref, *, keeps):
    g = pl.program_id(2)
    seq = x_ref.shape[2]
    off = 0
    for k, keep in enumerate(keeps):
        @pl.when(g == k)
        def _(keep=keep, off=off):
            o_ref[0, :, off:off + keep] = x_ref[0, 0, seq - keep:, :].T
        off += keep


def _keep_t_call(x, keeps, c, *, lane_major=False):
    b, seq = x.shape[1:3] if lane_major else x.shape[:2]
    starts, first_rb = [], []
    n = 0
    for keep in keeps:
        assert keep % BLOCK == 0 and seq % BLOCK == 0
        starts.append(n)
        first_rb.append((seq - keep) // BLOCK)
        n += keep // BLOCK

    def src_block(j):
        rb = jnp.int32(0)
        cb = jnp.int32(0)
        for g in range(len(keeps)):
            inside = j >= starts[g]
            rb = jnp.where(inside, first_rb[g] + j - starts[g], rb)
            cb = jnp.where(inside, g, cb)
        return rb, cb

    if lane_major:
        ct = c // V7X_LANES
        grid = (b, ct, len(keeps))
        in_spec = pl.BlockSpec((1, 1, seq, V7X_LANES), lambda bi, t, g: (g * ct + t, bi, 0, 0))
        out_spec = pl.BlockSpec((1, V7X_LANES, n * BLOCK), lambda bi, t, g: (bi, t, 0))
        kern = functools.partial(_keep_t_lane_major_kernel, keeps=tuple(keeps))
        sem = ("parallel", "parallel", "arbitrary")
    else:
        grid = (b, n)

        def src(bi, j):
            rb, cb = src_block(j)
            return bi, rb, cb

        in_spec = pl.BlockSpec((1, BLOCK, c), src)
        out_spec = pl.BlockSpec((1, c, BLOCK), lambda bi, j: (bi, 0, j))
        kern = _keep_t_kernel
        sem = ("parallel", "parallel")
    return pl.pallas_call(
        kern,
        grid=grid,
        in_specs=[in_spec],
        out_specs=out_spec,
        out_shape=jax.ShapeDtypeStruct((b, c, n * BLOCK), F32),
        compiler_params=_params(*sem),
        name="keep_rows_t",
    )(x)


def _wo_kernel(y_ref, o_ref, w_ref, out_ref):
    out_ref[...] = y_ref[...] + jnp.dot(o_ref[...].astype(BF16), w_ref[...],
                                        preferred_element_type=F32)


def _wo_comb_kernel(y_ref, o0_ref, o1_ref, o2_ref, l0_ref, l1_ref, l2_ref, w_ref, out_ref):
    tiles = []
    for t in range(o0_ref.shape[0]):
        l0, l1, l2 = l0_ref[t], l1_ref[t], l2_ref[t]
        mx = jnp.maximum(jnp.maximum(l0, l1), l2)
        e0, e1, e2 = jnp.exp(l0 - mx), jnp.exp(l1 - mx), jnp.exp(l2 - mx)
        den = e0 + e1 + e2
        comb = (e0 / den) * o0_ref[t] + (e1 / den) * o1_ref[t] + (e2 / den) * o2_ref[t]
        tiles.append(comb.astype(BF16))
    comb = jnp.concatenate(tiles, axis=1)
    out_ref[...] = y_ref[...] + jnp.dot(comb, w_ref[...], preferred_element_type=F32)


def _wo_call(y, o, w, *, tm):
    m = y.shape[0]
    c = o.shape[1]
    return pl.pallas_call(
        _wo_kernel,
        grid=(m // tm,),
        in_specs=[
            pl.BlockSpec((tm, D_MODEL), lambda i: (i, 0)),
            pl.BlockSpec((tm, c), lambda i: (i, 0)),
            pl.BlockSpec((c, D_MODEL), lambda i: (0, 0)),
        ],
        out_specs=pl.BlockSpec((tm, D_MODEL), lambda i: (i, 0)),
        out_shape=jax.ShapeDtypeStruct((m, D_MODEL), F32),
        compiler_params=_params("parallel"),
        name="wo_proj",
    )(y, o, w)


def _wo_comb_call(y, os_, ls_, w, *, tm):
    m = y.shape[0]
    c = w.shape[0]
    blk = pl.BlockSpec((c // V7X_LANES, tm, V7X_LANES), lambda i: (0, i, 0))
    return pl.pallas_call(
        _wo_comb_kernel,
        grid=(m // tm,),
        in_specs=[pl.BlockSpec((tm, D_MODEL), lambda i: (i, 0))] + [blk] * 6
                 + [pl.BlockSpec((c, D_MODEL), lambda i: (0, 0))],
        out_specs=pl.BlockSpec((tm, D_MODEL), lambda i: (i, 0)),
        out_shape=jax.ShapeDtypeStruct((m, D_MODEL), F32),
        compiler_params=_params("parallel"),
        name="wo_comb_proj",
    )(y, *os_, *ls_, w)


def _ffn_tail(gate, g1, g2, up, cw_ref, cb_ref, wd_ref):
    conv = cb_ref[...] + cw_ref[0:1, :] * g2 + cw_ref[1:2, :] * g1 + cw_ref[2:3, :] * gate
    h = conv * jax.nn.sigmoid(conv) * up
    return jnp.dot(h.astype(BF16), wd_ref[...], preferred_element_type=F32)


def _ffn_prompt_kernel(y_ref, g_ref, wg_ref, wu_ref, wd_ref, cw_ref, cb_ref,
                       out_ref, tail_ref, xn_ref, carry_ref, *, tiles_per_seq):
    m = pl.program_id(0)
    f = pl.program_id(1)

    @pl.when(f == 0)
    def _():
        x = y_ref[...]
        xn_ref[...] = _rms(x, g_ref[...]).astype(BF16)
        out_ref[...] = x

    @pl.when(m % tiles_per_seq == 0)
    def _():
        carry_ref[f] = jnp.zeros(carry_ref.shape[1:], F32)

    rc = xn_ref.shape[0] // FFN_ROW_CHUNKS
    last = carry_ref[f]
    for c in range(FFN_ROW_CHUNKS):
        rows = slice(rc * c, rc * (c + 1))
        xn = xn_ref[rows, :]
        gate = jnp.dot(xn, wg_ref[...], preferred_element_type=F32)
        up = jnp.dot(xn, wu_ref[...], preferred_element_type=F32)
        row = lax.broadcasted_iota(jnp.int32, gate.shape, 0)
        g1 = jnp.where(row == 0, last[7:8, :], pltpu.roll(gate, 1, 0))
        g2 = jnp.where(row == 0, last[6:7, :],
                       jnp.where(row == 1, last[7:8, :], pltpu.roll(gate, 2, 0)))
        out_ref[rows, :] += _ffn_tail(gate, g1, g2, up, cw_ref, cb_ref, wd_ref)
        last = gate[rc - V7X_SUBLANES:, :]
    carry_ref[f] = last
    tail_ref[0] = last


def _ffn_prompt_call(y, g, wg, wu, wd, cw, cb, *, seq):
    m = y.shape[0]
    tm, tf = TM_FFN, TF_FFN
    nf = D_FF // tf
    assert m % tm == 0 and seq % tm == 0 and D_FF % tf == 0
    return pl.pallas_call(
        functools.partial(_ffn_prompt_kernel, tiles_per_seq=seq // tm),
        grid=(m // tm, nf),
        in_specs=[
            pl.BlockSpec((tm, D_MODEL), lambda i, f: (i, 0)),
            pl.BlockSpec((1, D_MODEL), lambda i, f: (0, 0)),
            pl.BlockSpec((D_MODEL, tf), lambda i, f: (0, f)),
            pl.BlockSpec((D_MODEL, tf), lambda i, f: (0, f)),
            pl.BlockSpec((tf, D_MODEL), lambda i, f: (f, 0)),
            pl.BlockSpec((CONV_W, tf), lambda i, f: (0, f)),
            pl.BlockSpec((1, tf), lambda i, f: (0, f)),
        ],
        out_specs=[
            pl.BlockSpec((tm, D_MODEL), lambda i, f: (i, 0)),
            pl.BlockSpec((1, V7X_SUBLANES, tf), lambda i, f: (i, 0, f)),
        ],
        out_shape=[
            jax.ShapeDtypeStruct((m, D_MODEL), F32),
            jax.ShapeDtypeStruct((m // tm, V7X_SUBLANES, D_FF), F32),
        ],
        scratch_shapes=[pltpu.VMEM((tm, D_MODEL), BF16),
                        pltpu.VMEM((nf, V7X_SUBLANES, tf), F32)],
        compiler_params=_params("arbitrary", "arbitrary"),
        name="conv_ffn_prompt",
    )(y, g, wg, wu, wd, cw, cb)


def _ffn_sample_kernel(y_ref, g_ref, wg_ref, wu_ref, wd_ref, cw_ref, cb_ref, s0_ref, s1_ref,
                       out_ref, gate_ref, xn_ref, *, seq):
    f = pl.program_id(0)

    @pl.when(f == 0)
    def _():
        x = y_ref[...]
        xn_ref[...] = _rms(x, g_ref[...]).astype(BF16)
        out_ref[...] = x

    xn = xn_ref[...]
    gate = jnp.dot(xn, wg_ref[...], preferred_element_type=F32)
    up = jnp.dot(xn, wu_ref[...], preferred_element_type=F32)
    t = lax.broadcasted_iota(jnp.int32, gate.shape, 0) & (seq - 1)
    s0, s1 = s0_ref[...], s1_ref[...]
    g1 = jnp.where(t == 0, s1, pltpu.roll(gate, 1, 0))
    g2 = jnp.where(t == 0, s0, jnp.where(t == 1, s1, pltpu.roll(gate, 2, 0)))
    out_ref[...] += _ffn_tail(gate, g1, g2, up, cw_ref, cb_ref, wd_ref)
    gate_ref[...] = gate


def _ffn_sample_call(y, g, wg, wu, wd, cw, cb, s0, s1, *, seq):
    m = y.shape[0]
    tf = TF_FFN
    nf = D_FF // tf
    full = pl.BlockSpec((m, D_MODEL), lambda f: (0, 0))
    col = pl.BlockSpec((m, tf), lambda f: (0, f))
    return pl.pallas_call(
        functools.partial(_ffn_sample_kernel, seq=seq),
        grid=(nf,),
        in_specs=[
            full,
            pl.BlockSpec((1, D_MODEL), lambda f: (0, 0)),
            pl.BlockSpec((D_MODEL, tf), lambda f: (0, f)),
            pl.BlockSpec((D_MODEL, tf), lambda f: (0, f)),
            pl.BlockSpec((tf, D_MODEL), lambda f: (f, 0)),
            pl.BlockSpec((CONV_W, tf), lambda f: (0, f)),
            pl.BlockSpec((1, tf), lambda f: (0, f)),
            col, col,
        ],
        out_specs=[full, col],
        out_shape=[jax.ShapeDtypeStruct((m, D_MODEL), F32),
                   jax.ShapeDtypeStruct((m, D_FF), F32)],
        scratch_shapes=[pltpu.VMEM((m, D_MODEL), BF16)],
        compiler_params=_params("arbitrary"),
        name="conv_ffn_sample",
    )(y, g, wg, wu, wd, cw, cb, s0, s1)


def _block_diag_q(q_ref, col0, hkv, grp, s):
    blocks = []
    for h in range(hkv):
        qh = jnp.concatenate(
            [q_ref[:, col0 + HEAD_DIM * (grp * h + j):col0 + HEAD_DIM * (grp * h + j + 1)]
             for j in range(grp)], axis=0)
        pieces = []
        if h > 0:
            pieces.append(jnp.zeros((grp * s, HEAD_DIM * h), F32))
        pieces.append(qh)
        if h < hkv - 1:
            pieces.append(jnp.zeros((grp * s, HEAD_DIM * (hkv - 1 - h)), F32))
        blocks.append(jnp.concatenate(pieces, axis=1) if len(pieces) > 1 else qh)
    return jnp.concatenate(blocks, axis=0).astype(BF16)


def _new_rows_t(x, s):
    pad = jnp.zeros((V7X_LANES - s, x.shape[1]), F32)
    return jnp.concatenate([pad, x], axis=0).T


def _cached_attend(qbd, ck_t, kn_t, cv_t, vn_t, d, s, sink_col):
    r_, lc = qbd.shape[0], ck_t.shape[1]
    sc = jnp.dot(qbd, ck_t.astype(BF16), preferred_element_type=F32)
    sn = jnp.dot(qbd, kn_t.astype(BF16), preferred_element_type=F32)
    ic = lax.broadcasted_iota(jnp.int32, (r_, lc), 0) & (s - 1)
    c = lax.broadcasted_iota(jnp.int32, (r_, lc), 1)
    i_n = lax.broadcasted_iota(jnp.int32, (r_, V7X_LANES), 0) & (s - 1)
    j = lax.broadcasted_iota(jnp.int32, (r_, V7X_LANES), 1) - (V7X_LANES - s)
    valid_c = c >= ic
    valid_n = (j >= 0) & (j <= i_n)
    if d > 1:
        valid_c = valid_c & ((c & (d - 1)) == (ic & (d - 1)))
        valid_n = valid_n & ((j & (d - 1)) == (i_n & (d - 1)))
    sc = jnp.where(valid_c, sc, NEG_INF)
    sn = jnp.where(valid_n, sn, NEG_INF)
    m = jnp.maximum(jnp.max(sc, axis=1, keepdims=True), jnp.max(sn, axis=1, keepdims=True))
    if sink_col is not None:
        m = jnp.maximum(m, sink_col)
    pc = jnp.exp2(sc - m)
    pn = jnp.exp2(sn - m)
    l = jnp.sum(pc, axis=1, keepdims=True) + jnp.sum(pn, axis=1, keepdims=True)
    if sink_col is not None:
        l = l + jnp.exp2(sink_col - m)
    nt = (((1,), (1,)), ((), ()))
    o = (lax.dot_general(pc.astype(BF16), cv_t.astype(BF16), nt, preferred_element_type=F32)
         + lax.dot_general(pn.astype(BF16), vn_t.astype(BF16), nt, preferred_element_type=F32)) / l
    return o, (m + jnp.log2(l)) * LN2


def _diag_heads(o, hkv, grp, s):
    pieces = []
    for h in range(hkv):
        for j in range(grp):
            r0 = (h * grp + j) * s
            pieces.append(o[r0:r0 + s, HEAD_DIM * h:HEAD_DIM * (h + 1)])
    return jnp.concatenate(pieces, axis=1)


def _store_shifted(out_ref, off, c_t, n_t, s):
    lc = c_t.shape[1]
    rolled = pltpu.roll(c_t, lc - s, 1)
    lane = lax.broadcasted_iota(jnp.int32, n_t.shape, 1)
    if lc > V7X_LANES:
        out_ref[0, :, off:off + lc - V7X_LANES] = rolled[:, :lc - V7X_LANES]
    out_ref[0, :, off + lc - V7X_LANES:off + lc] = jnp.where(
        lane < V7X_LANES - s, rolled[:, lc - V7X_LANES:], n_t)


def _sample_a_kernel(q_ref, kn_ref, vn_ref, ck_ref, cv_ref, sink_ref, o_ref, ko_ref, vo_ref, *, s):
    hkv, grp = A_KV_HEADS, A_Q_HEADS // A_KV_HEADS
    kn_t, vn_t = _new_rows_t(kn_ref[...], s), _new_rows_t(vn_ref[...], s)
    ck_t, cv_t = ck_ref[0], cv_ref[0]
    qbd = _block_diag_q(q_ref, 0, hkv, grp, s)
    sink_col = jnp.concatenate(
        [jnp.full((s, 1), sink_ref[hq] * LOG2E, F32) for hq in range(hkv * grp)], axis=0)
    o, _ = _cached_attend(qbd, ck_t, kn_t, cv_t, vn_t, 1, s, sink_col)
    o_ref[...] = _diag_heads(o, hkv, grp, s)
    _store_shifted(ko_ref, 0, ck_t, kn_t, s)
    _store_shifted(vo_ref, 0, cv_t, vn_t, s)


def _sample_a_call(q, kn, vn, ck_t, cv_t, sinks, *, s):
    nb, c, lc = ck_t.shape
    row = lambda w: pl.BlockSpec((s, w), lambda b: (b, 0))
    cache = pl.BlockSpec((1, c, lc), lambda b: (b, 0, 0))
    return pl.pallas_call(
        functools.partial(_sample_a_kernel, s=s),
        grid=(nb,),
        in_specs=[row(q.shape[1]), row(c), row(c), cache, cache,
                  pl.BlockSpec(memory_space=pltpu.SMEM)],
        out_specs=[row(q.shape[1]), cache, cache],
        out_shape=[jax.ShapeDtypeStruct(q.shape, F32),
                   jax.ShapeDtypeStruct(ck_t.shape, F32),
                   jax.ShapeDtypeStruct(cv_t.shape, F32)],
        compiler_params=_params("parallel"),
        name="sample_attn_a",
    )(q, kn, vn, ck_t, cv_t, sinks)


def _sample_b_kernel(q_ref, kn_ref, vn_ref, ck_ref, cv_ref, o_ref, ko_ref, vo_ref, *, s):
    hkv, grp = B_KV_HEADS, B_Q_HEADS // B_KV_HEADS
    ckv = hkv * HEAD_DIM
    outs, lses = [], []
    off = 0
    for g, (w, d) in enumerate(B_PATTERNS):
        lc = w
        kn_t = _new_rows_t(kn_ref[:, ckv * g:ckv * (g + 1)], s)
        vn_t = _new_rows_t(vn_ref[:, ckv * g:ckv * (g + 1)], s)
        ck_t = ck_ref[0, :, off:off + lc]
        cv_t = cv_ref[0, :, off:off + lc]
        qbd = _block_diag_q(q_ref, B_Q_HEADS * HEAD_DIM * g, hkv, grp, s)
        o, lse = _cached_attend(qbd, ck_t, kn_t, cv_t, vn_t, d, s, None)
        outs.append(o)
        lses.append(lse)
        _store_shifted(ko_ref, off, ck_t, kn_t, s)
        _store_shifted(vo_ref, off, cv_t, vn_t, s)
        off += lc
    mx = jnp.maximum(jnp.maximum(lses[0], lses[1]), lses[2])
    es = [jnp.exp(l - mx) for l in lses]
    den = es[0] + es[1] + es[2]
    comb = (es[0] / den) * outs[0] + (es[1] / den) * outs[1] + (es[2] / den) * outs[2]
    o_ref[...] = _diag_heads(comb, hkv, grp, s)


def _sample_b_call(q, kn, vn, ck_t, cv_t, *, s):
    nb, c, lb = ck_t.shape
    row = lambda w: pl.BlockSpec((s, w), lambda b: (b, 0))
    cache = pl.BlockSpec((1, c, lb), lambda b: (b, 0, 0))
    co = B_Q_HEADS * HEAD_DIM
    return pl.pallas_call(
        functools.partial(_sample_b_kernel, s=s),
        grid=(nb,),
        in_specs=[row(q.shape[1]), row(kn.shape[1]), row(vn.shape[1]), cache, cache],
        out_specs=[row(co), cache, cache],
        out_shape=[jax.ShapeDtypeStruct((q.shape[0], co), F32),
                   jax.ShapeDtypeStruct(ck_t.shape, F32),
                   jax.ShapeDtypeStruct(cv_t.shape, F32)],
        compiler_params=_params("parallel"),
        name="sample_attn_b",
    )(q, kn, vn, ck_t, cv_t)


def _rope_tables(pos):
    inv_freq = ROPE_THETA ** (-jnp.arange(HALF, dtype=F32) / HALF)
    ang = pos.astype(F32)[:, None] * inv_freq[None, :]
    cos, sin = jnp.cos(ang), jnp.sin(ang)
    return jnp.tile(cos, (1, 4)), jnp.tile(jnp.concatenate([-sin, sin], axis=1), (1, 2))


def _cache_t(cache):
    b, l, h, dh = cache.shape
    return jnp.transpose(cache, (0, 2, 3, 1)).reshape(b, h * dh, l)


def _cache_from_t(x, h):
    b, c, l = x.shape
    return jnp.transpose(x.reshape(b, h, c // h, l), (0, 3, 1, 2))[None]


def kernel(x_prompt, x_sample, cache_a_k, cache_a_v, cache_b_k, cache_b_v, state_ffn_conv,
           attn_norm, ffn_norm, a_w_qkv, a_q_norm, a_k_norm, a_sinks, a_w_o,
           b_w_qkv, b_q_norm, b_k_norm, b_w_o,
           ffn_w_gate, ffn_w_up, ffn_conv_w, ffn_conv_b, ffn_w_down):
    nb, seq, _ = x_prompt.shape
    ns, dec, _ = x_sample.shape
    mp, ms = nb * seq, ns * dec

    cos_p, sin_p = _rope_tables(jnp.tile(jnp.arange(seq, dtype=jnp.int32), nb))
    cos_s, sin_s = _rope_tables(jnp.tile(PAST_LEN + jnp.arange(dec, dtype=jnp.int32), ns))
    idx = jnp.arange(V7X_MXU_DIM, dtype=jnp.int32) // HEAD_DIM
    ones_blk = (idx[:, None] == idx[None, :]).astype(BF16)

    yp = x_prompt.reshape(mp, D_MODEL)
    ys = x_sample.reshape(ms, D_MODEL)
    row = lambda a: a.reshape(1, -1)

    def head_gain(gv, n_heads_per_tile):
        return jnp.tile(gv, n_heads_per_tile).reshape(1, -1)

    wqkv = _cast_call(a_w_qkv, 0)
    wo = _cast_call(a_w_o, 0)
    tn = wqkv.shape[1] // N_QKV_TILES
    qg, kg = head_gain(a_q_norm[0], tn // HEAD_DIM), head_gain(a_k_norm[0], tn // HEAD_DIM)
    g_attn = row(attn_norm[0])
    grp_a = A_Q_HEADS // A_KV_HEADS

    q, k, v = _qkv_call(yp, g_attn, wqkv, cos_p, sin_p, qg, kg, ones_blk,
                        n_q=A_Q_HEADS, n_kv=A_KV_HEADS, tm=TM_QKV, q_dtype=BF16)
    k3, v3 = k.reshape(nb, seq, -1), v.reshape(nb, seq, -1)
    o = _band_call(q.reshape(nb, seq, -1), k3, v3, a_sinks[0], hkv=A_KV_HEADS, grp=grp_a)
    yp = _wo_call(yp, o.reshape(mp, -1), wo, tm=TM_PROJ)
    keep = min(A_WINDOW, seq)
    ca = A_KV_HEADS * HEAD_DIM
    a_k_prompt = _cache_from_t(_keep_t_call(k3, [keep], ca), A_KV_HEADS)
    a_v_prompt = _cache_from_t(_keep_t_call(v3, [keep], ca), A_KV_HEADS)

    qs, ks, vs = _qkv_call(ys, g_attn, wqkv, cos_s, sin_s, qg, kg, ones_blk,
                           n_q=A_Q_HEADS, n_kv=A_KV_HEADS, tm=ms, q_dtype=F32)
    os_, ako, avo = _sample_a_call(qs, ks, vs, _cache_t(cache_a_k[0]), _cache_t(cache_a_v[0]),
                                   a_sinks[0], s=dec)
    ys = _wo_call(ys, os_, wo, tm=ms)
    a_k_sample = _cache_from_t(ako, A_KV_HEADS)
    a_v_sample = _cache_from_t(avo, A_KV_HEADS)

    conv_p, conv_s = [], []

    def ffn(layer, yp, ys):
        wg = _cast_call(ffn_w_gate, layer)
        wu = _cast_call(ffn_w_up, layer)
        wd = _cast_call(ffn_w_down, layer)
        g_ffn = row(ffn_norm[layer])
        cw, cb = ffn_conv_w[layer], row(ffn_conv_b[layer])
        yp, tail = _ffn_prompt_call(yp, g_ffn, wg, wu, wd, cw, cb, seq=seq)
        tiles = seq // TM_FFN
        conv_p.append(tail[tiles - 1::tiles, V7X_SUBLANES - (CONV_W - 1):, :])
        st = state_ffn_conv[layer]
        s0 = jnp.repeat(st[:, 0, :], dec, axis=0)
        s1 = jnp.repeat(st[:, 1, :], dec, axis=0)
        ys, gate_s = _ffn_sample_call(ys, g_ffn, wg, wu, wd, cw, cb, s0, s1, seq=dec)
        conv_s.append(gate_s.reshape(ns, dec, D_FF)[:, dec - (CONV_W - 1):, :])
        return yp, ys

    yp, ys = ffn(0, yp, ys)

    wqkv = _cast_call(b_w_qkv, 0)
    wo = _cast_call(b_w_o, 0)
    tn = wqkv.shape[1] // N_QKV_TILES
    qg, kg = head_gain(b_q_norm[0], tn // HEAD_DIM), head_gain(b_k_norm[0], tn // HEAD_DIM)
    g_attn = row(attn_norm[1])
    n_grp = len(B_PATTERNS)
    grp_b = B_Q_HEADS // B_KV_HEADS
    nqb, nkvb = n_grp * B_Q_HEADS, n_grp * B_KV_HEADS
    cb_ = B_KV_HEADS * HEAD_DIM

    q, k, v = _qkv_call(yp, g_attn, wqkv, cos_p, sin_p, qg, kg, ones_blk,
                        n_q=nqb, n_kv=nkvb, tm=TM_QKV, q_dtype=F32, lane_major=True)
    outs, lses = [], []
    for g, (w, d) in enumerate(B_PATTERNS):
        assert w // d == BLOCK
        og, lg = _dil_call(q, k, v, nb=nb, hkv=B_KV_HEADS, grp=grp_b, d=d, group=g)
        outs.append(og)
        lses.append(lg)
    yp = _wo_comb_call(yp, outs, lses, wo, tm=TM_PROJ // 2)
    keeps = [min(w, seq) for w, _ in B_PATTERNS]
    k4 = k.reshape(k.shape[0], nb, seq, V7X_LANES)
    v4 = v.reshape(v.shape[0], nb, seq, V7X_LANES)
    b_k_prompt = _cache_from_t(_keep_t_call(k4, keeps, cb_, lane_major=True), B_KV_HEADS)
    b_v_prompt = _cache_from_t(_keep_t_call(v4, keeps, cb_, lane_major=True), B_KV_HEADS)

    qs, ks, vs = _qkv_call(ys, g_attn, wqkv, cos_s, sin_s, qg, kg, ones_blk,
                           n_q=nqb, n_kv=nkvb, tm=ms, q_dtype=F32)
    os_, bko, bvo = _sample_b_call(qs, ks, vs, _cache_t(cache_b_k[0]), _cache_t(cache_b_v[0]),
                                   s=dec)
    ys = _wo_call(ys, os_, wo, tm=ms)
    b_k_sample = _cache_from_t(bko, B_KV_HEADS)
    b_v_sample = _cache_from_t(bvo, B_KV_HEADS)

    yp, ys = ffn(1, yp, ys)

    return (yp.reshape(nb, seq, D_MODEL), ys.reshape(ns, dec, D_MODEL),
            a_k_prompt, a_v_prompt, a_k_sample, a_v_sample,
            b_k_prompt, b_v_prompt, b_k_sample, b_v_sample,
            jnp.stack(conv_p), jnp.stack(conv_s))
```

```python
import functools

import jax
import jax.numpy as jnp
from jax import lax
from jax.experimental import pallas as pl
from jax.experimental.pallas import tpu as pltpu

F32 = jnp.float32
BF16 = jnp.bfloat16

D_MODEL = 2048
HEAD_DIM = 64
HALF = HEAD_DIM // 2
ROPE_THETA = 10000.0
NORM_EPS = 1e-6
BLOCK = 128
PAST_LEN = 16384
A_WINDOW = 128
A_Q_HEADS = 32
A_KV_HEADS = 8
B_PATTERNS = ((128, 1), (512, 4), (2048, 16))
B_Q_HEADS = 16
B_KV_HEADS = 4
D_FF = 5632
CONV_W = 3
NEG_INF = -1e30
LOG2E = 1.4426950408889634
LN2 = 0.6931471805599453
Q_SCALE = HEAD_DIM ** -0.5 * LOG2E

V7X_MXU_DIM = 256
V7X_LANES = 128
V7X_SUBLANES = 8
VMEM_LIMIT = 56 * 1024 * 1024

TM_PROJ = 512
TM_QKV = 1024
TM_FFN = 1024
TF_FFN = 512
TR_CAST = 256
PROJ_CHUNK_ROWS = 256
FFN_ROW_CHUNKS = 1
N_QKV_TILES = 6
DIL_MAX_ROWS = 1024
ATTN_DEPTH = 2
DIL_TILES_PER_TRIP = 4


def _params(*sem):
    return pltpu.CompilerParams(dimension_semantics=sem, vmem_limit_bytes=VMEM_LIMIT)


def _rms(x, g):
    ms = jnp.mean(x * x, axis=-1, keepdims=True)
    return x * lax.rsqrt(ms + NORM_EPS) * g


def _cast_kernel(w_ref, o_ref):
    o_ref[...] = w_ref[0].astype(BF16)


def _cast_call(w, layer):
    _, r, c = w.shape
    tr = TR_CAST
    assert r % tr == 0
    return pl.pallas_call(
        _cast_kernel,
        grid=(r // tr,),
        in_specs=[pl.BlockSpec((1, tr, c), lambda i: (layer, i, 0))],
        out_specs=pl.BlockSpec((tr, c), lambda i: (i, 0)),
        out_shape=jax.ShapeDtypeStruct((r, c), BF16),
        compiler_params=_params("parallel"),
        name="cast_bf16",
    )(w)


def _head_norm_rope(a, gain, ones_blk, cos, sin):
    tn = a.shape[1]
    x2 = a * a
    hi = x2.astype(BF16)
    lo = (x2 - hi.astype(F32)).astype(BF16)
    parts = []
    for c in range(tn // V7X_MXU_DIM):
        sl = slice(V7X_MXU_DIM * c, V7X_MXU_DIM * (c + 1))
        parts.append(jnp.dot(hi[:, sl], ones_blk, preferred_element_type=F32)
                     + jnp.dot(lo[:, sl], ones_blk, preferred_element_type=F32))
    ss = jnp.concatenate(parts, axis=1)
    y = a * lax.rsqrt(ss * (1.0 / HEAD_DIM) + NORM_EPS) * gain
    lane = lax.broadcasted_iota(jnp.int32, y.shape, 1)
    first_half = (lane & (HEAD_DIM - 1)) < HALF
    partner = jnp.where(first_half, pltpu.roll(y, tn - HALF, 1), pltpu.roll(y, HALF, 1))
    reps = tn // V7X_LANES
    return y * jnp.tile(cos, (1, reps)) + partner * jnp.tile(sin, (1, reps))


def _store_cols(ref, rows, val, lane_major):
    if not lane_major:
        ref[rows, :] = val.astype(ref.dtype)
        return
    for t in range(val.shape[1] // V7X_LANES):
        ref[t, rows, :] = val[:, V7X_LANES * t:V7X_LANES * (t + 1)].astype(ref.dtype)


def _qkv_kernel(x_ref, g_ref, w_ref, cos_ref, sin_ref, qg_ref, kg_ref, ones_ref,
                q_ref, k_ref, v_ref, xn_ref, *, n_q_tiles, lane_major, row_chunks):
    n = pl.program_id(1)

    @pl.when(n == 0)
    def _():
        xn_ref[...] = _rms(x_ref[...], g_ref[...]).astype(BF16)

    rc = xn_ref.shape[0] // row_chunks

    def project(out_ref, epilogue):
        chunk = lambda c: slice(rc * c, rc * (c + 1))
        matmul = lambda c: jnp.dot(xn_ref[chunk(c), :], w_ref[...], preferred_element_type=F32)
        acc_next = matmul(0)
        for c in range(row_chunks):
            acc = acc_next
            if c + 1 < row_chunks:
                acc_next = matmul(c + 1)
            _store_cols(out_ref, chunk(c), epilogue(acc, chunk(c)), lane_major)

    def norm_rope(gain_ref, scale):
        def fn(acc, rows):
            r = _head_norm_rope(acc, gain_ref[...], ones_ref[...], cos_ref[rows, :],
                                sin_ref[rows, :])
            return r * scale if scale != 1.0 else r
        return fn

    @pl.when(n < n_q_tiles)
    def _():
        project(q_ref, norm_rope(qg_ref, Q_SCALE))

    @pl.when(n == n_q_tiles)
    def _():
        project(k_ref, norm_rope(kg_ref, 1.0))

    @pl.when(n == n_q_tiles + 1)
    def _():
        project(v_ref, lambda acc, rows: acc)


def _qkv_call(x, g, w, cos, sin, qg, kg, ones_blk, *, n_q, n_kv, tm, q_dtype, lane_major=False):
    m = x.shape[0]
    ncols = w.shape[1]
    tn = ncols // N_QKV_TILES
    n_q_tiles = (n_q * HEAD_DIM) // tn
    assert n_q_tiles * tn == n_q * HEAD_DIM and n_kv * HEAD_DIM == tn and tn % V7X_MXU_DIM == 0
    assert m % tm == 0
    last_q = n_q_tiles - 1
    if lane_major:
        lt = tn // V7X_LANES
        out_specs = [
            pl.BlockSpec((lt, tm, V7X_LANES), lambda i, n: (jnp.minimum(n, last_q), i, 0)),
            pl.BlockSpec((lt, tm, V7X_LANES), lambda i, n: (0, i, 0)),
            pl.BlockSpec((lt, tm, V7X_LANES), lambda i, n: (0, i, 0)),
        ]
        out_shape = [
            jax.ShapeDtypeStruct((n_q_tiles * lt, m, V7X_LANES), q_dtype),
            jax.ShapeDtypeStruct((lt, m, V7X_LANES), F32),
            jax.ShapeDtypeStruct((lt, m, V7X_LANES), F32),
        ]
    else:
        out_specs = [
            pl.BlockSpec((tm, tn), lambda i, n: (i, jnp.minimum(n, last_q))),
            pl.BlockSpec((tm, tn), lambda i, n: (i, 0)),
            pl.BlockSpec((tm, tn), lambda i, n: (i, 0)),
        ]
        out_shape = [
            jax.ShapeDtypeStruct((m, n_q * HEAD_DIM), q_dtype),
            jax.ShapeDtypeStruct((m, tn), F32),
            jax.ShapeDtypeStruct((m, tn), F32),
        ]
    return pl.pallas_call(
        functools.partial(_qkv_kernel, n_q_tiles=n_q_tiles, lane_major=lane_major,
                          row_chunks=max(1, tm // PROJ_CHUNK_ROWS)),
        grid=(m // tm, N_QKV_TILES),
        in_specs=[
            pl.BlockSpec((tm, D_MODEL), lambda i, n: (i, 0)),
            pl.BlockSpec((1, D_MODEL), lambda i, n: (0, 0)),
            pl.BlockSpec((D_MODEL, tn), lambda i, n: (0, n)),
            pl.BlockSpec((tm, V7X_LANES), lambda i, n: (i, 0)),
            pl.BlockSpec((tm, V7X_LANES), lambda i, n: (i, 0)),
            pl.BlockSpec((1, tn), lambda i, n: (0, 0)),
            pl.BlockSpec((1, tn), lambda i, n: (0, 0)),
            pl.BlockSpec((V7X_MXU_DIM, V7X_MXU_DIM), lambda i, n: (0, 0)),
        ],
        out_specs=out_specs,
        out_shape=out_shape,
        scratch_shapes=[pltpu.VMEM((tm, D_MODEL), BF16)],
        compiler_params=_params("parallel", "arbitrary"),
        name="qkv_proj",
    )(x, g, w, cos, sin, qg, kg, ones_blk)


GRP = 4


def _band_mask_t(first):
    shape = (2 * BLOCK, BLOCK)
    kj = lax.broadcasted_iota(jnp.int32, shape, 0)
    qi = lax.broadcasted_iota(jnp.int32, shape, 1)
    seen = (kj >= qi) & (kj <= qi + BLOCK) & ((kj >= BLOCK) | jnp.logical_not(first))
    return jnp.where(seen, 0.0, NEG_INF)


def _kv_tile_forms(kt, vt):
    lo = lax.broadcasted_iota(jnp.int32, kt.shape, 1) < HEAD_DIM
    kt_sw = pltpu.roll(kt, HEAD_DIM, 1)
    k_forms = [(jnp.where(lo, kt, 0.0).astype(BF16), jnp.where(lo, 0.0, kt_sw).astype(BF16)),
               (jnp.where(lo, kt_sw, 0.0).astype(BF16), jnp.where(lo, 0.0, kt).astype(BF16))]
    return k_forms, vt.T.astype(BF16)


def _head_scores(k_lo, k_hi, q_a, q_b):
    nt = (((1,), (1,)), ((), ()))
    qpair = jnp.concatenate([q_a, q_b], axis=0)
    return jnp.concatenate([lax.dot_general(k_lo, qpair, nt, preferred_element_type=F32),
                            lax.dot_general(k_hi, qpair, nt, preferred_element_type=F32)], axis=1)


def _head_softmax_pv(s, mask4, v_t, sinks, want_lse):
    s = s + mask4
    m = jnp.max(s, axis=0, keepdims=True)
    if sinks is not None:
        sk = jnp.concatenate([jnp.full((1, BLOCK), sinks[j] * LOG2E, F32)
                              for j in (0, 2, 1, 3)], axis=1)
        m = jnp.maximum(m, sk)
    p = jnp.exp2(s - m)
    l = jnp.sum(p, axis=0, keepdims=True)
    if sinks is not None:
        l = l + jnp.exp2(sk - m)
    o_t = jnp.dot(v_t, p.astype(BF16), preferred_element_type=F32) / l
    lse_t = jnp.broadcast_to((m + jnp.log2(l)) * LN2, o_t.shape) if want_lse else None
    o_tiles, l_tiles = [], []
    for u in range(2):
        c0, c1 = slice(BLOCK * u, BLOCK * (u + 1)), slice(BLOCK * (2 + u), BLOCK * (3 + u))
        o_tiles.append(jnp.concatenate([o_t[:, c0], o_t[:, c1]], axis=0).T)
        if want_lse:
            l_tiles.append(jnp.concatenate([lse_t[:, c0], lse_t[:, c1]], axis=0).T)
    return o_tiles, l_tiles


def _attend_tiles(tiles, mask, want_lse):
    mask4 = jnp.tile(mask, (1, GRP))
    jobs = [(ti, e) for ti in range(len(tiles)) for e in range(2)]
    forms = {}

    def scores(job):
        ti, e = job
        load_kv, load_q = tiles[ti][:2]
        if ti not in forms:
            forms[ti] = _kv_tile_forms(*load_kv())
        k_lo, k_hi = forms[ti][0][e]
        return _head_scores(k_lo, k_hi, load_q(2 * e), load_q(2 * e + 1))

    pending = [scores(job) for job in jobs[:ATTN_DEPTH]]
    for idx, (ti, e) in enumerate(jobs):
        s = pending.pop(0)
        if idx + ATTN_DEPTH < len(jobs):
            pending.append(scores(jobs[idx + ATTN_DEPTH]))
        sink_of, store = tiles[ti][2:]
        v_t = forms[ti][1][HEAD_DIM * e:HEAD_DIM * (e + 1)]
        sinks = None if sink_of is None else [sink_of(GRP * e + j) for j in range(GRP)]
        o_tiles, l_tiles = _head_softmax_pv(s, mask4, v_t, sinks, want_lse)
        for u in range(2):
            store(2 * e + u, o_tiles[u], l_tiles[u] if want_lse else None)


def _band_kernel(q_ref, kc_ref, kp_ref, vc_ref, vp_ref, sink_ref, o_ref, *, kv_tiles):
    mask = _band_mask_t(pl.program_id(1) == 0)
    lanes = lambda t: slice(V7X_LANES * t, V7X_LANES * (t + 1))

    def tile(t):
        def load_kv():
            return (jnp.concatenate([kp_ref[0, :, lanes(t)], kc_ref[0, :, lanes(t)]], axis=0),
                    jnp.concatenate([vp_ref[0, :, lanes(t)], vc_ref[0, :, lanes(t)]], axis=0))

        def store(u, o_tile, _):
            o_ref[0, :, lanes(GRP * t + u)] = o_tile.astype(o_ref.dtype)

        return (load_kv, lambda u: q_ref[0, :, lanes(GRP * t + u)],
                lambda h: sink_ref[2 * GRP * t + h], store)

    _attend_tiles([tile(t) for t in range(kv_tiles)], mask, False)


def _band_call(q, k, v, sinks, *, hkv, grp):
    b, seq, cq = q.shape
    ck = hkv * HEAD_DIM
    assert grp == GRP and hkv % 2 == 0
    cur = lambda bi, c: (bi, c, 0)
    prev = lambda bi, c: (bi, jnp.maximum(c - 1, 0), 0)
    return pl.pallas_call(
        functools.partial(_band_kernel, kv_tiles=hkv // 2),
        grid=(b, seq // BLOCK),
        in_specs=[
            pl.BlockSpec((1, BLOCK, cq), cur),
            pl.BlockSpec((1, BLOCK, ck), cur),
            pl.BlockSpec((1, BLOCK, ck), prev),
            pl.BlockSpec((1, BLOCK, ck), cur),
            pl.BlockSpec((1, BLOCK, ck), prev),
            pl.BlockSpec(memory_space=pltpu.SMEM),
        ],
        out_specs=pl.BlockSpec((1, BLOCK, cq), cur),
        out_shape=jax.ShapeDtypeStruct((b, seq, cq), BF16),
        compiler_params=_params("parallel", "arbitrary"),
        name="band_attn",
    )(q, k, k, v, v, sinks)


def _dil_kernel(q_ref, kc_ref, kp_ref, vc_ref, vp_ref, o_ref, lse_ref, *, kv_tiles, grp, d):
    mask = _band_mask_t(pl.program_id(1) == 0)

    def tile(r, t):
        rs = pl.ds(r, BLOCK, stride=d) if d > 1 else slice(None)

        def load_kv():
            return (jnp.concatenate([kp_ref[t, rs, :], kc_ref[t, rs, :]], axis=0),
                    jnp.concatenate([vp_ref[t, rs, :], vc_ref[t, rs, :]], axis=0))

        def store(u, o_tile, l_tile):
            o_ref[grp * t + u, rs, :] = o_tile
            lse_ref[grp * t + u, rs, :] = l_tile

        return load_kv, lambda u: q_ref[grp * t + u, rs, :].astype(BF16), None, store

    unroll = min(d, max(1, DIL_TILES_PER_TRIP // kv_tiles))

    def body(i, carry):
        _attend_tiles([tile(i * unroll + rr, t) for rr in range(unroll) for t in range(kv_tiles)],
                      mask, True)
        return carry

    if d > unroll:
        lax.fori_loop(0, d // unroll, body, 0)
    else:
        body(0, 0)


def _dil_call(q, k, v, *, nb, hkv, grp, d, group):
    m = q.shape[1]
    seq = m // nb
    rows = BLOCK * d
    hsplit = max(1, rows // DIL_MAX_ROWS)
    kv_tiles = hkv // 2 // hsplit
    q_tiles = kv_tiles * grp
    nchunk = seq // rows
    assert seq % rows == 0 and kv_tiles * 2 * hsplit == hkv and grp % 2 == 0
    cur = lambda bi, c, hp: (group * hsplit + hp, bi * nchunk + c, 0)
    prev = lambda bi, c, hp: (group * hsplit + hp, bi * nchunk + jnp.maximum(c - 1, 0), 0)
    o_spec = pl.BlockSpec((q_tiles, rows, V7X_LANES), lambda bi, c, hp: (hp, bi * nchunk + c, 0))
    o_shape = jax.ShapeDtypeStruct((q_tiles * hsplit, m, V7X_LANES), F32)
    return pl.pallas_call(
        functools.partial(_dil_kernel, kv_tiles=kv_tiles, grp=grp, d=d),
        grid=(nb, nchunk, hsplit),
        in_specs=[
            pl.BlockSpec((q_tiles, rows, V7X_LANES), cur),
            pl.BlockSpec((kv_tiles, rows, V7X_LANES), cur),
            pl.BlockSpec((kv_tiles, rows, V7X_LANES), prev),
            pl.BlockSpec((kv_tiles, rows, V7X_LANES), cur),
            pl.BlockSpec((kv_tiles, rows, V7X_LANES), prev),
        ],
        out_specs=[o_spec, o_spec],
        out_shape=[o_shape, o_shape],
        compiler_params=_params("parallel", "arbitrary", "arbitrary"),
        name="dilated_attn",
    )(q, k, k, v, v)


def _keep_t_kernel(x_ref, o_ref):
    o_ref[0] = x_ref[0].T


def _keep_t_lane_major_kernel(x_ref, o_ref, *, keeps):
    g = pl.program_id(2)
    seq = x_ref.shape[2]
    off = 0
    for k, keep in enumerate(keeps):
        @pl.when(g == k)
        def _(keep=keep, off=off):
            o_ref[0, :, off:off + keep] = x_ref[0, 0, seq - keep:, :].T
        off += keep


def _keep_t_call(x, keeps, c, *, lane_major=False):
    b, seq = x.shape[1:3] if lane_major else x.shape[:2]
    starts, first_rb = [], []
    n = 0
    for keep in keeps:
        assert keep % BLOCK == 0 and seq % BLOCK == 0
        starts.append(n)
        first_rb.append((seq - keep) // BLOCK)
        n += keep // BLOCK

    def src_block(j):
        rb = jnp.int32(0)
        cb = jnp.int32(0)
        for g in range(len(keeps)):
            inside = j >= starts[g]
            rb = jnp.where(inside, first_rb[g] + j - starts[g], rb)
            cb = jnp.where(inside, g, cb)
        return rb, cb

    if lane_major:
        ct = c // V7X_LANES
        grid = (b, ct, len(keeps))
        in_spec = pl.BlockSpec((1, 1, seq, V7X_LANES), lambda bi, t, g: (g * ct + t, bi, 0, 0))
        out_spec = pl.BlockSpec((1, V7X_LANES, n * BLOCK), lambda bi, t, g: (bi, t, 0))
        kern = functools.partial(_keep_t_lane_major_kernel, keeps=tuple(keeps))
        sem = ("parallel", "parallel", "arbitrary")
    else:
        grid = (b, n)

        def src(bi, j):
            rb, cb = src_block(j)
            return bi, rb, cb

        in_spec = pl.BlockSpec((1, BLOCK, c), src)
        out_spec = pl.BlockSpec((1, c, BLOCK), lambda bi, j: (bi, 0, j))
        kern = _keep_t_kernel
        sem = ("parallel", "parallel")
    return pl.pallas_call(
        kern,
        grid=grid,
        in_specs=[in_spec],
        out_specs=out_spec,
        out_shape=jax.ShapeDtypeStruct((b, c, n * BLOCK), F32),
        compiler_params=_params(*sem),
        name="keep_rows_t",
    )(x)


def _wo_kernel(y_ref, o_ref, w_ref, out_ref):
    out_ref[...] = y_ref[...] + jnp.dot(o_ref[...].astype(BF16), w_ref[...],
                                        preferred_element_type=F32)


def _wo_comb_kernel(y_ref, o0_ref, o1_ref, o2_ref, l0_ref, l1_ref, l2_ref, w_ref, out_ref):
    tiles = []
    for t in range(o0_ref.shape[0]):
        l0, l1, l2 = l0_ref[t], l1_ref[t], l2_ref[t]
        mx = jnp.maximum(jnp.maximum(l0, l1), l2)
        e0, e1, e2 = jnp.exp(l0 - mx), jnp.exp(l1 - mx), jnp.exp(l2 - mx)
        den = e0 + e1 + e2
        comb = (e0 / den) * o0_ref[t] + (e1 / den) * o1_ref[t] + (e2 / den) * o2_ref[t]
        tiles.append(comb.astype(BF16))
    comb = jnp.concatenate(tiles, axis=1)
    out_ref[...] = y_ref[...] + jnp.dot(comb, w_ref[...], preferred_element_type=F32)


def _wo_call(y, o, w, *, tm):
    m = y.shape[0]
    c = o.shape[1]
    return pl.pallas_call(
        _wo_kernel,
        grid=(m // tm,),
        in_specs=[
            pl.BlockSpec((tm, D_MODEL), lambda i: (i, 0)),
            pl.BlockSpec((tm, c), lambda i: (i, 0)),
            pl.BlockSpec((c, D_MODEL), lambda i: (0, 0)),
        ],
        out_specs=pl.BlockSpec((tm, D_MODEL), lambda i: (i, 0)),
        out_shape=jax.ShapeDtypeStruct((m, D_MODEL), F32),
        compiler_params=_params("parallel"),
        name="wo_proj",
    )(y, o, w)


def _wo_comb_call(y, os_, ls_, w, *, tm):
    m = y.shape[0]
    c = w.shape[0]
    blk = pl.BlockSpec((c // V7X_LANES, tm, V7X_LANES), lambda i: (0, i, 0))
    return pl.pallas_call(
        _wo_comb_kernel,
        grid=(m // tm,),
        in_specs=[pl.BlockSpec((tm, D_MODEL), lambda i: (i, 0))] + [blk] * 6
                 + [pl.BlockSpec((c, D_MODEL), lambda i: (0, 0))],
        out_specs=pl.BlockSpec((tm, D_MODEL), lambda i: (i, 0)),
        out_shape=jax.ShapeDtypeStruct((m, D_MODEL), F32),
        compiler_params=_params("parallel"),
        name="wo_comb_proj",
    )(y, *os_, *ls_, w)


def _ffn_tail(gate, g1, g2, up, cw_ref, cb_ref, wd_ref):
    conv = cb_ref[...] + cw_ref[0:1, :] * g2 + cw_ref[1:2, :] * g1 + cw_ref[2:3, :] * gate
    h = conv * jax.nn.sigmoid(conv) * up
    return jnp.dot(h.astype(BF16), wd_ref[...], preferred_element_type=F32)


def _ffn_prompt_kernel(y_ref, g_ref, wg_ref, wu_ref, wd_ref, cw_ref, cb_ref,
                       out_ref, tail_ref, xn_ref, carry_ref, *, tiles_per_seq):
    m = pl.program_id(0)
    f = pl.program_id(1)

    @pl.when(f == 0)
    def _():
        x = y_ref[...]
        xn_ref[...] = _rms(x, g_ref[...]).astype(BF16)
        out_ref[...] = x

    @pl.when(m % tiles_per_seq == 0)
    def _():
        carry_ref[f] = jnp.zeros(carry_ref.shape[1:], F32)

    rc = xn_ref.shape[0] // FFN_ROW_CHUNKS
    last = carry_ref[f]
    for c in range(FFN_ROW_CHUNKS):
        rows = slice(rc * c, rc * (c + 1))
        xn = xn_ref[rows, :]
        gate = jnp.dot(xn, wg_ref[...], preferred_element_type=F32)
        up = jnp.dot(xn, wu_ref[...], preferred_element_type=F32)
        row = lax.broadcasted_iota(jnp.int32, gate.shape, 0)
        g1 = jnp.where(row == 0, last[7:8, :], pltpu.roll(gate, 1, 0))
        g2 = jnp.where(row == 0, last[6:7, :],
                       jnp.where(row == 1, last[7:8, :], pltpu.roll(gate, 2, 0)))
        out_ref[rows, :] += _ffn_tail(gate, g1, g2, up, cw_ref, cb_ref, wd_ref)
        last = gate[rc - V7X_SUBLANES:, :]
    carry_ref[f] = last
    tail_ref[0] = last


def _ffn_prompt_call(y, g, wg, wu, wd, cw, cb, *, seq):
    m = y.shape[0]
    tm, tf = TM_FFN, TF_FFN
    nf = D_FF // tf
    assert m % tm == 0 and seq % tm == 0 and D_FF % tf == 0
    return pl.pallas_call(
        functools.partial(_ffn_prompt_kernel, tiles_per_seq=seq // tm),
        grid=(m // tm, nf),
        in_specs=[
            pl.BlockSpec((tm, D_MODEL), lambda i, f: (i, 0)),
            pl.BlockSpec((1, D_MODEL), lambda i, f: (0, 0)),
            pl.BlockSpec((D_MODEL, tf), lambda i, f: (0, f)),
            pl.BlockSpec((D_MODEL, tf), lambda i, f: (0, f)),
            pl.BlockSpec((tf, D_MODEL), lambda i, f: (f, 0)),
            pl.BlockSpec((CONV_W, tf), lambda i, f: (0, f)),
            pl.BlockSpec((1, tf), lambda i, f: (0, f)),
        ],
        out_specs=[
            pl.BlockSpec((tm, D_MODEL), lambda i, f: (i, 0)),
            pl.BlockSpec((1, V7X_SUBLANES, tf), lambda i, f: (i, 0, f)),
        ],
        out_shape=[
            jax.ShapeDtypeStruct((m, D_MODEL), F32),
            jax.ShapeDtypeStruct((m // tm, V7X_SUBLANES, D_FF), F32),
        ],
        scratch_shapes=[pltpu.VMEM((tm, D_MODEL), BF16),
                        pltpu.VMEM((nf, V7X_SUBLANES, tf), F32)],
        compiler_params=_params("arbitrary", "arbitrary"),
        name="conv_ffn_prompt",
    )(y, g, wg, wu, wd, cw, cb)


def _ffn_sample_kernel(y_ref, g_ref, wg_ref, wu_ref, wd_ref, cw_ref, cb_ref, s0_ref, s1_ref,
                       out_ref, gate_ref, xn_ref, *, seq):
    f = pl.program_id(0)

    @pl.when(f == 0)
    def _():
        x = y_ref[...]
        xn_ref[...] = _rms(x, g_ref[...]).astype(BF16)
        out_ref[...] = x

    xn = xn_ref[...]
    gate = jnp.dot(xn, wg_ref[...], preferred_element_type=F32)
    up = jnp.dot(xn, wu_ref[...], preferred_element_type=F32)
    t = lax.broadcasted_iota(jnp.int32, gate.shape, 0) & (seq - 1)
    s0, s1 = s0_ref[...], s1_ref[...]
    g1 = jnp.where(t == 0, s1, pltpu.roll(gate, 1, 0))
    g2 = jnp.where(t == 0, s0, jnp.where(t == 1, s1, pltpu.roll(gate, 2, 0)))
    out_ref[...] += _ffn_tail(gate, g1, g2, up, cw_ref, cb_ref, wd_ref)
    gate_ref[...] = gate


def _ffn_sample_call(y, g, wg, wu, wd, cw, cb, s0, s1, *, seq):
    m = y.shape[0]
    tf = TF_FFN
    nf = D_FF // tf
    full = pl.BlockSpec((m, D_MODEL), lambda f: (0, 0))
    col = pl.BlockSpec((m, tf), lambda f: (0, f))
    return pl.pallas_call(
        functools.partial(_ffn_sample_kernel, seq=seq),
        grid=(nf,),
        in_specs=[
            full,
            pl.BlockSpec((1, D_MODEL), lambda f: (0, 0)),
            pl.BlockSpec((D_MODEL, tf), lambda f: (0, f)),
            pl.BlockSpec((D_MODEL, tf), lambda f: (0, f)),
            pl.BlockSpec((tf, D_MODEL), lambda f: (f, 0)),
            pl.BlockSpec((CONV_W, tf), lambda f: (0, f)),
            pl.BlockSpec((1, tf), lambda f: (0, f)),
            col, col,
        ],
        out_specs=[full, col],
        out_shape=[jax.ShapeDtypeStruct((m, D_MODEL), F32),
                   jax.ShapeDtypeStruct((m, D_FF), F32)],
        scratch_shapes=[pltpu.VMEM((m, D_MODEL), BF16)],
        compiler_params=_params("arbitrary"),
        name="conv_ffn_sample",
    )(y, g, wg, wu, wd, cw, cb, s0, s1)


def _block_diag_q(q_ref, col0, hkv, grp, s):
    blocks = []
    for h in range(hkv):
        qh = jnp.concatenate(
            [q_ref[:, col0 + HEAD_DIM * (grp * h + j):col0 + HEAD_DIM * (grp * h + j + 1)]
             for j in range(grp)], axis=0)
        pieces = []
        if h > 0:
            pieces.append(jnp.zeros((grp * s, HEAD_DIM * h), F32))
        pieces.append(qh)
        if h < hkv - 1:
            pieces.append(jnp.zeros((grp * s, HEAD_DIM * (hkv - 1 - h)), F32))
        blocks.append(jnp.concatenate(pieces, axis=1) if len(pieces) > 1 else qh)
    return jnp.concatenate(blocks, axis=0).astype(BF16)


def _new_rows_t(x, s):
    pad = jnp.zeros((V7X_LANES - s, x.shape[1]), F32)
    return jnp.concatenate([pad, x], axis=0).T


def _cached_attend(qbd, ck_t, kn_t, cv_t, vn_t, d, s, sink_col):
    r_, lc = qbd.shape[0], ck_t.shape[1]
    sc = jnp.dot(qbd, ck_t.astype(BF16), preferred_element_type=F32)
    sn = jnp.dot(qbd, kn_t.astype(BF16), preferred_element_type=F32)
    ic = lax.broadcasted_iota(jnp.int32, (r_, lc), 0) & (s - 1)
    c = lax.broadcasted_iota(jnp.int32, (r_, lc), 1)
    i_n = lax.broadcasted_iota(jnp.int32, (r_, V7X_LANES), 0) & (s - 1)
    j = lax.broadcasted_iota(jnp.int32, (r_, V7X_LANES), 1) - (V7X_LANES - s)
    valid_c = c >= ic
    valid_n = (j >= 0) & (j <= i_n)
    if d > 1:
        valid_c = valid_c & ((c & (d - 1)) == (ic & (d - 1)))
        valid_n = valid_n & ((j & (d - 1)) == (i_n & (d - 1)))
    sc = jnp.where(valid_c, sc, NEG_INF)
    sn = jnp.where(valid_n, sn, NEG_INF)
    m = jnp.maximum(jnp.max(sc, axis=1, keepdims=True), jnp.max(sn, axis=1, keepdims=True))
    if sink_col is not None:
        m = jnp.maximum(m, sink_col)
    pc = jnp.exp2(sc - m)
    pn = jnp.exp2(sn - m)
    l = jnp.sum(pc, axis=1, keepdims=True) + jnp.sum(pn, axis=1, keepdims=True)
    if sink_col is not None:
        l = l + jnp.exp2(sink_col - m)
    nt = (((1,), (1,)), ((), ()))
    o = (lax.dot_general(pc.astype(BF16), cv_t.astype(BF16), nt, preferred_element_type=F32)
         + lax.dot_general(pn.astype(BF16), vn_t.astype(BF16), nt, preferred_element_type=F32)) / l
    return o, (m + jnp.log2(l)) * LN2


def _diag_heads(o, hkv, grp, s):
    pieces = []
    for h in range(hkv):
        for j in range(grp):
            r0 = (h * grp + j) * s
            pieces.append(o[r0:r0 + s, HEAD_DIM * h:HEAD_DIM * (h + 1)])
    return jnp.concatenate(pieces, axis=1)


def _store_shifted(out_ref, off, c_t, n_t, s):
    lc = c_t.shape[1]
    rolled = pltpu.roll(c_t, lc - s, 1)
    lane = lax.broadcasted_iota(jnp.int32, n_t.shape, 1)
    if lc > V7X_LANES:
        out_ref[0, :, off:off + lc - V7X_LANES] = rolled[:, :lc - V7X_LANES]
    out_ref[0, :, off + lc - V7X_LANES:off + lc] = jnp.where(
        lane < V7X_LANES - s, rolled[:, lc - V7X_LANES:], n_t)


def _sample_a_kernel(q_ref, kn_ref, vn_ref, ck_ref, cv_ref, sink_ref, o_ref, ko_ref, vo_ref, *, s):
    hkv, grp = A_KV_HEADS, A_Q_HEADS // A_KV_HEADS
    kn_t, vn_t = _new_rows_t(kn_ref[...], s), _new_rows_t(vn_ref[...], s)
    ck_t, cv_t = ck_ref[0], cv_ref[0]
    qbd = _block_diag_q(q_ref, 0, hkv, grp, s)
    sink_col = jnp.concatenate(
        [jnp.full((s, 1), sink_ref[hq] * LOG2E, F32) for hq in range(hkv * grp)], axis=0)
    o, _ = _cached_attend(qbd, ck_t, kn_t, cv_t, vn_t, 1, s, sink_col)
    o_ref[...] = _diag_heads(o, hkv, grp, s)
    _store_shifted(ko_ref, 0, ck_t, kn_t, s)
    _store_shifted(vo_ref, 0, cv_t, vn_t, s)


def _sample_a_call(q, kn, vn, ck_t, cv_t, sinks, *, s):
    nb, c, lc = ck_t.shape
    row = lambda w: pl.BlockSpec((s, w), lambda b: (b, 0))
    cache = pl.BlockSpec((1, c, lc), lambda b: (b, 0, 0))
    return pl.pallas_call(
        functools.partial(_sample_a_kernel, s=s),
        grid=(nb,),
        in_specs=[row(q.shape[1]), row(c), row(c), cache, cache,
                  pl.BlockSpec(memory_space=pltpu.SMEM)],
        out_specs=[row(q.shape[1]), cache, cache],
        out_shape=[jax.ShapeDtypeStruct(q.shape, F32),
                   jax.ShapeDtypeStruct(ck_t.shape, F32),
                   jax.ShapeDtypeStruct(cv_t.shape, F32)],
        compiler_params=_params("parallel"),
        name="sample_attn_a",
    )(q, kn, vn, ck_t, cv_t, sinks)


def _sample_b_kernel(q_ref, kn_ref, vn_ref, ck_ref, cv_ref, o_ref, ko_ref, vo_ref, *, s):
    hkv, grp = B_KV_HEADS, B_Q_HEADS // B_KV_HEADS
    ckv = hkv * HEAD_DIM
    outs, lses = [], []
    off = 0
    for g, (w, d) in enumerate(B_PATTERNS):
        lc = w
        kn_t = _new_rows_t(kn_ref[:, ckv * g:ckv * (g + 1)], s)
        vn_t = _new_rows_t(vn_ref[:, ckv * g:ckv * (g + 1)], s)
        ck_t = ck_ref[0, :, off:off + lc]
        cv_t = cv_ref[0, :, off:off + lc]
        qbd = _block_diag_q(q_ref, B_Q_HEADS * HEAD_DIM * g, hkv, grp, s)
        o, lse = _cached_attend(qbd, ck_t, kn_t, cv_t, vn_t, d, s, None)
        outs.append(o)
        lses.append(lse)
        _store_shifted(ko_ref, off, ck_t, kn_t, s)
        _store_shifted(vo_ref, off, cv_t, vn_t, s)
        off += lc
    mx = jnp.maximum(jnp.maximum(lses[0], lses[1]), lses[2])
    es = [jnp.exp(l - mx) for l in lses]
    den = es[0] + es[1] + es[2]
    comb = (es[0] / den) * outs[0] + (es[1] / den) * outs[1] + (es[2] / den) * outs[2]
    o_ref[...] = _diag_heads(comb, hkv, grp, s)


def _sample_b_call(q, kn, vn, ck_t, cv_t, *, s):
    nb, c, lb = ck_t.shape
    row = lambda w: pl.BlockSpec((s, w), lambda b: (b, 0))
    cache = pl.BlockSpec((1, c, lb), lambda b: (b, 0, 0))
    co = B_Q_HEADS * HEAD_DIM
    return pl.pallas_call(
        functools.partial(_sample_b_kernel, s=s),
        grid=(nb,),
        in_specs=[row(q.shape[1]), row(kn.shape[1]), row(vn.shape[1]), cache, cache],
        out_specs=[row(co), cache, cache],
        out_shape=[jax.ShapeDtypeStruct((q.shape[0], co), F32),
                   jax.ShapeDtypeStruct(ck_t.shape, F32),
                   jax.ShapeDtypeStruct(cv_t.shape, F32)],
        compiler_params=_params("parallel"),
        name="sample_attn_b",
    )(q, kn, vn, ck_t, cv_t)


def _rope_tables(pos):
    inv_freq = ROPE_THETA ** (-jnp.arange(HALF, dtype=F32) / HALF)
    ang = pos.astype(F32)[:, None] * inv_freq[None, :]
    cos, sin = jnp.cos(ang), jnp.sin(ang)
    return jnp.tile(cos, (1, 4)), jnp.tile(jnp.concatenate([-sin, sin], axis=1), (1, 2))


def _cache_t(cache):
    b, l, h, dh = cache.shape
    return jnp.transpose(cache, (0, 2, 3, 1)).reshape(b, h * dh, l)


def _cache_from_t(x, h):
    b, c, l = x.shape
    return jnp.transpose(x.reshape(b, h, c // h, l), (0, 3, 1, 2))[None]


def kernel(x_prompt, x_sample, cache_a_k, cache_a_v, cache_b_k, cache_b_v, state_ffn_conv,
           attn_norm, ffn_norm, a_w_qkv, a_q_norm, a_k_norm, a_sinks, a_w_o,
           b_w_qkv, b_q_norm, b_k_norm, b_w_o,
           ffn_w_gate, ffn_w_up, ffn_conv_w, ffn_conv_b, ffn_w_down):
    nb, seq, _ = x_prompt.shape
    ns, dec, _ = x_sample.shape
    mp, ms = nb * seq, ns * dec

    cos_p, sin_p = _rope_tables(jnp.tile(jnp.arange(seq, dtype=jnp.int32), nb))
    cos_s, sin_s = _rope_tables(jnp.tile(PAST_LEN + jnp.arange(dec, dtype=jnp.int32), ns))
    idx = jnp.arange(V7X_MXU_DIM, dtype=jnp.int32) // HEAD_DIM
    ones_blk = (idx[:, None] == idx[None, :]).astype(BF16)

    yp = x_prompt.reshape(mp, D_MODEL)
    ys = x_sample.reshape(ms, D_MODEL)
    row = lambda a: a.reshape(1, -1)

    def head_gain(gv, n_heads_per_tile):
        return jnp.tile(gv, n_heads_per_tile).reshape(1, -1)

    wqkv = _cast_call(a_w_qkv, 0)
    wo = _cast_call(a_w_o, 0)
    tn = wqkv.shape[1] // N_QKV_TILES
    qg, kg = head_gain(a_q_norm[0], tn // HEAD_DIM), head_gain(a_k_norm[0], tn // HEAD_DIM)
    g_attn = row(attn_norm[0])
    grp_a = A_Q_HEADS // A_KV_HEADS

    q, k, v = _qkv_call(yp, g_attn, wqkv, cos_p, sin_p, qg, kg, ones_blk,
                        n_q=A_Q_HEADS, n_kv=A_KV_HEADS, tm=TM_QKV, q_dtype=BF16)
    k3, v3 = k.reshape(nb, seq, -1), v.reshape(nb, seq, -1)
    o = _band_call(q.reshape(nb, seq, -1), k3, v3, a_sinks[0], hkv=A_KV_HEADS, grp=grp_a)
    yp = _wo_call(yp, o.reshape(mp, -1), wo, tm=TM_PROJ)
    keep = min(A_WINDOW, seq)
    ca = A_KV_HEADS * HEAD_DIM
    a_k_prompt = _cache_from_t(_keep_t_call(k3, [keep], ca), A_KV_HEADS)
    a_v_prompt = _cache_from_t(_keep_t_call(v3, [keep], ca), A_KV_HEADS)

    qs, ks, vs = _qkv_call(ys, g_attn, wqkv, cos_s, sin_s, qg, kg, ones_blk,
                           n_q=A_Q_HEADS, n_kv=A_KV_HEADS, tm=ms, q_dtype=F32)
    os_, ako, avo = _sample_a_call(qs, ks, vs, _cache_t(cache_a_k[0]), _cache_t(cache_a_v[0]),
                                   a_sinks[0], s=dec)
    ys = _wo_call(ys, os_, wo, tm=ms)
    a_k_sample = _cache_from_t(ako, A_KV_HEADS)
    a_v_sample = _cache_from_t(avo, A_KV_HEADS)

    conv_p, conv_s = [], []

    def ffn(layer, yp, ys):
        wg = _cast_call(ffn_w_gate, layer)
        wu = _cast_call(ffn_w_up, layer)
        wd = _cast_call(ffn_w_down, layer)
        g_ffn = row(ffn_norm[layer])
        cw, cb = ffn_conv_w[layer], row(ffn_conv_b[layer])
        yp, tail = _ffn_prompt_call(yp, g_ffn, wg, wu, wd, cw, cb, seq=seq)
        tiles = seq // TM_FFN
        conv_p.append(tail[tiles - 1::tiles, V7X_SUBLANES - (CONV_W - 1):, :])
        st = state_ffn_conv[layer]
        s0 = jnp.repeat(st[:, 0, :], dec, axis=0)
        s1 = jnp.repeat(st[:, 1, :], dec, axis=0)
        ys, gate_s = _ffn_sample_call(ys, g_ffn, wg, wu, wd, cw, cb, s0, s1, seq=dec)
        conv_s.append(gate_s.reshape(ns, dec, D_FF)[:, dec - (CONV_W - 1):, :])
        return yp, ys

    yp, ys = ffn(0, yp, ys)

    wqkv = _cast_call(b_w_qkv, 0)
    wo = _cast_call(b_w_o, 0)
    tn = wqkv.shape[1] // N_QKV_TILES
    qg, kg = head_gain(b_q_norm[0], tn // HEAD_DIM), head_gain(b_k_norm[0], tn // HEAD_DIM)
    g_attn = row(attn_norm[1])
    n_grp = len(B_PATTERNS)
    grp_b = B_Q_HEADS // B_KV_HEADS
    nqb, nkvb = n_grp * B_Q_HEADS, n_grp * B_KV_HEADS
    cb_ = B_KV_HEADS * HEAD_DIM

    q, k, v = _qkv_call(yp, g_attn, wqkv, cos_p, sin_p, qg, kg, ones_blk,
                        n_q=nqb, n_kv=nkvb, tm=TM_QKV, q_dtype=F32, lane_major=True)
    outs, lses = [], []
    for g, (w, d) in enumerate(B_PATTERNS):
        assert w // d == BLOCK
        og, lg = _dil_call(q, k, v, nb=nb, hkv=B_KV_HEADS, grp=grp_b, d=d, group=g)
        outs.append(og)
        lses.append(lg)
    yp = _wo_comb_call(yp, outs, lses, wo, tm=TM_PROJ // 2)
    keeps = [min(w, seq) for w, _ in B_PATTERNS]
    k4 = k.reshape(k.shape[0], nb, seq, V7X_LANES)
    v4 = v.reshape(v.shape[0], nb, seq, V7X_LANES)
    b_k_prompt = _cache_from_t(_keep_t_call(k4, keeps, cb_, lane_major=True), B_KV_HEADS)
    b_v_prompt = _cache_from_t(_keep_t_call(v4, keeps, cb_, lane_major=True), B_KV_HEADS)

    qs, ks, vs = _qkv_call(ys, g_attn, wqkv, cos_s, sin_s, qg, kg, ones_blk,
                           n_q=nqb, n_kv=nkvb, tm=ms, q_dtype=F32)
    os_, bko, bvo = _sample_b_call(qs, ks, vs, _cache_t(cache_b_k[0]), _cache_t(cache_b_v[0]),
                                   s=dec)
    ys = _wo_call(ys, os_, wo, tm=ms)
    b_k_sample = _cache_from_t(bko, B_KV_HEADS)
    b_v_sample = _cache_from_t(bvo, B_KV_HEADS)

    yp, ys = ffn(1, yp, ys)

    return (yp.reshape(nb, seq, D_MODEL), ys.reshape(ns, dec, D_MODEL),
            a_k_prompt, a_v_prompt, a_k_sample, a_v_sample,
            b_k_prompt, b_v_prompt, b_k_sample, b_v_sample,
            jnp.stack(conv_p), jnp.stack(conv_s))
```

```python
import functools

import jax
import jax.numpy as jnp
from jax import lax
from jax.experimental import pallas as pl
from jax.experimental.pallas import tpu as pltpu

F32 = jnp.float32
BF16 = jnp.bfloat16

D_MODEL = 2048
HEAD_DIM = 64
HALF = HEAD_DIM // 2
ROPE_THETA = 10000.0
NORM_EPS = 1e-6
BLOCK = 128
PAST_LEN = 16384
A_WINDOW = 128
A_Q_HEADS = 32
A_KV_HEADS = 8
B_PATTERNS = ((128, 1), (512, 4), (2048, 16))
B_Q_HEADS = 16
B_KV_HEADS = 4
D_FF = 5632
CONV_W = 3
NEG_INF = -1e30
LOG2E = 1.4426950408889634
LN2 = 0.6931471805599453
Q_SCALE = HEAD_DIM ** -0.5 * LOG2E

V7X_MXU_DIM = 256
V7X_LANES = 128
V7X_SUBLANES = 8
VMEM_LIMIT = 56 * 1024 * 1024

TM_PROJ = 512
TM_QKV = 1024
TM_FFN = 1024
TF_FFN = 512
TN_WO_CAST = 512
PROJ_CHUNK_ROWS = 256
FFN_ROW_CHUNKS = 1
N_QKV_TILES = 6
DIL_MAX_ROWS = 1024
ATTN_DEPTH = 2
DIL_TILES_PER_TRIP = 4


def _params(*sem):
    return pltpu.CompilerParams(dimension_semantics=sem, vmem_limit_bytes=VMEM_LIMIT)


def _rms(x, g):
    ms = jnp.mean(x * x, axis=-1, keepdims=True)
    return x * lax.rsqrt(ms + NORM_EPS) * g


def _head_norm_rope(a, gain, ones_blk, cos, sin):
    tn = a.shape[1]
    x2 = a * a
    hi = x2.astype(BF16)
    lo = (x2 - hi.astype(F32)).astype(BF16)
    parts = []
    for c in range(tn // V7X_MXU_DIM):
        sl = slice(V7X_MXU_DIM * c, V7X_MXU_DIM * (c + 1))
        parts.append(jnp.dot(hi[:, sl], ones_blk, preferred_element_type=F32)
                     + jnp.dot(lo[:, sl], ones_blk, preferred_element_type=F32))
    ss = jnp.concatenate(parts, axis=1)
    y = a * lax.rsqrt(ss * (1.0 / HEAD_DIM) + NORM_EPS) * gain
    lane = lax.broadcasted_iota(jnp.int32, y.shape, 1)
    first_half = (lane & (HEAD_DIM - 1)) < HALF
    partner = jnp.where(first_half, pltpu.roll(y, tn - HALF, 1), pltpu.roll(y, HALF, 1))
    reps = tn // V7X_LANES
    return y * jnp.tile(cos, (1, reps)) + partner * jnp.tile(sin, (1, reps))


def _store_cols(ref, rows, val, lane_major):
    if not lane_major:
        ref[rows, :] = val.astype(ref.dtype)
        return
    for t in range(val.shape[1] // V7X_LANES):
        ref[t, rows, :] = val[:, V7X_LANES * t:V7X_LANES * (t + 1)].astype(ref.dtype)


def _qkv_kernel(x_ref, g_ref, w_ref, cos_ref, sin_ref, qg_ref, kg_ref, ones_ref,
                q_ref, k_ref, v_ref, *rest, n_q_tiles, lane_major, row_chunks, emit_bf16):
    n = pl.program_id(1)
    if emit_bf16:
        w_out_ref, xn_ref = rest
        w_out_ref[...] = w_ref[0].astype(BF16)
        w_ref = w_out_ref
    else:
        (xn_ref,) = rest

    @pl.when(n == 0)
    def _():
        xn_ref[...] = _rms(x_ref[...], g_ref[...]).astype(BF16)

    rc = xn_ref.shape[0] // row_chunks

    def project(out_ref, epilogue):
        chunk = lambda c: slice(rc * c, rc * (c + 1))
        matmul = lambda c: jnp.dot(xn_ref[chunk(c), :], w_ref[...], preferred_element_type=F32)
        acc_next = matmul(0)
        for c in range(row_chunks):
            acc = acc_next
            if c + 1 < row_chunks:
                acc_next = matmul(c + 1)
            _store_cols(out_ref, chunk(c), epilogue(acc, chunk(c)), lane_major)

    def norm_rope(gain_ref, scale):
        def fn(acc, rows):
            r = _head_norm_rope(acc, gain_ref[...], ones_ref[...], cos_ref[rows, :],
                                sin_ref[rows, :])
            return r * scale if scale != 1.0 else r
        return fn

    @pl.when(n < n_q_tiles)
    def _():
        project(q_ref, norm_rope(qg_ref, Q_SCALE))

    @pl.when(n == n_q_tiles)
    def _():
        project(k_ref, norm_rope(kg_ref, 1.0))

    @pl.when(n == n_q_tiles + 1)
    def _():
        project(v_ref, lambda acc, rows: acc)


def _qkv_call(x, g, w, cos, sin, qg, kg, ones_blk, *, n_q, n_kv, tm, q_dtype, lane_major=False,
              cast_layer=None):
    m = x.shape[0]
    ncols = w.shape[-1]
    tn = ncols // N_QKV_TILES
    n_q_tiles = (n_q * HEAD_DIM) // tn
    assert n_q_tiles * tn == n_q * HEAD_DIM and n_kv * HEAD_DIM == tn and tn % V7X_MXU_DIM == 0
    assert m % tm == 0
    last_q = n_q_tiles - 1
    if lane_major:
        lt = tn // V7X_LANES
        out_specs = [
            pl.BlockSpec((lt, tm, V7X_LANES), lambda i, n: (jnp.minimum(n, last_q), i, 0)),
            pl.BlockSpec((lt, tm, V7X_LANES), lambda i, n: (0, i, 0)),
            pl.BlockSpec((lt, tm, V7X_LANES), lambda i, n: (0, i, 0)),
        ]
        out_shape = [
            jax.ShapeDtypeStruct((n_q_tiles * lt, m, V7X_LANES), q_dtype),
            jax.ShapeDtypeStruct((lt, m, V7X_LANES), F32),
            jax.ShapeDtypeStruct((lt, m, V7X_LANES), F32),
        ]
    else:
        out_specs = [
            pl.BlockSpec((tm, tn), lambda i, n: (i, jnp.minimum(n, last_q))),
            pl.BlockSpec((tm, tn), lambda i, n: (i, 0)),
            pl.BlockSpec((tm, tn), lambda i, n: (i, 0)),
        ]
        out_shape = [
            jax.ShapeDtypeStruct((m, n_q * HEAD_DIM), q_dtype),
            jax.ShapeDtypeStruct((m, tn), F32),
            jax.ShapeDtypeStruct((m, tn), F32),
        ]
    emit_bf16 = cast_layer is not None
    if emit_bf16:
        assert m == tm
        w_spec = pl.BlockSpec((1, D_MODEL, tn), lambda i, n: (cast_layer, 0, n))
        out_specs = out_specs + [pl.BlockSpec((D_MODEL, tn), lambda i, n: (0, n))]
        out_shape = out_shape + [jax.ShapeDtypeStruct((D_MODEL, ncols), BF16)]
    else:
        w_spec = pl.BlockSpec((D_MODEL, tn), lambda i, n: (0, n))
    return pl.pallas_call(
        functools.partial(_qkv_kernel, n_q_tiles=n_q_tiles, lane_major=lane_major,
                          row_chunks=max(1, tm // PROJ_CHUNK_ROWS), emit_bf16=emit_bf16),
        grid=(m // tm, N_QKV_TILES),
        in_specs=[
            pl.BlockSpec((tm, D_MODEL), lambda i, n: (i, 0)),
            pl.BlockSpec((1, D_MODEL), lambda i, n: (0, 0)),
            w_spec,
            pl.BlockSpec((tm, V7X_LANES), lambda i, n: (i, 0)),
            pl.BlockSpec((tm, V7X_LANES), lambda i, n: (i, 0)),
            pl.BlockSpec((1, tn), lambda i, n: (0, 0)),
            pl.BlockSpec((1, tn), lambda i, n: (0, 0)),
            pl.BlockSpec((V7X_MXU_DIM, V7X_MXU_DIM), lambda i, n: (0, 0)),
        ],
        out_specs=out_specs,
        out_shape=out_shape,
        scratch_shapes=[pltpu.VMEM((tm, D_MODEL), BF16)],
        compiler_params=_params("parallel", "arbitrary"),
        name="qkv_proj",
    )(x, g, w, cos, sin, qg, kg, ones_blk)


GRP = 4


def _band_mask_t(first):
    shape = (2 * BLOCK, BLOCK)
    kj = lax.broadcasted_iota(jnp.int32, shape, 0)
    qi = lax.broadcasted_iota(jnp.int32, shape, 1)
    seen = (kj >= qi) & (kj <= qi + BLOCK) & ((kj >= BLOCK) | jnp.logical_not(first))
    return jnp.where(seen, 0.0, NEG_INF)


def _kv_tile_forms(kt, vt):
    lo = lax.broadcasted_iota(jnp.int32, kt.shape, 1) < HEAD_DIM
    kt_sw = pltpu.roll(kt, HEAD_DIM, 1)
    k_forms = [(jnp.where(lo, kt, 0.0).astype(BF16), jnp.where(lo, 0.0, kt_sw).astype(BF16)),
               (jnp.where(lo, kt_sw, 0.0).astype(BF16), jnp.where(lo, 0.0, kt).astype(BF16))]
    return k_forms, vt.T.astype(BF16)


def _head_scores(k_lo, k_hi, q_a, q_b):
    nt = (((1,), (1,)), ((), ()))
    qpair = jnp.concatenate([q_a, q_b], axis=0)
    return jnp.concatenate([lax.dot_general(k_lo, qpair, nt, preferred_element_type=F32),
                            lax.dot_general(k_hi, qpair, nt, preferred_element_type=F32)], axis=1)


def _head_softmax_pv(s, mask4, v_t, sinks, want_lse):
    s = s + mask4
    m = jnp.max(s, axis=0, keepdims=True)
    if sinks is not None:
        sk = jnp.concatenate([jnp.full((1, BLOCK), sinks[j] * LOG2E, F32)
                              for j in (0, 2, 1, 3)], axis=1)
        m = jnp.maximum(m, sk)
    p = jnp.exp2(s - m)
    l = jnp.sum(p, axis=0, keepdims=True)
    if sinks is not None:
        l = l + jnp.exp2(sk - m)
    o_t = jnp.dot(v_t, p.astype(BF16), preferred_element_type=F32) / l
    lse_t = jnp.broadcast_to((m + jnp.log2(l)) * LN2, o_t.shape) if want_lse else None
    o_tiles, l_tiles = [], []
    for u in range(2):
        c0, c1 = slice(BLOCK * u, BLOCK * (u + 1)), slice(BLOCK * (2 + u), BLOCK * (3 + u))
        o_tiles.append(jnp.concatenate([o_t[:, c0], o_t[:, c1]], axis=0).T)
        if want_lse:
            l_tiles.append(jnp.concatenate([lse_t[:, c0], lse_t[:, c1]], axis=0).T)
    return o_tiles, l_tiles


def _attend_tiles(tiles, mask, want_lse):
    mask4 = jnp.tile(mask, (1, GRP))
    jobs = [(ti, e) for ti in range(len(tiles)) for e in range(2)]
    forms = {}

    def scores(job):
        ti, e = job
        load_kv, load_q = tiles[ti][:2]
        if ti not in forms:
            forms[ti] = _kv_tile_forms(*load_kv())
        k_lo, k_hi = forms[ti][0][e]
        return _head_scores(k_lo, k_hi, load_q(2 * e), load_q(2 * e + 1))

    pending = [scores(job) for job in jobs[:ATTN_DEPTH]]
    for idx, (ti, e) in enumerate(jobs):
        s = pending.pop(0)
        if idx + ATTN_DEPTH < len(jobs):
            pending.append(scores(jobs[idx + ATTN_DEPTH]))
        sink_of, store = tiles[ti][2:]
        v_t = forms[ti][1][HEAD_DIM * e:HEAD_DIM * (e + 1)]
        sinks = None if sink_of is None else [sink_of(GRP * e + j) for j in range(GRP)]
        o_tiles, l_tiles = _head_softmax_pv(s, mask4, v_t, sinks, want_lse)
        for u in range(2):
            store(2 * e + u, o_tiles[u], l_tiles[u] if want_lse else None)


def _band_kernel(q_ref, kc_ref, kp_ref, vc_ref, vp_ref, sink_ref, o_ref, *, kv_tiles):
    mask = _band_mask_t(pl.program_id(1) == 0)
    lanes = lambda t: slice(V7X_LANES * t, V7X_LANES * (t + 1))

    def tile(t):
        def load_kv():
            return (jnp.concatenate([kp_ref[0, :, lanes(t)], kc_ref[0, :, lanes(t)]], axis=0),
                    jnp.concatenate([vp_ref[0, :, lanes(t)], vc_ref[0, :, lanes(t)]], axis=0))

        def store(u, o_tile, _):
            o_ref[0, :, lanes(GRP * t + u)] = o_tile.astype(o_ref.dtype)

        return (load_kv, lambda u: q_ref[0, :, lanes(GRP * t + u)],
                lambda h: sink_ref[2 * GRP * t + h], store)

    _attend_tiles([tile(t) for t in range(kv_tiles)], mask, False)


def _band_call(q, k, v, sinks, *, hkv, grp):
    b, seq, cq = q.shape
    ck = hkv * HEAD_DIM
    assert grp == GRP and hkv % 2 == 0
    cur = lambda bi, c: (bi, c, 0)
    prev = lambda bi, c: (bi, jnp.maximum(c - 1, 0), 0)
    return pl.pallas_call(
        functools.partial(_band_kernel, kv_tiles=hkv // 2),
        grid=(b, seq // BLOCK),
        in_specs=[
            pl.BlockSpec((1, BLOCK, cq), cur),
            pl.BlockSpec((1, BLOCK, ck), cur),
            pl.BlockSpec((1, BLOCK, ck), prev),
            pl.BlockSpec((1, BLOCK, ck), cur),
            pl.BlockSpec((1, BLOCK, ck), prev),
            pl.BlockSpec(memory_space=pltpu.SMEM),
        ],
        out_specs=pl.BlockSpec((1, BLOCK, cq), cur),
        out_shape=jax.ShapeDtypeStruct((b, seq, cq), BF16),
        compiler_params=_params("parallel", "arbitrary"),
        name="band_attn",
    )(q, k, k, v, v, sinks)


def _dil_kernel(q_ref, kc_ref, kp_ref, vc_ref, vp_ref, o_ref, lse_ref, *, kv_tiles, grp, d):
    mask = _band_mask_t(pl.program_id(1) == 0)

    def tile(r, t):
        rs = pl.ds(r, BLOCK, stride=d) if d > 1 else slice(None)

        def load_kv():
            return (jnp.concatenate([kp_ref[t, rs, :], kc_ref[t, rs, :]], axis=0),
                    jnp.concatenate([vp_ref[t, rs, :], vc_ref[t, rs, :]], axis=0))

        def store(u, o_tile, l_tile):
            o_ref[grp * t + u, rs, :] = o_tile
            lse_ref[grp * t + u, rs, :] = l_tile

        return load_kv, lambda u: q_ref[grp * t + u, rs, :].astype(BF16), None, store

    unroll = min(d, max(1, DIL_TILES_PER_TRIP // kv_tiles))

    def body(i, carry):
        _attend_tiles([tile(i * unroll + rr, t) for rr in range(unroll) for t in range(kv_tiles)],
                      mask, True)
        return carry

    if d > unroll:
        lax.fori_loop(0, d // unroll, body, 0)
    else:
        body(0, 0)


def _dil_call(q, k, v, *, nb, hkv, grp, d, group):
    m = q.shape[1]
    seq = m // nb
    rows = BLOCK * d
    hsplit = max(1, rows // DIL_MAX_ROWS)
    kv_tiles = hkv // 2 // hsplit
    q_tiles = kv_tiles * grp
    nchunk = seq // rows
    assert seq % rows == 0 and kv_tiles * 2 * hsplit == hkv and grp % 2 == 0
    cur = lambda bi, c, hp: (group * hsplit + hp, bi * nchunk + c, 0)
    prev = lambda bi, c, hp: (group * hsplit + hp, bi * nchunk + jnp.maximum(c - 1, 0), 0)
    o_spec = pl.BlockSpec((q_tiles, rows, V7X_LANES), lambda bi, c, hp: (hp, bi * nchunk + c, 0))
    o_shape = jax.ShapeDtypeStruct((q_tiles * hsplit, m, V7X_LANES), F32)
    return pl.pallas_call(
        functools.partial(_dil_kernel, kv_tiles=kv_tiles, grp=grp, d=d),
        grid=(nb, nchunk, hsplit),
        in_specs=[
            pl.BlockSpec((q_tiles, rows, V7X_LANES), cur),
            pl.BlockSpec((kv_tiles, rows, V7X_LANES), cur),
            pl.BlockSpec((kv_tiles, rows, V7X_LANES), prev),
            pl.BlockSpec((kv_tiles, rows, V7X_LANES), cur),
            pl.BlockSpec((kv_tiles, rows, V7X_LANES), prev),
        ],
        out_specs=[o_spec, o_spec],
        out_shape=[o_shape, o_shape],
        compiler_params=_params("parallel", "arbitrary", "arbitrary"),
        name="dilated_attn",
    )(q, k, k, v, v)


def _keep_t_kernel(x_ref, o_ref):
    o_ref[0] = x_ref[0].T


def _keep_t_lane_major_kernel(x_ref, o_ref, *, keeps):
    g = pl.program_id(2)
    seq = x_ref.shape[2]
    off = 0
    for k, keep in enumerate(keeps):
        @pl.when(g == k)
        def _(keep=keep, off=off):
            o_ref[0, :, off:off + keep] = x_ref[0, 0, seq - keep:, :].T
        off += keep


def _keep_t_call(x, keeps, c, *, lane_major=False):
    b, seq = x.shape[1:3] if lane_major else x.shape[:2]
    starts, first_rb = [], []
    n = 0
    for keep in keeps:
        assert keep % BLOCK == 0 and seq % BLOCK == 0
        starts.append(n)
        first_rb.append((seq - keep) // BLOCK)
        n += keep // BLOCK

    def src_block(j):
        rb = jnp.int32(0)
        cb = jnp.int32(0)
        for g in range(len(keeps)):
            inside = j >= starts[g]
            rb = jnp.where(inside, first_rb[g] + j - starts[g], rb)
            cb = jnp.where(inside, g, cb)
        return rb, cb

    if lane_major:
        ct = c // V7X_LANES
        grid = (b, ct, len(keeps))
        in_spec = pl.BlockSpec((1, 1, seq, V7X_LANES), lambda bi, t, g: (g * ct + t, bi, 0, 0))
        out_spec = pl.BlockSpec((1, V7X_LANES, n * BLOCK), lambda bi, t, g: (bi, t, 0))
        kern = functools.partial(_keep_t_lane_major_kernel, keeps=tuple(keeps))
        sem = ("parallel", "parallel", "arbitrary")
    else:
        grid = (b, n)

        def src(bi, j):
            rb, cb = src_block(j)
            return bi, rb, cb

        in_spec = pl.BlockSpec((1, BLOCK, c), src)
        out_spec = pl.BlockSpec((1, c, BLOCK), lambda bi, j: (bi, 0, j))
        kern = _keep_t_kernel
        sem = ("parallel", "parallel")
    return pl.pallas_call(
        kern,
        grid=grid,
        in_specs=[in_spec],
        out_specs=out_spec,
        out_shape=jax.ShapeDtypeStruct((b, c, n * BLOCK), F32),
        compiler_params=_params(*sem),
        name="keep_rows_t",
    )(x)


def _wo_kernel(y_ref, o_ref, w_ref, out_ref):
    out_ref[...] = y_ref[...] + jnp.dot(o_ref[...].astype(BF16), w_ref[...],
                                        preferred_element_type=F32)


def _wo_comb_kernel(y_ref, o0_ref, o1_ref, o2_ref, l0_ref, l1_ref, l2_ref, w_ref, out_ref):
    tiles = []
    for t in range(o0_ref.shape[0]):
        l0, l1, l2 = l0_ref[t], l1_ref[t], l2_ref[t]
        mx = jnp.maximum(jnp.maximum(l0, l1), l2)
        e0, e1, e2 = jnp.exp(l0 - mx), jnp.exp(l1 - mx), jnp.exp(l2 - mx)
        den = e0 + e1 + e2
        comb = (e0 / den) * o0_ref[t] + (e1 / den) * o1_ref[t] + (e2 / den) * o2_ref[t]
        tiles.append(comb.astype(BF16))
    comb = jnp.concatenate(tiles, axis=1)
    out_ref[...] = y_ref[...] + jnp.dot(comb, w_ref[...], preferred_element_type=F32)


def _wo_call(y, o, w, *, tm):
    m = y.shape[0]
    c = o.shape[1]
    return pl.pallas_call(
        _wo_kernel,
        grid=(m // tm,),
        in_specs=[
            pl.BlockSpec((tm, D_MODEL), lambda i: (i, 0)),
            pl.BlockSpec((tm, c), lambda i: (i, 0)),
            pl.BlockSpec((c, D_MODEL), lambda i: (0, 0)),
        ],
        out_specs=pl.BlockSpec((tm, D_MODEL), lambda i: (i, 0)),
        out_shape=jax.ShapeDtypeStruct((m, D_MODEL), F32),
        compiler_params=_params("parallel"),
        name="wo_proj",
    )(y, o, w)


def _wo_cast_kernel(y_ref, o_ref, w_ref, out_ref, w_out_ref):
    w_out_ref[...] = w_ref[0].astype(BF16)
    out_ref[...] = y_ref[...] + jnp.dot(o_ref[...].astype(BF16), w_out_ref[...],
                                        preferred_element_type=F32)


def _wo_cast_call(y, o, w, layer):
    m = y.shape[0]
    c = o.shape[1]
    tn = TN_WO_CAST
    return pl.pallas_call(
        _wo_cast_kernel,
        grid=(D_MODEL // tn,),
        in_specs=[
            pl.BlockSpec((m, tn), lambda n: (0, n)),
            pl.BlockSpec((m, c), lambda n: (0, 0)),
            pl.BlockSpec((1, c, tn), lambda n: (layer, 0, n)),
        ],
        out_specs=[pl.BlockSpec((m, tn), lambda n: (0, n)),
                   pl.BlockSpec((c, tn), lambda n: (0, n))],
        out_shape=[jax.ShapeDtypeStruct((m, D_MODEL), F32),
                   jax.ShapeDtypeStruct((c, D_MODEL), BF16)],
        compiler_params=_params("parallel"),
        name="wo_proj_cast",
    )(y, o, w)


def _wo_comb_call(y, os_, ls_, w, *, tm):
    m = y.shape[0]
    c = w.shape[0]
    blk = pl.BlockSpec((c // V7X_LANES, tm, V7X_LANES), lambda i: (0, i, 0))
    return pl.pallas_call(
        _wo_comb_kernel,
        grid=(m // tm,),
        in_specs=[pl.BlockSpec((tm, D_MODEL), lambda i: (i, 0))] + [blk] * 6
                 + [pl.BlockSpec((c, D_MODEL), lambda i: (0, 0))],
        out_specs=pl.BlockSpec((tm, D_MODEL), lambda i: (i, 0)),
        out_shape=jax.ShapeDtypeStruct((m, D_MODEL), F32),
        compiler_params=_params("parallel"),
        name="wo_comb_proj",
    )(y, *os_, *ls_, w)


def _ffn_tail(gate, g1, g2, up, cw_ref, cb_ref, wd_ref):
    conv = cb_ref[...] + cw_ref[0:1, :] * g2 + cw_ref[1:2, :] * g1 + cw_ref[2:3, :] * gate
    h = conv * jax.nn.sigmoid(conv) * up
    return jnp.dot(h.astype(BF16), wd_ref[...], preferred_element_type=F32)


def _ffn_prompt_kernel(y_ref, g_ref, wg_ref, wu_ref, wd_ref, cw_ref, cb_ref,
                       out_ref, tail_ref, xn_ref, carry_ref, *, tiles_per_seq):
    m = pl.program_id(0)
    f = pl.program_id(1)

    @pl.when(f == 0)
    def _():
        x = y_ref[...]
        xn_ref[...] = _rms(x, g_ref[...]).astype(BF16)
        out_ref[...] = x

    @pl.when(m % tiles_per_seq == 0)
    def _():
        carry_ref[f] = jnp.zeros(carry_ref.shape[1:], F32)

    rc = xn_ref.shape[0] // FFN_ROW_CHUNKS
    last = carry_ref[f]
    for c in range(FFN_ROW_CHUNKS):
        rows = slice(rc * c, rc * (c + 1))
        xn = xn_ref[rows, :]
        gate = jnp.dot(xn, wg_ref[...], preferred_element_type=F32)
        up = jnp.dot(xn, wu_ref[...], preferred_element_type=F32)
        row = lax.broadcasted_iota(jnp.int32, gate.shape, 0)
        g1 = jnp.where(row == 0, last[7:8, :], pltpu.roll(gate, 1, 0))
        g2 = jnp.where(row == 0, last[6:7, :],
                       jnp.where(row == 1, last[7:8, :], pltpu.roll(gate, 2, 0)))
        out_ref[rows, :] += _ffn_tail(gate, g1, g2, up, cw_ref, cb_ref, wd_ref)
        last = gate[rc - V7X_SUBLANES:, :]
    carry_ref[f] = last
    tail_ref[0] = last


def _ffn_prompt_call(y, g, wg, wu, wd, cw, cb, *, seq):
    m = y.shape[0]
    tm, tf = TM_FFN, TF_FFN
    nf = D_FF // tf
    assert m % tm == 0 and seq % tm == 0 and D_FF % tf == 0
    return pl.pallas_call(
        functools.partial(_ffn_prompt_kernel, tiles_per_seq=seq // tm),
        grid=(m // tm, nf),
        in_specs=[
            pl.BlockSpec((tm, D_MODEL), lambda i, f: (i, 0)),
            pl.BlockSpec((1, D_MODEL), lambda i, f: (0, 0)),
            pl.BlockSpec((D_MODEL, tf), lambda i, f: (0, f)),
            pl.BlockSpec((D_MODEL, tf), lambda i, f: (0, f)),
            pl.BlockSpec((tf, D_MODEL), lambda i, f: (f, 0)),
            pl.BlockSpec((CONV_W, tf), lambda i, f: (0, f)),
            pl.BlockSpec((1, tf), lambda i, f: (0, f)),
        ],
        out_specs=[
            pl.BlockSpec((tm, D_MODEL), lambda i, f: (i, 0)),
            pl.BlockSpec((1, V7X_SUBLANES, tf), lambda i, f: (i, 0, f)),
        ],
        out_shape=[
            jax.ShapeDtypeStruct((m, D_MODEL), F32),
            jax.ShapeDtypeStruct((m // tm, V7X_SUBLANES, D_FF), F32),
        ],
        scratch_shapes=[pltpu.VMEM((tm, D_MODEL), BF16),
                        pltpu.VMEM((nf, V7X_SUBLANES, tf), F32)],
        compiler_params=_params("arbitrary", "arbitrary"),
        name="conv_ffn_prompt",
    )(y, g, wg, wu, wd, cw, cb)


def _ffn_sample_kernel(y_ref, g_ref, wg_ref, wu_ref, wd_ref, cw_ref, cb_ref, s0_ref, s1_ref,
                       out_ref, gate_ref, wg_out_ref, wu_out_ref, wd_out_ref, xn_ref, *, seq):
    f = pl.program_id(0)

    @pl.when(f == 0)
    def _():
        x = y_ref[...]
        xn_ref[...] = _rms(x, g_ref[...]).astype(BF16)
        out_ref[...] = x

    wg_out_ref[...] = wg_ref[0].astype(BF16)
    wu_out_ref[...] = wu_ref[0].astype(BF16)
    wd_out_ref[...] = wd_ref[0].astype(BF16)
    xn = xn_ref[...]
    gate = jnp.dot(xn, wg_out_ref[...], preferred_element_type=F32)
    up = jnp.dot(xn, wu_out_ref[...], preferred_element_type=F32)
    t = lax.broadcasted_iota(jnp.int32, gate.shape, 0) & (seq - 1)
    s0, s1 = s0_ref[...], s1_ref[...]
    g1 = jnp.where(t == 0, s1, pltpu.roll(gate, 1, 0))
    g2 = jnp.where(t == 0, s0, jnp.where(t == 1, s1, pltpu.roll(gate, 2, 0)))
    out_ref[...] += _ffn_tail(gate, g1, g2, up, cw_ref, cb_ref, wd_out_ref)
    gate_ref[...] = gate


def _ffn_sample_call(y, g, wg, wu, wd, layer, cw, cb, s0, s1, *, seq):
    m = y.shape[0]
    tf = TF_FFN
    nf = D_FF // tf
    full = pl.BlockSpec((m, D_MODEL), lambda f: (0, 0))
    col = pl.BlockSpec((m, tf), lambda f: (0, f))
    return pl.pallas_call(
        functools.partial(_ffn_sample_kernel, seq=seq),
        grid=(nf,),
        in_specs=[
            full,
            pl.BlockSpec((1, D_MODEL), lambda f: (0, 0)),
            pl.BlockSpec((1, D_MODEL, tf), lambda f: (layer, 0, f)),
            pl.BlockSpec((1, D_MODEL, tf), lambda f: (layer, 0, f)),
            pl.BlockSpec((1, tf, D_MODEL), lambda f: (layer, f, 0)),
            pl.BlockSpec((CONV_W, tf), lambda f: (0, f)),
            pl.BlockSpec((1, tf), lambda f: (0, f)),
            col, col,
        ],
        out_specs=[full, col,
                   pl.BlockSpec((D_MODEL, tf), lambda f: (0, f)),
                   pl.BlockSpec((D_MODEL, tf), lambda f: (0, f)),
                   pl.BlockSpec((tf, D_MODEL), lambda f: (f, 0))],
        out_shape=[jax.ShapeDtypeStruct((m, D_MODEL), F32),
                   jax.ShapeDtypeStruct((m, D_FF), F32),
                   jax.ShapeDtypeStruct((D_MODEL, D_FF), BF16),
                   jax.ShapeDtypeStruct((D_MODEL, D_FF), BF16),
                   jax.ShapeDtypeStruct((D_FF, D_MODEL), BF16)],
        scratch_shapes=[pltpu.VMEM((m, D_MODEL), BF16)],
        compiler_params=_params("arbitrary"),
        name="conv_ffn_sample",
    )(y, g, wg, wu, wd, cw, cb, s0, s1)


def _block_diag_q(q_ref, col0, hkv, grp, s):
    blocks = []
    for h in range(hkv):
        qh = jnp.concatenate(
            [q_ref[:, col0 + HEAD_DIM * (grp * h + j):col0 + HEAD_DIM * (grp * h + j + 1)]
             for j in range(grp)], axis=0)
        pieces = []
        if h > 0:
            pieces.append(jnp.zeros((grp * s, HEAD_DIM * h), F32))
        pieces.append(qh)
        if h < hkv - 1:
            pieces.append(jnp.zeros((grp * s, HEAD_DIM * (hkv - 1 - h)), F32))
        blocks.append(jnp.concatenate(pieces, axis=1) if len(pieces) > 1 else qh)
    return jnp.concatenate(blocks, axis=0).astype(BF16)


def _new_rows_t(x, s):
    pad = jnp.zeros((V7X_LANES - s, x.shape[1]), F32)
    return jnp.concatenate([pad, x], axis=0).T


def _cached_attend(qbd, ck_t, kn_t, cv_t, vn_t, d, s, sink_col):
    r_, lc = qbd.shape[0], ck_t.shape[1]
    sc = jnp.dot(qbd, ck_t.astype(BF16), preferred_element_type=F32)
    sn = jnp.dot(qbd, kn_t.astype(BF16), preferred_element_type=F32)
    ic = lax.broadcasted_iota(jnp.int32, (r_, lc), 0) & (s - 1)
    c = lax.broadcasted_iota(jnp.int32, (r_, lc), 1)
    i_n = lax.broadcasted_iota(jnp.int32, (r_, V7X_LANES), 0) & (s - 1)
    j = lax.broadcasted_iota(jnp.int32, (r_, V7X_LANES), 1) - (V7X_LANES - s)
    valid_c = c >= ic
    valid_n = (j >= 0) & (j <= i_n)
    if d > 1:
        valid_c = valid_c & ((c & (d - 1)) == (ic & (d - 1)))
        valid_n = valid_n & ((j & (d - 1)) == (i_n & (d - 1)))
    sc = jnp.where(valid_c, sc, NEG_INF)
    sn = jnp.where(valid_n, sn, NEG_INF)
    m = jnp.maximum(jnp.max(sc, axis=1, keepdims=True), jnp.max(sn, axis=1, keepdims=True))
    if sink_col is not None:
        m = jnp.maximum(m, sink_col)
    pc = jnp.exp2(sc - m)
    pn = jnp.exp2(sn - m)
    l = jnp.sum(pc, axis=1, keepdims=True) + jnp.sum(pn, axis=1, keepdims=True)
    if sink_col is not None:
        l = l + jnp.exp2(sink_col - m)
    nt = (((1,), (1,)), ((), ()))
    o = (lax.dot_general(pc.astype(BF16), cv_t.astype(BF16), nt, preferred_element_type=F32)
         + lax.dot_general(pn.astype(BF16), vn_t.astype(BF16), nt, preferred_element_type=F32)) / l
    return o, (m + jnp.log2(l)) * LN2


def _diag_heads(o, hkv, grp, s):
    pieces = []
    for h in range(hkv):
        for j in range(grp):
            r0 = (h * grp + j) * s
            pieces.append(o[r0:r0 + s, HEAD_DIM * h:HEAD_DIM * (h + 1)])
    return jnp.concatenate(pieces, axis=1)


def _store_shifted(out_ref, off, c_t, n_t, s):
    lc = c_t.shape[1]
    rolled = pltpu.roll(c_t, lc - s, 1)
    lane = lax.broadcasted_iota(jnp.int32, n_t.shape, 1)
    if lc > V7X_LANES:
        out_ref[0, :, off:off + lc - V7X_LANES] = rolled[:, :lc - V7X_LANES]
    out_ref[0, :, off + lc - V7X_LANES:off + lc] = jnp.where(
        lane < V7X_LANES - s, rolled[:, lc - V7X_LANES:], n_t)


def _sample_a_kernel(q_ref, kn_ref, vn_ref, ck_ref, cv_ref, sink_ref, o_ref, ko_ref, vo_ref, *, s):
    hkv, grp = A_KV_HEADS, A_Q_HEADS // A_KV_HEADS
    kn_t, vn_t = _new_rows_t(kn_ref[...], s), _new_rows_t(vn_ref[...], s)
    ck_t, cv_t = ck_ref[0], cv_ref[0]
    qbd = _block_diag_q(q_ref, 0, hkv, grp, s)
    sink_col = jnp.concatenate(
        [jnp.full((s, 1), sink_ref[hq] * LOG2E, F32) for hq in range(hkv * grp)], axis=0)
    o, _ = _cached_attend(qbd, ck_t, kn_t, cv_t, vn_t, 1, s, sink_col)
    o_ref[...] = _diag_heads(o, hkv, grp, s)
    _store_shifted(ko_ref, 0, ck_t, kn_t, s)
    _store_shifted(vo_ref, 0, cv_t, vn_t, s)


def _sample_a_call(q, kn, vn, ck_t, cv_t, sinks, *, s):
    nb, c, lc = ck_t.shape
    row = lambda w: pl.BlockSpec((s, w), lambda b: (b, 0))
    cache = pl.BlockSpec((1, c, lc), lambda b: (b, 0, 0))
    return pl.pallas_call(
        functools.partial(_sample_a_kernel, s=s),
        grid=(nb,),
        in_specs=[row(q.shape[1]), row(c), row(c), cache, cache,
                  pl.BlockSpec(memory_space=pltpu.SMEM)],
        out_specs=[row(q.shape[1]), cache, cache],
        out_shape=[jax.ShapeDtypeStruct(q.shape, F32),
                   jax.ShapeDtypeStruct(ck_t.shape, F32),
                   jax.ShapeDtypeStruct(cv_t.shape, F32)],
        compiler_params=_params("parallel"),
        name="sample_attn_a",
    )(q, kn, vn, ck_t, cv_t, sinks)


def _sample_b_kernel(q_ref, kn_ref, vn_ref, ck_ref, cv_ref, o_ref, ko_ref, vo_ref, *, s):
    hkv, grp = B_KV_HEADS, B_Q_HEADS // B_KV_HEADS
    ckv = hkv * HEAD_DIM
    outs, lses = [], []
    off = 0
    for g, (w, d) in enumerate(B_PATTERNS):
        lc = w
        kn_t = _new_rows_t(kn_ref[:, ckv * g:ckv * (g + 1)], s)
        vn_t = _new_rows_t(vn_ref[:, ckv * g:ckv * (g + 1)], s)
        ck_t = ck_ref[0, :, off:off + lc]
        cv_t = cv_ref[0, :, off:off + lc]
        qbd = _block_diag_q(q_ref, B_Q_HEADS * HEAD_DIM * g, hkv, grp, s)
        o, lse = _cached_attend(qbd, ck_t, kn_t, cv_t, vn_t, d, s, None)
        outs.append(o)
        lses.append(lse)
        _store_shifted(ko_ref, off, ck_t, kn_t, s)
        _store_shifted(vo_ref, off, cv_t, vn_t, s)
        off += lc
    mx = jnp.maximum(jnp.maximum(lses[0], lses[1]), lses[2])
    es = [jnp.exp(l - mx) for l in lses]
    den = es[0] + es[1] + es[2]
    comb = (es[0] / den) * outs[0] + (es[1] / den) * outs[1] + (es[2] / den) * outs[2]
    o_ref[...] = _diag_heads(comb, hkv, grp, s)


def _sample_b_call(q, kn, vn, ck_t, cv_t, *, s):
    nb, c, lb = ck_t.shape
    row = lambda w: pl.BlockSpec((s, w), lambda b: (b, 0))
    cache = pl.BlockSpec((1, c, lb), lambda b: (b, 0, 0))
    co = B_Q_HEADS * HEAD_DIM
    return pl.pallas_call(
        functools.partial(_sample_b_kernel, s=s),
        grid=(nb,),
        in_specs=[row(q.shape[1]), row(kn.shape[1]), row(vn.shape[1]), cache, cache],
        out_specs=[row(co), cache, cache],
        out_shape=[jax.ShapeDtypeStruct((q.shape[0], co), F32),
                   jax.ShapeDtypeStruct(ck_t.shape, F32),
                   jax.ShapeDtypeStruct(cv_t.shape, F32)],
        compiler_params=_params("parallel"),
        name="sample_attn_b",
    )(q, kn, vn, ck_t, cv_t)


def _rope_tables(pos):
    inv_freq = ROPE_THETA ** (-jnp.arange(HALF, dtype=F32) / HALF)
    ang = pos.astype(F32)[:, None] * inv_freq[None, :]
    cos, sin = jnp.cos(ang), jnp.sin(ang)
    return jnp.tile(cos, (1, 4)), jnp.tile(jnp.concatenate([-sin, sin], axis=1), (1, 2))


def _cache_t(cache):
    b, l, h, dh = cache.shape
    return jnp.transpose(cache, (0, 2, 3, 1)).reshape(b, h * dh, l)


def _cache_from_t(x, h):
    b, c, l = x.shape
    return jnp.transpose(x.reshape(b, h, c // h, l), (0, 3, 1, 2))[None]


def kernel(x_prompt, x_sample, cache_a_k, cache_a_v, cache_b_k, cache_b_v, state_ffn_conv,
           attn_norm, ffn_norm, a_w_qkv, a_q_norm, a_k_norm, a_sinks, a_w_o,
           b_w_qkv, b_q_norm, b_k_norm, b_w_o,
           ffn_w_gate, ffn_w_up, ffn_conv_w, ffn_conv_b, ffn_w_down):
    nb, seq, _ = x_prompt.shape
    ns, dec, _ = x_sample.shape
    mp, ms = nb * seq, ns * dec

    cos_p, sin_p = _rope_tables(jnp.tile(jnp.arange(seq, dtype=jnp.int32), nb))
    cos_s, sin_s = _rope_tables(jnp.tile(PAST_LEN + jnp.arange(dec, dtype=jnp.int32), ns))
    idx = jnp.arange(V7X_MXU_DIM, dtype=jnp.int32) // HEAD_DIM
    ones_blk = (idx[:, None] == idx[None, :]).astype(BF16)

    yp = x_prompt.reshape(mp, D_MODEL)
    ys = x_sample.reshape(ms, D_MODEL)
    row = lambda a: a.reshape(1, -1)

    def head_gain(gv, n_heads_per_tile):
        return jnp.tile(gv, n_heads_per_tile).reshape(1, -1)

    tn = a_w_qkv.shape[-1] // N_QKV_TILES
    qg, kg = head_gain(a_q_norm[0], tn // HEAD_DIM), head_gain(a_k_norm[0], tn // HEAD_DIM)
    g_attn = row(attn_norm[0])
    grp_a = A_Q_HEADS // A_KV_HEADS

    qs, ks, vs, wqkv = _qkv_call(ys, g_attn, a_w_qkv, cos_s, sin_s, qg, kg, ones_blk,
                                 n_q=A_Q_HEADS, n_kv=A_KV_HEADS, tm=ms, q_dtype=F32, cast_layer=0)
    os_, ako, avo = _sample_a_call(qs, ks, vs, _cache_t(cache_a_k[0]), _cache_t(cache_a_v[0]),
                                   a_sinks[0], s=dec)
    ys, wo = _wo_cast_call(ys, os_, a_w_o, 0)
    a_k_sample = _cache_from_t(ako, A_KV_HEADS)
    a_v_sample = _cache_from_t(avo, A_KV_HEADS)

    q, k, v = _qkv_call(yp, g_attn, wqkv, cos_p, sin_p, qg, kg, ones_blk,
                        n_q=A_Q_HEADS, n_kv=A_KV_HEADS, tm=TM_QKV, q_dtype=BF16)
    k3, v3 = k.reshape(nb, seq, -1), v.reshape(nb, seq, -1)
    o = _band_call(q.reshape(nb, seq, -1), k3, v3, a_sinks[0], hkv=A_KV_HEADS, grp=grp_a)
    yp = _wo_call(yp, o.reshape(mp, -1), wo, tm=TM_PROJ)
    keep = min(A_WINDOW, seq)
    ca = A_KV_HEADS * HEAD_DIM
    a_k_prompt = _cache_from_t(_keep_t_call(k3, [keep], ca), A_KV_HEADS)
    a_v_prompt = _cache_from_t(_keep_t_call(v3, [keep], ca), A_KV_HEADS)

    conv_p, conv_s = [], []

    def ffn(layer, yp, ys):
        g_ffn = row(ffn_norm[layer])
        cw, cb = ffn_conv_w[layer], row(ffn_conv_b[layer])
        st = state_ffn_conv[layer]
        s0 = jnp.repeat(st[:, 0, :], dec, axis=0)
        s1 = jnp.repeat(st[:, 1, :], dec, axis=0)
        ys, gate_s, wg, wu, wd = _ffn_sample_call(ys, g_ffn, ffn_w_gate, ffn_w_up, ffn_w_down,
                                                  layer, cw, cb, s0, s1, seq=dec)
        conv_s.append(gate_s.reshape(ns, dec, D_FF)[:, dec - (CONV_W - 1):, :])
        yp, tail = _ffn_prompt_call(yp, g_ffn, wg, wu, wd, cw, cb, seq=seq)
        tiles = seq // TM_FFN
        conv_p.append(tail[tiles - 1::tiles, V7X_SUBLANES - (CONV_W - 1):, :])
        return yp, ys

    yp, ys = ffn(0, yp, ys)

    tn = b_w_qkv.shape[-1] // N_QKV_TILES
    qg, kg = head_gain(b_q_norm[0], tn // HEAD_DIM), head_gain(b_k_norm[0], tn // HEAD_DIM)
    g_attn = row(attn_norm[1])
    n_grp = len(B_PATTERNS)
    grp_b = B_Q_HEADS // B_KV_HEADS
    nqb, nkvb = n_grp * B_Q_HEADS, n_grp * B_KV_HEADS
    cb_ = B_KV_HEADS * HEAD_DIM

    qs, ks, vs, wqkv = _qkv_call(ys, g_attn, b_w_qkv, cos_s, sin_s, qg, kg, ones_blk,
                                 n_q=nqb, n_kv=nkvb, tm=ms, q_dtype=F32, cast_layer=0)
    os_, bko, bvo = _sample_b_call(qs, ks, vs, _cache_t(cache_b_k[0]), _cache_t(cache_b_v[0]),
                                   s=dec)
    ys, wo = _wo_cast_call(ys, os_, b_w_o, 0)
    b_k_sample = _cache_from_t(bko, B_KV_HEADS)
    b_v_sample = _cache_from_t(bvo, B_KV_HEADS)

    q, k, v = _qkv_call(yp, g_attn, wqkv, cos_p, sin_p, qg, kg, ones_blk,
                        n_q=nqb, n_kv=nkvb, tm=TM_QKV, q_dtype=F32, lane_major=True)
    outs, lses = [], []
    for g, (w, d) in enumerate(B_PATTERNS):
        assert w // d == BLOCK
        og, lg = _dil_call(q, k, v, nb=nb, hkv=B_KV_HEADS, grp=grp_b, d=d, group=g)
        outs.append(og)
        lses.append(lg)
    yp = _wo_comb_call(yp, outs, lses, wo, tm=TM_PROJ // 2)
    keeps = [min(w, seq) for w, _ in B_PATTERNS]
    k4 = k.reshape(k.shape[0], nb, seq, V7X_LANES)
    v4 = v.reshape(v.shape[0], nb, seq, V7X_LANES)
    b_k_prompt = _cache_from_t(_keep_t_call(k4, keeps, cb_, lane_major=True), B_KV_HEADS)
    b_v_prompt = _cache_from_t(_keep_t_call(v4, keeps, cb_, lane_major=True), B_KV_HEADS)

    yp, ys = ffn(1, yp, ys)

    return (yp.reshape(nb, seq, D_MODEL), ys.reshape(ns, dec, D_MODEL),
            a_k_prompt, a_v_prompt, a_k_sample, a_v_sample,
            b_k_prompt, b_v_prompt, b_k_sample, b_v_sample,
            jnp.stack(conv_p), jnp.stack(conv_s))
```

```python
import functools

import jax
import jax.numpy as jnp
from jax import lax
from jax.experimental import pallas as pl
from jax.experimental.pallas import tpu as pltpu

F32 = jnp.float32
BF16 = jnp.bfloat16

D_MODEL = 2048
HEAD_DIM = 64
HALF = HEAD_DIM // 2
ROPE_THETA = 10000.0
NORM_EPS = 1e-6
BLOCK = 128
PAST_LEN = 16384
A_WINDOW = 128
A_Q_HEADS = 32
A_KV_HEADS = 8
B_PATTERNS = ((128, 1), (512, 4), (2048, 16))
B_Q_HEADS = 16
B_KV_HEADS = 4
D_FF = 5632
CONV_W = 3
NEG_INF = -1e30
LOG2E = 1.4426950408889634
LN2 = 0.6931471805599453
Q_SCALE = HEAD_DIM ** -0.5 * LOG2E

V7X_MXU_DIM = 256
V7X_LANES = 128
V7X_SUBLANES = 8
VMEM_LIMIT = 56 * 1024 * 1024

TM_PROJ = 512
TM_QKV = 1024
TM_FFN = 1024
TF_FFN = 512
TN_WO_CAST = 512
PROJ_CHUNK_ROWS = 256
FFN_ROW_CHUNKS = 1
N_QKV_TILES = 6
DIL_MAX_ROWS = 1024
BAND_SUBBLOCKS = 4
ATTN_DEPTH = 2
DIL_TILES_PER_TRIP = 4


def _params(*sem):
    return pltpu.CompilerParams(dimension_semantics=sem, vmem_limit_bytes=VMEM_LIMIT)


def _rms(x, g):
    ms = jnp.mean(x * x, axis=-1, keepdims=True)
    return x * lax.rsqrt(ms + NORM_EPS) * g


def _head_norm_rope(a, gain, ones_blk, cos, sin):
    tn = a.shape[1]
    x2 = a * a
    hi = x2.astype(BF16)
    lo = (x2 - hi.astype(F32)).astype(BF16)
    parts = []
    for c in range(tn // V7X_MXU_DIM):
        sl = slice(V7X_MXU_DIM * c, V7X_MXU_DIM * (c + 1))
        parts.append(jnp.dot(hi[:, sl], ones_blk, preferred_element_type=F32)
                     + jnp.dot(lo[:, sl], ones_blk, preferred_element_type=F32))
    ss = jnp.concatenate(parts, axis=1)
    y = a * lax.rsqrt(ss * (1.0 / HEAD_DIM) + NORM_EPS) * gain
    lane = lax.broadcasted_iota(jnp.int32, y.shape, 1)
    first_half = (lane & (HEAD_DIM - 1)) < HALF
    partner = jnp.where(first_half, pltpu.roll(y, tn - HALF, 1), pltpu.roll(y, HALF, 1))
    reps = tn // V7X_LANES
    return y * jnp.tile(cos, (1, reps)) + partner * jnp.tile(sin, (1, reps))


def _store_cols(ref, rows, val, lane_major):
    if not lane_major:
        ref[rows, :] = val.astype(ref.dtype)
        return
    for t in range(val.shape[1] // V7X_LANES):
        ref[t, rows, :] = val[:, V7X_LANES * t:V7X_LANES * (t + 1)].astype(ref.dtype)


def _qkv_kernel(x_ref, g_ref, w_ref, cos_ref, sin_ref, qg_ref, kg_ref, ones_ref,
                q_ref, k_ref, v_ref, *rest, n_q_tiles, lane_major, row_chunks, emit_bf16):
    n = pl.program_id(1)
    if emit_bf16:
        w_out_ref, xn_ref = rest
        w_out_ref[...] = w_ref[0].astype(BF16)
        w_ref = w_out_ref
    else:
        (xn_ref,) = rest

    @pl.when(n == 0)
    def _():
        xn_ref[...] = _rms(x_ref[...], g_ref[...]).astype(BF16)

    rc = xn_ref.shape[0] // row_chunks

    def project(out_ref, epilogue):
        chunk = lambda c: slice(rc * c, rc * (c + 1))
        matmul = lambda c: jnp.dot(xn_ref[chunk(c), :], w_ref[...], preferred_element_type=F32)
        acc_next = matmul(0)
        for c in range(row_chunks):
            acc = acc_next
            if c + 1 < row_chunks:
                acc_next = matmul(c + 1)
            _store_cols(out_ref, chunk(c), epilogue(acc, chunk(c)), lane_major)

    def norm_rope(gain_ref, scale):
        def fn(acc, rows):
            r = _head_norm_rope(acc, gain_ref[...], ones_ref[...], cos_ref[rows, :],
                                sin_ref[rows, :])
            return r * scale if scale != 1.0 else r
        return fn

    @pl.when(n < n_q_tiles)
    def _():
        project(q_ref, norm_rope(qg_ref, Q_SCALE))

    @pl.when(n == n_q_tiles)
    def _():
        project(k_ref, norm_rope(kg_ref, 1.0))

    @pl.when(n == n_q_tiles + 1)
    def _():
        project(v_ref, lambda acc, rows: acc)


def _qkv_call(x, g, w, cos, sin, qg, kg, ones_blk, *, n_q, n_kv, tm, q_dtype, lane_major=False,
              cast_layer=None):
    m = x.shape[0]
    ncols = w.shape[-1]
    tn = ncols // N_QKV_TILES
    n_q_tiles = (n_q * HEAD_DIM) // tn
    assert n_q_tiles * tn == n_q * HEAD_DIM and n_kv * HEAD_DIM == tn and tn % V7X_MXU_DIM == 0
    assert m % tm == 0
    last_q = n_q_tiles - 1
    if lane_major:
        lt = tn // V7X_LANES
        out_specs = [
            pl.BlockSpec((lt, tm, V7X_LANES), lambda i, n: (jnp.minimum(n, last_q), i, 0)),
            pl.BlockSpec((lt, tm, V7X_LANES), lambda i, n: (0, i, 0)),
            pl.BlockSpec((lt, tm, V7X_LANES), lambda i, n: (0, i, 0)),
        ]
        out_shape = [
            jax.ShapeDtypeStruct((n_q_tiles * lt, m, V7X_LANES), q_dtype),
            jax.ShapeDtypeStruct((lt, m, V7X_LANES), F32),
            jax.ShapeDtypeStruct((lt, m, V7X_LANES), F32),
        ]
    else:
        out_specs = [
            pl.BlockSpec((tm, tn), lambda i, n: (i, jnp.minimum(n, last_q))),
            pl.BlockSpec((tm, tn), lambda i, n: (i, 0)),
            pl.BlockSpec((tm, tn), lambda i, n: (i, 0)),
        ]
        out_shape = [
            jax.ShapeDtypeStruct((m, n_q * HEAD_DIM), q_dtype),
            jax.ShapeDtypeStruct((m, tn), F32),
            jax.ShapeDtypeStruct((m, tn), F32),
        ]
    emit_bf16 = cast_layer is not None
    if emit_bf16:
        assert m == tm
        w_spec = pl.BlockSpec((1, D_MODEL, tn), lambda i, n: (cast_layer, 0, n))
        out_specs = out_specs + [pl.BlockSpec((D_MODEL, tn), lambda i, n: (0, n))]
        out_shape = out_shape + [jax.ShapeDtypeStruct((D_MODEL, ncols), BF16)]
    else:
        w_spec = pl.BlockSpec((D_MODEL, tn), lambda i, n: (0, n))
    return pl.pallas_call(
        functools.partial(_qkv_kernel, n_q_tiles=n_q_tiles, lane_major=lane_major,
                          row_chunks=max(1, tm // PROJ_CHUNK_ROWS), emit_bf16=emit_bf16),
        grid=(m // tm, N_QKV_TILES),
        in_specs=[
            pl.BlockSpec((tm, D_MODEL), lambda i, n: (i, 0)),
            pl.BlockSpec((1, D_MODEL), lambda i, n: (0, 0)),
            w_spec,
            pl.BlockSpec((tm, V7X_LANES), lambda i, n: (i, 0)),
            pl.BlockSpec((tm, V7X_LANES), lambda i, n: (i, 0)),
            pl.BlockSpec((1, tn), lambda i, n: (0, 0)),
            pl.BlockSpec((1, tn), lambda i, n: (0, 0)),
            pl.BlockSpec((V7X_MXU_DIM, V7X_MXU_DIM), lambda i, n: (0, 0)),
        ],
        out_specs=out_specs,
        out_shape=out_shape,
        scratch_shapes=[pltpu.VMEM((tm, D_MODEL), BF16)],
        compiler_params=_params("parallel", "arbitrary"),
        name="qkv_proj",
    )(x, g, w, cos, sin, qg, kg, ones_blk)


GRP = 4


def _band_mask_t(first):
    shape = (2 * BLOCK, BLOCK)
    kj = lax.broadcasted_iota(jnp.int32, shape, 0)
    qi = lax.broadcasted_iota(jnp.int32, shape, 1)
    seen = (kj >= qi) & (kj <= qi + BLOCK) & ((kj >= BLOCK) | jnp.logical_not(first))
    return jnp.where(seen, 0.0, NEG_INF)


def _kv_tile_forms(kt, vt):
    lo = lax.broadcasted_iota(jnp.int32, kt.shape, 1) < HEAD_DIM
    kt_sw = pltpu.roll(kt, HEAD_DIM, 1)
    k_forms = [(jnp.where(lo, kt, 0.0).astype(BF16), jnp.where(lo, 0.0, kt_sw).astype(BF16)),
               (jnp.where(lo, kt_sw, 0.0).astype(BF16), jnp.where(lo, 0.0, kt).astype(BF16))]
    return k_forms, vt.T.astype(BF16)


def _head_scores(k_lo, k_hi, q_a, q_b):
    nt = (((1,), (1,)), ((), ()))
    qpair = jnp.concatenate([q_a, q_b], axis=0)
    return jnp.concatenate([lax.dot_general(k_lo, qpair, nt, preferred_element_type=F32),
                            lax.dot_general(k_hi, qpair, nt, preferred_element_type=F32)], axis=1)


def _head_softmax_pv(s, mask4, v_t, sinks, want_lse):
    s = s + mask4
    m = jnp.max(s, axis=0, keepdims=True)
    if sinks is not None:
        sk = jnp.concatenate([jnp.full((1, BLOCK), sinks[j] * LOG2E, F32)
                              for j in (0, 2, 1, 3)], axis=1)
        m = jnp.maximum(m, sk)
    p = jnp.exp2(s - m)
    l = jnp.sum(p, axis=0, keepdims=True)
    if sinks is not None:
        l = l + jnp.exp2(sk - m)
    o_t = jnp.dot(v_t, p.astype(BF16), preferred_element_type=F32) / l
    lse_t = jnp.broadcast_to((m + jnp.log2(l)) * LN2, o_t.shape) if want_lse else None
    o_tiles, l_tiles = [], []
    for u in range(2):
        c0, c1 = slice(BLOCK * u, BLOCK * (u + 1)), slice(BLOCK * (2 + u), BLOCK * (3 + u))
        o_tiles.append(jnp.concatenate([o_t[:, c0], o_t[:, c1]], axis=0).T)
        if want_lse:
            l_tiles.append(jnp.concatenate([lse_t[:, c0], lse_t[:, c1]], axis=0).T)
    return o_tiles, l_tiles


def _attend_tiles(tiles, want_lse, fillers=None):
    jobs = [(ti, e) for ti in range(len(tiles)) for e in range(2)]
    forms = {}

    def scores(job):
        ti, e = job
        load_kv, load_q = tiles[ti][:2]
        if ti not in forms:
            forms[ti] = _kv_tile_forms(*load_kv())
        k_lo, k_hi = forms[ti][0][e]
        return _head_scores(k_lo, k_hi, load_q(2 * e), load_q(2 * e + 1))

    pending = [scores(job) for job in jobs[:ATTN_DEPTH]]
    for idx, (ti, e) in enumerate(jobs):
        s = pending.pop(0)
        if idx + ATTN_DEPTH < len(jobs):
            pending.append(scores(jobs[idx + ATTN_DEPTH]))
        sink_of, store, mask4 = tiles[ti][2:]
        v_t = forms[ti][1][HEAD_DIM * e:HEAD_DIM * (e + 1)]
        sinks = None if sink_of is None else [sink_of(GRP * e + j) for j in range(GRP)]
        o_tiles, l_tiles = _head_softmax_pv(s, mask4, v_t, sinks, want_lse)
        for u in range(2):
            store(2 * e + u, o_tiles[u], l_tiles[u] if want_lse else None)
        if fillers and idx in fillers:
            fillers[idx]()


def _band_wo_kernel(q_ref, kc_ref, kp_ref, vc_ref, vp_ref, sink_ref, y_ref, wo_ref, out_ref,
                    o_scr, *, kv_tiles, nsub):
    mask_first = jnp.tile(_band_mask_t(pl.program_id(1) == 0), (1, GRP))
    mask_rest = jnp.tile(_band_mask_t(False), (1, GRP))
    lanes = lambda t: slice(V7X_LANES * t, V7X_LANES * (t + 1))
    rows = lambda j: slice(BLOCK * j, BLOCK * (j + 1))

    def tile(j, t):
        def load_kv():
            if j == 0:
                kp, vp = kp_ref[0, :, lanes(t)], vp_ref[0, :, lanes(t)]
            else:
                kp, vp = kc_ref[0, rows(j - 1), lanes(t)], vc_ref[0, rows(j - 1), lanes(t)]
            return (jnp.concatenate([kp, kc_ref[0, rows(j), lanes(t)]], axis=0),
                    jnp.concatenate([vp, vc_ref[0, rows(j), lanes(t)]], axis=0))

        def store(u, o_tile, _):
            o_scr[rows(j), lanes(GRP * t + u)] = o_tile.astype(o_scr.dtype)

        return (load_kv, lambda u: q_ref[0, rows(j), lanes(GRP * t + u)],
                lambda h: sink_ref[2 * GRP * t + h], store, mask_first if j == 0 else mask_rest)

    jobs_per_block = 2 * kv_tiles
    n_out = out_ref.shape[2]
    piece = n_out // jobs_per_block

    def project(j, c):
        cols = slice(piece * c, piece * (c + 1))
        out_ref[0, rows(j), cols] = y_ref[0, rows(j), cols] + jnp.dot(
            o_scr[rows(j), :], wo_ref[:, cols], preferred_element_type=F32)

    fillers = {jobs_per_block * j + c: functools.partial(project, j - 1, c)
               for j in range(1, nsub) for c in range(jobs_per_block)}
    _attend_tiles([tile(j, t) for j in range(nsub) for t in range(kv_tiles)], False, fillers)
    for c in range(jobs_per_block):
        project(nsub - 1, c)


def _band_wo_call(q, k, v, sinks, y, wo, *, hkv, grp):
    b, seq, cq = q.shape
    ck = hkv * HEAD_DIM
    nsub = BAND_SUBBLOCKS
    rows = BLOCK * nsub
    assert grp == GRP and hkv % 2 == 0 and seq % rows == 0
    cur = lambda bi, c: (bi, c, 0)
    prev = lambda bi, c: (bi, jnp.maximum(c * nsub - 1, 0), 0)
    return pl.pallas_call(
        functools.partial(_band_wo_kernel, kv_tiles=hkv // 2, nsub=nsub),
        grid=(b, seq // rows),
        in_specs=[
            pl.BlockSpec((1, rows, cq), cur),
            pl.BlockSpec((1, rows, ck), cur),
            pl.BlockSpec((1, BLOCK, ck), prev),
            pl.BlockSpec((1, rows, ck), cur),
            pl.BlockSpec((1, BLOCK, ck), prev),
            pl.BlockSpec(memory_space=pltpu.SMEM),
            pl.BlockSpec((1, rows, D_MODEL), cur),
            pl.BlockSpec((cq, D_MODEL), lambda bi, c: (0, 0)),
        ],
        out_specs=pl.BlockSpec((1, rows, D_MODEL), cur),
        out_shape=jax.ShapeDtypeStruct((b, seq, D_MODEL), F32),
        scratch_shapes=[pltpu.VMEM((rows, cq), BF16)],
        compiler_params=_params("parallel", "arbitrary"),
        name="band_attn_wo",
    )(q, k, k, v, v, sinks, y, wo)


def _dil_kernel(q_ref, kc_ref, kp_ref, vc_ref, vp_ref, o_ref, lse_ref, *, kv_tiles, grp, d, nsub):
    mask_first = jnp.tile(_band_mask_t(pl.program_id(1) == 0), (1, GRP))

    if d == 1:
        mask_rest = jnp.tile(_band_mask_t(False), (1, GRP))
        rows = lambda j: slice(BLOCK * j, BLOCK * (j + 1))

        def sub_tile(j, t):
            def load_kv():
                if j == 0:
                    kp, vp = kp_ref[t], vp_ref[t]
                else:
                    kp, vp = kc_ref[t, rows(j - 1), :], vc_ref[t, rows(j - 1), :]
                return (jnp.concatenate([kp, kc_ref[t, rows(j), :]], axis=0),
                        jnp.concatenate([vp, vc_ref[t, rows(j), :]], axis=0))

            def store(u, o_tile, l_tile):
                o_ref[grp * t + u, rows(j), :] = o_tile
                lse_ref[grp * t + u, rows(j), :] = l_tile

            return (load_kv, lambda u: q_ref[grp * t + u, rows(j), :].astype(BF16), None, store,
                    mask_first if j == 0 else mask_rest)

        _attend_tiles([sub_tile(j, t) for j in range(nsub) for t in range(kv_tiles)], True)
        return

    def tile(r, t):
        rs = pl.ds(r, BLOCK, stride=d)

        def load_kv():
            return (jnp.concatenate([kp_ref[t, rs, :], kc_ref[t, rs, :]], axis=0),
                    jnp.concatenate([vp_ref[t, rs, :], vc_ref[t, rs, :]], axis=0))

        def store(u, o_tile, l_tile):
            o_ref[grp * t + u, rs, :] = o_tile
            lse_ref[grp * t + u, rs, :] = l_tile

        return (load_kv, lambda u: q_ref[grp * t + u, rs, :].astype(BF16), None, store,
                mask_first)

    unroll = min(d, max(1, DIL_TILES_PER_TRIP // kv_tiles))

    def body(i, carry):
        _attend_tiles([tile(i * unroll + rr, t) for rr in range(unroll) for t in range(kv_tiles)],
                      True)
        return carry

    if d > unroll:
        lax.fori_loop(0, d // unroll, body, 0)
    else:
        body(0, 0)


def _dil_call(q, k, v, *, nb, hkv, grp, d, group):
    m = q.shape[1]
    seq = m // nb
    nsub = BAND_SUBBLOCKS if d == 1 else 1
    rows = BLOCK * d * nsub
    hsplit = max(1, rows // DIL_MAX_ROWS)
    kv_tiles = hkv // 2 // hsplit
    q_tiles = kv_tiles * grp
    nchunk = seq // rows
    assert seq % rows == 0 and kv_tiles * 2 * hsplit == hkv and grp % 2 == 0
    cur = lambda bi, c, hp: (group * hsplit + hp, bi * nchunk + c, 0)
    if d == 1:
        prev_rows = BLOCK
        prev = lambda bi, c, hp: (group * hsplit + hp,
                                  jnp.maximum((bi * nchunk + c) * nsub - 1, 0), 0)
    else:
        prev_rows = rows
        prev = lambda bi, c, hp: (group * hsplit + hp, bi * nchunk + jnp.maximum(c - 1, 0), 0)
    o_spec = pl.BlockSpec((q_tiles, rows, V7X_LANES), lambda bi, c, hp: (hp, bi * nchunk + c, 0))
    o_shape = jax.ShapeDtypeStruct((q_tiles * hsplit, m, V7X_LANES), F32)
    return pl.pallas_call(
        functools.partial(_dil_kernel, kv_tiles=kv_tiles, grp=grp, d=d, nsub=nsub),
        grid=(nb, nchunk, hsplit),
        in_specs=[
            pl.BlockSpec((q_tiles, rows, V7X_LANES), cur),
            pl.BlockSpec((kv_tiles, rows, V7X_LANES), cur),
            pl.BlockSpec((kv_tiles, prev_rows, V7X_LANES), prev),
            pl.BlockSpec((kv_tiles, rows, V7X_LANES), cur),
            pl.BlockSpec((kv_tiles, prev_rows, V7X_LANES), prev),
        ],
        out_specs=[o_spec, o_spec],
        out_shape=[o_shape, o_shape],
        compiler_params=_params("parallel", "arbitrary", "arbitrary"),
        name="dilated_attn",
    )(q, k, k, v, v)


def _keep_t_kernel(x_ref, o_ref):
    o_ref[0] = x_ref[0].T


def _keep_t_lane_major_kernel(x_ref, o_ref, *, keeps):
    g = pl.program_id(2)
    seq = x_ref.shape[2]
    off = 0
    for k, keep in enumerate(keeps):
        @pl.when(g == k)
        def _(keep=keep, off=off):
            o_ref[0, :, off:off + keep] = x_ref[0, 0, seq - keep:, :].T
        off += keep


def _keep_t_call(x, keeps, c, *, lane_major=False):
    b, seq = x.shape[1:3] if lane_major else x.shape[:2]
    starts, first_rb = [], []
    n = 0
    for keep in keeps:
        assert keep % BLOCK == 0 and seq % BLOCK == 0
        starts.append(n)
        first_rb.append((seq - keep) // BLOCK)
        n += keep // BLOCK

    def src_block(j):
        rb = jnp.int32(0)
        cb = jnp.int32(0)
        for g in range(len(keeps)):
            inside = j >= starts[g]
            rb = jnp.where(inside, first_rb[g] + j - starts[g], rb)
            cb = jnp.where(inside, g, cb)
        return rb, cb

    if lane_major:
        ct = c // V7X_LANES
        grid = (b, ct, len(keeps))
        in_spec = pl.BlockSpec((1, 1, seq, V7X_LANES), lambda bi, t, g: (g * ct + t, bi, 0, 0))
        out_spec = pl.BlockSpec((1, V7X_LANES, n * BLOCK), lambda bi, t, g: (bi, t, 0))
        kern = functools.partial(_keep_t_lane_major_kernel, keeps=tuple(keeps))
        sem = ("parallel", "parallel", "arbitrary")
    else:
        grid = (b, n)

        def src(bi, j):
            rb, cb = src_block(j)
            return bi, rb, cb

        in_spec = pl.BlockSpec((1, BLOCK, c), src)
        out_spec = pl.BlockSpec((1, c, BLOCK), lambda bi, j: (bi, 0, j))
        kern = _keep_t_kernel
        sem = ("parallel", "parallel")
    return pl.pallas_call(
        kern,
        grid=grid,
        in_specs=[in_spec],
        out_specs=out_spec,
        out_shape=jax.ShapeDtypeStruct((b, c, n * BLOCK), F32),
        compiler_params=_params(*sem),
        name="keep_rows_t",
    )(x)


def _wo_comb_kernel(y_ref, o0_ref, o1_ref, o2_ref, l0_ref, l1_ref, l2_ref, w_ref, out_ref):
    tiles = []
    for t in range(o0_ref.shape[0]):
        l0, l1, l2 = l0_ref[t], l1_ref[t], l2_ref[t]
        mx = jnp.maximum(jnp.maximum(l0, l1), l2)
        e0, e1, e2 = jnp.exp(l0 - mx), jnp.exp(l1 - mx), jnp.exp(l2 - mx)
        den = e0 + e1 + e2
        comb = (e0 / den) * o0_ref[t] + (e1 / den) * o1_ref[t] + (e2 / den) * o2_ref[t]
        tiles.append(comb.astype(BF16))
    comb = jnp.concatenate(tiles, axis=1)
    out_ref[...] = y_ref[...] + jnp.dot(comb, w_ref[...], preferred_element_type=F32)


def _wo_cast_kernel(y_ref, o_ref, w_ref, out_ref, w_out_ref):
    w_out_ref[...] = w_ref[0].astype(BF16)
    out_ref[...] = y_ref[...] + jnp.dot(o_ref[...].astype(BF16), w_out_ref[...],
                                        preferred_element_type=F32)


def _wo_cast_call(y, o, w, layer):
    m = y.shape[0]
    c = o.shape[1]
    tn = TN_WO_CAST
    return pl.pallas_call(
        _wo_cast_kernel,
        grid=(D_MODEL // tn,),
        in_specs=[
            pl.BlockSpec((m, tn), lambda n: (0, n)),
            pl.BlockSpec((m, c), lambda n: (0, 0)),
            pl.BlockSpec((1, c, tn), lambda n: (layer, 0, n)),
        ],
        out_specs=[pl.BlockSpec((m, tn), lambda n: (0, n)),
                   pl.BlockSpec((c, tn), lambda n: (0, n))],
        out_shape=[jax.ShapeDtypeStruct((m, D_MODEL), F32),
                   jax.ShapeDtypeStruct((c, D_MODEL), BF16)],
        compiler_params=_params("parallel"),
        name="wo_proj_cast",
    )(y, o, w)


def _wo_comb_call(y, os_, ls_, w, *, tm):
    m = y.shape[0]
    c = w.shape[0]
    blk = pl.BlockSpec((c // V7X_LANES, tm, V7X_LANES), lambda i: (0, i, 0))
    return pl.pallas_call(
        _wo_comb_kernel,
        grid=(m // tm,),
        in_specs=[pl.BlockSpec((tm, D_MODEL), lambda i: (i, 0))] + [blk] * 6
                 + [pl.BlockSpec((c, D_MODEL), lambda i: (0, 0))],
        out_specs=pl.BlockSpec((tm, D_MODEL), lambda i: (i, 0)),
        out_shape=jax.ShapeDtypeStruct((m, D_MODEL), F32),
        compiler_params=_params("parallel"),
        name="wo_comb_proj",
    )(y, *os_, *ls_, w)


def _ffn_tail(gate, g1, g2, up, cw_ref, cb_ref, wd_ref):
    conv = cb_ref[...] + cw_ref[0:1, :] * g2 + cw_ref[1:2, :] * g1 + cw_ref[2:3, :] * gate
    h = conv * jax.nn.sigmoid(conv) * up
    return jnp.dot(h.astype(BF16), wd_ref[...], preferred_element_type=F32)


def _ffn_prompt_kernel(y_ref, g_ref, wg_ref, wu_ref, wd_ref, cw_ref, cb_ref,
                       out_ref, tail_ref, xn_ref, carry_ref, *, tiles_per_seq):
    m = pl.program_id(0)
    f = pl.program_id(1)

    @pl.when(f == 0)
    def _():
        x = y_ref[...]
        xn_ref[...] = _rms(x, g_ref[...]).astype(BF16)
        out_ref[...] = x

    @pl.when(m % tiles_per_seq == 0)
    def _():
        carry_ref[f] = jnp.zeros(carry_ref.shape[1:], F32)

    rc = xn_ref.shape[0] // FFN_ROW_CHUNKS
    last = carry_ref[f]
    for c in range(FFN_ROW_CHUNKS):
        rows = slice(rc * c, rc * (c + 1))
        xn = xn_ref[rows, :]
        gate = jnp.dot(xn, wg_ref[...], preferred_element_type=F32)
        up = jnp.dot(xn, wu_ref[...], preferred_element_type=F32)
        row = lax.broadcasted_iota(jnp.int32, gate.shape, 0)
        g1 = jnp.where(row == 0, last[7:8, :], pltpu.roll(gate, 1, 0))
        g2 = jnp.where(row == 0, last[6:7, :],
                       jnp.where(row == 1, last[7:8, :], pltpu.roll(gate, 2, 0)))
        out_ref[rows, :] += _ffn_tail(gate, g1, g2, up, cw_ref, cb_ref, wd_ref)
        last = gate[rc - V7X_SUBLANES:, :]
    carry_ref[f] = last
    tail_ref[0] = last


def _ffn_prompt_call(y, g, wg, wu, wd, cw, cb, *, seq):
    m = y.shape[0]
    tm, tf = TM_FFN, TF_FFN
    nf = D_FF // tf
    assert m % tm == 0 and seq % tm == 0 and D_FF % tf == 0
    return pl.pallas_call(
        functools.partial(_ffn_prompt_kernel, tiles_per_seq=seq // tm),
        grid=(m // tm, nf),
        in_specs=[
            pl.BlockSpec((tm, D_MODEL), lambda i, f: (i, 0)),
            pl.BlockSpec((1, D_MODEL), lambda i, f: (0, 0)),
            pl.BlockSpec((D_MODEL, tf), lambda i, f: (0, f)),
            pl.BlockSpec((D_MODEL, tf), lambda i, f: (0, f)),
            pl.BlockSpec((tf, D_MODEL), lambda i, f: (f, 0)),
            pl.BlockSpec((CONV_W, tf), lambda i, f: (0, f)),
            pl.BlockSpec((1, tf), lambda i, f: (0, f)),
        ],
        out_specs=[
            pl.BlockSpec((tm, D_MODEL), lambda i, f: (i, 0)),
            pl.BlockSpec((1, V7X_SUBLANES, tf), lambda i, f: (i, 0, f)),
        ],
        out_shape=[
            jax.ShapeDtypeStruct((m, D_MODEL), F32),
            jax.ShapeDtypeStruct((m // tm, V7X_SUBLANES, D_FF), F32),
        ],
        scratch_shapes=[pltpu.VMEM((tm, D_MODEL), BF16),
                        pltpu.VMEM((nf, V7X_SUBLANES, tf), F32)],
        compiler_params=_params("arbitrary", "arbitrary"),
        name="conv_ffn_prompt",
    )(y, g, wg, wu, wd, cw, cb)


def _ffn_sample_kernel(y_ref, g_ref, wg_ref, wu_ref, wd_ref, cw_ref, cb_ref, s0_ref, s1_ref,
                       out_ref, gate_ref, wg_out_ref, wu_out_ref, wd_out_ref, xn_ref, *, seq):
    f = pl.program_id(0)

    @pl.when(f == 0)
    def _():
        x = y_ref[...]
        xn_ref[...] = _rms(x, g_ref[...]).astype(BF16)
        out_ref[...] = x

    wg_out_ref[...] = wg_ref[0].astype(BF16)
    wu_out_ref[...] = wu_ref[0].astype(BF16)
    wd_out_ref[...] = wd_ref[0].astype(BF16)
    xn = xn_ref[...]
    gate = jnp.dot(xn, wg_out_ref[...], preferred_element_type=F32)
    up = jnp.dot(xn, wu_out_ref[...], preferred_element_type=F32)
    t = lax.broadcasted_iota(jnp.int32, gate.shape, 0) & (seq - 1)
    s0, s1 = s0_ref[...], s1_ref[...]
    g1 = jnp.where(t == 0, s1, pltpu.roll(gate, 1, 0))
    g2 = jnp.where(t == 0, s0, jnp.where(t == 1, s1, pltpu.roll(gate, 2, 0)))
    out_ref[...] += _ffn_tail(gate, g1, g2, up, cw_ref, cb_ref, wd_out_ref)
    gate_ref[...] = gate


def _ffn_sample_call(y, g, wg, wu, wd, layer, cw, cb, s0, s1, *, seq):
    m = y.shape[0]
    tf = TF_FFN
    nf = D_FF // tf
    full = pl.BlockSpec((m, D_MODEL), lambda f: (0, 0))
    col = pl.BlockSpec((m, tf), lambda f: (0, f))
    return pl.pallas_call(
        functools.partial(_ffn_sample_kernel, seq=seq),
        grid=(nf,),
        in_specs=[
            full,
            pl.BlockSpec((1, D_MODEL), lambda f: (0, 0)),
            pl.BlockSpec((1, D_MODEL, tf), lambda f: (layer, 0, f)),
            pl.BlockSpec((1, D_MODEL, tf), lambda f: (layer, 0, f)),
            pl.BlockSpec((1, tf, D_MODEL), lambda f: (layer, f, 0)),
            pl.BlockSpec((CONV_W, tf), lambda f: (0, f)),
            pl.BlockSpec((1, tf), lambda f: (0, f)),
            col, col,
        ],
        out_specs=[full, col,
                   pl.BlockSpec((D_MODEL, tf), lambda f: (0, f)),
                   pl.BlockSpec((D_MODEL, tf), lambda f: (0, f)),
                   pl.BlockSpec((tf, D_MODEL), lambda f: (f, 0))],
        out_shape=[jax.ShapeDtypeStruct((m, D_MODEL), F32),
                   jax.ShapeDtypeStruct((m, D_FF), F32),
                   jax.ShapeDtypeStruct((D_MODEL, D_FF), BF16),
                   jax.ShapeDtypeStruct((D_MODEL, D_FF), BF16),
                   jax.ShapeDtypeStruct((D_FF, D_MODEL), BF16)],
        scratch_shapes=[pltpu.VMEM((m, D_MODEL), BF16)],
        compiler_params=_params("arbitrary"),
        name="conv_ffn_sample",
    )(y, g, wg, wu, wd, cw, cb, s0, s1)


def _block_diag_q(q_ref, col0, hkv, grp, s):
    blocks = []
    for h in range(hkv):
        qh = jnp.concatenate(
            [q_ref[:, col0 + HEAD_DIM * (grp * h + j):col0 + HEAD_DIM * (grp * h + j + 1)]
             for j in range(grp)], axis=0)
        pieces = []
        if h > 0:
            pieces.append(jnp.zeros((grp * s, HEAD_DIM * h), F32))
        pieces.append(qh)
        if h < hkv - 1:
            pieces.append(jnp.zeros((grp * s, HEAD_DIM * (hkv - 1 - h)), F32))
        blocks.append(jnp.concatenate(pieces, axis=1) if len(pieces) > 1 else qh)
    return jnp.concatenate(blocks, axis=0).astype(BF16)


def _new_rows_t(x, s):
    pad = jnp.zeros((V7X_LANES - s, x.shape[1]), F32)
    return jnp.concatenate([pad, x], axis=0).T


def _cached_attend(qbd, ck_t, kn_t, cv_t, vn_t, d, s, sink_col):
    r_, lc = qbd.shape[0], ck_t.shape[1]
    sc = jnp.dot(qbd, ck_t.astype(BF16), preferred_element_type=F32)
    sn = jnp.dot(qbd, kn_t.astype(BF16), preferred_element_type=F32)
    ic = lax.broadcasted_iota(jnp.int32, (r_, lc), 0) & (s - 1)
    c = lax.broadcasted_iota(jnp.int32, (r_, lc), 1)
    i_n = lax.broadcasted_iota(jnp.int32, (r_, V7X_LANES), 0) & (s - 1)
    j = lax.broadcasted_iota(jnp.int32, (r_, V7X_LANES), 1) - (V7X_LANES - s)
    valid_c = c >= ic
    valid_n = (j >= 0) & (j <= i_n)
    if d > 1:
        valid_c = valid_c & ((c & (d - 1)) == (ic & (d - 1)))
        valid_n = valid_n & ((j & (d - 1)) == (i_n & (d - 1)))
    sc = jnp.where(valid_c, sc, NEG_INF)
    sn = jnp.where(valid_n, sn, NEG_INF)
    m = jnp.maximum(jnp.max(sc, axis=1, keepdims=True), jnp.max(sn, axis=1, keepdims=True))
    if sink_col is not None:
        m = jnp.maximum(m, sink_col)
    pc = jnp.exp2(sc - m)
    pn = jnp.exp2(sn - m)
    l = jnp.sum(pc, axis=1, keepdims=True) + jnp.sum(pn, axis=1, keepdims=True)
    if sink_col is not None:
        l = l + jnp.exp2(sink_col - m)
    nt = (((1,), (1,)), ((), ()))
    o = (lax.dot_general(pc.astype(BF16), cv_t.astype(BF16), nt, preferred_element_type=F32)
         + lax.dot_general(pn.astype(BF16), vn_t.astype(BF16), nt, preferred_element_type=F32)) / l
    return o, (m + jnp.log2(l)) * LN2


def _diag_heads(o, hkv, grp, s):
    pieces = []
    for h in range(hkv):
        for j in range(grp):
            r0 = (h * grp + j) * s
            pieces.append(o[r0:r0 + s, HEAD_DIM * h:HEAD_DIM * (h + 1)])
    return jnp.concatenate(pieces, axis=1)


def _store_shifted(out_ref, off, c_t, n_t, s):
    lc = c_t.shape[1]
    rolled = pltpu.roll(c_t, lc - s, 1)
    lane = lax.broadcasted_iota(jnp.int32, n_t.shape, 1)
    if lc > V7X_LANES:
        out_ref[0, :, off:off + lc - V7X_LANES] = rolled[:, :lc - V7X_LANES]
    out_ref[0, :, off + lc - V7X_LANES:off + lc] = jnp.where(
        lane < V7X_LANES - s, rolled[:, lc - V7X_LANES:], n_t)


def _sample_a_kernel(q_ref, kn_ref, vn_ref, ck_ref, cv_ref, sink_ref, o_ref, ko_ref, vo_ref, *, s):
    hkv, grp = A_KV_HEADS, A_Q_HEADS // A_KV_HEADS
    kn_t, vn_t = _new_rows_t(kn_ref[...], s), _new_rows_t(vn_ref[...], s)
    ck_t, cv_t = ck_ref[0], cv_ref[0]
    qbd = _block_diag_q(q_ref, 0, hkv, grp, s)
    sink_col = jnp.concatenate(
        [jnp.full((s, 1), sink_ref[hq] * LOG2E, F32) for hq in range(hkv * grp)], axis=0)
    o, _ = _cached_attend(qbd, ck_t, kn_t, cv_t, vn_t, 1, s, sink_col)
    o_ref[...] = _diag_heads(o, hkv, grp, s)
    _store_shifted(ko_ref, 0, ck_t, kn_t, s)
    _store_shifted(vo_ref, 0, cv_t, vn_t, s)


def _sample_a_call(q, kn, vn, ck_t, cv_t, sinks, *, s):
    nb, c, lc = ck_t.shape
    row = lambda w: pl.BlockSpec((s, w), lambda b: (b, 0))
    cache = pl.BlockSpec((1, c, lc), lambda b: (b, 0, 0))
    return pl.pallas_call(
        functools.partial(_sample_a_kernel, s=s),
        grid=(nb,),
        in_specs=[row(q.shape[1]), row(c), row(c), cache, cache,
                  pl.BlockSpec(memory_space=pltpu.SMEM)],
        out_specs=[row(q.shape[1]), cache, cache],
        out_shape=[jax.ShapeDtypeStruct(q.shape, F32),
                   jax.ShapeDtypeStruct(ck_t.shape, F32),
                   jax.ShapeDtypeStruct(cv_t.shape, F32)],
        compiler_params=_params("parallel"),
        name="sample_attn_a",
    )(q, kn, vn, ck_t, cv_t, sinks)


def _sample_b_kernel(q_ref, kn_ref, vn_ref, ck_ref, cv_ref, o_ref, ko_ref, vo_ref, *, s):
    hkv, grp = B_KV_HEADS, B_Q_HEADS // B_KV_HEADS
    ckv = hkv * HEAD_DIM
    outs, lses = [], []
    off = 0
    for g, (w, d) in enumerate(B_PATTERNS):
        lc = w
        kn_t = _new_rows_t(kn_ref[:, ckv * g:ckv * (g + 1)], s)
        vn_t = _new_rows_t(vn_ref[:, ckv * g:ckv * (g + 1)], s)
        ck_t = ck_ref[0, :, off:off + lc]
        cv_t = cv_ref[0, :, off:off + lc]
        qbd = _block_diag_q(q_ref, B_Q_HEADS * HEAD_DIM * g, hkv, grp, s)
        o, lse = _cached_attend(qbd, ck_t, kn_t, cv_t, vn_t, d, s, None)
        outs.append(o)
        lses.append(lse)
        _store_shifted(ko_ref, off, ck_t, kn_t, s)
        _store_shifted(vo_ref, off, cv_t, vn_t, s)
        off += lc
    mx = jnp.maximum(jnp.maximum(lses[0], lses[1]), lses[2])
    es = [jnp.exp(l - mx) for l in lses]
    den = es[0] + es[1] + es[2]
    comb = (es[0] / den) * outs[0] + (es[1] / den) * outs[1] + (es[2] / den) * outs[2]
    o_ref[...] = _diag_heads(comb, hkv, grp, s)


def _sample_b_call(q, kn, vn, ck_t, cv_t, *, s):
    nb, c, lb = ck_t.shape
    row = lambda w: pl.BlockSpec((s, w), lambda b: (b, 0))
    cache = pl.BlockSpec((1, c, lb), lambda b: (b, 0, 0))
    co = B_Q_HEADS * HEAD_DIM
    return pl.pallas_call(
        functools.partial(_sample_b_kernel, s=s),
        grid=(nb,),
        in_specs=[row(q.shape[1]), row(kn.shape[1]), row(vn.shape[1]), cache, cache],
        out_specs=[row(co), cache, cache],
        out_shape=[jax.ShapeDtypeStruct((q.shape[0], co), F32),
                   jax.ShapeDtypeStruct(ck_t.shape, F32),
                   jax.ShapeDtypeStruct(cv_t.shape, F32)],
        compiler_params=_params("parallel"),
        name="sample_attn_b",
    )(q, kn, vn, ck_t, cv_t)


def _rope_tables(pos):
    inv_freq = ROPE_THETA ** (-jnp.arange(HALF, dtype=F32) / HALF)
    ang = pos.astype(F32)[:, None] * inv_freq[None, :]
    cos, sin = jnp.cos(ang), jnp.sin(ang)
    return jnp.tile(cos, (1, 4)), jnp.tile(jnp.concatenate([-sin, sin], axis=1), (1, 2))


def _cache_t(cache):
    b, l, h, dh = cache.shape
    return jnp.transpose(cache, (0, 2, 3, 1)).reshape(b, h * dh, l)


def _cache_from_t(x, h):
    b, c, l = x.shape
    return jnp.transpose(x.reshape(b, h, c // h, l), (0, 3, 1, 2))[None]


def kernel(x_prompt, x_sample, cache_a_k, cache_a_v, cache_b_k, cache_b_v, state_ffn_conv,
           attn_norm, ffn_norm, a_w_qkv, a_q_norm, a_k_norm, a_sinks, a_w_o,
           b_w_qkv, b_q_norm, b_k_norm, b_w_o,
           ffn_w_gate, ffn_w_up, ffn_conv_w, ffn_conv_b, ffn_w_down):
    nb, seq, _ = x_prompt.shape
    ns, dec, _ = x_sample.shape
    mp, ms = nb * seq, ns * dec

    cos_p, sin_p = _rope_tables(jnp.tile(jnp.arange(seq, dtype=jnp.int32), nb))
    cos_s, sin_s = _rope_tables(jnp.tile(PAST_LEN + jnp.arange(dec, dtype=jnp.int32), ns))
    idx = jnp.arange(V7X_MXU_DIM, dtype=jnp.int32) // HEAD_DIM
    ones_blk = (idx[:, None] == idx[None, :]).astype(BF16)

    yp = x_prompt.reshape(mp, D_MODEL)
    ys = x_sample.reshape(ms, D_MODEL)
    row = lambda a: a.reshape(1, -1)

    def head_gain(gv, n_heads_per_tile):
        return jnp.tile(gv, n_heads_per_tile).reshape(1, -1)

    tn = a_w_qkv.shape[-1] // N_QKV_TILES
    qg, kg = head_gain(a_q_norm[0], tn // HEAD_DIM), head_gain(a_k_norm[0], tn // HEAD_DIM)
    g_attn = row(attn_norm[0])
    grp_a = A_Q_HEADS // A_KV_HEADS

    qs, ks, vs, wqkv = _qkv_call(ys, g_attn, a_w_qkv, cos_s, sin_s, qg, kg, ones_blk,
                                 n_q=A_Q_HEADS, n_kv=A_KV_HEADS, tm=ms, q_dtype=F32, cast_layer=0)
    os_, ako, avo = _sample_a_call(qs, ks, vs, _cache_t(cache_a_k[0]), _cache_t(cache_a_v[0]),
                                   a_sinks[0], s=dec)
    ys, wo = _wo_cast_call(ys, os_, a_w_o, 0)
    a_k_sample = _cache_from_t(ako, A_KV_HEADS)
    a_v_sample = _cache_from_t(avo, A_KV_HEADS)

    q, k, v = _qkv_call(yp, g_attn, wqkv, cos_p, sin_p, qg, kg, ones_blk,
                        n_q=A_Q_HEADS, n_kv=A_KV_HEADS, tm=TM_QKV, q_dtype=BF16)
    k3, v3 = k.reshape(nb, seq, -1), v.reshape(nb, seq, -1)
    yp = _band_wo_call(q.reshape(nb, seq, -1), k3, v3, a_sinks[0], yp.reshape(nb, seq, -1), wo,
                       hkv=A_KV_HEADS, grp=grp_a).reshape(mp, D_MODEL)
    keep = min(A_WINDOW, seq)
    ca = A_KV_HEADS * HEAD_DIM
    a_k_prompt = _cache_from_t(_keep_t_call(k3, [keep], ca), A_KV_HEADS)
    a_v_prompt = _cache_from_t(_keep_t_call(v3, [keep], ca), A_KV_HEADS)

    conv_p, conv_s = [], []

    def ffn(layer, yp, ys):
        g_ffn = row(ffn_norm[layer])
        cw, cb = ffn_conv_w[layer], row(ffn_conv_b[layer])
        st = state_ffn_conv[layer]
        s0 = jnp.repeat(st[:, 0, :], dec, axis=0)
        s1 = jnp.repeat(st[:, 1, :], dec, axis=0)
        ys, gate_s, wg, wu, wd = _ffn_sample_call(ys, g_ffn, ffn_w_gate, ffn_w_up, ffn_w_down,
                                                  layer, cw, cb, s0, s1, seq=dec)
        conv_s.append(gate_s.reshape(ns, dec, D_FF)[:, dec - (CONV_W - 1):, :])
        yp, tail = _ffn_prompt_call(yp, g_ffn, wg, wu, wd, cw, cb, seq=seq)
        tiles = seq // TM_FFN
        conv_p.append(tail[tiles - 1::tiles, V7X_SUBLANES - (CONV_W - 1):, :])
        return yp, ys

    yp, ys = ffn(0, yp, ys)

    tn = b_w_qkv.shape[-1] // N_QKV_TILES
    qg, kg = head_gain(b_q_norm[0], tn // HEAD_DIM), head_gain(b_k_norm[0], tn // HEAD_DIM)
    g_attn = row(attn_norm[1])
    n_grp = len(B_PATTERNS)
    grp_b = B_Q_HEADS // B_KV_HEADS
    nqb, nkvb = n_grp * B_Q_HEADS, n_grp * B_KV_HEADS
    cb_ = B_KV_HEADS * HEAD_DIM

    qs, ks, vs, wqkv = _qkv_call(ys, g_attn, b_w_qkv, cos_s, sin_s, qg, kg, ones_blk,
                                 n_q=nqb, n_kv=nkvb, tm=ms, q_dtype=F32, cast_layer=0)
    os_, bko, bvo = _sample_b_call(qs, ks, vs, _cache_t(cache_b_k[0]), _cache_t(cache_b_v[0]),
                                   s=dec)
    ys, wo = _wo_cast_call(ys, os_, b_w_o, 0)
    b_k_sample = _cache_from_t(bko, B_KV_HEADS)
    b_v_sample = _cache_from_t(bvo, B_KV_HEADS)

    q, k, v = _qkv_call(yp, g_attn, wqkv, cos_p, sin_p, qg, kg, ones_blk,
                        n_q=nqb, n_kv=nkvb, tm=TM_QKV, q_dtype=F32, lane_major=True)
    outs, lses = [], []
    for g, (w, d) in enumerate(B_PATTERNS):
        assert w // d == BLOCK
        og, lg = _dil_call(q, k, v, nb=nb, hkv=B_KV_HEADS, grp=grp_b, d=d, group=g)
        outs.append(og)
        lses.append(lg)
    yp = _wo_comb_call(yp, outs, lses, wo, tm=TM_PROJ // 2)
    keeps = [min(w, seq) for w, _ in B_PATTERNS]
    k4 = k.reshape(k.shape[0], nb, seq, V7X_LANES)
    v4 = v.reshape(v.shape[0], nb, seq, V7X_LANES)
    b_k_prompt = _cache_from_t(_keep_t_call(k4, keeps, cb_, lane_major=True), B_KV_HEADS)
    b_v_prompt = _cache_from_t(_keep_t_call(v4, keeps, cb_, lane_major=True), B_KV_HEADS)

    yp, ys = ffn(1, yp, ys)

    return (yp.reshape(nb, seq, D_MODEL), ys.reshape(ns, dec, D_MODEL),
            a_k_prompt, a_v_prompt, a_k_sample, a_v_sample,
            b_k_prompt, b_v_prompt, b_k_sample, b_v_sample,
            jnp.stack(conv_p), jnp.stack(conv_s))
```

```python
import functools

import jax
import jax.numpy as jnp
from jax import lax
from jax.experimental import pallas as pl
from jax.experimental.pallas import tpu as pltpu

F32 = jnp.float32
BF16 = jnp.bfloat16

D_MODEL = 2048
HEAD_DIM = 64
HALF = HEAD_DIM // 2
ROPE_THETA = 10000.0
NORM_EPS = 1e-6
BLOCK = 128
PAST_LEN = 16384
A_WINDOW = 128
A_Q_HEADS = 32
A_KV_HEADS = 8
B_PATTERNS = ((128, 1), (512, 4), (2048, 16))
B_Q_HEADS = 16
B_KV_HEADS = 4
D_FF = 5632
CONV_W = 3
NEG_INF = -1e30
LOG2E = 1.4426950408889634
LN2 = 0.6931471805599453
Q_SCALE = HEAD_DIM ** -0.5 * LOG2E

V7X_MXU_DIM = 256
V7X_LANES = 128
V7X_SUBLANES = 8
VMEM_LIMIT = 56 * 1024 * 1024

TM_PROJ = 512
TM_QKV = 1024
TM_FFN = 1024
TF_FFN = 512
TN_WO_CAST = 512
PROJ_CHUNK_ROWS = 256
FFN_ROW_CHUNKS = 1
N_QKV_TILES = 6
DIL_MAX_ROWS = 1024
BAND_SUBBLOCKS = 4
ATTN_DEPTH = 2
DIL_TILES_PER_TRIP = 8


def _params(*sem):
    return pltpu.CompilerParams(dimension_semantics=sem, vmem_limit_bytes=VMEM_LIMIT)


def _rms(x, g):
    ms = jnp.mean(x * x, axis=-1, keepdims=True)
    return x * lax.rsqrt(ms + NORM_EPS) * g


def _head_norm_rope(a, gain, ones_blk, cos, sin):
    tn = a.shape[1]
    x2 = a * a
    hi = x2.astype(BF16)
    lo = (x2 - hi.astype(F32)).astype(BF16)
    parts = []
    for c in range(tn // V7X_MXU_DIM):
        sl = slice(V7X_MXU_DIM * c, V7X_MXU_DIM * (c + 1))
        parts.append(jnp.dot(hi[:, sl], ones_blk, preferred_element_type=F32)
                     + jnp.dot(lo[:, sl], ones_blk, preferred_element_type=F32))
    ss = jnp.concatenate(parts, axis=1)
    y = a * lax.rsqrt(ss * (1.0 / HEAD_DIM) + NORM_EPS) * gain
    lane = lax.broadcasted_iota(jnp.int32, y.shape, 1)
    first_half = (lane & (HEAD_DIM - 1)) < HALF
    partner = jnp.where(first_half, pltpu.roll(y, tn - HALF, 1), pltpu.roll(y, HALF, 1))
    reps = tn // V7X_LANES
    return y * jnp.tile(cos, (1, reps)) + partner * jnp.tile(sin, (1, reps))


def _store_cols(ref, rows, val, lane_major):
    if not lane_major:
        ref[rows, :] = val.astype(ref.dtype)
        return
    for t in range(val.shape[1] // V7X_LANES):
        ref[t, rows, :] = val[:, V7X_LANES * t:V7X_LANES * (t + 1)].astype(ref.dtype)


def _qkv_kernel(x_ref, g_ref, w_ref, cos_ref, sin_ref, qg_ref, kg_ref, ones_ref,
                q_ref, k_ref, v_ref, *rest, n_q_tiles, lane_major, row_chunks, emit_bf16):
    n = pl.program_id(1)
    if emit_bf16:
        w_out_ref, xn_ref = rest
        w_out_ref[...] = w_ref[0].astype(BF16)
        w_ref = w_out_ref
    else:
        (xn_ref,) = rest

    @pl.when(n == 0)
    def _():
        xn_ref[...] = _rms(x_ref[...], g_ref[...]).astype(BF16)

    rc = xn_ref.shape[0] // row_chunks

    def project(out_ref, epilogue):
        chunk = lambda c: slice(rc * c, rc * (c + 1))
        matmul = lambda c: jnp.dot(xn_ref[chunk(c), :], w_ref[...], preferred_element_type=F32)
        acc_next = matmul(0)
        for c in range(row_chunks):
            acc = acc_next
            if c + 1 < row_chunks:
                acc_next = matmul(c + 1)
            _store_cols(out_ref, chunk(c), epilogue(acc, chunk(c)), lane_major)

    def norm_rope(gain_ref, scale):
        def fn(acc, rows):
            r = _head_norm_rope(acc, gain_ref[...], ones_ref[...], cos_ref[rows, :],
                                sin_ref[rows, :])
            return r * scale if scale != 1.0 else r
        return fn

    @pl.when(n < n_q_tiles)
    def _():
        project(q_ref, norm_rope(qg_ref, Q_SCALE))

    @pl.when(n == n_q_tiles)
    def _():
        project(k_ref, norm_rope(kg_ref, 1.0))

    @pl.when(n == n_q_tiles + 1)
    def _():
        project(v_ref, lambda acc, rows: acc)


def _qkv_call(x, g, w, cos, sin, qg, kg, ones_blk, *, n_q, n_kv, tm, q_dtype, lane_major=False,
              cast_layer=None):
    m = x.shape[0]
    ncols = w.shape[-1]
    tn = ncols // N_QKV_TILES
    n_q_tiles = (n_q * HEAD_DIM) // tn
    assert n_q_tiles * tn == n_q * HEAD_DIM and n_kv * HEAD_DIM == tn and tn % V7X_MXU_DIM == 0
    assert m % tm == 0
    last_q = n_q_tiles - 1
    if lane_major:
        lt = tn // V7X_LANES
        out_specs = [
            pl.BlockSpec((lt, tm, V7X_LANES), lambda i, n: (jnp.minimum(n, last_q), i, 0)),
            pl.BlockSpec((lt, tm, V7X_LANES), lambda i, n: (0, i, 0)),
            pl.BlockSpec((lt, tm, V7X_LANES), lambda i, n: (0, i, 0)),
        ]
        out_shape = [
            jax.ShapeDtypeStruct((n_q_tiles * lt, m, V7X_LANES), q_dtype),
            jax.ShapeDtypeStruct((lt, m, V7X_LANES), F32),
            jax.ShapeDtypeStruct((lt, m, V7X_LANES), F32),
        ]
    else:
        out_specs = [
            pl.BlockSpec((tm, tn), lambda i, n: (i, jnp.minimum(n, last_q))),
            pl.BlockSpec((tm, tn), lambda i, n: (i, 0)),
            pl.BlockSpec((tm, tn), lambda i, n: (i, 0)),
        ]
        out_shape = [
            jax.ShapeDtypeStruct((m, n_q * HEAD_DIM), q_dtype),
            jax.ShapeDtypeStruct((m, tn), F32),
            jax.ShapeDtypeStruct((m, tn), F32),
        ]
    emit_bf16 = cast_layer is not None
    if emit_bf16:
        assert m == tm
        w_spec = pl.BlockSpec((1, D_MODEL, tn), lambda i, n: (cast_layer, 0, n))
        out_specs = out_specs + [pl.BlockSpec((D_MODEL, tn), lambda i, n: (0, n))]
        out_shape = out_shape + [jax.ShapeDtypeStruct((D_MODEL, ncols), BF16)]
    else:
        w_spec = pl.BlockSpec((D_MODEL, tn), lambda i, n: (0, n))
    return pl.pallas_call(
        functools.partial(_qkv_kernel, n_q_tiles=n_q_tiles, lane_major=lane_major,
                          row_chunks=max(1, tm // PROJ_CHUNK_ROWS), emit_bf16=emit_bf16),
        grid=(m // tm, N_QKV_TILES),
        in_specs=[
            pl.BlockSpec((tm, D_MODEL), lambda i, n: (i, 0)),
            pl.BlockSpec((1, D_MODEL), lambda i, n: (0, 0)),
            w_spec,
            pl.BlockSpec((tm, V7X_LANES), lambda i, n: (i, 0)),
            pl.BlockSpec((tm, V7X_LANES), lambda i, n: (i, 0)),
            pl.BlockSpec((1, tn), lambda i, n: (0, 0)),
            pl.BlockSpec((1, tn), lambda i, n: (0, 0)),
            pl.BlockSpec((V7X_MXU_DIM, V7X_MXU_DIM), lambda i, n: (0, 0)),
        ],
        out_specs=out_specs,
        out_shape=out_shape,
        scratch_shapes=[pltpu.VMEM((tm, D_MODEL), BF16)],
        compiler_params=_params("parallel", "arbitrary"),
        name="qkv_proj",
    )(x, g, w, cos, sin, qg, kg, ones_blk)


GRP = 4


def _band_mask_t(first):
    shape = (2 * BLOCK, BLOCK)
    kj = lax.broadcasted_iota(jnp.int32, shape, 0)
    qi = lax.broadcasted_iota(jnp.int32, shape, 1)
    seen = (kj >= qi) & (kj <= qi + BLOCK) & ((kj >= BLOCK) | jnp.logical_not(first))
    return jnp.where(seen, 0.0, NEG_INF)


def _kv_tile_forms(kt, vt):
    lo = lax.broadcasted_iota(jnp.int32, kt.shape, 1) < HEAD_DIM
    kt_sw = pltpu.roll(kt, HEAD_DIM, 1)
    k_forms = [(jnp.where(lo, kt, 0.0).astype(BF16), jnp.where(lo, 0.0, kt_sw).astype(BF16)),
               (jnp.where(lo, kt_sw, 0.0).astype(BF16), jnp.where(lo, 0.0, kt).astype(BF16))]
    return k_forms, vt.T.astype(BF16)


def _head_scores(k_lo, k_hi, q_a, q_b):
    nt = (((1,), (1,)), ((), ()))
    qpair = jnp.concatenate([q_a, q_b], axis=0)
    return jnp.concatenate([lax.dot_general(k_lo, qpair, nt, preferred_element_type=F32),
                            lax.dot_general(k_hi, qpair, nt, preferred_element_type=F32)], axis=1)


def _head_softmax_pv(s, mask4, v_t, sinks, want_lse):
    s = s + mask4
    m = jnp.max(s, axis=0, keepdims=True)
    if sinks is not None:
        sk = jnp.concatenate([jnp.full((1, BLOCK), sinks[j] * LOG2E, F32)
                              for j in (0, 2, 1, 3)], axis=1)
        m = jnp.maximum(m, sk)
    p = jnp.exp2(s - m)
    l = jnp.sum(p, axis=0, keepdims=True)
    if sinks is not None:
        l = l + jnp.exp2(sk - m)
    o_t = jnp.dot(v_t, p.astype(BF16), preferred_element_type=F32) / l
    lse_t = jnp.broadcast_to((m + jnp.log2(l)) * LN2, o_t.shape) if want_lse else None
    o_tiles, l_tiles = [], []
    for u in range(2):
        c0, c1 = slice(BLOCK * u, BLOCK * (u + 1)), slice(BLOCK * (2 + u), BLOCK * (3 + u))
        o_tiles.append(jnp.concatenate([o_t[:, c0], o_t[:, c1]], axis=0).T)
        if want_lse:
            l_tiles.append(jnp.concatenate([lse_t[:, c0], lse_t[:, c1]], axis=0).T)
    return o_tiles, l_tiles


def _attend_tiles(tiles, want_lse, fillers=None):
    jobs = [(ti, e) for ti in range(len(tiles)) for e in range(2)]
    forms = {}

    def scores(job):
        ti, e = job
        load_kv, load_q = tiles[ti][:2]
        if ti not in forms:
            forms[ti] = _kv_tile_forms(*load_kv())
        k_lo, k_hi = forms[ti][0][e]
        return _head_scores(k_lo, k_hi, load_q(2 * e), load_q(2 * e + 1))

    pending = [scores(job) for job in jobs[:ATTN_DEPTH]]
    for idx, (ti, e) in enumerate(jobs):
        s = pending.pop(0)
        if idx + ATTN_DEPTH < len(jobs):
            pending.append(scores(jobs[idx + ATTN_DEPTH]))
        sink_of, store, mask4 = tiles[ti][2:]
        v_t = forms[ti][1][HEAD_DIM * e:HEAD_DIM * (e + 1)]
        sinks = None if sink_of is None else [sink_of(GRP * e + j) for j in range(GRP)]
        o_tiles, l_tiles = _head_softmax_pv(s, mask4, v_t, sinks, want_lse)
        for u in range(2):
            store(2 * e + u, o_tiles[u], l_tiles[u] if want_lse else None)
        if fillers and idx in fillers:
            fillers[idx]()


def _band_wo_kernel(q_ref, kc_ref, kp_ref, vc_ref, vp_ref, sink_ref, y_ref, wo_ref, out_ref,
                    o_scr, *, kv_tiles, nsub):
    mask_first = jnp.tile(_band_mask_t(pl.program_id(1) == 0), (1, GRP))
    mask_rest = jnp.tile(_band_mask_t(False), (1, GRP))
    lanes = lambda t: slice(V7X_LANES * t, V7X_LANES * (t + 1))
    rows = lambda j: slice(BLOCK * j, BLOCK * (j + 1))

    def tile(j, t):
        def load_kv():
            if j == 0:
                kp, vp = kp_ref[0, :, lanes(t)], vp_ref[0, :, lanes(t)]
            else:
                kp, vp = kc_ref[0, rows(j - 1), lanes(t)], vc_ref[0, rows(j - 1), lanes(t)]
            return (jnp.concatenate([kp, kc_ref[0, rows(j), lanes(t)]], axis=0),
                    jnp.concatenate([vp, vc_ref[0, rows(j), lanes(t)]], axis=0))

        def store(u, o_tile, _):
            o_scr[rows(j), lanes(GRP * t + u)] = o_tile.astype(o_scr.dtype)

        return (load_kv, lambda u: q_ref[0, rows(j), lanes(GRP * t + u)],
                lambda h: sink_ref[2 * GRP * t + h], store, mask_first if j == 0 else mask_rest)

    jobs_per_block = 2 * kv_tiles
    n_out = out_ref.shape[2]
    piece = n_out // jobs_per_block

    def project(j, c):
        cols = slice(piece * c, piece * (c + 1))
        out_ref[0, rows(j), cols] = y_ref[0, rows(j), cols] + jnp.dot(
            o_scr[rows(j), :], wo_ref[:, cols], preferred_element_type=F32)

    fillers = {jobs_per_block * j + c: functools.partial(project, j - 1, c)
               for j in range(1, nsub) for c in range(jobs_per_block)}
    _attend_tiles([tile(j, t) for j in range(nsub) for t in range(kv_tiles)], False, fillers)
    for c in range(jobs_per_block):
        project(nsub - 1, c)


def _band_wo_call(q, k, v, sinks, y, wo, *, hkv, grp):
    b, seq, cq = q.shape
    ck = hkv * HEAD_DIM
    nsub = BAND_SUBBLOCKS
    rows = BLOCK * nsub
    assert grp == GRP and hkv % 2 == 0 and seq % rows == 0
    cur = lambda bi, c: (bi, c, 0)
    prev = lambda bi, c: (bi, jnp.maximum(c * nsub - 1, 0), 0)
    return pl.pallas_call(
        functools.partial(_band_wo_kernel, kv_tiles=hkv // 2, nsub=nsub),
        grid=(b, seq // rows),
        in_specs=[
            pl.BlockSpec((1, rows, cq), cur),
            pl.BlockSpec((1, rows, ck), cur),
            pl.BlockSpec((1, BLOCK, ck), prev),
            pl.BlockSpec((1, rows, ck), cur),
            pl.BlockSpec((1, BLOCK, ck), prev),
            pl.BlockSpec(memory_space=pltpu.SMEM),
            pl.BlockSpec((1, rows, D_MODEL), cur),
            pl.BlockSpec((cq, D_MODEL), lambda bi, c: (0, 0)),
        ],
        out_specs=pl.BlockSpec((1, rows, D_MODEL), cur),
        out_shape=jax.ShapeDtypeStruct((b, seq, D_MODEL), F32),
        scratch_shapes=[pltpu.VMEM((rows, cq), BF16)],
        compiler_params=_params("parallel", "arbitrary"),
        name="band_attn_wo",
    )(q, k, k, v, v, sinks, y, wo)


def _dil_kernel(q_ref, kc_ref, kp_ref, vc_ref, vp_ref, o_ref, lse_ref, *, kv_tiles, grp, d, nsub,
                residue_major):
    mask_first = jnp.tile(_band_mask_t(pl.program_id(1) == 0), (1, GRP))

    if d == 1:
        mask_rest = jnp.tile(_band_mask_t(False), (1, GRP))
        rows = lambda j: slice(BLOCK * j, BLOCK * (j + 1))

        def sub_tile(j, t):
            def load_kv():
                if j == 0:
                    kp, vp = kp_ref[t], vp_ref[t]
                else:
                    kp, vp = kc_ref[t, rows(j - 1), :], vc_ref[t, rows(j - 1), :]
                return (jnp.concatenate([kp, kc_ref[t, rows(j), :]], axis=0),
                        jnp.concatenate([vp, vc_ref[t, rows(j), :]], axis=0))

            def store(u, o_tile, l_tile):
                o_ref[grp * t + u, rows(j), :] = o_tile
                lse_ref[grp * t + u, rows(j), :] = l_tile

            return (load_kv, lambda u: q_ref[grp * t + u, rows(j), :].astype(BF16), None, store,
                    mask_first if j == 0 else mask_rest)

        _attend_tiles([sub_tile(j, t) for j in range(nsub) for t in range(kv_tiles)], True)
        return

    def tile(r, t):
        rs = pl.ds(r, BLOCK, stride=d)

        def load_kv():
            return (jnp.concatenate([kp_ref[t, rs, :], kc_ref[t, rs, :]], axis=0),
                    jnp.concatenate([vp_ref[t, rs, :], vc_ref[t, rs, :]], axis=0))

        ws = pl.ds(pl.multiple_of(r * BLOCK, BLOCK), BLOCK) if residue_major else rs

        def store(u, o_tile, l_tile):
            o_ref[grp * t + u, ws, :] = o_tile
            lse_ref[grp * t + u, ws, :] = l_tile

        return (load_kv, lambda u: q_ref[grp * t + u, rs, :].astype(BF16), None, store,
                mask_first)

    unroll = min(d, max(1, DIL_TILES_PER_TRIP // kv_tiles))

    def body(i, carry):
        _attend_tiles([tile(i * unroll + rr, t) for rr in range(unroll) for t in range(kv_tiles)],
                      True)
        return carry

    if d > unroll:
        lax.fori_loop(0, d // unroll, body, 0)
    else:
        body(0, 0)


def _dil_call(q, k, v, *, nb, hkv, grp, d, group, residue_major=False):
    m = q.shape[1]
    seq = m // nb
    nsub = BAND_SUBBLOCKS if d == 1 else 1
    rows = BLOCK * d * nsub
    hsplit = max(1, rows // DIL_MAX_ROWS)
    kv_tiles = hkv // 2 // hsplit
    q_tiles = kv_tiles * grp
    nchunk = seq // rows
    assert seq % rows == 0 and kv_tiles * 2 * hsplit == hkv and grp % 2 == 0
    cur = lambda bi, c, hp: (group * hsplit + hp, bi * nchunk + c, 0)
    if d == 1:
        prev_rows = BLOCK
        prev = lambda bi, c, hp: (group * hsplit + hp,
                                  jnp.maximum((bi * nchunk + c) * nsub - 1, 0), 0)
    else:
        prev_rows = rows
        prev = lambda bi, c, hp: (group * hsplit + hp, bi * nchunk + jnp.maximum(c - 1, 0), 0)
    o_spec = pl.BlockSpec((q_tiles, rows, V7X_LANES), lambda bi, c, hp: (hp, bi * nchunk + c, 0))
    o_shape = jax.ShapeDtypeStruct((q_tiles * hsplit, m, V7X_LANES), F32)
    return pl.pallas_call(
        functools.partial(_dil_kernel, kv_tiles=kv_tiles, grp=grp, d=d, nsub=nsub,
                          residue_major=residue_major and d > 1),
        grid=(nb, nchunk, hsplit),
        in_specs=[
            pl.BlockSpec((q_tiles, rows, V7X_LANES), cur),
            pl.BlockSpec((kv_tiles, rows, V7X_LANES), cur),
            pl.BlockSpec((kv_tiles, prev_rows, V7X_LANES), prev),
            pl.BlockSpec((kv_tiles, rows, V7X_LANES), cur),
            pl.BlockSpec((kv_tiles, prev_rows, V7X_LANES), prev),
        ],
        out_specs=[o_spec, o_spec],
        out_shape=[o_shape, o_shape],
        compiler_params=_params("parallel", "arbitrary", "arbitrary"),
        name="dilated_attn",
    )(q, k, k, v, v)


def _keep_t_kernel(x_ref, o_ref):
    o_ref[0] = x_ref[0].T


def _keep_t_lane_major_kernel(x_ref, o_ref, *, keeps):
    g = pl.program_id(2)
    seq = x_ref.shape[2]
    off = 0
    for k, keep in enumerate(keeps):
        @pl.when(g == k)
        def _(keep=keep, off=off):
            o_ref[0, :, off:off + keep] = x_ref[0, 0, seq - keep:, :].T
        off += keep


def _keep_t_call(x, keeps, c, *, lane_major=False):
    b, seq = x.shape[1:3] if lane_major else x.shape[:2]
    starts, first_rb = [], []
    n = 0
    for keep in keeps:
        assert keep % BLOCK == 0 and seq % BLOCK == 0
        starts.append(n)
        first_rb.append((seq - keep) // BLOCK)
        n += keep // BLOCK

    def src_block(j):
        rb = jnp.int32(0)
        cb = jnp.int32(0)
        for g in range(len(keeps)):
            inside = j >= starts[g]
            rb = jnp.where(inside, first_rb[g] + j - starts[g], rb)
            cb = jnp.where(inside, g, cb)
        return rb, cb

    if lane_major:
        ct = c // V7X_LANES
        grid = (b, ct, len(keeps))
        in_spec = pl.BlockSpec((1, 1, seq, V7X_LANES), lambda bi, t, g: (g * ct + t, bi, 0, 0))
        out_spec = pl.BlockSpec((1, V7X_LANES, n * BLOCK), lambda bi, t, g: (bi, t, 0))
        kern = functools.partial(_keep_t_lane_major_kernel, keeps=tuple(keeps))
        sem = ("parallel", "parallel", "arbitrary")
    else:
        grid = (b, n)

        def src(bi, j):
            rb, cb = src_block(j)
            return bi, rb, cb

        in_spec = pl.BlockSpec((1, BLOCK, c), src)
        out_spec = pl.BlockSpec((1, c, BLOCK), lambda bi, j: (bi, 0, j))
        kern = _keep_t_kernel
        sem = ("parallel", "parallel")
    return pl.pallas_call(
        kern,
        grid=grid,
        in_specs=[in_spec],
        out_specs=out_spec,
        out_shape=jax.ShapeDtypeStruct((b, c, n * BLOCK), F32),
        compiler_params=_params(*sem),
        name="keep_rows_t",
    )(x)


def _wo_comb_kernel(y_ref, o0_ref, o1_ref, o2_ref, l0_ref, l1_ref, l2_ref, w_ref, out_ref):
    def natural(ref, t):
        return jnp.concatenate([ref[t, :, n, :] for n in range(ref.shape[2])], axis=0)

    tiles = []
    for t in range(o0_ref.shape[0]):
        l0, l1, l2 = l0_ref[t], l1_ref[t], natural(l2_ref, t)
        mx = jnp.maximum(jnp.maximum(l0, l1), l2)
        e0, e1, e2 = jnp.exp(l0 - mx), jnp.exp(l1 - mx), jnp.exp(l2 - mx)
        den = e0 + e1 + e2
        comb = ((e0 / den) * o0_ref[t] + (e1 / den) * o1_ref[t]
                + (e2 / den) * natural(o2_ref, t))
        tiles.append(comb.astype(BF16))
    comb = jnp.concatenate(tiles, axis=1)
    out_ref[...] = y_ref[...] + jnp.dot(comb, w_ref[...], preferred_element_type=F32)


def _wo_cast_kernel(y_ref, o_ref, w_ref, out_ref, w_out_ref):
    w_out_ref[...] = w_ref[0].astype(BF16)
    out_ref[...] = y_ref[...] + jnp.dot(o_ref[...].astype(BF16), w_out_ref[...],
                                        preferred_element_type=F32)


def _wo_cast_call(y, o, w, layer):
    m = y.shape[0]
    c = o.shape[1]
    tn = TN_WO_CAST
    return pl.pallas_call(
        _wo_cast_kernel,
        grid=(D_MODEL // tn,),
        in_specs=[
            pl.BlockSpec((m, tn), lambda n: (0, n)),
            pl.BlockSpec((m, c), lambda n: (0, 0)),
            pl.BlockSpec((1, c, tn), lambda n: (layer, 0, n)),
        ],
        out_specs=[pl.BlockSpec((m, tn), lambda n: (0, n)),
                   pl.BlockSpec((c, tn), lambda n: (0, n))],
        out_shape=[jax.ShapeDtypeStruct((m, D_MODEL), F32),
                   jax.ShapeDtypeStruct((c, D_MODEL), BF16)],
        compiler_params=_params("parallel"),
        name="wo_proj_cast",
    )(y, o, w)


def _wo_comb_call(y, os_, ls_, w, *, tm, d_last):
    m = y.shape[0]
    c = w.shape[0]
    nt = c // V7X_LANES
    per_blk = BLOCK * d_last // tm
    assert (BLOCK * d_last) % tm == 0 and tm % d_last == 0 and (tm // d_last) % V7X_SUBLANES == 0
    blk = pl.BlockSpec((nt, tm, V7X_LANES), lambda i: (0, i, 0))
    blk_rm = pl.BlockSpec((nt, d_last, tm // d_last, V7X_LANES),
                          lambda i: (0, i // per_blk, i % per_blk, 0))
    rm = lambda a: a.reshape(nt, m // BLOCK, BLOCK, V7X_LANES)
    return pl.pallas_call(
        _wo_comb_kernel,
        grid=(m // tm,),
        in_specs=[pl.BlockSpec((tm, D_MODEL), lambda i: (i, 0)), blk, blk, blk_rm, blk, blk, blk_rm,
                  pl.BlockSpec((c, D_MODEL), lambda i: (0, 0))],
        out_specs=pl.BlockSpec((tm, D_MODEL), lambda i: (i, 0)),
        out_shape=jax.ShapeDtypeStruct((m, D_MODEL), F32),
        compiler_params=_params("parallel"),
        name="wo_comb_proj",
    )(y, os_[0], os_[1], rm(os_[2]), ls_[0], ls_[1], rm(ls_[2]), w)


def _ffn_tail(gate, g1, g2, up, cw_ref, cb_ref, wd_ref):
    conv = cb_ref[...] + cw_ref[0:1, :] * g2 + cw_ref[1:2, :] * g1 + cw_ref[2:3, :] * gate
    h = conv * jax.nn.sigmoid(conv) * up
    return jnp.dot(h.astype(BF16), wd_ref[...], preferred_element_type=F32)


def _ffn_prompt_kernel(y_ref, g_ref, wg_ref, wu_ref, wd_ref, cw_ref, cb_ref,
                       out_ref, tail_ref, xn_ref, carry_ref, *, tiles_per_seq):
    m = pl.program_id(0)
    f = pl.program_id(1)

    @pl.when(f == 0)
    def _():
        x = y_ref[...]
        xn_ref[...] = _rms(x, g_ref[...]).astype(BF16)
        out_ref[...] = x

    @pl.when(m % tiles_per_seq == 0)
    def _():
        carry_ref[f] = jnp.zeros(carry_ref.shape[1:], F32)

    rc = xn_ref.shape[0] // FFN_ROW_CHUNKS
    last = carry_ref[f]
    for c in range(FFN_ROW_CHUNKS):
        rows = slice(rc * c, rc * (c + 1))
        xn = xn_ref[rows, :]
        gate = jnp.dot(xn, wg_ref[...], preferred_element_type=F32)
        up = jnp.dot(xn, wu_ref[...], preferred_element_type=F32)
        row = lax.broadcasted_iota(jnp.int32, gate.shape, 0)
        g1 = jnp.where(row == 0, last[7:8, :], pltpu.roll(gate, 1, 0))
        g2 = jnp.where(row == 0, last[6:7, :],
                       jnp.where(row == 1, last[7:8, :], pltpu.roll(gate, 2, 0)))
        out_ref[rows, :] += _ffn_tail(gate, g1, g2, up, cw_ref, cb_ref, wd_ref)
        last = gate[rc - V7X_SUBLANES:, :]
    carry_ref[f] = last
    tail_ref[0] = last


def _ffn_prompt_call(y, g, wg, wu, wd, cw, cb, *, seq):
    m = y.shape[0]
    tm, tf = TM_FFN, TF_FFN
    nf = D_FF // tf
    assert m % tm == 0 and seq % tm == 0 and D_FF % tf == 0
    return pl.pallas_call(
        functools.partial(_ffn_prompt_kernel, tiles_per_seq=seq // tm),
        grid=(m // tm, nf),
        in_specs=[
            pl.BlockSpec((tm, D_MODEL), lambda i, f: (i, 0)),
            pl.BlockSpec((1, D_MODEL), lambda i, f: (0, 0)),
            pl.BlockSpec((D_MODEL, tf), lambda i, f: (0, f)),
            pl.BlockSpec((D_MODEL, tf), lambda i, f: (0, f)),
            pl.BlockSpec((tf, D_MODEL), lambda i, f: (f, 0)),
            pl.BlockSpec((CONV_W, tf), lambda i, f: (0, f)),
            pl.BlockSpec((1, tf), lambda i, f: (0, f)),
        ],
        out_specs=[
            pl.BlockSpec((tm, D_MODEL), lambda i, f: (i, 0)),
            pl.BlockSpec((1, V7X_SUBLANES, tf), lambda i, f: (i, 0, f)),
        ],
        out_shape=[
            jax.ShapeDtypeStruct((m, D_MODEL), F32),
            jax.ShapeDtypeStruct((m // tm, V7X_SUBLANES, D_FF), F32),
        ],
        scratch_shapes=[pltpu.VMEM((tm, D_MODEL), BF16),
                        pltpu.VMEM((nf, V7X_SUBLANES, tf), F32)],
        compiler_params=_params("arbitrary", "arbitrary"),
        name="conv_ffn_prompt",
    )(y, g, wg, wu, wd, cw, cb)


def _ffn_sample_kernel(y_ref, g_ref, wg_ref, wu_ref, wd_ref, cw_ref, cb_ref, s0_ref, s1_ref,
                       out_ref, gate_ref, wg_out_ref, wu_out_ref, wd_out_ref, xn_ref, *, seq):
    f = pl.program_id(0)

    @pl.when(f == 0)
    def _():
        x = y_ref[...]
        xn_ref[...] = _rms(x, g_ref[...]).astype(BF16)
        out_ref[...] = x

    wg_out_ref[...] = wg_ref[0].astype(BF16)
    wu_out_ref[...] = wu_ref[0].astype(BF16)
    wd_out_ref[...] = wd_ref[0].astype(BF16)
    xn = xn_ref[...]
    gate = jnp.dot(xn, wg_out_ref[...], preferred_element_type=F32)
    up = jnp.dot(xn, wu_out_ref[...], preferred_element_type=F32)
    t = lax.broadcasted_iota(jnp.int32, gate.shape, 0) & (seq - 1)
    s0, s1 = s0_ref[...], s1_ref[...]
    g1 = jnp.where(t == 0, s1, pltpu.roll(gate, 1, 0))
    g2 = jnp.where(t == 0, s0, jnp.where(t == 1, s1, pltpu.roll(gate, 2, 0)))
    out_ref[...] += _ffn_tail(gate, g1, g2, up, cw_ref, cb_ref, wd_out_ref)
    gate_ref[...] = gate


def _ffn_sample_call(y, g, wg, wu, wd, layer, cw, cb, s0, s1, *, seq):
    m = y.shape[0]
    tf = TF_FFN
    nf = D_FF // tf
    full = pl.BlockSpec((m, D_MODEL), lambda f: (0, 0))
    col = pl.BlockSpec((m, tf), lambda f: (0, f))
    return pl.pallas_call(
        functools.partial(_ffn_sample_kernel, seq=seq),
        grid=(nf,),
        in_specs=[
            full,
            pl.BlockSpec((1, D_MODEL), lambda f: (0, 0)),
            pl.BlockSpec((1, D_MODEL, tf), lambda f: (layer, 0, f)),
            pl.BlockSpec((1, D_MODEL, tf), lambda f: (layer, 0, f)),
            pl.BlockSpec((1, tf, D_MODEL), lambda f: (layer, f, 0)),
            pl.BlockSpec((CONV_W, tf), lambda f: (0, f)),
            pl.BlockSpec((1, tf), lambda f: (0, f)),
            col, col,
        ],
        out_specs=[full, col,
                   pl.BlockSpec((D_MODEL, tf), lambda f: (0, f)),
                   pl.BlockSpec((D_MODEL, tf), lambda f: (0, f)),
                   pl.BlockSpec((tf, D_MODEL), lambda f: (f, 0))],
        out_shape=[jax.ShapeDtypeStruct((m, D_MODEL), F32),
                   jax.ShapeDtypeStruct((m, D_FF), F32),
                   jax.ShapeDtypeStruct((D_MODEL, D_FF), BF16),
                   jax.ShapeDtypeStruct((D_MODEL, D_FF), BF16),
                   jax.ShapeDtypeStruct((D_FF, D_MODEL), BF16)],
        scratch_shapes=[pltpu.VMEM((m, D_MODEL), BF16)],
        compiler_params=_params("arbitrary"),
        name="conv_ffn_sample",
    )(y, g, wg, wu, wd, cw, cb, s0, s1)


def _block_diag_q(q_ref, col0, hkv, grp, s):
    blocks = []
    for h in range(hkv):
        qh = jnp.concatenate(
            [q_ref[:, col0 + HEAD_DIM * (grp * h + j):col0 + HEAD_DIM * (grp * h + j + 1)]
             for j in range(grp)], axis=0)
        pieces = []
        if h > 0:
            pieces.append(jnp.zeros((grp * s, HEAD_DIM * h), F32))
        pieces.append(qh)
        if h < hkv - 1:
            pieces.append(jnp.zeros((grp * s, HEAD_DIM * (hkv - 1 - h)), F32))
        blocks.append(jnp.concatenate(pieces, axis=1) if len(pieces) > 1 else qh)
    return jnp.concatenate(blocks, axis=0).astype(BF16)


def _new_rows_t(x, s):
    pad = jnp.zeros((V7X_LANES - s, x.shape[1]), F32)
    return jnp.concatenate([pad, x], axis=0).T


def _cached_attend(qbd, ck_t, kn_t, cv_t, vn_t, d, s, sink_col):
    r_, lc = qbd.shape[0], ck_t.shape[1]
    sc = jnp.dot(qbd, ck_t.astype(BF16), preferred_element_type=F32)
    sn = jnp.dot(qbd, kn_t.astype(BF16), preferred_element_type=F32)
    ic = lax.broadcasted_iota(jnp.int32, (r_, lc), 0) & (s - 1)
    c = lax.broadcasted_iota(jnp.int32, (r_, lc), 1)
    i_n = lax.broadcasted_iota(jnp.int32, (r_, V7X_LANES), 0) & (s - 1)
    j = lax.broadcasted_iota(jnp.int32, (r_, V7X_LANES), 1) - (V7X_LANES - s)
    valid_c = c >= ic
    valid_n = (j >= 0) & (j <= i_n)
    if d > 1:
        valid_c = valid_c & ((c & (d - 1)) == (ic & (d - 1)))
        valid_n = valid_n & ((j & (d - 1)) == (i_n & (d - 1)))
    sc = jnp.where(valid_c, sc, NEG_INF)
    sn = jnp.where(valid_n, sn, NEG_INF)
    m = jnp.maximum(jnp.max(sc, axis=1, keepdims=True), jnp.max(sn, axis=1, keepdims=True))
    if sink_col is not None:
        m = jnp.maximum(m, sink_col)
    pc = jnp.exp2(sc - m)
    pn = jnp.exp2(sn - m)
    l = jnp.sum(pc, axis=1, keepdims=True) + jnp.sum(pn, axis=1, keepdims=True)
    if sink_col is not None:
        l = l + jnp.exp2(sink_col - m)
    nt = (((1,), (1,)), ((), ()))
    o = (lax.dot_general(pc.astype(BF16), cv_t.astype(BF16), nt, preferred_element_type=F32)
         + lax.dot_general(pn.astype(BF16), vn_t.astype(BF16), nt, preferred_element_type=F32)) / l
    return o, (m + jnp.log2(l)) * LN2


def _diag_heads(o, hkv, grp, s):
    pieces = []
    for h in range(hkv):
        for j in range(grp):
            r0 = (h * grp + j) * s
            pieces.append(o[r0:r0 + s, HEAD_DIM * h:HEAD_DIM * (h + 1)])
    return jnp.concatenate(pieces, axis=1)


def _store_shifted(out_ref, off, c_t, n_t, s):
    lc = c_t.shape[1]
    rolled = pltpu.roll(c_t, lc - s, 1)
    lane = lax.broadcasted_iota(jnp.int32, n_t.shape, 1)
    if lc > V7X_LANES:
        out_ref[0, :, off:off + lc - V7X_LANES] = rolled[:, :lc - V7X_LANES]
    out_ref[0, :, off + lc - V7X_LANES:off + lc] = jnp.where(
        lane < V7X_LANES - s, rolled[:, lc - V7X_LANES:], n_t)


def _sample_a_kernel(q_ref, kn_ref, vn_ref, ck_ref, cv_ref, sink_ref, o_ref, ko_ref, vo_ref, *, s):
    hkv, grp = A_KV_HEADS, A_Q_HEADS // A_KV_HEADS
    kn_t, vn_t = _new_rows_t(kn_ref[...], s), _new_rows_t(vn_ref[...], s)
    ck_t, cv_t = ck_ref[0], cv_ref[0]
    qbd = _block_diag_q(q_ref, 0, hkv, grp, s)
    sink_col = jnp.concatenate(
        [jnp.full((s, 1), sink_ref[hq] * LOG2E, F32) for hq in range(hkv * grp)], axis=0)
    o, _ = _cached_attend(qbd, ck_t, kn_t, cv_t, vn_t, 1, s, sink_col)
    o_ref[...] = _diag_heads(o, hkv, grp, s)
    _store_shifted(ko_ref, 0, ck_t, kn_t, s)
    _store_shifted(vo_ref, 0, cv_t, vn_t, s)


def _sample_a_call(q, kn, vn, ck_t, cv_t, sinks, *, s):
    nb, c, lc = ck_t.shape
    row = lambda w: pl.BlockSpec((s, w), lambda b: (b, 0))
    cache = pl.BlockSpec((1, c, lc), lambda b: (b, 0, 0))
    return pl.pallas_call(
        functools.partial(_sample_a_kernel, s=s),
        grid=(nb,),
        in_specs=[row(q.shape[1]), row(c), row(c), cache, cache,
                  pl.BlockSpec(memory_space=pltpu.SMEM)],
        out_specs=[row(q.shape[1]), cache, cache],
        out_shape=[jax.ShapeDtypeStruct(q.shape, F32),
                   jax.ShapeDtypeStruct(ck_t.shape, F32),
                   jax.ShapeDtypeStruct(cv_t.shape, F32)],
        compiler_params=_params("parallel"),
        name="sample_attn_a",
    )(q, kn, vn, ck_t, cv_t, sinks)


def _sample_b_kernel(q_ref, kn_ref, vn_ref, ck_ref, cv_ref, o_ref, ko_ref, vo_ref, *, s):
    hkv, grp = B_KV_HEADS, B_Q_HEADS // B_KV_HEADS
    ckv = hkv * HEAD_DIM
    outs, lses = [], []
    off = 0
    for g, (w, d) in enumerate(B_PATTERNS):
        lc = w
        kn_t = _new_rows_t(kn_ref[:, ckv * g:ckv * (g + 1)], s)
        vn_t = _new_rows_t(vn_ref[:, ckv * g:ckv * (g + 1)], s)
        ck_t = ck_ref[0, :, off:off + lc]
        cv_t = cv_ref[0, :, off:off + lc]
        qbd = _block_diag_q(q_ref, B_Q_HEADS * HEAD_DIM * g, hkv, grp, s)
        o, lse = _cached_attend(qbd, ck_t, kn_t, cv_t, vn_t, d, s, None)
        outs.append(o)
        lses.append(lse)
        _store_shifted(ko_ref, off, ck_t, kn_t, s)
        _store_shifted(vo_ref, off, cv_t, vn_t, s)
        off += lc
    mx = jnp.maximum(jnp.maximum(lses[0], lses[1]), lses[2])
    es = [jnp.exp(l - mx) for l in lses]
    den = es[0] + es[1] + es[2]
    comb = (es[0] / den) * outs[0] + (es[1] / den) * outs[1] + (es[2] / den) * outs[2]
    o_ref[...] = _diag_heads(comb, hkv, grp, s)


def _sample_b_call(q, kn, vn, ck_t, cv_t, *, s):
    nb, c, lb = ck_t.shape
    row = lambda w: pl.BlockSpec((s, w), lambda b: (b, 0))
    cache = pl.BlockSpec((1, c, lb), lambda b: (b, 0, 0))
    co = B_Q_HEADS * HEAD_DIM
    return pl.pallas_call(
        functools.partial(_sample_b_kernel, s=s),
        grid=(nb,),
        in_specs=[row(q.shape[1]), row(kn.shape[1]), row(vn.shape[1]), cache, cache],
        out_specs=[row(co), cache, cache],
        out_shape=[jax.ShapeDtypeStruct((q.shape[0], co), F32),
                   jax.ShapeDtypeStruct(ck_t.shape, F32),
                   jax.ShapeDtypeStruct(cv_t.shape, F32)],
        compiler_params=_params("parallel"),
        name="sample_attn_b",
    )(q, kn, vn, ck_t, cv_t)


def _rope_tables(pos):
    inv_freq = ROPE_THETA ** (-jnp.arange(HALF, dtype=F32) / HALF)
    ang = pos.astype(F32)[:, None] * inv_freq[None, :]
    cos, sin = jnp.cos(ang), jnp.sin(ang)
    return jnp.tile(cos, (1, 4)), jnp.tile(jnp.concatenate([-sin, sin], axis=1), (1, 2))


def _cache_t(cache):
    b, l, h, dh = cache.shape
    return jnp.transpose(cache, (0, 2, 3, 1)).reshape(b, h * dh, l)


def _cache_from_t(x, h):
    b, c, l = x.shape
    return jnp.transpose(x.reshape(b, h, c // h, l), (0, 3, 1, 2))[None]


def kernel(x_prompt, x_sample, cache_a_k, cache_a_v, cache_b_k, cache_b_v, state_ffn_conv,
           attn_norm, ffn_norm, a_w_qkv, a_q_norm, a_k_norm, a_sinks, a_w_o,
           b_w_qkv, b_q_norm, b_k_norm, b_w_o,
           ffn_w_gate, ffn_w_up, ffn_conv_w, ffn_conv_b, ffn_w_down):
    nb, seq, _ = x_prompt.shape
    ns, dec, _ = x_sample.shape
    mp, ms = nb * seq, ns * dec

    cos_p, sin_p = _rope_tables(jnp.tile(jnp.arange(seq, dtype=jnp.int32), nb))
    cos_s, sin_s = _rope_tables(jnp.tile(PAST_LEN + jnp.arange(dec, dtype=jnp.int32), ns))
    idx = jnp.arange(V7X_MXU_DIM, dtype=jnp.int32) // HEAD_DIM
    ones_blk = (idx[:, None] == idx[None, :]).astype(BF16)

    yp = x_prompt.reshape(mp, D_MODEL)
    ys = x_sample.reshape(ms, D_MODEL)
    row = lambda a: a.reshape(1, -1)

    def head_gain(gv, n_heads_per_tile):
        return jnp.tile(gv, n_heads_per_tile).reshape(1, -1)

    tn = a_w_qkv.shape[-1] // N_QKV_TILES
    qg, kg = head_gain(a_q_norm[0], tn // HEAD_DIM), head_gain(a_k_norm[0], tn // HEAD_DIM)
    g_attn = row(attn_norm[0])
    grp_a = A_Q_HEADS // A_KV_HEADS

    qs, ks, vs, wqkv = _qkv_call(ys, g_attn, a_w_qkv, cos_s, sin_s, qg, kg, ones_blk,
                                 n_q=A_Q_HEADS, n_kv=A_KV_HEADS, tm=ms, q_dtype=F32, cast_layer=0)
    os_, ako, avo = _sample_a_call(qs, ks, vs, _cache_t(cache_a_k[0]), _cache_t(cache_a_v[0]),
                                   a_sinks[0], s=dec)
    ys, wo = _wo_cast_call(ys, os_, a_w_o, 0)
    a_k_sample = _cache_from_t(ako, A_KV_HEADS)
    a_v_sample = _cache_from_t(avo, A_KV_HEADS)

    q, k, v = _qkv_call(yp, g_attn, wqkv, cos_p, sin_p, qg, kg, ones_blk,
                        n_q=A_Q_HEADS, n_kv=A_KV_HEADS, tm=TM_QKV, q_dtype=BF16)
    k3, v3 = k.reshape(nb, seq, -1), v.reshape(nb, seq, -1)
    yp = _band_wo_call(q.reshape(nb, seq, -1), k3, v3, a_sinks[0], yp.reshape(nb, seq, -1), wo,
                       hkv=A_KV_HEADS, grp=grp_a).reshape(mp, D_MODEL)
    keep = min(A_WINDOW, seq)
    ca = A_KV_HEADS * HEAD_DIM
    a_k_prompt = _cache_from_t(_keep_t_call(k3, [keep], ca), A_KV_HEADS)
    a_v_prompt = _cache_from_t(_keep_t_call(v3, [keep], ca), A_KV_HEADS)

    conv_p, conv_s = [], []

    def ffn(layer, yp, ys):
        g_ffn = row(ffn_norm[layer])
        cw, cb = ffn_conv_w[layer], row(ffn_conv_b[layer])
        st = state_ffn_conv[layer]
        s0 = jnp.repeat(st[:, 0, :], dec, axis=0)
        s1 = jnp.repeat(st[:, 1, :], dec, axis=0)
        ys, gate_s, wg, wu, wd = _ffn_sample_call(ys, g_ffn, ffn_w_gate, ffn_w_up, ffn_w_down,
                                                  layer, cw, cb, s0, s1, seq=dec)
        conv_s.append(gate_s.reshape(ns, dec, D_FF)[:, dec - (CONV_W - 1):, :])
        yp, tail = _ffn_prompt_call(yp, g_ffn, wg, wu, wd, cw, cb, seq=seq)
        tiles = seq // TM_FFN
        conv_p.append(tail[tiles - 1::tiles, V7X_SUBLANES - (CONV_W - 1):, :])
        return yp, ys

    yp, ys = ffn(0, yp, ys)

    tn = b_w_qkv.shape[-1] // N_QKV_TILES
    qg, kg = head_gain(b_q_norm[0], tn // HEAD_DIM), head_gain(b_k_norm[0], tn // HEAD_DIM)
    g_attn = row(attn_norm[1])
    n_grp = len(B_PATTERNS)
    grp_b = B_Q_HEADS // B_KV_HEADS
    nqb, nkvb = n_grp * B_Q_HEADS, n_grp * B_KV_HEADS
    cb_ = B_KV_HEADS * HEAD_DIM

    qs, ks, vs, wqkv = _qkv_call(ys, g_attn, b_w_qkv, cos_s, sin_s, qg, kg, ones_blk,
                                 n_q=nqb, n_kv=nkvb, tm=ms, q_dtype=F32, cast_layer=0)
    os_, bko, bvo = _sample_b_call(qs, ks, vs, _cache_t(cache_b_k[0]), _cache_t(cache_b_v[0]),
                                   s=dec)
    ys, wo = _wo_cast_call(ys, os_, b_w_o, 0)
    b_k_sample = _cache_from_t(bko, B_KV_HEADS)
    b_v_sample = _cache_from_t(bvo, B_KV_HEADS)

    q, k, v = _qkv_call(yp, g_attn, wqkv, cos_p, sin_p, qg, kg, ones_blk,
                        n_q=nqb, n_kv=nkvb, tm=TM_QKV, q_dtype=F32, lane_major=True)
    outs, lses = [], []
    for g, (w, d) in enumerate(B_PATTERNS):
        assert w // d == BLOCK
        og, lg = _dil_call(q, k, v, nb=nb, hkv=B_KV_HEADS, grp=grp_b, d=d, group=g,
                           residue_major=(g == n_grp - 1))
        outs.append(og)
        lses.append(lg)
    yp = _wo_comb_call(yp, outs, lses, wo, tm=TM_PROJ // 2, d_last=B_PATTERNS[-1][1])
    keeps = [min(w, seq) for w, _ in B_PATTERNS]
    k4 = k.reshape(k.shape[0], nb, seq, V7X_LANES)
    v4 = v.reshape(v.shape[0], nb, seq, V7X_LANES)
    b_k_prompt = _cache_from_t(_keep_t_call(k4, keeps, cb_, lane_major=True), B_KV_HEADS)
    b_v_prompt = _cache_from_t(_keep_t_call(v4, keeps, cb_, lane_major=True), B_KV_HEADS)

    yp, ys = ffn(1, yp, ys)

    return (yp.reshape(nb, seq, D_MODEL), ys.reshape(ns, dec, D_MODEL),
            a_k_prompt, a_v_prompt, a_k_sample, a_v_sample,
            b_k_prompt, b_v_prompt, b_k_sample, b_v_sample,
            jnp.stack(conv_p), jnp.stack(conv_s))
```

```python
import functools

import jax
import jax.numpy as jnp
from jax import lax
from jax.experimental import pallas as pl
from jax.experimental.pallas import tpu as pltpu

F32 = jnp.float32
BF16 = jnp.bfloat16

D_MODEL = 2048
HEAD_DIM = 64
HALF = HEAD_DIM // 2
ROPE_THETA = 10000.0
NORM_EPS = 1e-6
BLOCK = 128
PAST_LEN = 16384
A_WINDOW = 128
A_Q_HEADS = 32
A_KV_HEADS = 8
B_PATTERNS = ((128, 1), (512, 4), (2048, 16))
B_Q_HEADS = 16
B_KV_HEADS = 4
D_FF = 5632
CONV_W = 3
NEG_INF = -1e30
LOG2E = 1.4426950408889634
LN2 = 0.6931471805599453
Q_SCALE = HEAD_DIM ** -0.5 * LOG2E

V7X_MXU_DIM = 256
V7X_LANES = 128
V7X_SUBLANES = 8
VMEM_LIMIT = 56 * 1024 * 1024

TM_PROJ = 512
TM_QKV = 512
TM_FFN = 1024
TF_FFN = 512
TN_WO_CAST = 512
PROJ_CHUNK_ROWS = 256
FFN_ROW_CHUNKS = 1
N_QKV_TILES = 6
DIL_MAX_ROWS = 1024
BAND_SUBBLOCKS = 4
ATTN_DEPTH = 2
DIL_TILES_PER_TRIP = 8


def _params(*sem):
    return pltpu.CompilerParams(dimension_semantics=sem, vmem_limit_bytes=VMEM_LIMIT)


def _rms(x, g):
    ms = jnp.mean(x * x, axis=-1, keepdims=True)
    return x * lax.rsqrt(ms + NORM_EPS) * g


def _head_norm_rope(a, gain, ones_blk, cos, sin):
    tn = a.shape[1]
    x2 = a * a
    hi = x2.astype(BF16)
    lo = (x2 - hi.astype(F32)).astype(BF16)
    parts = []
    for c in range(tn // V7X_MXU_DIM):
        sl = slice(V7X_MXU_DIM * c, V7X_MXU_DIM * (c + 1))
        parts.append(jnp.dot(hi[:, sl], ones_blk, preferred_element_type=F32)
                     + jnp.dot(lo[:, sl], ones_blk, preferred_element_type=F32))
    ss = jnp.concatenate(parts, axis=1)
    y = a * lax.rsqrt(ss * (1.0 / HEAD_DIM) + NORM_EPS) * gain
    lane = lax.broadcasted_iota(jnp.int32, y.shape, 1)
    first_half = (lane & (HEAD_DIM - 1)) < HALF
    partner = jnp.where(first_half, pltpu.roll(y, tn - HALF, 1), pltpu.roll(y, HALF, 1))
    reps = tn // V7X_LANES
    return y * jnp.tile(cos, (1, reps)) + partner * jnp.tile(sin, (1, reps))


def _store_cols(ref, rows, val, lane_major):
    if not lane_major:
        ref[rows, :] = val.astype(ref.dtype)
        return
    for t in range(val.shape[1] // V7X_LANES):
        ref[t, rows, :] = val[:, V7X_LANES * t:V7X_LANES * (t + 1)].astype(ref.dtype)


def _qkv_kernel(x_ref, g_ref, w_ref, cos_ref, sin_ref, qg_ref, kg_ref, ones_ref,
                q_ref, k_ref, v_ref, *rest, n_q_tiles, lane_major, row_chunks, emit_bf16):
    n = pl.program_id(1)
    if emit_bf16:
        w_out_ref, xn_ref = rest
        w_out_ref[...] = w_ref[0].astype(BF16)
        w_ref = w_out_ref
    else:
        (xn_ref,) = rest

    @pl.when(n == 0)
    def _():
        xn_ref[...] = _rms(x_ref[...], g_ref[...]).astype(BF16)

    rc = xn_ref.shape[0] // row_chunks

    def project(out_ref, epilogue):
        chunk = lambda c: slice(rc * c, rc * (c + 1))
        matmul = lambda c: jnp.dot(xn_ref[chunk(c), :], w_ref[...], preferred_element_type=F32)
        acc_next = matmul(0)
        for c in range(row_chunks):
            acc = acc_next
            if c + 1 < row_chunks:
                acc_next = matmul(c + 1)
            _store_cols(out_ref, chunk(c), epilogue(acc, chunk(c)), lane_major)

    def norm_rope(gain_ref, scale):
        def fn(acc, rows):
            r = _head_norm_rope(acc, gain_ref[...], ones_ref[...], cos_ref[rows, :],
                                sin_ref[rows, :])
            return r * scale if scale != 1.0 else r
        return fn

    @pl.when(n < n_q_tiles)
    def _():
        project(q_ref, norm_rope(qg_ref, Q_SCALE))

    @pl.when(n == n_q_tiles)
    def _():
        project(k_ref, norm_rope(kg_ref, 1.0))

    @pl.when(n == n_q_tiles + 1)
    def _():
        project(v_ref, lambda acc, rows: acc)


def _qkv_rows_kernel(x_ref, g_ref, w_ref, cos_ref, sin_ref, qg_ref, kg_ref, ones_ref,
                     q_ref, k_ref, v_ref, xn_ref, *, n_q_tiles, tn, lane_major, row_chunks):
    xn_ref[...] = _rms(x_ref[...], g_ref[...]).astype(BF16)
    rc = xn_ref.shape[0] // row_chunks
    chunk = lambda c: slice(rc * c, rc * (c + 1))
    jobs = [(n, c) for n in range(n_q_tiles + 2) for c in range(row_chunks)]

    def matmul(job):
        n, c = job
        return jnp.dot(xn_ref[chunk(c), :], w_ref[:, tn * n:tn * (n + 1)],
                       preferred_element_type=F32)

    def store(ref, rows, col0, val):
        if not lane_major:
            ref[rows, col0:col0 + tn] = val.astype(ref.dtype)
            return
        for t in range(tn // V7X_LANES):
            ref[col0 // V7X_LANES + t, rows, :] = (
                val[:, V7X_LANES * t:V7X_LANES * (t + 1)].astype(ref.dtype))

    def finish(job, acc):
        n, c = job
        rows = chunk(c)
        if n < n_q_tiles:
            r = _head_norm_rope(acc, qg_ref[...], ones_ref[...], cos_ref[rows, :], sin_ref[rows, :])
            store(q_ref, rows, tn * n, r * Q_SCALE)
        elif n == n_q_tiles:
            store(k_ref, rows, 0, _head_norm_rope(acc, kg_ref[...], ones_ref[...],
                                                  cos_ref[rows, :], sin_ref[rows, :]))
        else:
            store(v_ref, rows, 0, acc)

    acc_next = matmul(jobs[0])
    for i, job in enumerate(jobs):
        acc = acc_next
        if i + 1 < len(jobs):
            acc_next = matmul(jobs[i + 1])
        finish(job, acc)


def _qkv_rows_call(x, g, w, cos, sin, qg, kg, ones_blk, *, n_q, n_kv, tm, q_dtype, lane_major):
    m = x.shape[0]
    ncols = w.shape[1]
    tn = ncols // N_QKV_TILES
    n_q_tiles = (n_q * HEAD_DIM) // tn
    assert n_q_tiles * tn == n_q * HEAD_DIM and n_kv * HEAD_DIM == tn and tn % V7X_MXU_DIM == 0
    assert m % tm == 0 and n_q_tiles + 2 == N_QKV_TILES
    if lane_major:
        spec = lambda cols: pl.BlockSpec((cols // V7X_LANES, tm, V7X_LANES), lambda i: (0, i, 0))
        shape = lambda cols, dt: jax.ShapeDtypeStruct((cols // V7X_LANES, m, V7X_LANES), dt)
    else:
        spec = lambda cols: pl.BlockSpec((tm, cols), lambda i: (i, 0))
        shape = lambda cols, dt: jax.ShapeDtypeStruct((m, cols), dt)
    const = lambda shp: pl.BlockSpec(shp, lambda i: (0, 0))
    return pl.pallas_call(
        functools.partial(_qkv_rows_kernel, n_q_tiles=n_q_tiles, tn=tn, lane_major=lane_major,
                          row_chunks=max(1, tm // PROJ_CHUNK_ROWS)),
        grid=(m // tm,),
        in_specs=[
            pl.BlockSpec((tm, D_MODEL), lambda i: (i, 0)),
            const((1, D_MODEL)),
            pl.BlockSpec((D_MODEL, ncols), lambda i: (0, 0), pipeline_mode=pl.Buffered(1)),
            pl.BlockSpec((tm, V7X_LANES), lambda i: (i, 0)),
            pl.BlockSpec((tm, V7X_LANES), lambda i: (i, 0)),
            const((1, tn)),
            const((1, tn)),
            const((V7X_MXU_DIM, V7X_MXU_DIM)),
        ],
        out_specs=[spec(n_q * HEAD_DIM), spec(tn), spec(tn)],
        out_shape=[shape(n_q * HEAD_DIM, q_dtype), shape(tn, F32), shape(tn, F32)],
        scratch_shapes=[pltpu.VMEM((tm, D_MODEL), BF16)],
        compiler_params=_params("parallel"),
        name="qkv_proj_rows",
    )(x, g, w, cos, sin, qg, kg, ones_blk)


def _qkv_call(x, g, w, cos, sin, qg, kg, ones_blk, *, n_q, n_kv, tm, q_dtype, lane_major=False,
              cast_layer=None):
    m = x.shape[0]
    ncols = w.shape[-1]
    tn = ncols // N_QKV_TILES
    n_q_tiles = (n_q * HEAD_DIM) // tn
    assert n_q_tiles * tn == n_q * HEAD_DIM and n_kv * HEAD_DIM == tn and tn % V7X_MXU_DIM == 0
    assert m % tm == 0
    last_q = n_q_tiles - 1
    if lane_major:
        lt = tn // V7X_LANES
        out_specs = [
            pl.BlockSpec((lt, tm, V7X_LANES), lambda i, n: (jnp.minimum(n, last_q), i, 0)),
            pl.BlockSpec((lt, tm, V7X_LANES), lambda i, n: (0, i, 0)),
            pl.BlockSpec((lt, tm, V7X_LANES), lambda i, n: (0, i, 0)),
        ]
        out_shape = [
            jax.ShapeDtypeStruct((n_q_tiles * lt, m, V7X_LANES), q_dtype),
            jax.ShapeDtypeStruct((lt, m, V7X_LANES), F32),
            jax.ShapeDtypeStruct((lt, m, V7X_LANES), F32),
        ]
    else:
        out_specs = [
            pl.BlockSpec((tm, tn), lambda i, n: (i, jnp.minimum(n, last_q))),
            pl.BlockSpec((tm, tn), lambda i, n: (i, 0)),
            pl.BlockSpec((tm, tn), lambda i, n: (i, 0)),
        ]
        out_shape = [
            jax.ShapeDtypeStruct((m, n_q * HEAD_DIM), q_dtype),
            jax.ShapeDtypeStruct((m, tn), F32),
            jax.ShapeDtypeStruct((m, tn), F32),
        ]
    emit_bf16 = cast_layer is not None
    if emit_bf16:
        assert m == tm
        w_spec = pl.BlockSpec((1, D_MODEL, tn), lambda i, n: (cast_layer, 0, n))
        out_specs = out_specs + [pl.BlockSpec((D_MODEL, tn), lambda i, n: (0, n))]
        out_shape = out_shape + [jax.ShapeDtypeStruct((D_MODEL, ncols), BF16)]
    else:
        w_spec = pl.BlockSpec((D_MODEL, tn), lambda i, n: (0, n))
    return pl.pallas_call(
        functools.partial(_qkv_kernel, n_q_tiles=n_q_tiles, lane_major=lane_major,
                          row_chunks=max(1, tm // PROJ_CHUNK_ROWS), emit_bf16=emit_bf16),
        grid=(m // tm, N_QKV_TILES),
        in_specs=[
            pl.BlockSpec((tm, D_MODEL), lambda i, n: (i, 0)),
            pl.BlockSpec((1, D_MODEL), lambda i, n: (0, 0)),
            w_spec,
            pl.BlockSpec((tm, V7X_LANES), lambda i, n: (i, 0)),
            pl.BlockSpec((tm, V7X_LANES), lambda i, n: (i, 0)),
            pl.BlockSpec((1, tn), lambda i, n: (0, 0)),
            pl.BlockSpec((1, tn), lambda i, n: (0, 0)),
            pl.BlockSpec((V7X_MXU_DIM, V7X_MXU_DIM), lambda i, n: (0, 0)),
        ],
        out_specs=out_specs,
        out_shape=out_shape,
        scratch_shapes=[pltpu.VMEM((tm, D_MODEL), BF16)],
        compiler_params=_params("parallel", "arbitrary"),
        name="qkv_proj",
    )(x, g, w, cos, sin, qg, kg, ones_blk)


GRP = 4


def _band_mask_t(first):
    shape = (2 * BLOCK, BLOCK)
    kj = lax.broadcasted_iota(jnp.int32, shape, 0)
    qi = lax.broadcasted_iota(jnp.int32, shape, 1)
    seen = (kj >= qi) & (kj <= qi + BLOCK) & ((kj >= BLOCK) | jnp.logical_not(first))
    return jnp.where(seen, 0.0, NEG_INF)


def _kv_tile_forms(kt, vt):
    lo = lax.broadcasted_iota(jnp.int32, kt.shape, 1) < HEAD_DIM
    kt_sw = pltpu.roll(kt, HEAD_DIM, 1)
    k_forms = [(jnp.where(lo, kt, 0.0).astype(BF16), jnp.where(lo, 0.0, kt_sw).astype(BF16)),
               (jnp.where(lo, kt_sw, 0.0).astype(BF16), jnp.where(lo, 0.0, kt).astype(BF16))]
    return k_forms, vt.T.astype(BF16)


def _head_scores(k_lo, k_hi, q_a, q_b):
    nt = (((1,), (1,)), ((), ()))
    qpair = jnp.concatenate([q_a, q_b], axis=0)
    return jnp.concatenate([lax.dot_general(k_lo, qpair, nt, preferred_element_type=F32),
                            lax.dot_general(k_hi, qpair, nt, preferred_element_type=F32)], axis=1)


def _head_softmax_pv(s, mask4, v_t, sinks, want_lse):
    s = s + mask4
    m = jnp.max(s, axis=0, keepdims=True)
    if sinks is not None:
        sk = jnp.concatenate([jnp.full((1, BLOCK), sinks[j] * LOG2E, F32)
                              for j in (0, 2, 1, 3)], axis=1)
        m = jnp.maximum(m, sk)
    p = jnp.exp2(s - m)
    l = jnp.sum(p, axis=0, keepdims=True)
    if sinks is not None:
        l = l + jnp.exp2(sk - m)
    o_t = jnp.dot(v_t, p.astype(BF16), preferred_element_type=F32) / l
    lse_t = jnp.broadcast_to((m + jnp.log2(l)) * LN2, o_t.shape) if want_lse else None
    o_tiles, l_tiles = [], []
    for u in range(2):
        c0, c1 = slice(BLOCK * u, BLOCK * (u + 1)), slice(BLOCK * (2 + u), BLOCK * (3 + u))
        o_tiles.append(jnp.concatenate([o_t[:, c0], o_t[:, c1]], axis=0).T)
        if want_lse:
            l_tiles.append(jnp.concatenate([lse_t[:, c0], lse_t[:, c1]], axis=0).T)
    return o_tiles, l_tiles


def _attend_tiles(tiles, want_lse, fillers=None):
    jobs = [(ti, e) for ti in range(len(tiles)) for e in range(2)]
    forms = {}

    def scores(job):
        ti, e = job
        load_kv, load_q = tiles[ti][:2]
        if ti not in forms:
            forms[ti] = _kv_tile_forms(*load_kv())
        k_lo, k_hi = forms[ti][0][e]
        return _head_scores(k_lo, k_hi, load_q(2 * e), load_q(2 * e + 1))

    pending = [scores(job) for job in jobs[:ATTN_DEPTH]]
    for idx, (ti, e) in enumerate(jobs):
        s = pending.pop(0)
        if idx + ATTN_DEPTH < len(jobs):
            pending.append(scores(jobs[idx + ATTN_DEPTH]))
        sink_of, store, mask4 = tiles[ti][2:]
        v_t = forms[ti][1][HEAD_DIM * e:HEAD_DIM * (e + 1)]
        sinks = None if sink_of is None else [sink_of(GRP * e + j) for j in range(GRP)]
        o_tiles, l_tiles = _head_softmax_pv(s, mask4, v_t, sinks, want_lse)
        for u in range(2):
            store(2 * e + u, o_tiles[u], l_tiles[u] if want_lse else None)
        if fillers and idx in fillers:
            fillers[idx]()


def _band_wo_kernel(q_ref, kc_ref, kp_ref, vc_ref, vp_ref, sink_ref, y_ref, wo_ref, out_ref,
                    o_scr, *, kv_tiles, nsub):
    mask_first = jnp.tile(_band_mask_t(pl.program_id(1) == 0), (1, GRP))
    mask_rest = jnp.tile(_band_mask_t(False), (1, GRP))
    lanes = lambda t: slice(V7X_LANES * t, V7X_LANES * (t + 1))
    rows = lambda j: slice(BLOCK * j, BLOCK * (j + 1))

    def tile(j, t):
        def load_kv():
            if j == 0:
                kp, vp = kp_ref[0, :, lanes(t)], vp_ref[0, :, lanes(t)]
            else:
                kp, vp = kc_ref[0, rows(j - 1), lanes(t)], vc_ref[0, rows(j - 1), lanes(t)]
            return (jnp.concatenate([kp, kc_ref[0, rows(j), lanes(t)]], axis=0),
                    jnp.concatenate([vp, vc_ref[0, rows(j), lanes(t)]], axis=0))

        def store(u, o_tile, _):
            o_scr[rows(j), lanes(GRP * t + u)] = o_tile.astype(o_scr.dtype)

        return (load_kv, lambda u: q_ref[0, rows(j), lanes(GRP * t + u)],
                lambda h: sink_ref[2 * GRP * t + h], store, mask_first if j == 0 else mask_rest)

    jobs_per_block = 2 * kv_tiles
    n_out = out_ref.shape[2]
    piece = n_out // jobs_per_block

    def project(j, c):
        cols = slice(piece * c, piece * (c + 1))
        out_ref[0, rows(j), cols] = y_ref[0, rows(j), cols] + jnp.dot(
            o_scr[rows(j), :], wo_ref[:, cols], preferred_element_type=F32)

    fillers = {jobs_per_block * j + c: functools.partial(project, j - 1, c)
               for j in range(1, nsub) for c in range(jobs_per_block)}
    _attend_tiles([tile(j, t) for j in range(nsub) for t in range(kv_tiles)], False, fillers)
    for c in range(jobs_per_block):
        project(nsub - 1, c)


def _band_wo_call(q, k, v, sinks, y, wo, *, hkv, grp):
    b, seq, cq = q.shape
    ck = hkv * HEAD_DIM
    nsub = BAND_SUBBLOCKS
    rows = BLOCK * nsub
    assert grp == GRP and hkv % 2 == 0 and seq % rows == 0
    cur = lambda bi, c: (bi, c, 0)
    prev = lambda bi, c: (bi, jnp.maximum(c * nsub - 1, 0), 0)
    return pl.pallas_call(
        functools.partial(_band_wo_kernel, kv_tiles=hkv // 2, nsub=nsub),
        grid=(b, seq // rows),
        in_specs=[
            pl.BlockSpec((1, rows, cq), cur),
            pl.BlockSpec((1, rows, ck), cur),
            pl.BlockSpec((1, BLOCK, ck), prev),
            pl.BlockSpec((1, rows, ck), cur),
            pl.BlockSpec((1, BLOCK, ck), prev),
            pl.BlockSpec(memory_space=pltpu.SMEM),
            pl.BlockSpec((1, rows, D_MODEL), cur),
            pl.BlockSpec((cq, D_MODEL), lambda bi, c: (0, 0)),
        ],
        out_specs=pl.BlockSpec((1, rows, D_MODEL), cur),
        out_shape=jax.ShapeDtypeStruct((b, seq, D_MODEL), F32),
        scratch_shapes=[pltpu.VMEM((rows, cq), BF16)],
        compiler_params=_params("parallel", "arbitrary"),
        name="band_attn_wo",
    )(q, k, k, v, v, sinks, y, wo)


def _dil_kernel(q_ref, kc_ref, kp_ref, vc_ref, vp_ref, o_ref, lse_ref, *, kv_tiles, grp, d, nsub,
                residue_major):
    mask_first = jnp.tile(_band_mask_t(pl.program_id(1) == 0), (1, GRP))

    if d == 1:
        mask_rest = jnp.tile(_band_mask_t(False), (1, GRP))
        rows = lambda j: slice(BLOCK * j, BLOCK * (j + 1))

        def sub_tile(j, t):
            def load_kv():
                if j == 0:
                    kp, vp = kp_ref[t], vp_ref[t]
                else:
                    kp, vp = kc_ref[t, rows(j - 1), :], vc_ref[t, rows(j - 1), :]
                return (jnp.concatenate([kp, kc_ref[t, rows(j), :]], axis=0),
                        jnp.concatenate([vp, vc_ref[t, rows(j), :]], axis=0))

            def store(u, o_tile, l_tile):
                o_ref[grp * t + u, rows(j), :] = o_tile
                lse_ref[grp * t + u, rows(j), :] = l_tile

            return (load_kv, lambda u: q_ref[grp * t + u, rows(j), :].astype(BF16), None, store,
                    mask_first if j == 0 else mask_rest)

        _attend_tiles([sub_tile(j, t) for j in range(nsub) for t in range(kv_tiles)], True)
        return

    def tile(r, t):
        rs = pl.ds(r, BLOCK, stride=d)

        def load_kv():
            return (jnp.concatenate([kp_ref[t, rs, :], kc_ref[t, rs, :]], axis=0),
                    jnp.concatenate([vp_ref[t, rs, :], vc_ref[t, rs, :]], axis=0))

        ws = pl.ds(pl.multiple_of(r * BLOCK, BLOCK), BLOCK) if residue_major else rs

        def store(u, o_tile, l_tile):
            o_ref[grp * t + u, ws, :] = o_tile
            lse_ref[grp * t + u, ws, :] = l_tile

        return (load_kv, lambda u: q_ref[grp * t + u, rs, :].astype(BF16), None, store,
                mask_first)

    unroll = min(d, max(1, DIL_TILES_PER_TRIP // kv_tiles))

    def body(i, carry):
        _attend_tiles([tile(i * unroll + rr, t) for rr in range(unroll) for t in range(kv_tiles)],
                      True)
        return carry

    if d > unroll:
        lax.fori_loop(0, d // unroll, body, 0)
    else:
        body(0, 0)


def _dil_call(q, k, v, *, nb, hkv, grp, d, group, residue_major=False):
    m = q.shape[1]
    seq = m // nb
    nsub = BAND_SUBBLOCKS if d == 1 else 1
    rows = BLOCK * d * nsub
    hsplit = max(1, rows // DIL_MAX_ROWS)
    kv_tiles = hkv // 2 // hsplit
    q_tiles = kv_tiles * grp
    nchunk = seq // rows
    assert seq % rows == 0 and kv_tiles * 2 * hsplit == hkv and grp % 2 == 0
    cur = lambda bi, c, hp: (group * hsplit + hp, bi * nchunk + c, 0)
    if d == 1:
        prev_rows = BLOCK
        prev = lambda bi, c, hp: (group * hsplit + hp,
                                  jnp.maximum((bi * nchunk + c) * nsub - 1, 0), 0)
    else:
        prev_rows = rows
        prev = lambda bi, c, hp: (group * hsplit + hp, bi * nchunk + jnp.maximum(c - 1, 0), 0)
    o_spec = pl.BlockSpec((q_tiles, rows, V7X_LANES), lambda bi, c, hp: (hp, bi * nchunk + c, 0))
    o_shape = jax.ShapeDtypeStruct((q_tiles * hsplit, m, V7X_LANES), F32)
    return pl.pallas_call(
        functools.partial(_dil_kernel, kv_tiles=kv_tiles, grp=grp, d=d, nsub=nsub,
                          residue_major=residue_major and d > 1),
        grid=(nb, nchunk, hsplit),
        in_specs=[
            pl.BlockSpec((q_tiles, rows, V7X_LANES), cur),
            pl.BlockSpec((kv_tiles, rows, V7X_LANES), cur),
            pl.BlockSpec((kv_tiles, prev_rows, V7X_LANES), prev),
            pl.BlockSpec((kv_tiles, rows, V7X_LANES), cur),
            pl.BlockSpec((kv_tiles, prev_rows, V7X_LANES), prev),
        ],
        out_specs=[o_spec, o_spec],
        out_shape=[o_shape, o_shape],
        compiler_params=_params("parallel", "arbitrary", "arbitrary"),
        name="dilated_attn",
    )(q, k, k, v, v)


def _keep_t_kernel(x_ref, o_ref):
    o_ref[0] = x_ref[0].T


def _keep_t_lane_major_kernel(x_ref, o_ref, *, keeps):
    g = pl.program_id(2)
    seq = x_ref.shape[2]
    off = 0
    for k, keep in enumerate(keeps):
        @pl.when(g == k)
        def _(keep=keep, off=off):
            o_ref[0, :, off:off + keep] = x_ref[0, 0, seq - keep:, :].T
        off += keep


def _keep_t_call(x, keeps, c, *, lane_major=False):
    b, seq = x.shape[1:3] if lane_major else x.shape[:2]
    starts, first_rb = [], []
    n = 0
    for keep in keeps:
        assert keep % BLOCK == 0 and seq % BLOCK == 0
        starts.append(n)
        first_rb.append((seq - keep) // BLOCK)
        n += keep // BLOCK

    def src_block(j):
        rb = jnp.int32(0)
        cb = jnp.int32(0)
        for g in range(len(keeps)):
            inside = j >= starts[g]
            rb = jnp.where(inside, first_rb[g] + j - starts[g], rb)
            cb = jnp.where(inside, g, cb)
        return rb, cb

    if lane_major:
        ct = c // V7X_LANES
        grid = (b, ct, len(keeps))
        in_spec = pl.BlockSpec((1, 1, seq, V7X_LANES), lambda bi, t, g: (g * ct + t, bi, 0, 0))
        out_spec = pl.BlockSpec((1, V7X_LANES, n * BLOCK), lambda bi, t, g: (bi, t, 0))
        kern = functools.partial(_keep_t_lane_major_kernel, keeps=tuple(keeps))
        sem = ("parallel", "parallel", "arbitrary")
    else:
        grid = (b, n)

        def src(bi, j):
            rb, cb = src_block(j)
            return bi, rb, cb

        in_spec = pl.BlockSpec((1, BLOCK, c), src)
        out_spec = pl.BlockSpec((1, c, BLOCK), lambda bi, j: (bi, 0, j))
        kern = _keep_t_kernel
        sem = ("parallel", "parallel")
    return pl.pallas_call(
        kern,
        grid=grid,
        in_specs=[in_spec],
        out_specs=out_spec,
        out_shape=jax.ShapeDtypeStruct((b, c, n * BLOCK), F32),
        compiler_params=_params(*sem),
        name="keep_rows_t",
    )(x)


def _wo_comb_kernel(y_ref, o0_ref, o1_ref, o2_ref, l0_ref, l1_ref, l2_ref, w_ref, out_ref):
    def natural(ref, t):
        return jnp.concatenate([ref[t, :, n, :] for n in range(ref.shape[2])], axis=0)

    tiles = []
    for t in range(o0_ref.shape[0]):
        l0, l1, l2 = l0_ref[t], l1_ref[t], natural(l2_ref, t)
        mx = jnp.maximum(jnp.maximum(l0, l1), l2)
        e0, e1, e2 = jnp.exp(l0 - mx), jnp.exp(l1 - mx), jnp.exp(l2 - mx)
        den = e0 + e1 + e2
        comb = ((e0 / den) * o0_ref[t] + (e1 / den) * o1_ref[t]
                + (e2 / den) * natural(o2_ref, t))
        tiles.append(comb.astype(BF16))
    comb = jnp.concatenate(tiles, axis=1)
    out_ref[...] = y_ref[...] + jnp.dot(comb, w_ref[...], preferred_element_type=F32)


def _wo_cast_kernel(y_ref, o_ref, w_ref, out_ref, w_out_ref):
    w_out_ref[...] = w_ref[0].astype(BF16)
    out_ref[...] = y_ref[...] + jnp.dot(o_ref[...].astype(BF16), w_out_ref[...],
                                        preferred_element_type=F32)


def _wo_cast_call(y, o, w, layer):
    m = y.shape[0]
    c = o.shape[1]
    tn = TN_WO_CAST
    return pl.pallas_call(
        _wo_cast_kernel,
        grid=(D_MODEL // tn,),
        in_specs=[
            pl.BlockSpec((m, tn), lambda n: (0, n)),
            pl.BlockSpec((m, c), lambda n: (0, 0)),
            pl.BlockSpec((1, c, tn), lambda n: (layer, 0, n)),
        ],
        out_specs=[pl.BlockSpec((m, tn), lambda n: (0, n)),
                   pl.BlockSpec((c, tn), lambda n: (0, n))],
        out_shape=[jax.ShapeDtypeStruct((m, D_MODEL), F32),
                   jax.ShapeDtypeStruct((c, D_MODEL), BF16)],
        compiler_params=_params("parallel"),
        name="wo_proj_cast",
    )(y, o, w)


def _wo_comb_call(y, os_, ls_, w, *, tm, d_last):
    m = y.shape[0]
    c = w.shape[0]
    nt = c // V7X_LANES
    per_blk = BLOCK * d_last // tm
    assert (BLOCK * d_last) % tm == 0 and tm % d_last == 0 and (tm // d_last) % V7X_SUBLANES == 0
    blk = pl.BlockSpec((nt, tm, V7X_LANES), lambda i: (0, i, 0))
    blk_rm = pl.BlockSpec((nt, d_last, tm // d_last, V7X_LANES),
                          lambda i: (0, i // per_blk, i % per_blk, 0))
    rm = lambda a: a.reshape(nt, m // BLOCK, BLOCK, V7X_LANES)
    return pl.pallas_call(
        _wo_comb_kernel,
        grid=(m // tm,),
        in_specs=[pl.BlockSpec((tm, D_MODEL), lambda i: (i, 0)), blk, blk, blk_rm, blk, blk, blk_rm,
                  pl.BlockSpec((c, D_MODEL), lambda i: (0, 0))],
        out_specs=pl.BlockSpec((tm, D_MODEL), lambda i: (i, 0)),
        out_shape=jax.ShapeDtypeStruct((m, D_MODEL), F32),
        compiler_params=_params("parallel"),
        name="wo_comb_proj",
    )(y, os_[0], os_[1], rm(os_[2]), ls_[0], ls_[1], rm(ls_[2]), w)


def _ffn_tail(gate, g1, g2, up, cw_ref, cb_ref, wd_ref):
    conv = cb_ref[...] + cw_ref[0:1, :] * g2 + cw_ref[1:2, :] * g1 + cw_ref[2:3, :] * gate
    h = conv * jax.nn.sigmoid(conv) * up
    return jnp.dot(h.astype(BF16), wd_ref[...], preferred_element_type=F32)


def _ffn_prompt_kernel(y_ref, g_ref, wg_ref, wu_ref, wd_ref, cw_ref, cb_ref,
                       out_ref, tail_ref, xn_ref, carry_ref, *, tiles_per_seq):
    m = pl.program_id(0)
    f = pl.program_id(1)

    @pl.when(f == 0)
    def _():
        x = y_ref[...]
        xn_ref[...] = _rms(x, g_ref[...]).astype(BF16)
        out_ref[...] = x

    @pl.when(m % tiles_per_seq == 0)
    def _():
        carry_ref[f] = jnp.zeros(carry_ref.shape[1:], F32)

    rc = xn_ref.shape[0] // FFN_ROW_CHUNKS
    last = carry_ref[f]
    for c in range(FFN_ROW_CHUNKS):
        rows = slice(rc * c, rc * (c + 1))
        xn = xn_ref[rows, :]
        gate = jnp.dot(xn, wg_ref[...], preferred_element_type=F32)
        up = jnp.dot(xn, wu_ref[...], preferred_element_type=F32)
        row = lax.broadcasted_iota(jnp.int32, gate.shape, 0)
        g1 = jnp.where(row == 0, last[7:8, :], pltpu.roll(gate, 1, 0))
        g2 = jnp.where(row == 0, last[6:7, :],
                       jnp.where(row == 1, last[7:8, :], pltpu.roll(gate, 2, 0)))
        out_ref[rows, :] += _ffn_tail(gate, g1, g2, up, cw_ref, cb_ref, wd_ref)
        last = gate[rc - V7X_SUBLANES:, :]
    carry_ref[f] = last
    tail_ref[0] = last


def _ffn_prompt_call(y, g, wg, wu, wd, cw, cb, *, seq):
    m = y.shape[0]
    tm, tf = TM_FFN, TF_FFN
    nf = D_FF // tf
    assert m % tm == 0 and seq % tm == 0 and D_FF % tf == 0
    return pl.pallas_call(
        functools.partial(_ffn_prompt_kernel, tiles_per_seq=seq // tm),
        grid=(m // tm, nf),
        in_specs=[
            pl.BlockSpec((tm, D_MODEL), lambda i, f: (i, 0)),
            pl.BlockSpec((1, D_MODEL), lambda i, f: (0, 0)),
            pl.BlockSpec((D_MODEL, tf), lambda i, f: (0, f)),
            pl.BlockSpec((D_MODEL, tf), lambda i, f: (0, f)),
            pl.BlockSpec((tf, D_MODEL), lambda i, f: (f, 0)),
            pl.BlockSpec((CONV_W, tf), lambda i, f: (0, f)),
            pl.BlockSpec((1, tf), lambda i, f: (0, f)),
        ],
        out_specs=[
            pl.BlockSpec((tm, D_MODEL), lambda i, f: (i, 0)),
            pl.BlockSpec((1, V7X_SUBLANES, tf), lambda i, f: (i, 0, f)),
        ],
        out_shape=[
            jax.ShapeDtypeStruct((m, D_MODEL), F32),
            jax.ShapeDtypeStruct((m // tm, V7X_SUBLANES, D_FF), F32),
        ],
        scratch_shapes=[pltpu.VMEM((tm, D_MODEL), BF16),
                        pltpu.VMEM((nf, V7X_SUBLANES, tf), F32)],
        compiler_params=_params("arbitrary", "arbitrary"),
        name="conv_ffn_prompt",
    )(y, g, wg, wu, wd, cw, cb)


def _ffn_sample_kernel(y_ref, g_ref, wg_ref, wu_ref, wd_ref, cw_ref, cb_ref, s0_ref, s1_ref,
                       out_ref, gate_ref, wg_out_ref, wu_out_ref, wd_out_ref, xn_ref, *, seq):
    f = pl.program_id(0)

    @pl.when(f == 0)
    def _():
        x = y_ref[...]
        xn_ref[...] = _rms(x, g_ref[...]).astype(BF16)
        out_ref[...] = x

    wg_out_ref[...] = wg_ref[0].astype(BF16)
    wu_out_ref[...] = wu_ref[0].astype(BF16)
    wd_out_ref[...] = wd_ref[0].astype(BF16)
    xn = xn_ref[...]
    gate = jnp.dot(xn, wg_out_ref[...], preferred_element_type=F32)
    up = jnp.dot(xn, wu_out_ref[...], preferred_element_type=F32)
    t = lax.broadcasted_iota(jnp.int32, gate.shape, 0) & (seq - 1)
    s0, s1 = s0_ref[...], s1_ref[...]
    g1 = jnp.where(t == 0, s1, pltpu.roll(gate, 1, 0))
    g2 = jnp.where(t == 0, s0, jnp.where(t == 1, s1, pltpu.roll(gate, 2, 0)))
    out_ref[...] += _ffn_tail(gate, g1, g2, up, cw_ref, cb_ref, wd_out_ref)
    gate_ref[...] = gate


def _ffn_sample_call(y, g, wg, wu, wd, layer, cw, cb, s0, s1, *, seq):
    m = y.shape[0]
    tf = TF_FFN
    nf = D_FF // tf
    full = pl.BlockSpec((m, D_MODEL), lambda f: (0, 0))
    col = pl.BlockSpec((m, tf), lambda f: (0, f))
    return pl.pallas_call(
        functools.partial(_ffn_sample_kernel, seq=seq),
        grid=(nf,),
        in_specs=[
            full,
            pl.BlockSpec((1, D_MODEL), lambda f: (0, 0)),
            pl.BlockSpec((1, D_MODEL, tf), lambda f: (layer, 0, f)),
            pl.BlockSpec((1, D_MODEL, tf), lambda f: (layer, 0, f)),
            pl.BlockSpec((1, tf, D_MODEL), lambda f: (layer, f, 0)),
            pl.BlockSpec((CONV_W, tf), lambda f: (0, f)),
            pl.BlockSpec((1, tf), lambda f: (0, f)),
            col, col,
        ],
        out_specs=[full, col,
                   pl.BlockSpec((D_MODEL, tf), lambda f: (0, f)),
                   pl.BlockSpec((D_MODEL, tf), lambda f: (0, f)),
                   pl.BlockSpec((tf, D_MODEL), lambda f: (f, 0))],
        out_shape=[jax.ShapeDtypeStruct((m, D_MODEL), F32),
                   jax.ShapeDtypeStruct((m, D_FF), F32),
                   jax.ShapeDtypeStruct((D_MODEL, D_FF), BF16),
                   jax.ShapeDtypeStruct((D_MODEL, D_FF), BF16),
                   jax.ShapeDtypeStruct((D_FF, D_MODEL), BF16)],
        scratch_shapes=[pltpu.VMEM((m, D_MODEL), BF16)],
        compiler_params=_params("arbitrary"),
        name="conv_ffn_sample",
    )(y, g, wg, wu, wd, cw, cb, s0, s1)


def _block_diag_q(q_ref, col0, hkv, grp, s):
    blocks = []
    for h in range(hkv):
        qh = jnp.concatenate(
            [q_ref[:, col0 + HEAD_DIM * (grp * h + j):col0 + HEAD_DIM * (grp * h + j + 1)]
             for j in range(grp)], axis=0)
        pieces = []
        if h > 0:
            pieces.append(jnp.zeros((grp * s, HEAD_DIM * h), F32))
        pieces.append(qh)
        if h < hkv - 1:
            pieces.append(jnp.zeros((grp * s, HEAD_DIM * (hkv - 1 - h)), F32))
        blocks.append(jnp.concatenate(pieces, axis=1) if len(pieces) > 1 else qh)
    return jnp.concatenate(blocks, axis=0).astype(BF16)


def _new_rows_t(x, s):
    pad = jnp.zeros((V7X_LANES - s, x.shape[1]), F32)
    return jnp.concatenate([pad, x], axis=0).T


def _cached_attend(qbd, ck_t, kn_t, cv_t, vn_t, d, s, sink_col):
    r_, lc = qbd.shape[0], ck_t.shape[1]
    sc = jnp.dot(qbd, ck_t.astype(BF16), preferred_element_type=F32)
    sn = jnp.dot(qbd, kn_t.astype(BF16), preferred_element_type=F32)
    ic = lax.broadcasted_iota(jnp.int32, (r_, lc), 0) & (s - 1)
    c = lax.broadcasted_iota(jnp.int32, (r_, lc), 1)
    i_n = lax.broadcasted_iota(jnp.int32, (r_, V7X_LANES), 0) & (s - 1)
    j = lax.broadcasted_iota(jnp.int32, (r_, V7X_LANES), 1) - (V7X_LANES - s)
    valid_c = c >= ic
    valid_n = (j >= 0) & (j <= i_n)
    if d > 1:
        valid_c = valid_c & ((c & (d - 1)) == (ic & (d - 1)))
        valid_n = valid_n & ((j & (d - 1)) == (i_n & (d - 1)))
    sc = jnp.where(valid_c, sc, NEG_INF)
    sn = jnp.where(valid_n, sn, NEG_INF)
    m = jnp.maximum(jnp.max(sc, axis=1, keepdims=True), jnp.max(sn, axis=1, keepdims=True))
    if sink_col is not None:
        m = jnp.maximum(m, sink_col)
    pc = jnp.exp2(sc - m)
    pn = jnp.exp2(sn - m)
    l = jnp.sum(pc, axis=1, keepdims=True) + jnp.sum(pn, axis=1, keepdims=True)
    if sink_col is not None:
        l = l + jnp.exp2(sink_col - m)
    nt = (((1,), (1,)), ((), ()))
    o = (lax.dot_general(pc.astype(BF16), cv_t.astype(BF16), nt, preferred_element_type=F32)
         + lax.dot_general(pn.astype(BF16), vn_t.astype(BF16), nt, preferred_element_type=F32)) / l
    return o, (m + jnp.log2(l)) * LN2


def _diag_heads(o, hkv, grp, s):
    pieces = []
    for h in range(hkv):
        for j in range(grp):
            r0 = (h * grp + j) * s
            pieces.append(o[r0:r0 + s, HEAD_DIM * h:HEAD_DIM * (h + 1)])
    return jnp.concatenate(pieces, axis=1)


def _store_shifted(out_ref, off, c_t, n_t, s):
    lc = c_t.shape[1]
    rolled = pltpu.roll(c_t, lc - s, 1)
    lane = lax.broadcasted_iota(jnp.int32, n_t.shape, 1)
    if lc > V7X_LANES:
        out_ref[0, :, off:off + lc - V7X_LANES] = rolled[:, :lc - V7X_LANES]
    out_ref[0, :, off + lc - V7X_LANES:off + lc] = jnp.where(
        lane < V7X_LANES - s, rolled[:, lc - V7X_LANES:], n_t)


def _sample_a_kernel(q_ref, kn_ref, vn_ref, ck_ref, cv_ref, sink_ref, o_ref, ko_ref, vo_ref, *, s):
    hkv, grp = A_KV_HEADS, A_Q_HEADS // A_KV_HEADS
    kn_t, vn_t = _new_rows_t(kn_ref[...], s), _new_rows_t(vn_ref[...], s)
    ck_t, cv_t = ck_ref[0], cv_ref[0]
    qbd = _block_diag_q(q_ref, 0, hkv, grp, s)
    sink_col = jnp.concatenate(
        [jnp.full((s, 1), sink_ref[hq] * LOG2E, F32) for hq in range(hkv * grp)], axis=0)
    o, _ = _cached_attend(qbd, ck_t, kn_t, cv_t, vn_t, 1, s, sink_col)
    o_ref[...] = _diag_heads(o, hkv, grp, s)
    _store_shifted(ko_ref, 0, ck_t, kn_t, s)
    _store_shifted(vo_ref, 0, cv_t, vn_t, s)


def _sample_a_call(q, kn, vn, ck_t, cv_t, sinks, *, s):
    nb, c, lc = ck_t.shape
    row = lambda w: pl.BlockSpec((s, w), lambda b: (b, 0))
    cache = pl.BlockSpec((1, c, lc), lambda b: (b, 0, 0))
    return pl.pallas_call(
        functools.partial(_sample_a_kernel, s=s),
        grid=(nb,),
        in_specs=[row(q.shape[1]), row(c), row(c), cache, cache,
                  pl.BlockSpec(memory_space=pltpu.SMEM)],
        out_specs=[row(q.shape[1]), cache, cache],
        out_shape=[jax.ShapeDtypeStruct(q.shape, F32),
                   jax.ShapeDtypeStruct(ck_t.shape, F32),
                   jax.ShapeDtypeStruct(cv_t.shape, F32)],
        compiler_params=_params("parallel"),
        name="sample_attn_a",
    )(q, kn, vn, ck_t, cv_t, sinks)


def _sample_b_kernel(q_ref, kn_ref, vn_ref, ck_ref, cv_ref, o_ref, ko_ref, vo_ref, *, s):
    hkv, grp = B_KV_HEADS, B_Q_HEADS // B_KV_HEADS
    ckv = hkv * HEAD_DIM
    outs, lses = [], []
    off = 0
    for g, (w, d) in enumerate(B_PATTERNS):
        lc = w
        kn_t = _new_rows_t(kn_ref[:, ckv * g:ckv * (g + 1)], s)
        vn_t = _new_rows_t(vn_ref[:, ckv * g:ckv * (g + 1)], s)
        ck_t = ck_ref[0, :, off:off + lc]
        cv_t = cv_ref[0, :, off:off + lc]
        qbd = _block_diag_q(q_ref, B_Q_HEADS * HEAD_DIM * g, hkv, grp, s)
        o, lse = _cached_attend(qbd, ck_t, kn_t, cv_t, vn_t, d, s, None)
        outs.append(o)
        lses.append(lse)
        _store_shifted(ko_ref, off, ck_t, kn_t, s)
        _store_shifted(vo_ref, off, cv_t, vn_t, s)
        off += lc
    mx = jnp.maximum(jnp.maximum(lses[0], lses[1]), lses[2])
    es = [jnp.exp(l - mx) for l in lses]
    den = es[0] + es[1] + es[2]
    comb = (es[0] / den) * outs[0] + (es[1] / den) * outs[1] + (es[2] / den) * outs[2]
    o_ref[...] = _diag_heads(comb, hkv, grp, s)


def _sample_b_call(q, kn, vn, ck_t, cv_t, *, s):
    nb, c, lb = ck_t.shape
    row = lambda w: pl.BlockSpec((s, w), lambda b: (b, 0))
    cache = pl.BlockSpec((1, c, lb), lambda b: (b, 0, 0))
    co = B_Q_HEADS * HEAD_DIM
    return pl.pallas_call(
        functools.partial(_sample_b_kernel, s=s),
        grid=(nb,),
        in_specs=[row(q.shape[1]), row(kn.shape[1]), row(vn.shape[1]), cache, cache],
        out_specs=[row(co), cache, cache],
        out_shape=[jax.ShapeDtypeStruct((q.shape[0], co), F32),
                   jax.ShapeDtypeStruct(ck_t.shape, F32),
                   jax.ShapeDtypeStruct(cv_t.shape, F32)],
        compiler_params=_params("parallel"),
        name="sample_attn_b",
    )(q, kn, vn, ck_t, cv_t)


def _rope_tables(pos):
    inv_freq = ROPE_THETA ** (-jnp.arange(HALF, dtype=F32) / HALF)
    ang = pos.astype(F32)[:, None] * inv_freq[None, :]
    cos, sin = jnp.cos(ang), jnp.sin(ang)
    return jnp.tile(cos, (1, 4)), jnp.tile(jnp.concatenate([-sin, sin], axis=1), (1, 2))


def _cache_t(cache):
    b, l, h, dh = cache.shape
    return jnp.transpose(cache, (0, 2, 3, 1)).reshape(b, h * dh, l)


def _cache_from_t(x, h):
    b, c, l = x.shape
    return jnp.transpose(x.reshape(b, h, c // h, l), (0, 3, 1, 2))[None]


def kernel(x_prompt, x_sample, cache_a_k, cache_a_v, cache_b_k, cache_b_v, state_ffn_conv,
           attn_norm, ffn_norm, a_w_qkv, a_q_norm, a_k_norm, a_sinks, a_w_o,
           b_w_qkv, b_q_norm, b_k_norm, b_w_o,
           ffn_w_gate, ffn_w_up, ffn_conv_w, ffn_conv_b, ffn_w_down):
    nb, seq, _ = x_prompt.shape
    ns, dec, _ = x_sample.shape
    mp, ms = nb * seq, ns * dec

    cos_p, sin_p = _rope_tables(jnp.tile(jnp.arange(seq, dtype=jnp.int32), nb))
    cos_s, sin_s = _rope_tables(jnp.tile(PAST_LEN + jnp.arange(dec, dtype=jnp.int32), ns))
    idx = jnp.arange(V7X_MXU_DIM, dtype=jnp.int32) // HEAD_DIM
    ones_blk = (idx[:, None] == idx[None, :]).astype(BF16)

    yp = x_prompt.reshape(mp, D_MODEL)
    ys = x_sample.reshape(ms, D_MODEL)
    row = lambda a: a.reshape(1, -1)

    def head_gain(gv, n_heads_per_tile):
        return jnp.tile(gv, n_heads_per_tile).reshape(1, -1)

    tn = a_w_qkv.shape[-1] // N_QKV_TILES
    qg, kg = head_gain(a_q_norm[0], tn // HEAD_DIM), head_gain(a_k_norm[0], tn // HEAD_DIM)
    g_attn = row(attn_norm[0])
    grp_a = A_Q_HEADS // A_KV_HEADS

    qs, ks, vs, wqkv = _qkv_call(ys, g_attn, a_w_qkv, cos_s, sin_s, qg, kg, ones_blk,
                                 n_q=A_Q_HEADS, n_kv=A_KV_HEADS, tm=ms, q_dtype=F32, cast_layer=0)
    os_, ako, avo = _sample_a_call(qs, ks, vs, _cache_t(cache_a_k[0]), _cache_t(cache_a_v[0]),
                                   a_sinks[0], s=dec)
    ys, wo = _wo_cast_call(ys, os_, a_w_o, 0)
    a_k_sample = _cache_from_t(ako, A_KV_HEADS)
    a_v_sample = _cache_from_t(avo, A_KV_HEADS)

    q, k, v = _qkv_rows_call(yp, g_attn, wqkv, cos_p, sin_p, qg, kg, ones_blk, n_q=A_Q_HEADS,
                             n_kv=A_KV_HEADS, tm=TM_QKV, q_dtype=BF16, lane_major=False)
    k3, v3 = k.reshape(nb, seq, -1), v.reshape(nb, seq, -1)
    yp = _band_wo_call(q.reshape(nb, seq, -1), k3, v3, a_sinks[0], yp.reshape(nb, seq, -1), wo,
                       hkv=A_KV_HEADS, grp=grp_a).reshape(mp, D_MODEL)
    keep = min(A_WINDOW, seq)
    ca = A_KV_HEADS * HEAD_DIM
    a_k_prompt = _cache_from_t(_keep_t_call(k3, [keep], ca), A_KV_HEADS)
    a_v_prompt = _cache_from_t(_keep_t_call(v3, [keep], ca), A_KV_HEADS)

    conv_p, conv_s = [], []

    def ffn(layer, yp, ys):
        g_ffn = row(ffn_norm[layer])
        cw, cb = ffn_conv_w[layer], row(ffn_conv_b[layer])
        st = state_ffn_conv[layer]
        s0 = jnp.repeat(st[:, 0, :], dec, axis=0)
        s1 = jnp.repeat(st[:, 1, :], dec, axis=0)
        ys, gate_s, wg, wu, wd = _ffn_sample_call(ys, g_ffn, ffn_w_gate, ffn_w_up, ffn_w_down,
                                                  layer, cw, cb, s0, s1, seq=dec)
        conv_s.append(gate_s.reshape(ns, dec, D_FF)[:, dec - (CONV_W - 1):, :])
        yp, tail = _ffn_prompt_call(yp, g_ffn, wg, wu, wd, cw, cb, seq=seq)
        tiles = seq // TM_FFN
        conv_p.append(tail[tiles - 1::tiles, V7X_SUBLANES - (CONV_W - 1):, :])
        return yp, ys

    yp, ys = ffn(0, yp, ys)

    tn = b_w_qkv.shape[-1] // N_QKV_TILES
    qg, kg = head_gain(b_q_norm[0], tn // HEAD_DIM), head_gain(b_k_norm[0], tn // HEAD_DIM)
    g_attn = row(attn_norm[1])
    n_grp = len(B_PATTERNS)
    grp_b = B_Q_HEADS // B_KV_HEADS
    nqb, nkvb = n_grp * B_Q_HEADS, n_grp * B_KV_HEADS
    cb_ = B_KV_HEADS * HEAD_DIM

    qs, ks, vs, wqkv = _qkv_call(ys, g_attn, b_w_qkv, cos_s, sin_s, qg, kg, ones_blk,
                                 n_q=nqb, n_kv=nkvb, tm=ms, q_dtype=F32, cast_layer=0)
    os_, bko, bvo = _sample_b_call(qs, ks, vs, _cache_t(cache_b_k[0]), _cache_t(cache_b_v[0]),
                                   s=dec)
    ys, wo = _wo_cast_call(ys, os_, b_w_o, 0)
    b_k_sample = _cache_from_t(bko, B_KV_HEADS)
    b_v_sample = _cache_from_t(bvo, B_KV_HEADS)

    q, k, v = _qkv_rows_call(yp, g_attn, wqkv, cos_p, sin_p, qg, kg, ones_blk, n_q=nqb,
                             n_kv=nkvb, tm=TM_QKV, q_dtype=F32, lane_major=True)
    outs, lses = [], []
    for g, (w, d) in enumerate(B_PATTERNS):
        assert w // d == BLOCK
        og, lg = _dil_call(q, k, v, nb=nb, hkv=B_KV_HEADS, grp=grp_b, d=d, group=g,
                           residue_major=(g == n_grp - 1))
        outs.append(og)
        lses.append(lg)
    yp = _wo_comb_call(yp, outs, lses, wo, tm=TM_PROJ // 2, d_last=B_PATTERNS[-1][1])
    keeps = [min(w, seq) for w, _ in B_PATTERNS]
    k4 = k.reshape(k.shape[0], nb, seq, V7X_LANES)
    v4 = v.reshape(v.shape[0], nb, seq, V7X_LANES)
    b_k_prompt = _cache_from_t(_keep_t_call(k4, keeps, cb_, lane_major=True), B_KV_HEADS)
    b_v_prompt = _cache_from_t(_keep_t_call(v4, keeps, cb_, lane_major=True), B_KV_HEADS)

    yp, ys = ffn(1, yp, ys)

    return (yp.reshape(nb, seq, D_MODEL), ys.reshape(ns, dec, D_MODEL),
            a_k_prompt, a_v_prompt, a_k_sample, a_v_sample,
            b_k_prompt, b_v_prompt, b_k_sample, b_v_sample,
            jnp.stack(conv_p), jnp.stack(conv_s))
```

```python
import functools

import jax
import jax.numpy as jnp
from jax import lax
from jax.experimental import pallas as pl
from jax.experimental.pallas import tpu as pltpu

F32 = jnp.float32
BF16 = jnp.bfloat16

D_MODEL = 2048
HEAD_DIM = 64
HALF = HEAD_DIM // 2
ROPE_THETA = 10000.0
NORM_EPS = 1e-6
BLOCK = 128
PAST_LEN = 16384
A_WINDOW = 128
A_Q_HEADS = 32
A_KV_HEADS = 8
B_PATTERNS = ((128, 1), (512, 4), (2048, 16))
B_Q_HEADS = 16
B_KV_HEADS = 4
D_FF = 5632
CONV_W = 3
NEG_INF = -1e30
LOG2E = 1.4426950408889634
LN2 = 0.6931471805599453
Q_SCALE = HEAD_DIM ** -0.5 * LOG2E

V7X_MXU_DIM = 256
V7X_LANES = 128
V7X_SUBLANES = 8
VMEM_LIMIT = 56 * 1024 * 1024

TM_PROJ = 512
TM_QKV = 512
TM_FFN = 1024
TF_FFN = 512
TN_WO_CAST = 512
PROJ_CHUNK_ROWS = 256
FFN_ROW_CHUNKS = 1
N_QKV_TILES = 6
DIL_MAX_ROWS = 1024
BAND_SUBBLOCKS = 4
ATTN_DEPTH = 2
DIL_TILES_PER_TRIP = 8


def _params(*sem):
    return pltpu.CompilerParams(dimension_semantics=sem, vmem_limit_bytes=VMEM_LIMIT)


def _rms(x, g):
    ms = jnp.mean(x * x, axis=-1, keepdims=True)
    return x * lax.rsqrt(ms + NORM_EPS) * g


def _head_norm_rope(a, gain, ones_blk, cos, sin):
    tn = a.shape[1]
    x2 = a * a
    hi = x2.astype(BF16)
    lo = (x2 - hi.astype(F32)).astype(BF16)
    parts = []
    for c in range(tn // V7X_MXU_DIM):
        sl = slice(V7X_MXU_DIM * c, V7X_MXU_DIM * (c + 1))
        parts.append(jnp.dot(hi[:, sl], ones_blk, preferred_element_type=F32)
                     + jnp.dot(lo[:, sl], ones_blk, preferred_element_type=F32))
    ss = jnp.concatenate(parts, axis=1)
    y = a * lax.rsqrt(ss * (1.0 / HEAD_DIM) + NORM_EPS) * gain
    lane = lax.broadcasted_iota(jnp.int32, y.shape, 1)
    first_half = (lane & (HEAD_DIM - 1)) < HALF
    partner = jnp.where(first_half, pltpu.roll(y, tn - HALF, 1), pltpu.roll(y, HALF, 1))
    reps = tn // V7X_LANES
    return y * jnp.tile(cos, (1, reps)) + partner * jnp.tile(sin, (1, reps))


def _store_cols(ref, rows, val, lane_major):
    if not lane_major:
        ref[rows, :] = val.astype(ref.dtype)
        return
    for t in range(val.shape[1] // V7X_LANES):
        ref[t, rows, :] = val[:, V7X_LANES * t:V7X_LANES * (t + 1)].astype(ref.dtype)


def _qkv_kernel(x_ref, g_ref, w_ref, cos_ref, sin_ref, qg_ref, kg_ref, ones_ref,
                q_ref, k_ref, v_ref, *rest, n_q_tiles, lane_major, row_chunks, emit_bf16):
    n = pl.program_id(1)
    if emit_bf16:
        w_out_ref, xn_ref = rest
        w_out_ref[...] = w_ref[0].astype(BF16)
        w_ref = w_out_ref
    else:
        (xn_ref,) = rest

    @pl.when(n == 0)
    def _():
        xn_ref[...] = _rms(x_ref[...], g_ref[...]).astype(BF16)

    rc = xn_ref.shape[0] // row_chunks

    def project(out_ref, epilogue):
        chunk = lambda c: slice(rc * c, rc * (c + 1))
        matmul = lambda c: jnp.dot(xn_ref[chunk(c), :], w_ref[...], preferred_element_type=F32)
        acc_next = matmul(0)
        for c in range(row_chunks):
            acc = acc_next
            if c + 1 < row_chunks:
                acc_next = matmul(c + 1)
            _store_cols(out_ref, chunk(c), epilogue(acc, chunk(c)), lane_major)

    def norm_rope(gain_ref, scale):
        def fn(acc, rows):
            r = _head_norm_rope(acc, gain_ref[...], ones_ref[...], cos_ref[rows, :],
                                sin_ref[rows, :])
            return r * scale if scale != 1.0 else r
        return fn

    @pl.when(n < n_q_tiles)
    def _():
        project(q_ref, norm_rope(qg_ref, Q_SCALE))

    @pl.when(n == n_q_tiles)
    def _():
        project(k_ref, norm_rope(kg_ref, 1.0))

    @pl.when(n == n_q_tiles + 1)
    def _():
        project(v_ref, lambda acc, rows: acc)


def _qkv_rows_kernel(x_ref, g_ref, w_ref, cos_ref, sin_ref, qg_ref, kg_ref, ones_ref,
                     q_ref, k_ref, v_ref, xn_ref, *, n_q_tiles, tn, lane_major, row_chunks):
    xn_ref[...] = _rms(x_ref[...], g_ref[...]).astype(BF16)
    rc = xn_ref.shape[0] // row_chunks
    chunk = lambda c: slice(rc * c, rc * (c + 1))
    jobs = [(n, c) for n in range(n_q_tiles + 2) for c in range(row_chunks)]

    def matmul(job):
        n, c = job
        return jnp.dot(xn_ref[chunk(c), :], w_ref[:, tn * n:tn * (n + 1)],
                       preferred_element_type=F32)

    def store(ref, rows, col0, val):
        if not lane_major:
            ref[rows, col0:col0 + tn] = val.astype(ref.dtype)
            return
        for t in range(tn // V7X_LANES):
            ref[col0 // V7X_LANES + t, rows, :] = (
                val[:, V7X_LANES * t:V7X_LANES * (t + 1)].astype(ref.dtype))

    def finish(job, acc):
        n, c = job
        rows = chunk(c)
        if n < n_q_tiles:
            r = _head_norm_rope(acc, qg_ref[...], ones_ref[...], cos_ref[rows, :], sin_ref[rows, :])
            store(q_ref, rows, tn * n, r * Q_SCALE)
        elif n == n_q_tiles:
            store(k_ref, rows, 0, _head_norm_rope(acc, kg_ref[...], ones_ref[...],
                                                  cos_ref[rows, :], sin_ref[rows, :]))
        else:
            store(v_ref, rows, 0, acc)

    acc_next = matmul(jobs[0])
    for i, job in enumerate(jobs):
        acc = acc_next
        if i + 1 < len(jobs):
            acc_next = matmul(jobs[i + 1])
        finish(job, acc)


def _qkv_rows_call(x, g, w, cos, sin, qg, kg, ones_blk, *, n_q, n_kv, tm, q_dtype, lane_major):
    m = x.shape[0]
    ncols = w.shape[1]
    tn = ncols // N_QKV_TILES
    n_q_tiles = (n_q * HEAD_DIM) // tn
    assert n_q_tiles * tn == n_q * HEAD_DIM and n_kv * HEAD_DIM == tn and tn % V7X_MXU_DIM == 0
    assert m % tm == 0 and n_q_tiles + 2 == N_QKV_TILES
    if lane_major:
        spec = lambda cols: pl.BlockSpec((cols // V7X_LANES, tm, V7X_LANES), lambda i: (0, i, 0))
        shape = lambda cols, dt: jax.ShapeDtypeStruct((cols // V7X_LANES, m, V7X_LANES), dt)
    else:
        spec = lambda cols: pl.BlockSpec((tm, cols), lambda i: (i, 0))
        shape = lambda cols, dt: jax.ShapeDtypeStruct((m, cols), dt)
    const = lambda shp: pl.BlockSpec(shp, lambda i: (0, 0))
    return pl.pallas_call(
        functools.partial(_qkv_rows_kernel, n_q_tiles=n_q_tiles, tn=tn, lane_major=lane_major,
                          row_chunks=max(1, tm // PROJ_CHUNK_ROWS)),
        grid=(m // tm,),
        in_specs=[
            pl.BlockSpec((tm, D_MODEL), lambda i: (i, 0)),
            const((1, D_MODEL)),
            pl.BlockSpec((D_MODEL, ncols), lambda i: (0, 0), pipeline_mode=pl.Buffered(1)),
            pl.BlockSpec((tm, V7X_LANES), lambda i: (i, 0)),
            pl.BlockSpec((tm, V7X_LANES), lambda i: (i, 0)),
            const((1, tn)),
            const((1, tn)),
            const((V7X_MXU_DIM, V7X_MXU_DIM)),
        ],
        out_specs=[spec(n_q * HEAD_DIM), spec(tn), spec(tn)],
        out_shape=[shape(n_q * HEAD_DIM, q_dtype), shape(tn, F32), shape(tn, F32)],
        scratch_shapes=[pltpu.VMEM((tm, D_MODEL), BF16)],
        compiler_params=_params("parallel"),
        name="qkv_proj_rows",
    )(x, g, w, cos, sin, qg, kg, ones_blk)


def _qkv_call(x, g, w, cos, sin, qg, kg, ones_blk, *, n_q, n_kv, tm, q_dtype, lane_major=False,
              cast_layer=None):
    m = x.shape[0]
    ncols = w.shape[-1]
    tn = ncols // N_QKV_TILES
    n_q_tiles = (n_q * HEAD_DIM) // tn
    assert n_q_tiles * tn == n_q * HEAD_DIM and n_kv * HEAD_DIM == tn and tn % V7X_MXU_DIM == 0
    assert m % tm == 0
    last_q = n_q_tiles - 1
    if lane_major:
        lt = tn // V7X_LANES
        out_specs = [
            pl.BlockSpec((lt, tm, V7X_LANES), lambda i, n: (jnp.minimum(n, last_q), i, 0)),
            pl.BlockSpec((lt, tm, V7X_LANES), lambda i, n: (0, i, 0)),
            pl.BlockSpec((lt, tm, V7X_LANES), lambda i, n: (0, i, 0)),
        ]
        out_shape = [
            jax.ShapeDtypeStruct((n_q_tiles * lt, m, V7X_LANES), q_dtype),
            jax.ShapeDtypeStruct((lt, m, V7X_LANES), F32),
            jax.ShapeDtypeStruct((lt, m, V7X_LANES), F32),
        ]
    else:
        out_specs = [
            pl.BlockSpec((tm, tn), lambda i, n: (i, jnp.minimum(n, last_q))),
            pl.BlockSpec((tm, tn), lambda i, n: (i, 0)),
            pl.BlockSpec((tm, tn), lambda i, n: (i, 0)),
        ]
        out_shape = [
            jax.ShapeDtypeStruct((m, n_q * HEAD_DIM), q_dtype),
            jax.ShapeDtypeStruct((m, tn), F32),
            jax.ShapeDtypeStruct((m, tn), F32),
        ]
    emit_bf16 = cast_layer is not None
    if emit_bf16:
        assert m == tm
        w_spec = pl.BlockSpec((1, D_MODEL, tn), lambda i, n: (cast_layer, 0, n))
        out_specs = out_specs + [pl.BlockSpec((D_MODEL, tn), lambda i, n: (0, n))]
        out_shape = out_shape + [jax.ShapeDtypeStruct((D_MODEL, ncols), BF16)]
    else:
        w_spec = pl.BlockSpec((D_MODEL, tn), lambda i, n: (0, n))
    return pl.pallas_call(
        functools.partial(_qkv_kernel, n_q_tiles=n_q_tiles, lane_major=lane_major,
                          row_chunks=max(1, tm // PROJ_CHUNK_ROWS), emit_bf16=emit_bf16),
        grid=(m // tm, N_QKV_TILES),
        in_specs=[
            pl.BlockSpec((tm, D_MODEL), lambda i, n: (i, 0)),
            pl.BlockSpec((1, D_MODEL), lambda i, n: (0, 0)),
            w_spec,
            pl.BlockSpec((tm, V7X_LANES), lambda i, n: (i, 0)),
            pl.BlockSpec((tm, V7X_LANES), lambda i, n: (i, 0)),
            pl.BlockSpec((1, tn), lambda i, n: (0, 0)),
            pl.BlockSpec((1, tn), lambda i, n: (0, 0)),
            pl.BlockSpec((V7X_MXU_DIM, V7X_MXU_DIM), lambda i, n: (0, 0)),
        ],
        out_specs=out_specs,
        out_shape=out_shape,
        scratch_shapes=[pltpu.VMEM((tm, D_MODEL), BF16)],
        compiler_params=_params("parallel", "arbitrary"),
        name="qkv_proj",
    )(x, g, w, cos, sin, qg, kg, ones_blk)


GRP = 4


def _band_mask_t(first):
    shape = (2 * BLOCK, BLOCK)
    kj = lax.broadcasted_iota(jnp.int32, shape, 0)
    qi = lax.broadcasted_iota(jnp.int32, shape, 1)
    seen = (kj >= qi) & (kj <= qi + BLOCK) & ((kj >= BLOCK) | jnp.logical_not(first))
    return jnp.where(seen, 0.0, NEG_INF)


def _kv_tile_forms(kt, vt):
    lo = lax.broadcasted_iota(jnp.int32, kt.shape, 1) < HEAD_DIM
    kt_sw = pltpu.roll(kt, HEAD_DIM, 1)
    k_forms = [(jnp.where(lo, kt, 0.0).astype(BF16), jnp.where(lo, 0.0, kt_sw).astype(BF16)),
               (jnp.where(lo, kt_sw, 0.0).astype(BF16), jnp.where(lo, 0.0, kt).astype(BF16))]
    return k_forms, vt.T.astype(BF16)


def _head_scores(k_lo, k_hi, q_a, q_b):
    nt = (((1,), (1,)), ((), ()))
    qpair = jnp.concatenate([q_a, q_b], axis=0)
    return jnp.concatenate([lax.dot_general(k_lo, qpair, nt, preferred_element_type=F32),
                            lax.dot_general(k_hi, qpair, nt, preferred_element_type=F32)], axis=1)


def _head_softmax_pv(s, mask4, v_t, sinks, want_lse):
    s = s + mask4
    m = jnp.max(s, axis=0, keepdims=True)
    if sinks is not None:
        sk = jnp.concatenate([jnp.full((1, BLOCK), sinks[j] * LOG2E, F32)
                              for j in (0, 2, 1, 3)], axis=1)
        m = jnp.maximum(m, sk)
    p = jnp.exp2(s - m)
    l = jnp.sum(p, axis=0, keepdims=True)
    if sinks is not None:
        l = l + jnp.exp2(sk - m)
    o_t = jnp.dot(v_t, p.astype(BF16), preferred_element_type=F32) / l
    lse_t = jnp.broadcast_to((m + jnp.log2(l)) * LN2, o_t.shape) if want_lse else None
    o_tiles, l_tiles = [], []
    for u in range(2):
        c0, c1 = slice(BLOCK * u, BLOCK * (u + 1)), slice(BLOCK * (2 + u), BLOCK * (3 + u))
        o_tiles.append(jnp.concatenate([o_t[:, c0], o_t[:, c1]], axis=0).T)
        if want_lse:
            l_tiles.append(jnp.concatenate([lse_t[:, c0], lse_t[:, c1]], axis=0).T)
    return o_tiles, l_tiles


def _attend_tiles(tiles, want_lse, fillers=None):
    jobs = [(ti, e) for ti in range(len(tiles)) for e in range(2)]
    forms = {}

    def scores(job):
        ti, e = job
        load_kv, load_q = tiles[ti][:2]
        if ti not in forms:
            forms[ti] = _kv_tile_forms(*load_kv())
        k_lo, k_hi = forms[ti][0][e]
        return _head_scores(k_lo, k_hi, load_q(2 * e), load_q(2 * e + 1))

    pending = [scores(job) for job in jobs[:ATTN_DEPTH]]
    for idx, (ti, e) in enumerate(jobs):
        s = pending.pop(0)
        if idx + ATTN_DEPTH < len(jobs):
            pending.append(scores(jobs[idx + ATTN_DEPTH]))
        sink_of, store, mask4 = tiles[ti][2:]
        v_t = forms[ti][1][HEAD_DIM * e:HEAD_DIM * (e + 1)]
        sinks = None if sink_of is None else [sink_of(GRP * e + j) for j in range(GRP)]
        o_tiles, l_tiles = _head_softmax_pv(s, mask4, v_t, sinks, want_lse)
        for u in range(2):
            store(2 * e + u, o_tiles[u], l_tiles[u] if want_lse else None)
        if fillers and idx in fillers:
            fillers[idx]()


def _band_wo_kernel(q_ref, kc_ref, kp_ref, vc_ref, vp_ref, sink_ref, y_ref, wo_ref, out_ref,
                    o_scr, *, kv_tiles, nsub):
    mask_first = jnp.tile(_band_mask_t(pl.program_id(1) == 0), (1, GRP))
    mask_rest = jnp.tile(_band_mask_t(False), (1, GRP))
    lanes = lambda t: slice(V7X_LANES * t, V7X_LANES * (t + 1))
    rows = lambda j: slice(BLOCK * j, BLOCK * (j + 1))

    def tile(j, t):
        def load_kv():
            if j == 0:
                kp, vp = kp_ref[0, :, lanes(t)], vp_ref[0, :, lanes(t)]
            else:
                kp, vp = kc_ref[0, rows(j - 1), lanes(t)], vc_ref[0, rows(j - 1), lanes(t)]
            return (jnp.concatenate([kp, kc_ref[0, rows(j), lanes(t)]], axis=0),
                    jnp.concatenate([vp, vc_ref[0, rows(j), lanes(t)]], axis=0))

        def store(u, o_tile, _):
            o_scr[rows(j), lanes(GRP * t + u)] = o_tile.astype(o_scr.dtype)

        return (load_kv, lambda u: q_ref[0, rows(j), lanes(GRP * t + u)],
                lambda h: sink_ref[2 * GRP * t + h], store, mask_first if j == 0 else mask_rest)

    jobs_per_block = 2 * kv_tiles
    n_out = out_ref.shape[2]
    piece = n_out // jobs_per_block

    def project(j, c):
        cols = slice(piece * c, piece * (c + 1))
        out_ref[0, rows(j), cols] = y_ref[0, rows(j), cols] + jnp.dot(
            o_scr[rows(j), :], wo_ref[:, cols], preferred_element_type=F32)

    fillers = {jobs_per_block * j + c: functools.partial(project, j - 1, c)
               for j in range(1, nsub) for c in range(jobs_per_block)}
    _attend_tiles([tile(j, t) for j in range(nsub) for t in range(kv_tiles)], False, fillers)
    for c in range(jobs_per_block):
        project(nsub - 1, c)


def _band_wo_call(q, k, v, sinks, y, wo, *, hkv, grp):
    b, seq, cq = q.shape
    ck = hkv * HEAD_DIM
    nsub = BAND_SUBBLOCKS
    rows = BLOCK * nsub
    assert grp == GRP and hkv % 2 == 0 and seq % rows == 0
    cur = lambda bi, c: (bi, c, 0)
    prev = lambda bi, c: (bi, jnp.maximum(c * nsub - 1, 0), 0)
    return pl.pallas_call(
        functools.partial(_band_wo_kernel, kv_tiles=hkv // 2, nsub=nsub),
        grid=(b, seq // rows),
        in_specs=[
            pl.BlockSpec((1, rows, cq), cur),
            pl.BlockSpec((1, rows, ck), cur),
            pl.BlockSpec((1, BLOCK, ck), prev),
            pl.BlockSpec((1, rows, ck), cur),
            pl.BlockSpec((1, BLOCK, ck), prev),
            pl.BlockSpec(memory_space=pltpu.SMEM),
            pl.BlockSpec((1, rows, D_MODEL), cur),
            pl.BlockSpec((cq, D_MODEL), lambda bi, c: (0, 0)),
        ],
        out_specs=pl.BlockSpec((1, rows, D_MODEL), cur),
        out_shape=jax.ShapeDtypeStruct((b, seq, D_MODEL), F32),
        scratch_shapes=[pltpu.VMEM((rows, cq), BF16)],
        compiler_params=_params("parallel", "arbitrary"),
        name="band_attn_wo",
    )(q, k, k, v, v, sinks, y, wo)


def _dil_kernel(q_ref, kc_ref, kp_ref, vc_ref, vp_ref, o_ref, lse_ref, *, kv_tiles, grp, d, nsub,
                residue_major):
    mask_first = jnp.tile(_band_mask_t(pl.program_id(1) == 0), (1, GRP))

    if d == 1:
        mask_rest = jnp.tile(_band_mask_t(False), (1, GRP))
        rows = lambda j: slice(BLOCK * j, BLOCK * (j + 1))

        def sub_tile(j, t):
            def load_kv():
                if j == 0:
                    kp, vp = kp_ref[t], vp_ref[t]
                else:
                    kp, vp = kc_ref[t, rows(j - 1), :], vc_ref[t, rows(j - 1), :]
                return (jnp.concatenate([kp, kc_ref[t, rows(j), :]], axis=0),
                        jnp.concatenate([vp, vc_ref[t, rows(j), :]], axis=0))

            def store(u, o_tile, l_tile):
                o_ref[grp * t + u, rows(j), :] = o_tile
                lse_ref[grp * t + u, rows(j), :] = l_tile

            return (load_kv, lambda u: q_ref[grp * t + u, rows(j), :].astype(BF16), None, store,
                    mask_first if j == 0 else mask_rest)

        _attend_tiles([sub_tile(j, t) for j in range(nsub) for t in range(kv_tiles)], True)
        return

    def tile(r, t):
        rs = pl.ds(r, BLOCK, stride=d)

        def load_kv():
            return (jnp.concatenate([kp_ref[t, rs, :], kc_ref[t, rs, :]], axis=0),
                    jnp.concatenate([vp_ref[t, rs, :], vc_ref[t, rs, :]], axis=0))

        ws = pl.ds(pl.multiple_of(r * BLOCK, BLOCK), BLOCK) if residue_major else rs

        def store(u, o_tile, l_tile):
            o_ref[grp * t + u, ws, :] = o_tile
            lse_ref[grp * t + u, ws, :] = l_tile

        return (load_kv, lambda u: q_ref[grp * t + u, rs, :].astype(BF16), None, store,
                mask_first)

    unroll = min(d, max(1, DIL_TILES_PER_TRIP // kv_tiles))

    def body(i, carry):
        _attend_tiles([tile(i * unroll + rr, t) for rr in range(unroll) for t in range(kv_tiles)],
                      True)
        return carry

    if d > unroll:
        lax.fori_loop(0, d // unroll, body, 0)
    else:
        body(0, 0)


def _dil_call(q, k, v, *, nb, hkv, grp, d, group, residue_major=False):
    m = q.shape[1]
    seq = m // nb
    nsub = BAND_SUBBLOCKS if d == 1 else 1
    rows = BLOCK * d * nsub
    hsplit = max(1, rows // DIL_MAX_ROWS)
    kv_tiles = hkv // 2 // hsplit
    q_tiles = kv_tiles * grp
    nchunk = seq // rows
    assert seq % rows == 0 and kv_tiles * 2 * hsplit == hkv and grp % 2 == 0
    cur = lambda bi, c, hp: (group * hsplit + hp, bi * nchunk + c, 0)
    if d == 1:
        prev_rows = BLOCK
        prev = lambda bi, c, hp: (group * hsplit + hp,
                                  jnp.maximum((bi * nchunk + c) * nsub - 1, 0), 0)
    else:
        prev_rows = rows
        prev = lambda bi, c, hp: (group * hsplit + hp, bi * nchunk + jnp.maximum(c - 1, 0), 0)
    o_spec = pl.BlockSpec((q_tiles, rows, V7X_LANES), lambda bi, c, hp: (hp, bi * nchunk + c, 0))
    o_shape = jax.ShapeDtypeStruct((q_tiles * hsplit, m, V7X_LANES), F32)
    return pl.pallas_call(
        functools.partial(_dil_kernel, kv_tiles=kv_tiles, grp=grp, d=d, nsub=nsub,
                          residue_major=residue_major and d > 1),
        grid=(nb, nchunk, hsplit),
        in_specs=[
            pl.BlockSpec((q_tiles, rows, V7X_LANES), cur),
            pl.BlockSpec((kv_tiles, rows, V7X_LANES), cur),
            pl.BlockSpec((kv_tiles, prev_rows, V7X_LANES), prev),
            pl.BlockSpec((kv_tiles, rows, V7X_LANES), cur),
            pl.BlockSpec((kv_tiles, prev_rows, V7X_LANES), prev),
        ],
        out_specs=[o_spec, o_spec],
        out_shape=[o_shape, o_shape],
        compiler_params=_params("parallel", "arbitrary", "arbitrary"),
        name="dilated_attn",
    )(q, k, k, v, v)


def _keep_t_kernel(x_ref, o_ref):
    o_ref[0] = x_ref[0].T


def _keep_t_lane_major_kernel(x_ref, o_ref, *, keeps):
    g = pl.program_id(2)
    seq = x_ref.shape[2]
    off = 0
    for k, keep in enumerate(keeps):
        @pl.when(g == k)
        def _(keep=keep, off=off):
            o_ref[0, :, off:off + keep] = x_ref[0, 0, seq - keep:, :].T
        off += keep


def _keep_t_call(x, keeps, c, *, lane_major=False):
    b, seq = x.shape[1:3] if lane_major else x.shape[:2]
    starts, first_rb = [], []
    n = 0
    for keep in keeps:
        assert keep % BLOCK == 0 and seq % BLOCK == 0
        starts.append(n)
        first_rb.append((seq - keep) // BLOCK)
        n += keep // BLOCK

    def src_block(j):
        rb = jnp.int32(0)
        cb = jnp.int32(0)
        for g in range(len(keeps)):
            inside = j >= starts[g]
            rb = jnp.where(inside, first_rb[g] + j - starts[g], rb)
            cb = jnp.where(inside, g, cb)
        return rb, cb

    if lane_major:
        ct = c // V7X_LANES
        grid = (b, ct, len(keeps))
        in_spec = pl.BlockSpec((1, 1, seq, V7X_LANES), lambda bi, t, g: (g * ct + t, bi, 0, 0))
        out_spec = pl.BlockSpec((1, V7X_LANES, n * BLOCK), lambda bi, t, g: (bi, t, 0))
        kern = functools.partial(_keep_t_lane_major_kernel, keeps=tuple(keeps))
        sem = ("parallel", "parallel", "arbitrary")
    else:
        grid = (b, n)

        def src(bi, j):
            rb, cb = src_block(j)
            return bi, rb, cb

        in_spec = pl.BlockSpec((1, BLOCK, c), src)
        out_spec = pl.BlockSpec((1, c, BLOCK), lambda bi, j: (bi, 0, j))
        kern = _keep_t_kernel
        sem = ("parallel", "parallel")
    return pl.pallas_call(
        kern,
        grid=grid,
        in_specs=[in_spec],
        out_specs=out_spec,
        out_shape=jax.ShapeDtypeStruct((b, c, n * BLOCK), F32),
        compiler_params=_params(*sem),
        name="keep_rows_t",
    )(x)


def _wo_comb_kernel(y_ref, o0_ref, o1_ref, o2_ref, l0_ref, l1_ref, l2_ref, w_ref, out_ref):
    def natural(ref, t):
        return jnp.concatenate([ref[t, :, n, :] for n in range(ref.shape[2])], axis=0)

    def combined(t):
        l0, l1, l2 = l0_ref[t], l1_ref[t], natural(l2_ref, t)
        mx = jnp.maximum(jnp.maximum(l0, l1), l2)
        e0, e1, e2 = jnp.exp(l0 - mx), jnp.exp(l1 - mx), jnp.exp(l2 - mx)
        den = e0 + e1 + e2
        comb = ((e0 / den) * o0_ref[t] + (e1 / den) * o1_ref[t]
                + (e2 / den) * natural(o2_ref, t))
        return comb.astype(BF16)

    kp = 2 * V7X_LANES
    for p in range(o0_ref.shape[0] // 2):
        comb = jnp.concatenate([combined(2 * p), combined(2 * p + 1)], axis=1)
        part = jnp.dot(comb, w_ref[kp * p:kp * (p + 1), :], preferred_element_type=F32)
        if p == 0:
            out_ref[...] = y_ref[...] + part
        else:
            out_ref[...] += part


def _wo_cast_kernel(y_ref, o_ref, w_ref, out_ref, w_out_ref):
    w_out_ref[...] = w_ref[0].astype(BF16)
    out_ref[...] = y_ref[...] + jnp.dot(o_ref[...].astype(BF16), w_out_ref[...],
                                        preferred_element_type=F32)


def _wo_cast_call(y, o, w, layer):
    m = y.shape[0]
    c = o.shape[1]
    tn = TN_WO_CAST
    return pl.pallas_call(
        _wo_cast_kernel,
        grid=(D_MODEL // tn,),
        in_specs=[
            pl.BlockSpec((m, tn), lambda n: (0, n)),
            pl.BlockSpec((m, c), lambda n: (0, 0)),
            pl.BlockSpec((1, c, tn), lambda n: (layer, 0, n)),
        ],
        out_specs=[pl.BlockSpec((m, tn), lambda n: (0, n)),
                   pl.BlockSpec((c, tn), lambda n: (0, n))],
        out_shape=[jax.ShapeDtypeStruct((m, D_MODEL), F32),
                   jax.ShapeDtypeStruct((c, D_MODEL), BF16)],
        compiler_params=_params("parallel"),
        name="wo_proj_cast",
    )(y, o, w)


def _wo_comb_call(y, os_, ls_, w, *, tm, d_last):
    m = y.shape[0]
    c = w.shape[0]
    nt = c // V7X_LANES
    per_blk = BLOCK * d_last // tm
    assert (BLOCK * d_last) % tm == 0 and tm % d_last == 0 and (tm // d_last) % V7X_SUBLANES == 0
    blk = pl.BlockSpec((nt, tm, V7X_LANES), lambda i: (0, i, 0))
    blk_rm = pl.BlockSpec((nt, d_last, tm // d_last, V7X_LANES),
                          lambda i: (0, i // per_blk, i % per_blk, 0))
    rm = lambda a: a.reshape(nt, m // BLOCK, BLOCK, V7X_LANES)
    return pl.pallas_call(
        _wo_comb_kernel,
        grid=(m // tm,),
        in_specs=[pl.BlockSpec((tm, D_MODEL), lambda i: (i, 0)), blk, blk, blk_rm, blk, blk, blk_rm,
                  pl.BlockSpec((c, D_MODEL), lambda i: (0, 0))],
        out_specs=pl.BlockSpec((tm, D_MODEL), lambda i: (i, 0)),
        out_shape=jax.ShapeDtypeStruct((m, D_MODEL), F32),
        compiler_params=_params("parallel"),
        name="wo_comb_proj",
    )(y, os_[0], os_[1], rm(os_[2]), ls_[0], ls_[1], rm(ls_[2]), w)


def _ffn_tail(gate, g1, g2, up, cw_ref, cb_ref, wd_ref):
    conv = cb_ref[...] + cw_ref[0:1, :] * g2 + cw_ref[1:2, :] * g1 + cw_ref[2:3, :] * gate
    h = conv * jax.nn.sigmoid(conv) * up
    return jnp.dot(h.astype(BF16), wd_ref[...], preferred_element_type=F32)


def _ffn_prompt_kernel(y_ref, g_ref, wg_ref, wu_ref, wd_ref, cw_ref, cb_ref,
                       out_ref, tail_ref, xn_ref, carry_ref, *, tiles_per_seq):
    m = pl.program_id(0)
    f = pl.program_id(1)

    @pl.when(f == 0)
    def _():
        x = y_ref[...]
        xn_ref[...] = _rms(x, g_ref[...]).astype(BF16)
        out_ref[...] = x

    @pl.when(m % tiles_per_seq == 0)
    def _():
        carry_ref[f] = jnp.zeros(carry_ref.shape[1:], F32)

    rc = xn_ref.shape[0] // FFN_ROW_CHUNKS
    last = carry_ref[f]
    for c in range(FFN_ROW_CHUNKS):
        rows = slice(rc * c, rc * (c + 1))
        xn = xn_ref[rows, :]
        gate = jnp.dot(xn, wg_ref[...], preferred_element_type=F32)
        up = jnp.dot(xn, wu_ref[...], preferred_element_type=F32)
        row = lax.broadcasted_iota(jnp.int32, gate.shape, 0)
        g1 = jnp.where(row == 0, last[7:8, :], pltpu.roll(gate, 1, 0))
        g2 = jnp.where(row == 0, last[6:7, :],
                       jnp.where(row == 1, last[7:8, :], pltpu.roll(gate, 2, 0)))
        out_ref[rows, :] += _ffn_tail(gate, g1, g2, up, cw_ref, cb_ref, wd_ref)
        last = gate[rc - V7X_SUBLANES:, :]
    carry_ref[f] = last
    tail_ref[0] = last


def _ffn_prompt_call(y, g, wg, wu, wd, cw, cb, *, seq):
    m = y.shape[0]
    tm, tf = TM_FFN, TF_FFN
    nf = D_FF // tf
    assert m % tm == 0 and seq % tm == 0 and D_FF % tf == 0
    return pl.pallas_call(
        functools.partial(_ffn_prompt_kernel, tiles_per_seq=seq // tm),
        grid=(m // tm, nf),
        in_specs=[
            pl.BlockSpec((tm, D_MODEL), lambda i, f: (i, 0)),
            pl.BlockSpec((1, D_MODEL), lambda i, f: (0, 0)),
            pl.BlockSpec((D_MODEL, tf), lambda i, f: (0, f)),
            pl.BlockSpec((D_MODEL, tf), lambda i, f: (0, f)),
            pl.BlockSpec((tf, D_MODEL), lambda i, f: (f, 0)),
            pl.BlockSpec((CONV_W, tf), lambda i, f: (0, f)),
            pl.BlockSpec((1, tf), lambda i, f: (0, f)),
        ],
        out_specs=[
            pl.BlockSpec((tm, D_MODEL), lambda i, f: (i, 0)),
            pl.BlockSpec((1, V7X_SUBLANES, tf), lambda i, f: (i, 0, f)),
        ],
        out_shape=[
            jax.ShapeDtypeStruct((m, D_MODEL), F32),
            jax.ShapeDtypeStruct((m // tm, V7X_SUBLANES, D_FF), F32),
        ],
        scratch_shapes=[pltpu.VMEM((tm, D_MODEL), BF16),
                        pltpu.VMEM((nf, V7X_SUBLANES, tf), F32)],
        compiler_params=_params("arbitrary", "arbitrary"),
        name="conv_ffn_prompt",
    )(y, g, wg, wu, wd, cw, cb)


def _ffn_sample_kernel(y_ref, g_ref, wg_ref, wu_ref, wd_ref, cw_ref, cb_ref, s0_ref, s1_ref,
                       out_ref, gate_ref, wg_out_ref, wu_out_ref, wd_out_ref, xn_ref, *, seq):
    f = pl.program_id(0)

    @pl.when(f == 0)
    def _():
        x = y_ref[...]
        xn_ref[...] = _rms(x, g_ref[...]).astype(BF16)
        out_ref[...] = x

    wg_out_ref[...] = wg_ref[0].astype(BF16)
    wu_out_ref[...] = wu_ref[0].astype(BF16)
    wd_out_ref[...] = wd_ref[0].astype(BF16)
    xn = xn_ref[...]
    gate = jnp.dot(xn, wg_out_ref[...], preferred_element_type=F32)
    up = jnp.dot(xn, wu_out_ref[...], preferred_element_type=F32)
    t = lax.broadcasted_iota(jnp.int32, gate.shape, 0) & (seq - 1)
    s0, s1 = s0_ref[...], s1_ref[...]
    g1 = jnp.where(t == 0, s1, pltpu.roll(gate, 1, 0))
    g2 = jnp.where(t == 0, s0, jnp.where(t == 1, s1, pltpu.roll(gate, 2, 0)))
    out_ref[...] += _ffn_tail(gate, g1, g2, up, cw_ref, cb_ref, wd_out_ref)
    gate_ref[...] = gate


def _ffn_sample_call(y, g, wg, wu, wd, layer, cw, cb, s0, s1, *, seq):
    m = y.shape[0]
    tf = TF_FFN
    nf = D_FF // tf
    full = pl.BlockSpec((m, D_MODEL), lambda f: (0, 0))
    col = pl.BlockSpec((m, tf), lambda f: (0, f))
    return pl.pallas_call(
        functools.partial(_ffn_sample_kernel, seq=seq),
        grid=(nf,),
        in_specs=[
            full,
            pl.BlockSpec((1, D_MODEL), lambda f: (0, 0)),
            pl.BlockSpec((1, D_MODEL, tf), lambda f: (layer, 0, f)),
            pl.BlockSpec((1, D_MODEL, tf), lambda f: (layer, 0, f)),
            pl.BlockSpec((1, tf, D_MODEL), lambda f: (layer, f, 0)),
            pl.BlockSpec((CONV_W, tf), lambda f: (0, f)),
            pl.BlockSpec((1, tf), lambda f: (0, f)),
            col, col,
        ],
        out_specs=[full, col,
                   pl.BlockSpec((D_MODEL, tf), lambda f: (0, f)),
                   pl.BlockSpec((D_MODEL, tf), lambda f: (0, f)),
                   pl.BlockSpec((tf, D_MODEL), lambda f: (f, 0))],
        out_shape=[jax.ShapeDtypeStruct((m, D_MODEL), F32),
                   jax.ShapeDtypeStruct((m, D_FF), F32),
                   jax.ShapeDtypeStruct((D_MODEL, D_FF), BF16),
                   jax.ShapeDtypeStruct((D_MODEL, D_FF), BF16),
                   jax.ShapeDtypeStruct((D_FF, D_MODEL), BF16)],
        scratch_shapes=[pltpu.VMEM((m, D_MODEL), BF16)],
        compiler_params=_params("arbitrary"),
        name="conv_ffn_sample",
    )(y, g, wg, wu, wd, cw, cb, s0, s1)


def _block_diag_q(q_ref, col0, hkv, grp, s):
    blocks = []
    for h in range(hkv):
        qh = jnp.concatenate(
            [q_ref[:, col0 + HEAD_DIM * (grp * h + j):col0 + HEAD_DIM * (grp * h + j + 1)]
             for j in range(grp)], axis=0)
        pieces = []
        if h > 0:
            pieces.append(jnp.zeros((grp * s, HEAD_DIM * h), F32))
        pieces.append(qh)
        if h < hkv - 1:
            pieces.append(jnp.zeros((grp * s, HEAD_DIM * (hkv - 1 - h)), F32))
        blocks.append(jnp.concatenate(pieces, axis=1) if len(pieces) > 1 else qh)
    return jnp.concatenate(blocks, axis=0).astype(BF16)


def _new_rows_t(x, s):
    pad = jnp.zeros((V7X_LANES - s, x.shape[1]), F32)
    return jnp.concatenate([pad, x], axis=0).T


def _cached_attend(qbd, ck_t, kn_t, cv_t, vn_t, d, s, sink_col):
    return _cached_finish(*_cached_scores(qbd, ck_t, kn_t), cv_t, vn_t, d, s, sink_col)


def _cached_scores(qbd, ck_t, kn_t):
    return (jnp.dot(qbd, ck_t.astype(BF16), preferred_element_type=F32),
            jnp.dot(qbd, kn_t.astype(BF16), preferred_element_type=F32))


def _cached_finish(sc, sn, cv_t, vn_t, d, s, sink_col):
    r_, lc = sc.shape
    ic = lax.broadcasted_iota(jnp.int32, (r_, lc), 0) & (s - 1)
    c = lax.broadcasted_iota(jnp.int32, (r_, lc), 1)
    i_n = lax.broadcasted_iota(jnp.int32, (r_, V7X_LANES), 0) & (s - 1)
    j = lax.broadcasted_iota(jnp.int32, (r_, V7X_LANES), 1) - (V7X_LANES - s)
    valid_c = c >= ic
    valid_n = (j >= 0) & (j <= i_n)
    if d > 1:
        valid_c = valid_c & ((c & (d - 1)) == (ic & (d - 1)))
        valid_n = valid_n & ((j & (d - 1)) == (i_n & (d - 1)))
    sc = jnp.where(valid_c, sc, NEG_INF)
    sn = jnp.where(valid_n, sn, NEG_INF)
    m = jnp.maximum(jnp.max(sc, axis=1, keepdims=True), jnp.max(sn, axis=1, keepdims=True))
    if sink_col is not None:
        m = jnp.maximum(m, sink_col)
    pc = jnp.exp2(sc - m)
    pn = jnp.exp2(sn - m)
    l = jnp.sum(pc, axis=1, keepdims=True) + jnp.sum(pn, axis=1, keepdims=True)
    if sink_col is not None:
        l = l + jnp.exp2(sink_col - m)
    nt = (((1,), (1,)), ((), ()))
    o = (lax.dot_general(pc.astype(BF16), cv_t.astype(BF16), nt, preferred_element_type=F32)
         + lax.dot_general(pn.astype(BF16), vn_t.astype(BF16), nt, preferred_element_type=F32)) / l
    return o, (m + jnp.log2(l)) * LN2


def _diag_heads(o, hkv, grp, s):
    pieces = []
    for h in range(hkv):
        for j in range(grp):
            r0 = (h * grp + j) * s
            pieces.append(o[r0:r0 + s, HEAD_DIM * h:HEAD_DIM * (h + 1)])
    return jnp.concatenate(pieces, axis=1)


def _store_shifted(out_ref, off, c_t, n_t, s):
    lc = c_t.shape[1]
    rolled = pltpu.roll(c_t, lc - s, 1)
    lane = lax.broadcasted_iota(jnp.int32, n_t.shape, 1)
    if lc > V7X_LANES:
        out_ref[0, :, off:off + lc - V7X_LANES] = rolled[:, :lc - V7X_LANES]
    out_ref[0, :, off + lc - V7X_LANES:off + lc] = jnp.where(
        lane < V7X_LANES - s, rolled[:, lc - V7X_LANES:], n_t)


def _sample_a_kernel(q_ref, kn_ref, vn_ref, ck_ref, cv_ref, sink_ref, o_ref, ko_ref, vo_ref, *, s):
    hkv, grp = A_KV_HEADS, A_Q_HEADS // A_KV_HEADS
    kn_t, vn_t = _new_rows_t(kn_ref[...], s), _new_rows_t(vn_ref[...], s)
    ck_t, cv_t = ck_ref[0], cv_ref[0]
    _store_shifted(ko_ref, 0, ck_t, kn_t, s)
    _store_shifted(vo_ref, 0, cv_t, vn_t, s)
    qbd = _block_diag_q(q_ref, 0, hkv, grp, s)
    sink_col = jnp.concatenate(
        [jnp.full((s, 1), sink_ref[hq] * LOG2E, F32) for hq in range(hkv * grp)], axis=0)
    o, _ = _cached_attend(qbd, ck_t, kn_t, cv_t, vn_t, 1, s, sink_col)
    o_ref[...] = _diag_heads(o, hkv, grp, s)


def _sample_a_call(q, kn, vn, ck_t, cv_t, sinks, *, s):
    nb, c, lc = ck_t.shape
    row = lambda w: pl.BlockSpec((s, w), lambda b: (b, 0))
    cache = pl.BlockSpec((1, c, lc), lambda b: (b, 0, 0))
    return pl.pallas_call(
        functools.partial(_sample_a_kernel, s=s),
        grid=(nb,),
        in_specs=[row(q.shape[1]), row(c), row(c), cache, cache,
                  pl.BlockSpec(memory_space=pltpu.SMEM)],
        out_specs=[row(q.shape[1]), cache, cache],
        out_shape=[jax.ShapeDtypeStruct(q.shape, F32),
                   jax.ShapeDtypeStruct(ck_t.shape, F32),
                   jax.ShapeDtypeStruct(cv_t.shape, F32)],
        compiler_params=_params("parallel"),
        name="sample_attn_a",
    )(q, kn, vn, ck_t, cv_t, sinks)


def _sample_b_kernel(q_ref, kn_ref, vn_ref, ck_ref, cv_ref, o_ref, ko_ref, vo_ref, *, s):
    hkv, grp = B_KV_HEADS, B_Q_HEADS // B_KV_HEADS
    ckv = hkv * HEAD_DIM
    outs, lses, new_t = [], [], []
    off = 0
    for g, (w, d) in enumerate(B_PATTERNS):
        kn_t = _new_rows_t(kn_ref[:, ckv * g:ckv * (g + 1)], s)
        vn_t = _new_rows_t(vn_ref[:, ckv * g:ckv * (g + 1)], s)
        new_t.append((kn_t, vn_t))
        _store_shifted(ko_ref, off, ck_ref[0, :, off:off + w], kn_t, s)
        _store_shifted(vo_ref, off, cv_ref[0, :, off:off + w], vn_t, s)
        off += w
    off = 0
    scores = []
    for g, (w, d) in enumerate(B_PATTERNS):
        qbd = _block_diag_q(q_ref, B_Q_HEADS * HEAD_DIM * g, hkv, grp, s)
        scores.append(_cached_scores(qbd, ck_ref[0, :, off:off + w], new_t[g][0]))
        off += w
    off = 0
    for g, (w, d) in enumerate(B_PATTERNS):
        o, lse = _cached_finish(*scores[g], cv_ref[0, :, off:off + w], new_t[g][1], d, s, None)
        outs.append(o)
        lses.append(lse)
        off += w
    mx = jnp.maximum(jnp.maximum(lses[0], lses[1]), lses[2])
    es = [jnp.exp(l - mx) for l in lses]
    den = es[0] + es[1] + es[2]
    comb = (es[0] / den) * outs[0] + (es[1] / den) * outs[1] + (es[2] / den) * outs[2]
    o_ref[...] = _diag_heads(comb, hkv, grp, s)


def _sample_b_call(q, kn, vn, ck_t, cv_t, *, s):
    nb, c, lb = ck_t.shape
    row = lambda w: pl.BlockSpec((s, w), lambda b: (b, 0))
    cache = pl.BlockSpec((1, c, lb), lambda b: (b, 0, 0))
    co = B_Q_HEADS * HEAD_DIM
    return pl.pallas_call(
        functools.partial(_sample_b_kernel, s=s),
        grid=(nb,),
        in_specs=[row(q.shape[1]), row(kn.shape[1]), row(vn.shape[1]), cache, cache],
        out_specs=[row(co), cache, cache],
        out_shape=[jax.ShapeDtypeStruct((q.shape[0], co), F32),
                   jax.ShapeDtypeStruct(ck_t.shape, F32),
                   jax.ShapeDtypeStruct(cv_t.shape, F32)],
        compiler_params=_params("parallel"),
        name="sample_attn_b",
    )(q, kn, vn, ck_t, cv_t)


def _rope_tables(pos):
    inv_freq = ROPE_THETA ** (-jnp.arange(HALF, dtype=F32) / HALF)
    ang = pos.astype(F32)[:, None] * inv_freq[None, :]
    cos, sin = jnp.cos(ang), jnp.sin(ang)
    return jnp.tile(cos, (1, 4)), jnp.tile(jnp.concatenate([-sin, sin], axis=1), (1, 2))


def _cache_t(cache):
    b, l, h, dh = cache.shape
    return jnp.transpose(cache, (0, 2, 3, 1)).reshape(b, h * dh, l)


def _cache_from_t(x, h):
    b, c, l = x.shape
    return jnp.transpose(x.reshape(b, h, c // h, l), (0, 3, 1, 2))[None]


def kernel(x_prompt, x_sample, cache_a_k, cache_a_v, cache_b_k, cache_b_v, state_ffn_conv,
           attn_norm, ffn_norm, a_w_qkv, a_q_norm, a_k_norm, a_sinks, a_w_o,
           b_w_qkv, b_q_norm, b_k_norm, b_w_o,
           ffn_w_gate, ffn_w_up, ffn_conv_w, ffn_conv_b, ffn_w_down):
    nb, seq, _ = x_prompt.shape
    ns, dec, _ = x_sample.shape
    mp, ms = nb * seq, ns * dec

    cos_p, sin_p = _rope_tables(jnp.tile(jnp.arange(seq, dtype=jnp.int32), nb))
    cos_s, sin_s = _rope_tables(jnp.tile(PAST_LEN + jnp.arange(dec, dtype=jnp.int32), ns))
    idx = jnp.arange(V7X_MXU_DIM, dtype=jnp.int32) // HEAD_DIM
    ones_blk = (idx[:, None] == idx[None, :]).astype(BF16)

    yp = x_prompt.reshape(mp, D_MODEL)
    ys = x_sample.reshape(ms, D_MODEL)
    row = lambda a: a.reshape(1, -1)

    def head_gain(gv, n_heads_per_tile):
        return jnp.tile(gv, n_heads_per_tile).reshape(1, -1)

    tn = a_w_qkv.shape[-1] // N_QKV_TILES
    qg, kg = head_gain(a_q_norm[0], tn // HEAD_DIM), head_gain(a_k_norm[0], tn // HEAD_DIM)
    g_attn = row(attn_norm[0])
    grp_a = A_Q_HEADS // A_KV_HEADS

    qs, ks, vs, wqkv = _qkv_call(ys, g_attn, a_w_qkv, cos_s, sin_s, qg, kg, ones_blk,
                                 n_q=A_Q_HEADS, n_kv=A_KV_HEADS, tm=ms, q_dtype=F32, cast_layer=0)
    os_, ako, avo = _sample_a_call(qs, ks, vs, _cache_t(cache_a_k[0]), _cache_t(cache_a_v[0]),
                                   a_sinks[0], s=dec)
    ys, wo = _wo_cast_call(ys, os_, a_w_o, 0)
    a_k_sample = _cache_from_t(ako, A_KV_HEADS)
    a_v_sample = _cache_from_t(avo, A_KV_HEADS)

    q, k, v = _qkv_rows_call(yp, g_attn, wqkv, cos_p, sin_p, qg, kg, ones_blk, n_q=A_Q_HEADS,
                             n_kv=A_KV_HEADS, tm=TM_QKV, q_dtype=BF16, lane_major=False)
    k3, v3 = k.reshape(nb, seq, -1), v.reshape(nb, seq, -1)
    yp = _band_wo_call(q.reshape(nb, seq, -1), k3, v3, a_sinks[0], yp.reshape(nb, seq, -1), wo,
                       hkv=A_KV_HEADS, grp=grp_a).reshape(mp, D_MODEL)
    keep = min(A_WINDOW, seq)
    ca = A_KV_HEADS * HEAD_DIM
    a_k_prompt = _cache_from_t(_keep_t_call(k3, [keep], ca), A_KV_HEADS)
    a_v_prompt = _cache_from_t(_keep_t_call(v3, [keep], ca), A_KV_HEADS)

    conv_p, conv_s = [], []

    def ffn(layer, yp, ys):
        g_ffn = row(ffn_norm[layer])
        cw, cb = ffn_conv_w[layer], row(ffn_conv_b[layer])
        st = state_ffn_conv[layer]
        s0 = jnp.repeat(st[:, 0, :], dec, axis=0)
        s1 = jnp.repeat(st[:, 1, :], dec, axis=0)
        ys, gate_s, wg, wu, wd = _ffn_sample_call(ys, g_ffn, ffn_w_gate, ffn_w_up, ffn_w_down,
                                                  layer, cw, cb, s0, s1, seq=dec)
        conv_s.append(gate_s.reshape(ns, dec, D_FF)[:, dec - (CONV_W - 1):, :])
        yp, tail = _ffn_prompt_call(yp, g_ffn, wg, wu, wd, cw, cb, seq=seq)
        tiles = seq // TM_FFN
        conv_p.append(tail[tiles - 1::tiles, V7X_SUBLANES - (CONV_W - 1):, :])
        return yp, ys

    yp, ys = ffn(0, yp, ys)

    tn = b_w_qkv.shape[-1] // N_QKV_TILES
    qg, kg = head_gain(b_q_norm[0], tn // HEAD_DIM), head_gain(b_k_norm[0], tn // HEAD_DIM)
    g_attn = row(attn_norm[1])
    n_grp = len(B_PATTERNS)
    grp_b = B_Q_HEADS // B_KV_HEADS
    nqb, nkvb = n_grp * B_Q_HEADS, n_grp * B_KV_HEADS
    cb_ = B_KV_HEADS * HEAD_DIM

    qs, ks, vs, wqkv = _qkv_call(ys, g_attn, b_w_qkv, cos_s, sin_s, qg, kg, ones_blk,
                                 n_q=nqb, n_kv=nkvb, tm=ms, q_dtype=F32, cast_layer=0)
    os_, bko, bvo = _sample_b_call(qs, ks, vs, _cache_t(cache_b_k[0]), _cache_t(cache_b_v[0]),
                                   s=dec)
    ys, wo = _wo_cast_call(ys, os_, b_w_o, 0)
    b_k_sample = _cache_from_t(bko, B_KV_HEADS)
    b_v_sample = _cache_from_t(bvo, B_KV_HEADS)

    q, k, v = _qkv_rows_call(yp, g_attn, wqkv, cos_p, sin_p, qg, kg, ones_blk, n_q=nqb,
                             n_kv=nkvb, tm=TM_QKV, q_dtype=F32, lane_major=True)
    outs, lses = [], []
    for g, (w, d) in enumerate(B_PATTERNS):
        assert w // d == BLOCK
        og, lg = _dil_call(q, k, v, nb=nb, hkv=B_KV_HEADS, grp=grp_b, d=d, group=g,
                           residue_major=(g == n_grp - 1))
        outs.append(og)
        lses.append(lg)
    yp = _wo_comb_call(yp, outs, lses, wo, tm=TM_PROJ // 2, d_last=B_PATTERNS[-1][1])
    keeps = [min(w, seq) for w, _ in B_PATTERNS]
    k4 = k.reshape(k.shape[0], nb, seq, V7X_LANES)
    v4 = v.reshape(v.shape[0], nb, seq, V7X_LANES)
    b_k_prompt = _cache_from_t(_keep_t_call(k4, keeps, cb_, lane_major=True), B_KV_HEADS)
    b_v_prompt = _cache_from_t(_keep_t_call(v4, keeps, cb_, lane_major=True), B_KV_HEADS)

    yp, ys = ffn(1, yp, ys)

    return (yp.reshape(nb, seq, D_MODEL), ys.reshape(ns, dec, D_MODEL),
            a_k_prompt, a_v_prompt, a_k_sample, a_v_sample,
            b_k_prompt, b_v_prompt, b_k_sample, b_v_sample,
            jnp.stack(conv_p), jnp.stack(conv_s))
```

```python
import functools

import jax
import jax.numpy as jnp
from jax import lax
from jax.experimental import pallas as pl
from jax.experimental.pallas import tpu as pltpu

F32 = jnp.float32
BF16 = jnp.bfloat16

D_MODEL = 2048
HEAD_DIM = 64
HALF = HEAD_DIM // 2
ROPE_THETA = 10000.0
NORM_EPS = 1e-6
BLOCK = 128
PAST_LEN = 16384
A_WINDOW = 128
A_Q_HEADS = 32
A_KV_HEADS = 8
B_PATTERNS = ((128, 1), (512, 4), (2048, 16))
B_Q_HEADS = 16
B_KV_HEADS = 4
D_FF = 5632
CONV_W = 3
NEG_INF = -1e30
LOG2E = 1.4426950408889634
LN2 = 0.6931471805599453
Q_SCALE = HEAD_DIM ** -0.5 * LOG2E

V7X_MXU_DIM = 256
V7X_LANES = 128
V7X_SUBLANES = 8
VMEM_LIMIT = 56 * 1024 * 1024

TM_WO_COMB = 256
TM_QKV = 512
TM_FFN = 1024
TF_FFN = 512
TN_WO_CAST = 512
PROJ_CHUNK_ROWS = 256
N_QKV_TILES = 6
DIL_MAX_ROWS = 1024
BAND_SUBBLOCKS = 4
ATTN_DEPTH = 2
DIL_TILES_PER_TRIP = 8


def _params(*sem):
    return pltpu.CompilerParams(dimension_semantics=sem, vmem_limit_bytes=VMEM_LIMIT)


def _rms(x, g):
    ms = jnp.mean(x * x, axis=-1, keepdims=True)
    return x * lax.rsqrt(ms + NORM_EPS) * g


def _head_norm_rope(a, gain, ones_blk, cos, sin):
    tn = a.shape[1]
    x2 = a * a
    hi = x2.astype(BF16)
    lo = (x2 - hi.astype(F32)).astype(BF16)
    parts = []
    for c in range(tn // V7X_MXU_DIM):
        sl = slice(V7X_MXU_DIM * c, V7X_MXU_DIM * (c + 1))
        parts.append(jnp.dot(hi[:, sl], ones_blk, preferred_element_type=F32)
                     + jnp.dot(lo[:, sl], ones_blk, preferred_element_type=F32))
    ss = jnp.concatenate(parts, axis=1)
    y = a * lax.rsqrt(ss * (1.0 / HEAD_DIM) + NORM_EPS) * gain
    lane = lax.broadcasted_iota(jnp.int32, y.shape, 1)
    first_half = (lane & (HEAD_DIM - 1)) < HALF
    partner = jnp.where(first_half, pltpu.roll(y, tn - HALF, 1), pltpu.roll(y, HALF, 1))
    reps = tn // V7X_LANES
    return y * jnp.tile(cos, (1, reps)) + partner * jnp.tile(sin, (1, reps))


def _qkv_sample_kernel(x_ref, g_ref, w_ref, cos_ref, sin_ref, qg_ref, kg_ref, ones_ref,
                       q_ref, k_ref, v_ref, w_out_ref, xn_ref, *, n_q_tiles):
    n = pl.program_id(0)
    w_out_ref[...] = w_ref[0].astype(BF16)

    @pl.when(n == 0)
    def _():
        xn_ref[...] = _rms(x_ref[...], g_ref[...]).astype(BF16)

    acc = jnp.dot(xn_ref[...], w_out_ref[...], preferred_element_type=F32)

    @pl.when(n < n_q_tiles)
    def _():
        q_ref[...] = _head_norm_rope(acc, qg_ref[...], ones_ref[...], cos_ref[...],
                                     sin_ref[...]) * Q_SCALE

    @pl.when(n == n_q_tiles)
    def _():
        k_ref[...] = _head_norm_rope(acc, kg_ref[...], ones_ref[...], cos_ref[...], sin_ref[...])

    @pl.when(n == n_q_tiles + 1)
    def _():
        v_ref[...] = acc


def _qkv_rows_kernel(x_ref, g_ref, w_ref, cos_ref, sin_ref, qg_ref, kg_ref, ones_ref,
                     q_ref, k_ref, v_ref, xn_ref, *, n_q_tiles, tn, lane_major, row_chunks):
    xn_ref[...] = _rms(x_ref[...], g_ref[...]).astype(BF16)
    rc = xn_ref.shape[0] // row_chunks
    chunk = lambda c: slice(rc * c, rc * (c + 1))
    jobs = [(n, c) for n in range(n_q_tiles + 2) for c in range(row_chunks)]

    def matmul(job):
        n, c = job
        return jnp.dot(xn_ref[chunk(c), :], w_ref[:, tn * n:tn * (n + 1)],
                       preferred_element_type=F32)

    def store(ref, rows, col0, val):
        if not lane_major:
            ref[rows, col0:col0 + tn] = val.astype(ref.dtype)
            return
        for t in range(tn // V7X_LANES):
            ref[col0 // V7X_LANES + t, rows, :] = (
                val[:, V7X_LANES * t:V7X_LANES * (t + 1)].astype(ref.dtype))

    def finish(job, acc):
        n, c = job
        rows = chunk(c)
        if n < n_q_tiles:
            r = _head_norm_rope(acc, qg_ref[...], ones_ref[...], cos_ref[rows, :], sin_ref[rows, :])
            store(q_ref, rows, tn * n, r * Q_SCALE)
        elif n == n_q_tiles:
            store(k_ref, rows, 0, _head_norm_rope(acc, kg_ref[...], ones_ref[...],
                                                  cos_ref[rows, :], sin_ref[rows, :]))
        else:
            store(v_ref, rows, 0, acc)

    acc_next = matmul(jobs[0])
    for i, job in enumerate(jobs):
        acc = acc_next
        if i + 1 < len(jobs):
            acc_next = matmul(jobs[i + 1])
        finish(job, acc)


def _qkv_rows_call(x, g, w, cos, sin, qg, kg, ones_blk, *, n_q, n_kv, tm, q_dtype, lane_major):
    m = x.shape[0]
    ncols = w.shape[1]
    tn = ncols // N_QKV_TILES
    n_q_tiles = (n_q * HEAD_DIM) // tn
    assert n_q_tiles * tn == n_q * HEAD_DIM and n_kv * HEAD_DIM == tn and tn % V7X_MXU_DIM == 0
    assert m % tm == 0 and n_q_tiles + 2 == N_QKV_TILES
    if lane_major:
        spec = lambda cols: pl.BlockSpec((cols // V7X_LANES, tm, V7X_LANES), lambda i: (0, i, 0))
        shape = lambda cols, dt: jax.ShapeDtypeStruct((cols // V7X_LANES, m, V7X_LANES), dt)
    else:
        spec = lambda cols: pl.BlockSpec((tm, cols), lambda i: (i, 0))
        shape = lambda cols, dt: jax.ShapeDtypeStruct((m, cols), dt)
    const = lambda shp: pl.BlockSpec(shp, lambda i: (0, 0))
    return pl.pallas_call(
        functools.partial(_qkv_rows_kernel, n_q_tiles=n_q_tiles, tn=tn, lane_major=lane_major,
                          row_chunks=max(1, tm // PROJ_CHUNK_ROWS)),
        grid=(m // tm,),
        in_specs=[
            pl.BlockSpec((tm, D_MODEL), lambda i: (i, 0)),
            const((1, D_MODEL)),
            pl.BlockSpec((D_MODEL, ncols), lambda i: (0, 0), pipeline_mode=pl.Buffered(1)),
            pl.BlockSpec((tm, V7X_LANES), lambda i: (i, 0)),
            pl.BlockSpec((tm, V7X_LANES), lambda i: (i, 0)),
            const((1, tn)),
            const((1, tn)),
            const((V7X_MXU_DIM, V7X_MXU_DIM)),
        ],
        out_specs=[spec(n_q * HEAD_DIM), spec(tn), spec(tn)],
        out_shape=[shape(n_q * HEAD_DIM, q_dtype), shape(tn, F32), shape(tn, F32)],
        scratch_shapes=[pltpu.VMEM((tm, D_MODEL), BF16)],
        compiler_params=_params("parallel"),
        name="qkv_proj_rows",
    )(x, g, w, cos, sin, qg, kg, ones_blk)


def _qkv_sample_call(x, g, w, layer, cos, sin, qg, kg, ones_blk, *, n_q, n_kv):
    m = x.shape[0]
    ncols = w.shape[-1]
    tn = ncols // N_QKV_TILES
    n_q_tiles = (n_q * HEAD_DIM) // tn
    assert n_q_tiles * tn == n_q * HEAD_DIM and n_kv * HEAD_DIM == tn and tn % V7X_MXU_DIM == 0
    last_q = n_q_tiles - 1
    const = lambda shp: pl.BlockSpec(shp, lambda n: (0,) * len(shp))
    return pl.pallas_call(
        functools.partial(_qkv_sample_kernel, n_q_tiles=n_q_tiles),
        grid=(N_QKV_TILES,),
        in_specs=[
            const((m, D_MODEL)),
            const((1, D_MODEL)),
            pl.BlockSpec((1, D_MODEL, tn), lambda n: (layer, 0, n)),
            const((m, V7X_LANES)),
            const((m, V7X_LANES)),
            const((1, tn)),
            const((1, tn)),
            const((V7X_MXU_DIM, V7X_MXU_DIM)),
        ],
        out_specs=[
            pl.BlockSpec((m, tn), lambda n: (0, jnp.minimum(n, last_q))),
            const((m, tn)),
            const((m, tn)),
            pl.BlockSpec((D_MODEL, tn), lambda n: (0, n)),
        ],
        out_shape=[
            jax.ShapeDtypeStruct((m, n_q * HEAD_DIM), F32),
            jax.ShapeDtypeStruct((m, tn), F32),
            jax.ShapeDtypeStruct((m, tn), F32),
            jax.ShapeDtypeStruct((D_MODEL, ncols), BF16),
        ],
        scratch_shapes=[pltpu.VMEM((m, D_MODEL), BF16)],
        compiler_params=_params("arbitrary"),
        name="qkv_proj_sample",
    )(x, g, w, cos, sin, qg, kg, ones_blk)


GRP = 4


def _band_mask_t(first):
    shape = (2 * BLOCK, BLOCK)
    kj = lax.broadcasted_iota(jnp.int32, shape, 0)
    qi = lax.broadcasted_iota(jnp.int32, shape, 1)
    seen = (kj >= qi) & (kj <= qi + BLOCK) & ((kj >= BLOCK) | jnp.logical_not(first))
    return jnp.where(seen, 0.0, NEG_INF)


def _kv_tile_forms(kt, vt):
    lo = lax.broadcasted_iota(jnp.int32, kt.shape, 1) < HEAD_DIM
    kt_sw = pltpu.roll(kt, HEAD_DIM, 1)
    k_forms = [(jnp.where(lo, kt, 0.0).astype(BF16), jnp.where(lo, 0.0, kt_sw).astype(BF16)),
               (jnp.where(lo, kt_sw, 0.0).astype(BF16), jnp.where(lo, 0.0, kt).astype(BF16))]
    return k_forms, vt.T.astype(BF16)


def _head_scores(k_lo, k_hi, q_a, q_b):
    nt = (((1,), (1,)), ((), ()))
    qpair = jnp.concatenate([q_a, q_b], axis=0)
    return jnp.concatenate([lax.dot_general(k_lo, qpair, nt, preferred_element_type=F32),
                            lax.dot_general(k_hi, qpair, nt, preferred_element_type=F32)], axis=1)


def _head_softmax_pv(s, mask4, v_t, sinks, want_lse):
    s = s + mask4
    m = jnp.max(s, axis=0, keepdims=True)
    if sinks is not None:
        sk = jnp.concatenate([jnp.full((1, BLOCK), sinks[j] * LOG2E, F32)
                              for j in (0, 2, 1, 3)], axis=1)
        m = jnp.maximum(m, sk)
    p = jnp.exp2(s - m)
    l = jnp.sum(p, axis=0, keepdims=True)
    if sinks is not None:
        l = l + jnp.exp2(sk - m)
    o_t = jnp.dot(v_t, p.astype(BF16), preferred_element_type=F32) / l
    lse_t = jnp.broadcast_to((m + jnp.log2(l)) * LN2, o_t.shape) if want_lse else None
    o_tiles, l_tiles = [], []
    for u in range(2):
        c0, c1 = slice(BLOCK * u, BLOCK * (u + 1)), slice(BLOCK * (2 + u), BLOCK * (3 + u))
        o_tiles.append(jnp.concatenate([o_t[:, c0], o_t[:, c1]], axis=0).T)
        if want_lse:
            l_tiles.append(jnp.concatenate([lse_t[:, c0], lse_t[:, c1]], axis=0).T)
    return o_tiles, l_tiles


def _attend_tiles(tiles, want_lse, fillers=None):
    jobs = [(ti, e) for ti in range(len(tiles)) for e in range(2)]
    forms = {}

    def scores(job):
        ti, e = job
        load_kv, load_q = tiles[ti][:2]
        if ti not in forms:
            forms[ti] = _kv_tile_forms(*load_kv())
        k_lo, k_hi = forms[ti][0][e]
        return _head_scores(k_lo, k_hi, load_q(2 * e), load_q(2 * e + 1))

    pending = [scores(job) for job in jobs[:ATTN_DEPTH]]
    for idx, (ti, e) in enumerate(jobs):
        s = pending.pop(0)
        if idx + ATTN_DEPTH < len(jobs):
            pending.append(scores(jobs[idx + ATTN_DEPTH]))
        sink_of, store, mask4 = tiles[ti][2:]
        v_t = forms[ti][1][HEAD_DIM * e:HEAD_DIM * (e + 1)]
        sinks = None if sink_of is None else [sink_of(GRP * e + j) for j in range(GRP)]
        o_tiles, l_tiles = _head_softmax_pv(s, mask4, v_t, sinks, want_lse)
        for u in range(2):
            store(2 * e + u, o_tiles[u], l_tiles[u] if want_lse else None)
        if fillers and idx in fillers:
            fillers[idx]()


def _band_wo_kernel(q_ref, kc_ref, kp_ref, vc_ref, vp_ref, sink_ref, y_ref, wo_ref, out_ref,
                    o_scr, *, kv_tiles, nsub):
    mask_first = jnp.tile(_band_mask_t(pl.program_id(1) == 0), (1, GRP))
    mask_rest = jnp.tile(_band_mask_t(False), (1, GRP))
    lanes = lambda t: slice(V7X_LANES * t, V7X_LANES * (t + 1))
    rows = lambda j: slice(BLOCK * j, BLOCK * (j + 1))

    def tile(j, t):
        def load_kv():
            if j == 0:
                kp, vp = kp_ref[0, :, lanes(t)], vp_ref[0, :, lanes(t)]
            else:
                kp, vp = kc_ref[0, rows(j - 1), lanes(t)], vc_ref[0, rows(j - 1), lanes(t)]
            return (jnp.concatenate([kp, kc_ref[0, rows(j), lanes(t)]], axis=0),
                    jnp.concatenate([vp, vc_ref[0, rows(j), lanes(t)]], axis=0))

        def store(u, o_tile, _):
            o_scr[rows(j), lanes(GRP * t + u)] = o_tile.astype(o_scr.dtype)

        return (load_kv, lambda u: q_ref[0, rows(j), lanes(GRP * t + u)],
                lambda h: sink_ref[2 * GRP * t + h], store, mask_first if j == 0 else mask_rest)

    jobs_per_block = 2 * kv_tiles
    n_out = out_ref.shape[2]
    piece = n_out // jobs_per_block

    def project(j, c):
        cols = slice(piece * c, piece * (c + 1))
        out_ref[0, rows(j), cols] = y_ref[0, rows(j), cols] + jnp.dot(
            o_scr[rows(j), :], wo_ref[:, cols], preferred_element_type=F32)

    fillers = {jobs_per_block * j + c: functools.partial(project, j - 1, c)
               for j in range(1, nsub) for c in range(jobs_per_block)}
    _attend_tiles([tile(j, t) for j in range(nsub) for t in range(kv_tiles)], False, fillers)
    for c in range(jobs_per_block):
        project(nsub - 1, c)


def _band_wo_call(q, k, v, sinks, y, wo, *, hkv, grp):
    b, seq, cq = q.shape
    ck = hkv * HEAD_DIM
    nsub = BAND_SUBBLOCKS
    rows = BLOCK * nsub
    assert grp == GRP and hkv % 2 == 0 and seq % rows == 0
    cur = lambda bi, c: (bi, c, 0)
    prev = lambda bi, c: (bi, jnp.maximum(c * nsub - 1, 0), 0)
    return pl.pallas_call(
        functools.partial(_band_wo_kernel, kv_tiles=hkv // 2, nsub=nsub),
        grid=(b, seq // rows),
        in_specs=[
            pl.BlockSpec((1, rows, cq), cur),
            pl.BlockSpec((1, rows, ck), cur),
            pl.BlockSpec((1, BLOCK, ck), prev),
            pl.BlockSpec((1, rows, ck), cur),
            pl.BlockSpec((1, BLOCK, ck), prev),
            pl.BlockSpec(memory_space=pltpu.SMEM),
            pl.BlockSpec((1, rows, D_MODEL), cur),
            pl.BlockSpec((cq, D_MODEL), lambda bi, c: (0, 0)),
        ],
        out_specs=pl.BlockSpec((1, rows, D_MODEL), cur),
        out_shape=jax.ShapeDtypeStruct((b, seq, D_MODEL), F32),
        scratch_shapes=[pltpu.VMEM((rows, cq), BF16)],
        compiler_params=_params("parallel", "arbitrary"),
        name="band_attn_wo",
    )(q, k, k, v, v, sinks, y, wo)


def _dil_kernel(q_ref, kc_ref, kp_ref, vc_ref, vp_ref, o_ref, lse_ref, *, kv_tiles, grp, d, nsub,
                residue_major):
    mask_first = jnp.tile(_band_mask_t(pl.program_id(1) == 0), (1, GRP))

    if d == 1:
        mask_rest = jnp.tile(_band_mask_t(False), (1, GRP))
        rows = lambda j: slice(BLOCK * j, BLOCK * (j + 1))

        def sub_tile(j, t):
            def load_kv():
                if j == 0:
                    kp, vp = kp_ref[t], vp_ref[t]
                else:
                    kp, vp = kc_ref[t, rows(j - 1), :], vc_ref[t, rows(j - 1), :]
                return (jnp.concatenate([kp, kc_ref[t, rows(j), :]], axis=0),
                        jnp.concatenate([vp, vc_ref[t, rows(j), :]], axis=0))

            def store(u, o_tile, l_tile):
                o_ref[grp * t + u, rows(j), :] = o_tile
                lse_ref[grp * t + u, rows(j), :] = l_tile

            return (load_kv, lambda u: q_ref[grp * t + u, rows(j), :].astype(BF16), None, store,
                    mask_first if j == 0 else mask_rest)

        _attend_tiles([sub_tile(j, t) for j in range(nsub) for t in range(kv_tiles)], True)
        return

    def tile(r, t):
        rs = pl.ds(r, BLOCK, stride=d)

        def load_kv():
            return (jnp.concatenate([kp_ref[t, rs, :], kc_ref[t, rs, :]], axis=0),
                    jnp.concatenate([vp_ref[t, rs, :], vc_ref[t, rs, :]], axis=0))

        ws = pl.ds(pl.multiple_of(r * BLOCK, BLOCK), BLOCK) if residue_major else rs

        def store(u, o_tile, l_tile):
            o_ref[grp * t + u, ws, :] = o_tile
            lse_ref[grp * t + u, ws, :] = l_tile

        return (load_kv, lambda u: q_ref[grp * t + u, rs, :].astype(BF16), None, store,
                mask_first)

    unroll = min(d, max(1, DIL_TILES_PER_TRIP // kv_tiles))

    def body(i, carry):
        _attend_tiles([tile(i * unroll + rr, t) for rr in range(unroll) for t in range(kv_tiles)],
                      True)
        return carry

    if d > unroll:
        lax.fori_loop(0, d // unroll, body, 0)
    else:
        body(0, 0)


def _dil_call(q, k, v, *, nb, hkv, grp, d, group, residue_major=False):
    m = q.shape[1]
    seq = m // nb
    nsub = BAND_SUBBLOCKS if d == 1 else 1
    rows = BLOCK * d * nsub
    hsplit = max(1, rows // DIL_MAX_ROWS)
    kv_tiles = hkv // 2 // hsplit
    q_tiles = kv_tiles * grp
    nchunk = seq // rows
    assert seq % rows == 0 and kv_tiles * 2 * hsplit == hkv and grp % 2 == 0
    cur = lambda bi, c, hp: (group * hsplit + hp, bi * nchunk + c, 0)
    if d == 1:
        prev_rows = BLOCK
        prev = lambda bi, c, hp: (group * hsplit + hp,
                                  jnp.maximum((bi * nchunk + c) * nsub - 1, 0), 0)
    else:
        prev_rows = rows
        prev = lambda bi, c, hp: (group * hsplit + hp, bi * nchunk + jnp.maximum(c - 1, 0), 0)
    o_spec = pl.BlockSpec((q_tiles, rows, V7X_LANES), lambda bi, c, hp: (hp, bi * nchunk + c, 0))
    o_shape = jax.ShapeDtypeStruct((q_tiles * hsplit, m, V7X_LANES), F32)
    return pl.pallas_call(
        functools.partial(_dil_kernel, kv_tiles=kv_tiles, grp=grp, d=d, nsub=nsub,
                          residue_major=residue_major and d > 1),
        grid=(nb, nchunk, hsplit),
        in_specs=[
            pl.BlockSpec((q_tiles, rows, V7X_LANES), cur),
            pl.BlockSpec((kv_tiles, rows, V7X_LANES), cur),
            pl.BlockSpec((kv_tiles, prev_rows, V7X_LANES), prev),
            pl.BlockSpec((kv_tiles, rows, V7X_LANES), cur),
            pl.BlockSpec((kv_tiles, prev_rows, V7X_LANES), prev),
        ],
        out_specs=[o_spec, o_spec],
        out_shape=[o_shape, o_shape],
        compiler_params=_params("parallel", "arbitrary", "arbitrary"),
        name="dilated_attn",
    )(q, k, k, v, v)


def _keep_t_kernel(x_ref, o_ref):
    o_ref[0] = x_ref[0].T


def _keep_t_lane_major_kernel(x_ref, o_ref, *, keeps):
    g = pl.program_id(2)
    seq = x_ref.shape[2]
    off = 0
    for k, keep in enumerate(keeps):
        @pl.when(g == k)
        def _(keep=keep, off=off):
            o_ref[0, :, off:off + keep] = x_ref[0, 0, seq - keep:, :].T
        off += keep


def _keep_t_call(x, keeps, c, *, lane_major=False):
    b, seq = x.shape[1:3] if lane_major else x.shape[:2]
    starts, first_rb = [], []
    n = 0
    for keep in keeps:
        assert keep % BLOCK == 0 and seq % BLOCK == 0
        starts.append(n)
        first_rb.append((seq - keep) // BLOCK)
        n += keep // BLOCK

    def src_block(j):
        rb = jnp.int32(0)
        cb = jnp.int32(0)
        for g in range(len(keeps)):
            inside = j >= starts[g]
            rb = jnp.where(inside, first_rb[g] + j - starts[g], rb)
            cb = jnp.where(inside, g, cb)
        return rb, cb

    if lane_major:
        ct = c // V7X_LANES
        grid = (b, ct, len(keeps))
        in_spec = pl.BlockSpec((1, 1, seq, V7X_LANES), lambda bi, t, g: (g * ct + t, bi, 0, 0))
        out_spec = pl.BlockSpec((1, V7X_LANES, n * BLOCK), lambda bi, t, g: (bi, t, 0))
        kern = functools.partial(_keep_t_lane_major_kernel, keeps=tuple(keeps))
        sem = ("parallel", "parallel", "arbitrary")
    else:
        grid = (b, n)

        def src(bi, j):
            rb, cb = src_block(j)
            return bi, rb, cb

        in_spec = pl.BlockSpec((1, BLOCK, c), src)
        out_spec = pl.BlockSpec((1, c, BLOCK), lambda bi, j: (bi, 0, j))
        kern = _keep_t_kernel
        sem = ("parallel", "parallel")
    return pl.pallas_call(
        kern,
        grid=grid,
        in_specs=[in_spec],
        out_specs=out_spec,
        out_shape=jax.ShapeDtypeStruct((b, c, n * BLOCK), F32),
        compiler_params=_params(*sem),
        name="keep_rows_t",
    )(x)


def _wo_comb_kernel(y_ref, o0_ref, o1_ref, o2_ref, l0_ref, l1_ref, l2_ref, w_ref, out_ref):
    def natural(ref, t):
        return jnp.concatenate([ref[t, :, n, :] for n in range(ref.shape[2])], axis=0)

    def combined(t):
        l0, l1, l2 = l0_ref[t], l1_ref[t], natural(l2_ref, t)
        mx = jnp.maximum(jnp.maximum(l0, l1), l2)
        e0, e1, e2 = jnp.exp(l0 - mx), jnp.exp(l1 - mx), jnp.exp(l2 - mx)
        den = e0 + e1 + e2
        comb = ((e0 / den) * o0_ref[t] + (e1 / den) * o1_ref[t]
                + (e2 / den) * natural(o2_ref, t))
        return comb.astype(BF16)

    kp = 2 * V7X_LANES
    for p in range(o0_ref.shape[0] // 2):
        comb = jnp.concatenate([combined(2 * p), combined(2 * p + 1)], axis=1)
        part = jnp.dot(comb, w_ref[kp * p:kp * (p + 1), :], preferred_element_type=F32)
        if p == 0:
            out_ref[...] = y_ref[...] + part
        else:
            out_ref[...] += part


def _wo_cast_kernel(y_ref, o_ref, w_ref, out_ref, w_out_ref):
    w_out_ref[...] = w_ref[0].astype(BF16)
    out_ref[...] = y_ref[...] + jnp.dot(o_ref[...].astype(BF16), w_out_ref[...],
                                        preferred_element_type=F32)


def _wo_cast_call(y, o, w, layer):
    m = y.shape[0]
    c = o.shape[1]
    tn = TN_WO_CAST
    return pl.pallas_call(
        _wo_cast_kernel,
        grid=(D_MODEL // tn,),
        in_specs=[
            pl.BlockSpec((m, tn), lambda n: (0, n)),
            pl.BlockSpec((m, c), lambda n: (0, 0)),
            pl.BlockSpec((1, c, tn), lambda n: (layer, 0, n)),
        ],
        out_specs=[pl.BlockSpec((m, tn), lambda n: (0, n)),
                   pl.BlockSpec((c, tn), lambda n: (0, n))],
        out_shape=[jax.ShapeDtypeStruct((m, D_MODEL), F32),
                   jax.ShapeDtypeStruct((c, D_MODEL), BF16)],
        compiler_params=_params("parallel"),
        name="wo_proj_cast",
    )(y, o, w)


def _wo_comb_call(y, os_, ls_, w, *, tm, d_last):
    m = y.shape[0]
    c = w.shape[0]
    nt = c // V7X_LANES
    per_blk = BLOCK * d_last // tm
    assert (BLOCK * d_last) % tm == 0 and tm % d_last == 0 and (tm // d_last) % V7X_SUBLANES == 0
    blk = pl.BlockSpec((nt, tm, V7X_LANES), lambda i: (0, i, 0))
    blk_rm = pl.BlockSpec((nt, d_last, tm // d_last, V7X_LANES),
                          lambda i: (0, i // per_blk, i % per_blk, 0))
    rm = lambda a: a.reshape(nt, m // BLOCK, BLOCK, V7X_LANES)
    return pl.pallas_call(
        _wo_comb_kernel,
        grid=(m // tm,),
        in_specs=[pl.BlockSpec((tm, D_MODEL), lambda i: (i, 0)), blk, blk, blk_rm, blk, blk, blk_rm,
                  pl.BlockSpec((c, D_MODEL), lambda i: (0, 0))],
        out_specs=pl.BlockSpec((tm, D_MODEL), lambda i: (i, 0)),
        out_shape=jax.ShapeDtypeStruct((m, D_MODEL), F32),
        compiler_params=_params("parallel"),
        name="wo_comb_proj",
    )(y, os_[0], os_[1], rm(os_[2]), ls_[0], ls_[1], rm(ls_[2]), w)


def _ffn_tail(gate, g1, g2, up, cw_ref, cb_ref, wd_ref):
    conv = cb_ref[...] + cw_ref[0:1, :] * g2 + cw_ref[1:2, :] * g1 + cw_ref[2:3, :] * gate
    h = conv * jax.nn.sigmoid(conv) * up
    return jnp.dot(h.astype(BF16), wd_ref[...], preferred_element_type=F32)


def _ffn_prompt_kernel(y_ref, g_ref, wg_ref, wu_ref, wd_ref, cw_ref, cb_ref,
                       out_ref, tail_ref, xn_ref, carry_ref, *, tiles_per_seq):
    m = pl.program_id(0)
    f = pl.program_id(1)

    @pl.when(f == 0)
    def _():
        x = y_ref[...]
        xn_ref[...] = _rms(x, g_ref[...]).astype(BF16)
        out_ref[...] = x

    @pl.when(m % tiles_per_seq == 0)
    def _():
        carry_ref[f] = jnp.zeros(carry_ref.shape[1:], F32)

    xn = xn_ref[...]
    gate = jnp.dot(xn, wg_ref[...], preferred_element_type=F32)
    up = jnp.dot(xn, wu_ref[...], preferred_element_type=F32)
    c = carry_ref[f]
    row = lax.broadcasted_iota(jnp.int32, gate.shape, 0)
    g1 = jnp.where(row == 0, c[7:8, :], pltpu.roll(gate, 1, 0))
    g2 = jnp.where(row == 0, c[6:7, :], jnp.where(row == 1, c[7:8, :], pltpu.roll(gate, 2, 0)))
    out_ref[...] += _ffn_tail(gate, g1, g2, up, cw_ref, cb_ref, wd_ref)
    last = gate[gate.shape[0] - V7X_SUBLANES:, :]
    carry_ref[f] = last
    tail_ref[0] = last


def _ffn_prompt_call(y, g, wg, wu, wd, cw, cb, *, seq):
    m = y.shape[0]
    tm, tf = TM_FFN, TF_FFN
    nf = D_FF // tf
    assert m % tm == 0 and seq % tm == 0 and D_FF % tf == 0
    return pl.pallas_call(
        functools.partial(_ffn_prompt_kernel, tiles_per_seq=seq // tm),
        grid=(m // tm, nf),
        in_specs=[
            pl.BlockSpec((tm, D_MODEL), lambda i, f: (i, 0)),
            pl.BlockSpec((1, D_MODEL), lambda i, f: (0, 0)),
            pl.BlockSpec((D_MODEL, tf), lambda i, f: (0, f)),
            pl.BlockSpec((D_MODEL, tf), lambda i, f: (0, f)),
            pl.BlockSpec((tf, D_MODEL), lambda i, f: (f, 0)),
            pl.BlockSpec((CONV_W, tf), lambda i, f: (0, f)),
            pl.BlockSpec((1, tf), lambda i, f: (0, f)),
        ],
        out_specs=[
            pl.BlockSpec((tm, D_MODEL), lambda i, f: (i, 0)),
            pl.BlockSpec((1, V7X_SUBLANES, tf), lambda i, f: (i, 0, f)),
        ],
        out_shape=[
            jax.ShapeDtypeStruct((m, D_MODEL), F32),
            jax.ShapeDtypeStruct((m // tm, V7X_SUBLANES, D_FF), F32),
        ],
        scratch_shapes=[pltpu.VMEM((tm, D_MODEL), BF16),
                        pltpu.VMEM((nf, V7X_SUBLANES, tf), F32)],
        compiler_params=_params("arbitrary", "arbitrary"),
        name="conv_ffn_prompt",
    )(y, g, wg, wu, wd, cw, cb)


def _ffn_sample_kernel(y_ref, g_ref, wg_ref, wu_ref, wd_ref, cw_ref, cb_ref, s0_ref, s1_ref,
                       out_ref, gate_ref, wg_out_ref, wu_out_ref, wd_out_ref, xn_ref, *, seq):
    f = pl.program_id(0)

    @pl.when(f == 0)
    def _():
        x = y_ref[...]
        xn_ref[...] = _rms(x, g_ref[...]).astype(BF16)
        out_ref[...] = x

    wg_out_ref[...] = wg_ref[0].astype(BF16)
    wu_out_ref[...] = wu_ref[0].astype(BF16)
    wd_out_ref[...] = wd_ref[0].astype(BF16)
    xn = xn_ref[...]
    gate = jnp.dot(xn, wg_out_ref[...], preferred_element_type=F32)
    up = jnp.dot(xn, wu_out_ref[...], preferred_element_type=F32)
    t = lax.broadcasted_iota(jnp.int32, gate.shape, 0) & (seq - 1)
    s0, s1 = s0_ref[...], s1_ref[...]
    g1 = jnp.where(t == 0, s1, pltpu.roll(gate, 1, 0))
    g2 = jnp.where(t == 0, s0, jnp.where(t == 1, s1, pltpu.roll(gate, 2, 0)))
    out_ref[...] += _ffn_tail(gate, g1, g2, up, cw_ref, cb_ref, wd_out_ref)
    gate_ref[...] = gate


def _ffn_sample_call(y, g, wg, wu, wd, layer, cw, cb, s0, s1, *, seq):
    m = y.shape[0]
    tf = TF_FFN
    nf = D_FF // tf
    full = pl.BlockSpec((m, D_MODEL), lambda f: (0, 0))
    col = pl.BlockSpec((m, tf), lambda f: (0, f))
    return pl.pallas_call(
        functools.partial(_ffn_sample_kernel, seq=seq),
        grid=(nf,),
        in_specs=[
            full,
            pl.BlockSpec((1, D_MODEL), lambda f: (0, 0)),
            pl.BlockSpec((1, D_MODEL, tf), lambda f: (layer, 0, f)),
            pl.BlockSpec((1, D_MODEL, tf), lambda f: (layer, 0, f)),
            pl.BlockSpec((1, tf, D_MODEL), lambda f: (layer, f, 0)),
            pl.BlockSpec((CONV_W, tf), lambda f: (0, f)),
            pl.BlockSpec((1, tf), lambda f: (0, f)),
            col, col,
        ],
        out_specs=[full, col,
                   pl.BlockSpec((D_MODEL, tf), lambda f: (0, f)),
                   pl.BlockSpec((D_MODEL, tf), lambda f: (0, f)),
                   pl.BlockSpec((tf, D_MODEL), lambda f: (f, 0))],
        out_shape=[jax.ShapeDtypeStruct((m, D_MODEL), F32),
                   jax.ShapeDtypeStruct((m, D_FF), F32),
                   jax.ShapeDtypeStruct((D_MODEL, D_FF), BF16),
                   jax.ShapeDtypeStruct((D_MODEL, D_FF), BF16),
                   jax.ShapeDtypeStruct((D_FF, D_MODEL), BF16)],
        scratch_shapes=[pltpu.VMEM((m, D_MODEL), BF16)],
        compiler_params=_params("arbitrary"),
        name="conv_ffn_sample",
    )(y, g, wg, wu, wd, cw, cb, s0, s1)


def _block_diag_q(q_ref, col0, hkv, grp, s):
    blocks = []
    for h in range(hkv):
        qh = jnp.concatenate(
            [q_ref[:, col0 + HEAD_DIM * (grp * h + j):col0 + HEAD_DIM * (grp * h + j + 1)]
             for j in range(grp)], axis=0)
        pieces = []
        if h > 0:
            pieces.append(jnp.zeros((grp * s, HEAD_DIM * h), F32))
        pieces.append(qh)
        if h < hkv - 1:
            pieces.append(jnp.zeros((grp * s, HEAD_DIM * (hkv - 1 - h)), F32))
        blocks.append(jnp.concatenate(pieces, axis=1) if len(pieces) > 1 else qh)
    return jnp.concatenate(blocks, axis=0).astype(BF16)


def _new_rows_t(x, s):
    pad = jnp.zeros((V7X_LANES - s, x.shape[1]), F32)
    return jnp.concatenate([pad, x], axis=0).T


def _cached_attend(qbd, ck_t, kn_t, cv_t, vn_t, d, s, sink_col):
    return _cached_finish(*_cached_scores(qbd, ck_t, kn_t), cv_t, vn_t, d, s, sink_col)


def _cached_scores(qbd, ck_t, kn_t):
    return (jnp.dot(qbd, ck_t.astype(BF16), preferred_element_type=F32),
            jnp.dot(qbd, kn_t.astype(BF16), preferred_element_type=F32))


def _cached_finish(sc, sn, cv_t, vn_t, d, s, sink_col):
    r_, lc = sc.shape
    ic = lax.broadcasted_iota(jnp.int32, (r_, lc), 0) & (s - 1)
    c = lax.broadcasted_iota(jnp.int32, (r_, lc), 1)
    i_n = lax.broadcasted_iota(jnp.int32, (r_, V7X_LANES), 0) & (s - 1)
    j = lax.broadcasted_iota(jnp.int32, (r_, V7X_LANES), 1) - (V7X_LANES - s)
    valid_c = c >= ic
    valid_n = (j >= 0) & (j <= i_n)
    if d > 1:
        valid_c = valid_c & ((c & (d - 1)) == (ic & (d - 1)))
        valid_n = valid_n & ((j & (d - 1)) == (i_n & (d - 1)))
    sc = jnp.where(valid_c, sc, NEG_INF)
    sn = jnp.where(valid_n, sn, NEG_INF)
    m = jnp.maximum(jnp.max(sc, axis=1, keepdims=True), jnp.max(sn, axis=1, keepdims=True))
    if sink_col is not None:
        m = jnp.maximum(m, sink_col)
    pc = jnp.exp2(sc - m)
    pn = jnp.exp2(sn - m)
    l = jnp.sum(pc, axis=1, keepdims=True) + jnp.sum(pn, axis=1, keepdims=True)
    if sink_col is not None:
        l = l + jnp.exp2(sink_col - m)
    nt = (((1,), (1,)), ((), ()))
    o = (lax.dot_general(pc.astype(BF16), cv_t.astype(BF16), nt, preferred_element_type=F32)
         + lax.dot_general(pn.astype(BF16), vn_t.astype(BF16), nt, preferred_element_type=F32)) / l
    return o, (m + jnp.log2(l)) * LN2


def _diag_heads(o, hkv, grp, s):
    pieces = []
    for h in range(hkv):
        for j in range(grp):
            r0 = (h * grp + j) * s
            pieces.append(o[r0:r0 + s, HEAD_DIM * h:HEAD_DIM * (h + 1)])
    return jnp.concatenate(pieces, axis=1)


def _store_shifted(out_ref, off, c_t, n_t, s):
    lc = c_t.shape[1]
    rolled = pltpu.roll(c_t, lc - s, 1)
    lane = lax.broadcasted_iota(jnp.int32, n_t.shape, 1)
    if lc > V7X_LANES:
        out_ref[0, :, off:off + lc - V7X_LANES] = rolled[:, :lc - V7X_LANES]
    out_ref[0, :, off + lc - V7X_LANES:off + lc] = jnp.where(
        lane < V7X_LANES - s, rolled[:, lc - V7X_LANES:], n_t)


def _sample_a_kernel(q_ref, kn_ref, vn_ref, ck_ref, cv_ref, sink_ref, o_ref, ko_ref, vo_ref, *, s):
    hkv, grp = A_KV_HEADS, A_Q_HEADS // A_KV_HEADS
    kn_t, vn_t = _new_rows_t(kn_ref[...], s), _new_rows_t(vn_ref[...], s)
    ck_t, cv_t = ck_ref[0], cv_ref[0]
    _store_shifted(ko_ref, 0, ck_t, kn_t, s)
    _store_shifted(vo_ref, 0, cv_t, vn_t, s)
    qbd = _block_diag_q(q_ref, 0, hkv, grp, s)
    sink_col = jnp.concatenate(
        [jnp.full((s, 1), sink_ref[hq] * LOG2E, F32) for hq in range(hkv * grp)], axis=0)
    o, _ = _cached_attend(qbd, ck_t, kn_t, cv_t, vn_t, 1, s, sink_col)
    o_ref[...] = _diag_heads(o, hkv, grp, s)


def _sample_a_call(q, kn, vn, ck_t, cv_t, sinks, *, s):
    nb, c, lc = ck_t.shape
    row = lambda w: pl.BlockSpec((s, w), lambda b: (b, 0))
    cache = pl.BlockSpec((1, c, lc), lambda b: (b, 0, 0))
    return pl.pallas_call(
        functools.partial(_sample_a_kernel, s=s),
        grid=(nb,),
        in_specs=[row(q.shape[1]), row(c), row(c), cache, cache,
                  pl.BlockSpec(memory_space=pltpu.SMEM)],
        out_specs=[row(q.shape[1]), cache, cache],
        out_shape=[jax.ShapeDtypeStruct(q.shape, F32),
                   jax.ShapeDtypeStruct(ck_t.shape, F32),
                   jax.ShapeDtypeStruct(cv_t.shape, F32)],
        compiler_params=_params("parallel"),
        name="sample_attn_a",
    )(q, kn, vn, ck_t, cv_t, sinks)


def _sample_b_kernel(q_ref, kn_ref, vn_ref, ck_ref, cv_ref, o_ref, ko_ref, vo_ref, *, s):
    hkv, grp = B_KV_HEADS, B_Q_HEADS // B_KV_HEADS
    ckv = hkv * HEAD_DIM
    outs, lses, new_t = [], [], []
    off = 0
    for g, (w, d) in enumerate(B_PATTERNS):
        kn_t = _new_rows_t(kn_ref[:, ckv * g:ckv * (g + 1)], s)
        vn_t = _new_rows_t(vn_ref[:, ckv * g:ckv * (g + 1)], s)
        new_t.append((kn_t, vn_t))
        _store_shifted(ko_ref, off, ck_ref[0, :, off:off + w], kn_t, s)
        _store_shifted(vo_ref, off, cv_ref[0, :, off:off + w], vn_t, s)
        off += w
    off = 0
    scores = []
    for g, (w, d) in enumerate(B_PATTERNS):
        qbd = _block_diag_q(q_ref, B_Q_HEADS * HEAD_DIM * g, hkv, grp, s)
        scores.append(_cached_scores(qbd, ck_ref[0, :, off:off + w], new_t[g][0]))
        off += w
    off = 0
    for g, (w, d) in enumerate(B_PATTERNS):
        o, lse = _cached_finish(*scores[g], cv_ref[0, :, off:off + w], new_t[g][1], d, s, None)
        outs.append(o)
        lses.append(lse)
        off += w
    mx = jnp.maximum(jnp.maximum(lses[0], lses[1]), lses[2])
    es = [jnp.exp(l - mx) for l in lses]
    den = es[0] + es[1] + es[2]
    comb = (es[0] / den) * outs[0] + (es[1] / den) * outs[1] + (es[2] / den) * outs[2]
    o_ref[...] = _diag_heads(comb, hkv, grp, s)


def _sample_b_call(q, kn, vn, ck_t, cv_t, *, s):
    nb, c, lb = ck_t.shape
    row = lambda w: pl.BlockSpec((s, w), lambda b: (b, 0))
    cache = pl.BlockSpec((1, c, lb), lambda b: (b, 0, 0))
    co = B_Q_HEADS * HEAD_DIM
    return pl.pallas_call(
        functools.partial(_sample_b_kernel, s=s),
        grid=(nb,),
        in_specs=[row(q.shape[1]), row(kn.shape[1]), row(vn.shape[1]), cache, cache],
        out_specs=[row(co), cache, cache],
        out_shape=[jax.ShapeDtypeStruct((q.shape[0], co), F32),
                   jax.ShapeDtypeStruct(ck_t.shape, F32),
                   jax.ShapeDtypeStruct(cv_t.shape, F32)],
        compiler_params=_params("parallel"),
        name="sample_attn_b",
    )(q, kn, vn, ck_t, cv_t)


def _rope_tables(pos):
    inv_freq = ROPE_THETA ** (-jnp.arange(HALF, dtype=F32) / HALF)
    ang = pos.astype(F32)[:, None] * inv_freq[None, :]
    cos, sin = jnp.cos(ang), jnp.sin(ang)
    return jnp.tile(cos, (1, 4)), jnp.tile(jnp.concatenate([-sin, sin], axis=1), (1, 2))


def _cache_t(cache):
    b, l, h, dh = cache.shape
    return jnp.transpose(cache, (0, 2, 3, 1)).reshape(b, h * dh, l)


def _cache_from_t(x, h):
    b, c, l = x.shape
    return jnp.transpose(x.reshape(b, h, c // h, l), (0, 3, 1, 2))[None]


def kernel(x_prompt, x_sample, cache_a_k, cache_a_v, cache_b_k, cache_b_v, state_ffn_conv,
           attn_norm, ffn_norm, a_w_qkv, a_q_norm, a_k_norm, a_sinks, a_w_o,
           b_w_qkv, b_q_norm, b_k_norm, b_w_o,
           ffn_w_gate, ffn_w_up, ffn_conv_w, ffn_conv_b, ffn_w_down):
    nb, seq, _ = x_prompt.shape
    ns, dec, _ = x_sample.shape
    mp, ms = nb * seq, ns * dec

    cos_p, sin_p = _rope_tables(jnp.tile(jnp.arange(seq, dtype=jnp.int32), nb))
    cos_s, sin_s = _rope_tables(jnp.tile(PAST_LEN + jnp.arange(dec, dtype=jnp.int32), ns))
    idx = jnp.arange(V7X_MXU_DIM, dtype=jnp.int32) // HEAD_DIM
    ones_blk = (idx[:, None] == idx[None, :]).astype(BF16)

    yp = x_prompt.reshape(mp, D_MODEL)
    ys = x_sample.reshape(ms, D_MODEL)
    row = lambda a: a.reshape(1, -1)

    def head_gain(gv, n_heads_per_tile):
        return jnp.tile(gv, n_heads_per_tile).reshape(1, -1)

    tn = a_w_qkv.shape[-1] // N_QKV_TILES
    qg, kg = head_gain(a_q_norm[0], tn // HEAD_DIM), head_gain(a_k_norm[0], tn // HEAD_DIM)
    g_attn = row(attn_norm[0])
    grp_a = A_Q_HEADS // A_KV_HEADS

    qs, ks, vs, wqkv = _qkv_sample_call(ys, g_attn, a_w_qkv, 0, cos_s, sin_s, qg, kg, ones_blk,
                                        n_q=A_Q_HEADS, n_kv=A_KV_HEADS)
    os_, ako, avo = _sample_a_call(qs, ks, vs, _cache_t(cache_a_k[0]), _cache_t(cache_a_v[0]),
                                   a_sinks[0], s=dec)
    ys, wo = _wo_cast_call(ys, os_, a_w_o, 0)
    a_k_sample = _cache_from_t(ako, A_KV_HEADS)
    a_v_sample = _cache_from_t(avo, A_KV_HEADS)

    q, k, v = _qkv_rows_call(yp, g_attn, wqkv, cos_p, sin_p, qg, kg, ones_blk, n_q=A_Q_HEADS,
                             n_kv=A_KV_HEADS, tm=TM_QKV, q_dtype=BF16, lane_major=False)
    k3, v3 = k.reshape(nb, seq, -1), v.reshape(nb, seq, -1)
    yp = _band_wo_call(q.reshape(nb, seq, -1), k3, v3, a_sinks[0], yp.reshape(nb, seq, -1), wo,
                       hkv=A_KV_HEADS, grp=grp_a).reshape(mp, D_MODEL)
    keep = min(A_WINDOW, seq)
    ca = A_KV_HEADS * HEAD_DIM
    a_k_prompt = _cache_from_t(_keep_t_call(k3, [keep], ca), A_KV_HEADS)
    a_v_prompt = _cache_from_t(_keep_t_call(v3, [keep], ca), A_KV_HEADS)

    conv_p, conv_s = [], []

    def ffn(layer, yp, ys):
        g_ffn = row(ffn_norm[layer])
        cw, cb = ffn_conv_w[layer], row(ffn_conv_b[layer])
        st = state_ffn_conv[layer]
        s0 = jnp.repeat(st[:, 0, :], dec, axis=0)
        s1 = jnp.repeat(st[:, 1, :], dec, axis=0)
        ys, gate_s, wg, wu, wd = _ffn_sample_call(ys, g_ffn, ffn_w_gate, ffn_w_up, ffn_w_down,
                                                  layer, cw, cb, s0, s1, seq=dec)
        conv_s.append(gate_s.reshape(ns, dec, D_FF)[:, dec - (CONV_W - 1):, :])
        yp, tail = _ffn_prompt_call(yp, g_ffn, wg, wu, wd, cw, cb, seq=seq)
        tiles = seq // TM_FFN
        conv_p.append(tail[tiles - 1::tiles, V7X_SUBLANES - (CONV_W - 1):, :])
        return yp, ys

    yp, ys = ffn(0, yp, ys)

    tn = b_w_qkv.shape[-1] // N_QKV_TILES
    qg, kg = head_gain(b_q_norm[0], tn // HEAD_DIM), head_gain(b_k_norm[0], tn // HEAD_DIM)
    g_attn = row(attn_norm[1])
    n_grp = len(B_PATTERNS)
    grp_b = B_Q_HEADS // B_KV_HEADS
    nqb, nkvb = n_grp * B_Q_HEADS, n_grp * B_KV_HEADS
    cb_ = B_KV_HEADS * HEAD_DIM

    qs, ks, vs, wqkv = _qkv_sample_call(ys, g_attn, b_w_qkv, 0, cos_s, sin_s, qg, kg, ones_blk,
                                        n_q=nqb, n_kv=nkvb)
    os_, bko, bvo = _sample_b_call(qs, ks, vs, _cache_t(cache_b_k[0]), _cache_t(cache_b_v[0]),
                                   s=dec)
    ys, wo = _wo_cast_call(ys, os_, b_w_o, 0)
    b_k_sample = _cache_from_t(bko, B_KV_HEADS)
    b_v_sample = _cache_from_t(bvo, B_KV_HEADS)

    q, k, v = _qkv_rows_call(yp, g_attn, wqkv, cos_p, sin_p, qg, kg, ones_blk, n_q=nqb,
                             n_kv=nkvb, tm=TM_QKV, q_dtype=F32, lane_major=True)
    outs, lses = [], []
    for g, (w, d) in enumerate(B_PATTERNS):
        assert w // d == BLOCK
        og, lg = _dil_call(q, k, v, nb=nb, hkv=B_KV_HEADS, grp=grp_b, d=d, group=g,
                           residue_major=(g == n_grp - 1))
        outs.append(og)
        lses.append(lg)
    yp = _wo_comb_call(yp, outs, lses, wo, tm=TM_WO_COMB, d_last=B_PATTERNS[-1][1])
    keeps = [min(w, seq) for w, _ in B_PATTERNS]
    k4 = k.reshape(k.shape[0], nb, seq, V7X_LANES)
    v4 = v.reshape(v.shape[0], nb, seq, V7X_LANES)
    b_k_prompt = _cache_from_t(_keep_t_call(k4, keeps, cb_, lane_major=True), B_KV_HEADS)
    b_v_prompt = _cache_from_t(_keep_t_call(v4, keeps, cb_, lane_major=True), B_KV_HEADS)

    yp, ys = ffn(1, yp, ys)

    return (yp.reshape(nb, seq, D_MODEL), ys.reshape(ns, dec, D_MODEL),
            a_k_prompt, a_v_prompt, a_k_sample, a_v_sample,
            b_k_prompt, b_v_prompt, b_k_sample, b_v_sample,
            jnp.stack(conv_p), jnp.stack(conv_s))
```

```python
import functools

import jax
import jax.numpy as jnp
from jax import lax
from jax.experimental import pallas as pl
from jax.experimental.pallas import tpu as pltpu

F32 = jnp.float32
BF16 = jnp.bfloat16

D_MODEL = 2048
HEAD_DIM = 64
HALF = HEAD_DIM // 2
ROPE_THETA = 10000.0
NORM_EPS = 1e-6
BLOCK = 128
PAST_LEN = 16384
A_WINDOW = 128
A_Q_HEADS = 32
A_KV_HEADS = 8
B_PATTERNS = ((128, 1), (512, 4), (2048, 16))
B_Q_HEADS = 16
B_KV_HEADS = 4
D_FF = 5632
CONV_W = 3
NEG_INF = -1e30
LOG2E = 1.4426950408889634
LN2 = 0.6931471805599453
Q_SCALE = HEAD_DIM ** -0.5 * LOG2E

V7X_MXU_DIM = 256
V7X_LANES = 128
V7X_SUBLANES = 8
VMEM_LIMIT = 56 * 1024 * 1024

TM_WO_COMB = 256
RM_GROUP_ROWS = TM_WO_COMB
TM_QKV = 512
TM_FFN = 1024
TF_FFN = 512
TN_WO_CAST = 512
PROJ_CHUNK_ROWS = 256
N_QKV_TILES = 6
DIL_MAX_ROWS = 1024
BAND_SUBBLOCKS = 4
ATTN_DEPTH = 2
DIL_TILES_PER_TRIP = 16


def _params(*sem):
    return pltpu.CompilerParams(dimension_semantics=sem, vmem_limit_bytes=VMEM_LIMIT)


def _rms(x, g):
    ms = jnp.mean(x * x, axis=-1, keepdims=True)
    return x * lax.rsqrt(ms + NORM_EPS) * g


def _head_norm_rope(a, gain, ones_blk, cos, sin):
    tn = a.shape[1]
    x2 = a * a
    hi = x2.astype(BF16)
    lo = (x2 - hi.astype(F32)).astype(BF16)
    parts = []
    for c in range(tn // V7X_MXU_DIM):
        sl = slice(V7X_MXU_DIM * c, V7X_MXU_DIM * (c + 1))
        parts.append(jnp.dot(hi[:, sl], ones_blk, preferred_element_type=F32)
                     + jnp.dot(lo[:, sl], ones_blk, preferred_element_type=F32))
    ss = jnp.concatenate(parts, axis=1)
    y = a * lax.rsqrt(ss * (1.0 / HEAD_DIM) + NORM_EPS) * gain
    lane = lax.broadcasted_iota(jnp.int32, y.shape, 1)
    first_half = (lane & (HEAD_DIM - 1)) < HALF
    partner = jnp.where(first_half, pltpu.roll(y, tn - HALF, 1), pltpu.roll(y, HALF, 1))
    reps = tn // V7X_LANES
    return y * jnp.tile(cos, (1, reps)) + partner * jnp.tile(sin, (1, reps))


def _qkv_sample_kernel(x_ref, g_ref, w_ref, cos_ref, sin_ref, qg_ref, kg_ref, ones_ref,
                       q_ref, k_ref, v_ref, w_out_ref, xn_ref, *, n_q_tiles):
    n = pl.program_id(0)
    w_out_ref[...] = w_ref[0].astype(BF16)

    @pl.when(n == 0)
    def _():
        xn_ref[...] = _rms(x_ref[...], g_ref[...]).astype(BF16)

    acc = jnp.dot(xn_ref[...], w_out_ref[...], preferred_element_type=F32)

    @pl.when(n < n_q_tiles)
    def _():
        q_ref[...] = _head_norm_rope(acc, qg_ref[...], ones_ref[...], cos_ref[...],
                                     sin_ref[...]) * Q_SCALE

    @pl.when(n == n_q_tiles)
    def _():
        k_ref[...] = _head_norm_rope(acc, kg_ref[...], ones_ref[...], cos_ref[...], sin_ref[...])

    @pl.when(n == n_q_tiles + 1)
    def _():
        v_ref[...] = acc


def _qkv_rows_kernel(x_ref, g_ref, w_ref, cos_ref, sin_ref, qg_ref, kg_ref, ones_ref,
                     q_ref, k_ref, v_ref, xn_ref, *, n_q_tiles, tn, lane_major, row_chunks):
    xn_ref[...] = _rms(x_ref[...], g_ref[...]).astype(BF16)
    rc = xn_ref.shape[0] // row_chunks
    chunk = lambda c: slice(rc * c, rc * (c + 1))
    jobs = [(n, c) for n in range(n_q_tiles + 2) for c in range(row_chunks)]

    def matmul(job):
        n, c = job
        return jnp.dot(xn_ref[chunk(c), :], w_ref[:, tn * n:tn * (n + 1)],
                       preferred_element_type=F32)

    def store(ref, rows, col0, val):
        if not lane_major:
            ref[rows, col0:col0 + tn] = val.astype(ref.dtype)
            return
        for t in range(tn // V7X_LANES):
            ref[col0 // V7X_LANES + t, rows, :] = (
                val[:, V7X_LANES * t:V7X_LANES * (t + 1)].astype(ref.dtype))

    def finish(job, acc):
        n, c = job
        rows = chunk(c)
        if n < n_q_tiles:
            r = _head_norm_rope(acc, qg_ref[...], ones_ref[...], cos_ref[rows, :], sin_ref[rows, :])
            store(q_ref, rows, tn * n, r * Q_SCALE)
        elif n == n_q_tiles:
            store(k_ref, rows, 0, _head_norm_rope(acc, kg_ref[...], ones_ref[...],
                                                  cos_ref[rows, :], sin_ref[rows, :]))
        else:
            store(v_ref, rows, 0, acc)

    acc_next = matmul(jobs[0])
    for i, job in enumerate(jobs):
        acc = acc_next
        if i + 1 < len(jobs):
            acc_next = matmul(jobs[i + 1])
        finish(job, acc)


def _qkv_rows_call(x, g, w, cos, sin, qg, kg, ones_blk, *, n_q, n_kv, tm, q_dtype, lane_major):
    m = x.shape[0]
    ncols = w.shape[1]
    tn = ncols // N_QKV_TILES
    n_q_tiles = (n_q * HEAD_DIM) // tn
    assert n_q_tiles * tn == n_q * HEAD_DIM and n_kv * HEAD_DIM == tn and tn % V7X_MXU_DIM == 0
    assert m % tm == 0 and n_q_tiles + 2 == N_QKV_TILES
    if lane_major:
        spec = lambda cols: pl.BlockSpec((cols // V7X_LANES, tm, V7X_LANES), lambda i: (0, i, 0))
        shape = lambda cols, dt: jax.ShapeDtypeStruct((cols // V7X_LANES, m, V7X_LANES), dt)
    else:
        spec = lambda cols: pl.BlockSpec((tm, cols), lambda i: (i, 0))
        shape = lambda cols, dt: jax.ShapeDtypeStruct((m, cols), dt)
    const = lambda shp: pl.BlockSpec(shp, lambda i: (0, 0))
    return pl.pallas_call(
        functools.partial(_qkv_rows_kernel, n_q_tiles=n_q_tiles, tn=tn, lane_major=lane_major,
                          row_chunks=max(1, tm // PROJ_CHUNK_ROWS)),
        grid=(m // tm,),
        in_specs=[
            pl.BlockSpec((tm, D_MODEL), lambda i: (i, 0)),
            const((1, D_MODEL)),
            pl.BlockSpec((D_MODEL, ncols), lambda i: (0, 0), pipeline_mode=pl.Buffered(1)),
            pl.BlockSpec((tm, V7X_LANES), lambda i: (i, 0)),
            pl.BlockSpec((tm, V7X_LANES), lambda i: (i, 0)),
            const((1, tn)),
            const((1, tn)),
            const((V7X_MXU_DIM, V7X_MXU_DIM)),
        ],
        out_specs=[spec(n_q * HEAD_DIM), spec(tn), spec(tn)],
        out_shape=[shape(n_q * HEAD_DIM, q_dtype), shape(tn, F32), shape(tn, F32)],
        scratch_shapes=[pltpu.VMEM((tm, D_MODEL), BF16)],
        compiler_params=_params("parallel"),
        name="qkv_proj_rows",
    )(x, g, w, cos, sin, qg, kg, ones_blk)


def _qkv_sample_call(x, g, w, layer, cos, sin, qg, kg, ones_blk, *, n_q, n_kv):
    m = x.shape[0]
    ncols = w.shape[-1]
    tn = ncols // N_QKV_TILES
    n_q_tiles = (n_q * HEAD_DIM) // tn
    assert n_q_tiles * tn == n_q * HEAD_DIM and n_kv * HEAD_DIM == tn and tn % V7X_MXU_DIM == 0
    last_q = n_q_tiles - 1
    const = lambda shp: pl.BlockSpec(shp, lambda n: (0,) * len(shp))
    return pl.pallas_call(
        functools.partial(_qkv_sample_kernel, n_q_tiles=n_q_tiles),
        grid=(N_QKV_TILES,),
        in_specs=[
            const((m, D_MODEL)),
            const((1, D_MODEL)),
            pl.BlockSpec((1, D_MODEL, tn), lambda n: (layer, 0, n)),
            const((m, V7X_LANES)),
            const((m, V7X_LANES)),
            const((1, tn)),
            const((1, tn)),
            const((V7X_MXU_DIM, V7X_MXU_DIM)),
        ],
        out_specs=[
            pl.BlockSpec((m, tn), lambda n: (0, jnp.minimum(n, last_q))),
            const((m, tn)),
            const((m, tn)),
            pl.BlockSpec((D_MODEL, tn), lambda n: (0, n)),
        ],
        out_shape=[
            jax.ShapeDtypeStruct((m, n_q * HEAD_DIM), F32),
            jax.ShapeDtypeStruct((m, tn), F32),
            jax.ShapeDtypeStruct((m, tn), F32),
            jax.ShapeDtypeStruct((D_MODEL, ncols), BF16),
        ],
        scratch_shapes=[pltpu.VMEM((m, D_MODEL), BF16)],
        compiler_params=_params("arbitrary"),
        name="qkv_proj_sample",
    )(x, g, w, cos, sin, qg, kg, ones_blk)


GRP = 4


def _band_mask_t(first):
    shape = (2 * BLOCK, BLOCK)
    kj = lax.broadcasted_iota(jnp.int32, shape, 0)
    qi = lax.broadcasted_iota(jnp.int32, shape, 1)
    seen = (kj >= qi) & (kj <= qi + BLOCK) & ((kj >= BLOCK) | jnp.logical_not(first))
    return jnp.where(seen, 0.0, NEG_INF)


def _kv_tile_forms(kt, vt):
    lo = lax.broadcasted_iota(jnp.int32, kt.shape, 1) < HEAD_DIM
    kt_sw = pltpu.roll(kt, HEAD_DIM, 1)
    k_forms = [(jnp.where(lo, kt, 0.0).astype(BF16), jnp.where(lo, 0.0, kt_sw).astype(BF16)),
               (jnp.where(lo, kt_sw, 0.0).astype(BF16), jnp.where(lo, 0.0, kt).astype(BF16))]
    return k_forms, vt.T.astype(BF16)


def _head_scores(k_lo, k_hi, q_a, q_b):
    nt = (((1,), (1,)), ((), ()))
    qpair = jnp.concatenate([q_a, q_b], axis=0)
    return jnp.concatenate([lax.dot_general(k_lo, qpair, nt, preferred_element_type=F32),
                            lax.dot_general(k_hi, qpair, nt, preferred_element_type=F32)], axis=1)


def _head_softmax_pv(s, mask4, v_t, sinks, want_lse):
    s = s + mask4
    m = jnp.max(s, axis=0, keepdims=True)
    if sinks is not None:
        sk = jnp.concatenate([jnp.full((1, BLOCK), sinks[j] * LOG2E, F32)
                              for j in (0, 2, 1, 3)], axis=1)
        m = jnp.maximum(m, sk)
    p = jnp.exp2(s - m)
    l = jnp.sum(p, axis=0, keepdims=True)
    if sinks is not None:
        l = l + jnp.exp2(sk - m)
    o_t = jnp.dot(v_t, p.astype(BF16), preferred_element_type=F32) / l
    lse_t = jnp.broadcast_to((m + jnp.log2(l)) * LN2, o_t.shape) if want_lse else None
    o_tiles, l_tiles = [], []
    for u in range(2):
        c0, c1 = slice(BLOCK * u, BLOCK * (u + 1)), slice(BLOCK * (2 + u), BLOCK * (3 + u))
        o_tiles.append(jnp.concatenate([o_t[:, c0], o_t[:, c1]], axis=0).T)
        if want_lse:
            l_tiles.append(jnp.concatenate([lse_t[:, c0], lse_t[:, c1]], axis=0).T)
    return o_tiles, l_tiles


def _attend_tiles(tiles, want_lse, fillers=None):
    jobs = [(ti, e) for ti in range(len(tiles)) for e in range(2)]
    forms = {}

    def scores(job):
        ti, e = job
        load_kv, load_q = tiles[ti][:2]
        if ti not in forms:
            forms[ti] = _kv_tile_forms(*load_kv())
        k_lo, k_hi = forms[ti][0][e]
        return _head_scores(k_lo, k_hi, load_q(2 * e), load_q(2 * e + 1))

    pending = [scores(job) for job in jobs[:ATTN_DEPTH]]
    for idx, (ti, e) in enumerate(jobs):
        s = pending.pop(0)
        if idx + ATTN_DEPTH < len(jobs):
            pending.append(scores(jobs[idx + ATTN_DEPTH]))
        sink_of, store, mask4 = tiles[ti][2:]
        v_t = forms[ti][1][HEAD_DIM * e:HEAD_DIM * (e + 1)]
        sinks = None if sink_of is None else [sink_of(GRP * e + j) for j in range(GRP)]
        o_tiles, l_tiles = _head_softmax_pv(s, mask4, v_t, sinks, want_lse)
        for u in range(2):
            store(2 * e + u, o_tiles[u], l_tiles[u] if want_lse else None)
        if fillers and idx in fillers:
            fillers[idx]()


def _band_wo_kernel(q_ref, kc_ref, kp_ref, vc_ref, vp_ref, sink_ref, y_ref, wo_ref, out_ref,
                    o_scr, *, kv_tiles, nsub):
    mask_first = jnp.tile(_band_mask_t(pl.program_id(1) == 0), (1, GRP))
    mask_rest = jnp.tile(_band_mask_t(False), (1, GRP))
    lanes = lambda t: slice(V7X_LANES * t, V7X_LANES * (t + 1))
    rows = lambda j: slice(BLOCK * j, BLOCK * (j + 1))

    def tile(j, t):
        def load_kv():
            if j == 0:
                kp, vp = kp_ref[0, :, lanes(t)], vp_ref[0, :, lanes(t)]
            else:
                kp, vp = kc_ref[0, rows(j - 1), lanes(t)], vc_ref[0, rows(j - 1), lanes(t)]
            return (jnp.concatenate([kp, kc_ref[0, rows(j), lanes(t)]], axis=0),
                    jnp.concatenate([vp, vc_ref[0, rows(j), lanes(t)]], axis=0))

        def store(u, o_tile, _):
            o_scr[rows(j), lanes(GRP * t + u)] = o_tile.astype(o_scr.dtype)

        return (load_kv, lambda u: q_ref[0, rows(j), lanes(GRP * t + u)],
                lambda h: sink_ref[2 * GRP * t + h], store, mask_first if j == 0 else mask_rest)

    jobs_per_block = 2 * kv_tiles
    n_out = out_ref.shape[2]
    piece = n_out // jobs_per_block

    def project(j, c):
        cols = slice(piece * c, piece * (c + 1))
        out_ref[0, rows(j), cols] = y_ref[0, rows(j), cols] + jnp.dot(
            o_scr[rows(j), :], wo_ref[:, cols], preferred_element_type=F32)

    fillers = {jobs_per_block * j + c: functools.partial(project, j - 1, c)
               for j in range(1, nsub) for c in range(jobs_per_block)}
    _attend_tiles([tile(j, t) for j in range(nsub) for t in range(kv_tiles)], False, fillers)
    for c in range(jobs_per_block):
        project(nsub - 1, c)


def _band_wo_call(q, k, v, sinks, y, wo, *, hkv, grp):
    b, seq, cq = q.shape
    ck = hkv * HEAD_DIM
    nsub = BAND_SUBBLOCKS
    rows = BLOCK * nsub
    assert grp == GRP and hkv % 2 == 0 and seq % rows == 0
    cur = lambda bi, c: (bi, c, 0)
    prev = lambda bi, c: (bi, jnp.maximum(c * nsub - 1, 0), 0)
    return pl.pallas_call(
        functools.partial(_band_wo_kernel, kv_tiles=hkv // 2, nsub=nsub),
        grid=(b, seq // rows),
        in_specs=[
            pl.BlockSpec((1, rows, cq), cur),
            pl.BlockSpec((1, rows, ck), cur),
            pl.BlockSpec((1, BLOCK, ck), prev),
            pl.BlockSpec((1, rows, ck), cur),
            pl.BlockSpec((1, BLOCK, ck), prev),
            pl.BlockSpec(memory_space=pltpu.SMEM),
            pl.BlockSpec((1, rows, D_MODEL), cur),
            pl.BlockSpec((cq, D_MODEL), lambda bi, c: (0, 0)),
        ],
        out_specs=pl.BlockSpec((1, rows, D_MODEL), cur),
        out_shape=jax.ShapeDtypeStruct((b, seq, D_MODEL), F32),
        scratch_shapes=[pltpu.VMEM((rows, cq), BF16)],
        compiler_params=_params("parallel", "arbitrary"),
        name="band_attn_wo",
    )(q, k, k, v, v, sinks, y, wo)


def _dil_kernel(q_ref, kc_ref, kp_ref, vc_ref, vp_ref, o_ref, lse_ref, *, kv_tiles, grp, d, nsub,
                residue_major):
    mask_first = jnp.tile(_band_mask_t(pl.program_id(1) == 0), (1, GRP))

    if d == 1:
        mask_rest = jnp.tile(_band_mask_t(False), (1, GRP))
        rows = lambda j: slice(BLOCK * j, BLOCK * (j + 1))

        def sub_tile(j, t):
            def load_kv():
                if j == 0:
                    kp, vp = kp_ref[t], vp_ref[t]
                else:
                    kp, vp = kc_ref[t, rows(j - 1), :], vc_ref[t, rows(j - 1), :]
                return (jnp.concatenate([kp, kc_ref[t, rows(j), :]], axis=0),
                        jnp.concatenate([vp, vc_ref[t, rows(j), :]], axis=0))

            def store(u, o_tile, l_tile):
                o_ref[grp * t + u, rows(j), :] = o_tile
                lse_ref[grp * t + u, rows(j), :] = l_tile

            return (load_kv, lambda u: q_ref[grp * t + u, rows(j), :].astype(BF16), None, store,
                    mask_first if j == 0 else mask_rest)

        _attend_tiles([sub_tile(j, t) for j in range(nsub) for t in range(kv_tiles)], True)
        return

    def tile(r, t):
        rs = pl.ds(r, BLOCK, stride=d)

        def load_kv():
            return (jnp.concatenate([kp_ref[t, rs, :], kc_ref[t, rs, :]], axis=0),
                    jnp.concatenate([vp_ref[t, rs, :], vc_ref[t, rs, :]], axis=0))

        def store(u, o_tile, l_tile):
            if not residue_major:
                o_ref[grp * t + u, rs, :] = o_tile
                lse_ref[grp * t + u, rs, :] = l_tile
                return
            per = RM_GROUP_ROWS // d
            for gq in range(BLOCK // per):
                ws = pl.ds(gq * RM_GROUP_ROWS + pl.multiple_of(r * per, per), per)
                o_ref[grp * t + u, ws, :] = o_tile[per * gq:per * (gq + 1)]
                lse_ref[grp * t + u, ws, :] = l_tile[per * gq:per * (gq + 1)]

        return (load_kv, lambda u: q_ref[grp * t + u, rs, :].astype(BF16), None, store,
                mask_first)

    unroll = min(d, max(1, DIL_TILES_PER_TRIP // kv_tiles))

    def body(i, carry):
        _attend_tiles([tile(i * unroll + rr, t) for rr in range(unroll) for t in range(kv_tiles)],
                      True)
        return carry

    if d > unroll:
        lax.fori_loop(0, d // unroll, body, 0)
    else:
        body(0, 0)


def _dil_call(q, k, v, *, nb, hkv, grp, d, group, residue_major=False):
    m = q.shape[1]
    seq = m // nb
    nsub = BAND_SUBBLOCKS if d == 1 else 1
    rows = BLOCK * d * nsub
    hsplit = max(1, rows // DIL_MAX_ROWS)
    kv_tiles = hkv // 2 // hsplit
    q_tiles = kv_tiles * grp
    nchunk = seq // rows
    assert seq % rows == 0 and kv_tiles * 2 * hsplit == hkv and grp % 2 == 0
    cur = lambda bi, c, hp: (group * hsplit + hp, bi * nchunk + c, 0)
    if d == 1:
        prev_rows = BLOCK
        prev = lambda bi, c, hp: (group * hsplit + hp,
                                  jnp.maximum((bi * nchunk + c) * nsub - 1, 0), 0)
    else:
        prev_rows = rows
        prev = lambda bi, c, hp: (group * hsplit + hp, bi * nchunk + jnp.maximum(c - 1, 0), 0)
    o_spec = pl.BlockSpec((q_tiles, rows, V7X_LANES), lambda bi, c, hp: (hp, bi * nchunk + c, 0))
    o_shape = jax.ShapeDtypeStruct((q_tiles * hsplit, m, V7X_LANES), F32)
    return pl.pallas_call(
        functools.partial(_dil_kernel, kv_tiles=kv_tiles, grp=grp, d=d, nsub=nsub,
                          residue_major=residue_major and d > 1),
        grid=(nb, nchunk, hsplit),
        in_specs=[
            pl.BlockSpec((q_tiles, rows, V7X_LANES), cur),
            pl.BlockSpec((kv_tiles, rows, V7X_LANES), cur),
            pl.BlockSpec((kv_tiles, prev_rows, V7X_LANES), prev),
            pl.BlockSpec((kv_tiles, rows, V7X_LANES), cur),
            pl.BlockSpec((kv_tiles, prev_rows, V7X_LANES), prev),
        ],
        out_specs=[o_spec, o_spec],
        out_shape=[o_shape, o_shape],
        compiler_params=_params("parallel", "arbitrary", "arbitrary"),
        name="dilated_attn",
    )(q, k, k, v, v)


def _keep_t_kernel(x_ref, o_ref):
    o_ref[0] = x_ref[0].T


def _keep_t_lane_major_kernel(x_ref, o_ref, *, keeps):
    g = pl.program_id(2)
    seq = x_ref.shape[2]
    off = 0
    for k, keep in enumerate(keeps):
        @pl.when(g == k)
        def _(keep=keep, off=off):
            o_ref[0, :, off:off + keep] = x_ref[0, 0, seq - keep:, :].T
        off += keep


def _keep_t_call(x, keeps, c, *, lane_major=False):
    b, seq = x.shape[1:3] if lane_major else x.shape[:2]
    starts, first_rb = [], []
    n = 0
    for keep in keeps:
        assert keep % BLOCK == 0 and seq % BLOCK == 0
        starts.append(n)
        first_rb.append((seq - keep) // BLOCK)
        n += keep // BLOCK

    def src_block(j):
        rb = jnp.int32(0)
        cb = jnp.int32(0)
        for g in range(len(keeps)):
            inside = j >= starts[g]
            rb = jnp.where(inside, first_rb[g] + j - starts[g], rb)
            cb = jnp.where(inside, g, cb)
        return rb, cb

    if lane_major:
        ct = c // V7X_LANES
        grid = (b, ct, len(keeps))
        in_spec = pl.BlockSpec((1, 1, seq, V7X_LANES), lambda bi, t, g: (g * ct + t, bi, 0, 0))
        out_spec = pl.BlockSpec((1, V7X_LANES, n * BLOCK), lambda bi, t, g: (bi, t, 0))
        kern = functools.partial(_keep_t_lane_major_kernel, keeps=tuple(keeps))
        sem = ("parallel", "parallel", "arbitrary")
    else:
        grid = (b, n)

        def src(bi, j):
            rb, cb = src_block(j)
            return bi, rb, cb

        in_spec = pl.BlockSpec((1, BLOCK, c), src)
        out_spec = pl.BlockSpec((1, c, BLOCK), lambda bi, j: (bi, 0, j))
        kern = _keep_t_kernel
        sem = ("parallel", "parallel")
    return pl.pallas_call(
        kern,
        grid=grid,
        in_specs=[in_spec],
        out_specs=out_spec,
        out_shape=jax.ShapeDtypeStruct((b, c, n * BLOCK), F32),
        compiler_params=_params(*sem),
        name="keep_rows_t",
    )(x)


def _wo_comb_kernel(y_ref, o0_ref, o1_ref, o2_ref, l0_ref, l1_ref, l2_ref, w_ref, out_ref, *,
                    d_last):
    per = y_ref.shape[0] // d_last

    def natural(ref, t):
        return jnp.concatenate([ref[t, pl.ds(n, d_last, stride=per), :] for n in range(per)],
                               axis=0)

    def combined(t):
        l0, l1, l2 = l0_ref[t], l1_ref[t], natural(l2_ref, t)
        mx = jnp.maximum(jnp.maximum(l0, l1), l2)
        e0, e1, e2 = jnp.exp(l0 - mx), jnp.exp(l1 - mx), jnp.exp(l2 - mx)
        den = e0 + e1 + e2
        comb = ((e0 / den) * o0_ref[t] + (e1 / den) * o1_ref[t]
                + (e2 / den) * natural(o2_ref, t))
        return comb.astype(BF16)

    kp = 2 * V7X_LANES
    for p in range(o0_ref.shape[0] // 2):
        comb = jnp.concatenate([combined(2 * p), combined(2 * p + 1)], axis=1)
        part = jnp.dot(comb, w_ref[kp * p:kp * (p + 1), :], preferred_element_type=F32)
        if p == 0:
            out_ref[...] = y_ref[...] + part
        else:
            out_ref[...] += part


def _wo_cast_kernel(y_ref, o_ref, w_ref, out_ref, w_out_ref):
    w_out_ref[...] = w_ref[0].astype(BF16)
    out_ref[...] = y_ref[...] + jnp.dot(o_ref[...].astype(BF16), w_out_ref[...],
                                        preferred_element_type=F32)


def _wo_cast_call(y, o, w, layer):
    m = y.shape[0]
    c = o.shape[1]
    tn = TN_WO_CAST
    return pl.pallas_call(
        _wo_cast_kernel,
        grid=(D_MODEL // tn,),
        in_specs=[
            pl.BlockSpec((m, tn), lambda n: (0, n)),
            pl.BlockSpec((m, c), lambda n: (0, 0)),
            pl.BlockSpec((1, c, tn), lambda n: (layer, 0, n)),
        ],
        out_specs=[pl.BlockSpec((m, tn), lambda n: (0, n)),
                   pl.BlockSpec((c, tn), lambda n: (0, n))],
        out_shape=[jax.ShapeDtypeStruct((m, D_MODEL), F32),
                   jax.ShapeDtypeStruct((c, D_MODEL), BF16)],
        compiler_params=_params("parallel"),
        name="wo_proj_cast",
    )(y, o, w)


def _wo_comb_call(y, os_, ls_, w, *, tm, d_last):
    m = y.shape[0]
    c = w.shape[0]
    nt = c // V7X_LANES
    assert tm == RM_GROUP_ROWS and tm % d_last == 0 and (tm // d_last) % V7X_SUBLANES == 0
    blk = pl.BlockSpec((nt, tm, V7X_LANES), lambda i: (0, i, 0))
    return pl.pallas_call(
        functools.partial(_wo_comb_kernel, d_last=d_last),
        grid=(m // tm,),
        in_specs=[pl.BlockSpec((tm, D_MODEL), lambda i: (i, 0))] + [blk] * 6
                 + [pl.BlockSpec((c, D_MODEL), lambda i: (0, 0))],
        out_specs=pl.BlockSpec((tm, D_MODEL), lambda i: (i, 0)),
        out_shape=jax.ShapeDtypeStruct((m, D_MODEL), F32),
        compiler_params=_params("parallel"),
        name="wo_comb_proj",
    )(y, *os_, *ls_, w)


def _ffn_tail(gate, g1, g2, up, cw_ref, cb_ref, wd_ref):
    conv = cb_ref[...] + cw_ref[0:1, :] * g2 + cw_ref[1:2, :] * g1 + cw_ref[2:3, :] * gate
    h = conv * jax.nn.sigmoid(conv) * up
    return jnp.dot(h.astype(BF16), wd_ref[...], preferred_element_type=F32)


def _ffn_prompt_kernel(y_ref, g_ref, wg_ref, wu_ref, wd_ref, cw_ref, cb_ref,
                       out_ref, tail_ref, xn_ref, carry_ref, *, tiles_per_seq):
    m = pl.program_id(0)
    f = pl.program_id(1)

    @pl.when(f == 0)
    def _():
        x = y_ref[...]
        xn_ref[...] = _rms(x, g_ref[...]).astype(BF16)
        out_ref[...] = x

    @pl.when(m % tiles_per_seq == 0)
    def _():
        carry_ref[f] = jnp.zeros(carry_ref.shape[1:], F32)

    xn = xn_ref[...]
    gate = jnp.dot(xn, wg_ref[...], preferred_element_type=F32)
    up = jnp.dot(xn, wu_ref[...], preferred_element_type=F32)
    c = carry_ref[f]
    row = lax.broadcasted_iota(jnp.int32, gate.shape, 0)
    g1 = jnp.where(row == 0, c[7:8, :], pltpu.roll(gate, 1, 0))
    g2 = jnp.where(row == 0, c[6:7, :], jnp.where(row == 1, c[7:8, :], pltpu.roll(gate, 2, 0)))
    out_ref[...] += _ffn_tail(gate, g1, g2, up, cw_ref, cb_ref, wd_ref)
    last = gate[gate.shape[0] - V7X_SUBLANES:, :]
    carry_ref[f] = last
    tail_ref[0] = last


def _ffn_prompt_call(y, g, wg, wu, wd, cw, cb, *, seq):
    m = y.shape[0]
    tm, tf = TM_FFN, TF_FFN
    nf = D_FF // tf
    assert m % tm == 0 and seq % tm == 0 and D_FF % tf == 0
    return pl.pallas_call(
        functools.partial(_ffn_prompt_kernel, tiles_per_seq=seq // tm),
        grid=(m // tm, nf),
        in_specs=[
            pl.BlockSpec((tm, D_MODEL), lambda i, f: (i, 0)),
            pl.BlockSpec((1, D_MODEL), lambda i, f: (0, 0)),
            pl.BlockSpec((D_MODEL, tf), lambda i, f: (0, f)),
            pl.BlockSpec((D_MODEL, tf), lambda i, f: (0, f)),
            pl.BlockSpec((tf, D_MODEL), lambda i, f: (f, 0)),
            pl.BlockSpec((CONV_W, tf), lambda i, f: (0, f)),
            pl.BlockSpec((1, tf), lambda i, f: (0, f)),
        ],
        out_specs=[
            pl.BlockSpec((tm, D_MODEL), lambda i, f: (i, 0)),
            pl.BlockSpec((1, V7X_SUBLANES, tf), lambda i, f: (i, 0, f)),
        ],
        out_shape=[
            jax.ShapeDtypeStruct((m, D_MODEL), F32),
            jax.ShapeDtypeStruct((m // tm, V7X_SUBLANES, D_FF), F32),
        ],
        scratch_shapes=[pltpu.VMEM((tm, D_MODEL), BF16),
                        pltpu.VMEM((nf, V7X_SUBLANES, tf), F32)],
        compiler_params=_params("arbitrary", "arbitrary"),
        name="conv_ffn_prompt",
    )(y, g, wg, wu, wd, cw, cb)


def _ffn_sample_kernel(y_ref, g_ref, wg_ref, wu_ref, wd_ref, cw_ref, cb_ref, s0_ref, s1_ref,
                       out_ref, gate_ref, wg_out_ref, wu_out_ref, wd_out_ref, xn_ref, *, seq):
    f = pl.program_id(0)

    @pl.when(f == 0)
    def _():
        x = y_ref[...]
        xn_ref[...] = _rms(x, g_ref[...]).astype(BF16)
        out_ref[...] = x

    wg_out_ref[...] = wg_ref[0].astype(BF16)
    wu_out_ref[...] = wu_ref[0].astype(BF16)
    wd_out_ref[...] = wd_ref[0].astype(BF16)
    xn = xn_ref[...]
    gate = jnp.dot(xn, wg_out_ref[...], preferred_element_type=F32)
    up = jnp.dot(xn, wu_out_ref[...], preferred_element_type=F32)
    t = lax.broadcasted_iota(jnp.int32, gate.shape, 0) & (seq - 1)
    s0, s1 = s0_ref[...], s1_ref[...]
    g1 = jnp.where(t == 0, s1, pltpu.roll(gate, 1, 0))
    g2 = jnp.where(t == 0, s0, jnp.where(t == 1, s1, pltpu.roll(gate, 2, 0)))
    out_ref[...] += _ffn_tail(gate, g1, g2, up, cw_ref, cb_ref, wd_out_ref)
    gate_ref[...] = gate


def _ffn_sample_call(y, g, wg, wu, wd, layer, cw, cb, s0, s1, *, seq):
    m = y.shape[0]
    tf = TF_FFN
    nf = D_FF // tf
    full = pl.BlockSpec((m, D_MODEL), lambda f: (0, 0))
    col = pl.BlockSpec((m, tf), lambda f: (0, f))
    return pl.pallas_call(
        functools.partial(_ffn_sample_kernel, seq=seq),
        grid=(nf,),
        in_specs=[
            full,
            pl.BlockSpec((1, D_MODEL), lambda f: (0, 0)),
            pl.BlockSpec((1, D_MODEL, tf), lambda f: (layer, 0, f)),
            pl.BlockSpec((1, D_MODEL, tf), lambda f: (layer, 0, f)),
            pl.BlockSpec((1, tf, D_MODEL), lambda f: (layer, f, 0)),
            pl.BlockSpec((CONV_W, tf), lambda f: (0, f)),
            pl.BlockSpec((1, tf), lambda f: (0, f)),
            col, col,
        ],
        out_specs=[full, col,
                   pl.BlockSpec((D_MODEL, tf), lambda f: (0, f)),
                   pl.BlockSpec((D_MODEL, tf), lambda f: (0, f)),
                   pl.BlockSpec((tf, D_MODEL), lambda f: (f, 0))],
        out_shape=[jax.ShapeDtypeStruct((m, D_MODEL), F32),
                   jax.ShapeDtypeStruct((m, D_FF), F32),
                   jax.ShapeDtypeStruct((D_MODEL, D_FF), BF16),
                   jax.ShapeDtypeStruct((D_MODEL, D_FF), BF16),
                   jax.ShapeDtypeStruct((D_FF, D_MODEL), BF16)],
        scratch_shapes=[pltpu.VMEM((m, D_MODEL), BF16)],
        compiler_params=_params("arbitrary"),
        name="conv_ffn_sample",
    )(y, g, wg, wu, wd, cw, cb, s0, s1)


def _block_diag_q(q_ref, col0, hkv, grp, s):
    blocks = []
    for h in range(hkv):
        qh = jnp.concatenate(
            [q_ref[:, col0 + HEAD_DIM * (grp * h + j):col0 + HEAD_DIM * (grp * h + j + 1)]
             for j in range(grp)], axis=0)
        pieces = []
        if h > 0:
            pieces.append(jnp.zeros((grp * s, HEAD_DIM * h), F32))
        pieces.append(qh)
        if h < hkv - 1:
            pieces.append(jnp.zeros((grp * s, HEAD_DIM * (hkv - 1 - h)), F32))
        blocks.append(jnp.concatenate(pieces, axis=1) if len(pieces) > 1 else qh)
    return jnp.concatenate(blocks, axis=0).astype(BF16)


def _new_rows_t(x, s):
    pad = jnp.zeros((V7X_LANES - s, x.shape[1]), F32)
    return jnp.concatenate([pad, x], axis=0).T


def _cached_attend(qbd, ck_t, kn_t, cv_t, vn_t, d, s, sink_col):
    return _cached_finish(*_cached_scores(qbd, ck_t, kn_t), cv_t, vn_t, d, s, sink_col)


def _cached_scores(qbd, ck_t, kn_t):
    return (jnp.dot(qbd, ck_t.astype(BF16), preferred_element_type=F32),
            jnp.dot(qbd, kn_t.astype(BF16), preferred_element_type=F32))


def _cached_finish(sc, sn, cv_t, vn_t, d, s, sink_col):
    r_, lc = sc.shape
    ic = lax.broadcasted_iota(jnp.int32, (r_, lc), 0) & (s - 1)
    c = lax.broadcasted_iota(jnp.int32, (r_, lc), 1)
    i_n = lax.broadcasted_iota(jnp.int32, (r_, V7X_LANES), 0) & (s - 1)
    j = lax.broadcasted_iota(jnp.int32, (r_, V7X_LANES), 1) - (V7X_LANES - s)
    valid_c = c >= ic
    valid_n = (j >= 0) & (j <= i_n)
    if d > 1:
        valid_c = valid_c & ((c & (d - 1)) == (ic & (d - 1)))
        valid_n = valid_n & ((j & (d - 1)) == (i_n & (d - 1)))
    sc = jnp.where(valid_c, sc, NEG_INF)
    sn = jnp.where(valid_n, sn, NEG_INF)
    m = jnp.maximum(jnp.max(sc, axis=1, keepdims=True), jnp.max(sn, axis=1, keepdims=True))
    if sink_col is not None:
        m = jnp.maximum(m, sink_col)
    pc = jnp.exp2(sc - m)
    pn = jnp.exp2(sn - m)
    l = jnp.sum(pc, axis=1, keepdims=True) + jnp.sum(pn, axis=1, keepdims=True)
    if sink_col is not None:
        l = l + jnp.exp2(sink_col - m)
    nt = (((1,), (1,)), ((), ()))
    o = (lax.dot_general(pc.astype(BF16), cv_t.astype(BF16), nt, preferred_element_type=F32)
         + lax.dot_general(pn.astype(BF16), vn_t.astype(BF16), nt, preferred_element_type=F32)) / l
    return o, (m + jnp.log2(l)) * LN2


def _diag_heads(o, hkv, grp, s):
    pieces = []
    for h in range(hkv):
        for j in range(grp):
            r0 = (h * grp + j) * s
            pieces.append(o[r0:r0 + s, HEAD_DIM * h:HEAD_DIM * (h + 1)])
    return jnp.concatenate(pieces, axis=1)


def _store_shifted(out_ref, off, c_t, n_t, s):
    lc = c_t.shape[1]
    rolled = pltpu.roll(c_t, lc - s, 1)
    lane = lax.broadcasted_iota(jnp.int32, n_t.shape, 1)
    if lc > V7X_LANES:
        out_ref[0, :, off:off + lc - V7X_LANES] = rolled[:, :lc - V7X_LANES]
    out_ref[0, :, off + lc - V7X_LANES:off + lc] = jnp.where(
        lane < V7X_LANES - s, rolled[:, lc - V7X_LANES:], n_t)


def _sample_a_kernel(q_ref, kn_ref, vn_ref, ck_ref, cv_ref, sink_ref, o_ref, ko_ref, vo_ref, *, s):
    hkv, grp = A_KV_HEADS, A_Q_HEADS // A_KV_HEADS
    kn_t, vn_t = _new_rows_t(kn_ref[...], s), _new_rows_t(vn_ref[...], s)
    ck_t, cv_t = ck_ref[0], cv_ref[0]
    _store_shifted(ko_ref, 0, ck_t, kn_t, s)
    _store_shifted(vo_ref, 0, cv_t, vn_t, s)
    qbd = _block_diag_q(q_ref, 0, hkv, grp, s)
    sink_col = jnp.concatenate(
        [jnp.full((s, 1), sink_ref[hq] * LOG2E, F32) for hq in range(hkv * grp)], axis=0)
    o, _ = _cached_attend(qbd, ck_t, kn_t, cv_t, vn_t, 1, s, sink_col)
    o_ref[...] = _diag_heads(o, hkv, grp, s)


def _sample_a_call(q, kn, vn, ck_t, cv_t, sinks, *, s):
    nb, c, lc = ck_t.shape
    row = lambda w: pl.BlockSpec((s, w), lambda b: (b, 0))
    cache = pl.BlockSpec((1, c, lc), lambda b: (b, 0, 0))
    return pl.pallas_call(
        functools.partial(_sample_a_kernel, s=s),
        grid=(nb,),
        in_specs=[row(q.shape[1]), row(c), row(c), cache, cache,
                  pl.BlockSpec(memory_space=pltpu.SMEM)],
        out_specs=[row(q.shape[1]), cache, cache],
        out_shape=[jax.ShapeDtypeStruct(q.shape, F32),
                   jax.ShapeDtypeStruct(ck_t.shape, F32),
                   jax.ShapeDtypeStruct(cv_t.shape, F32)],
        compiler_params=_params("parallel"),
        name="sample_attn_a",
    )(q, kn, vn, ck_t, cv_t, sinks)


def _sample_b_kernel(q_ref, kn_ref, vn_ref, ck_ref, cv_ref, o_ref, ko_ref, vo_ref, *, s):
    hkv, grp = B_KV_HEADS, B_Q_HEADS // B_KV_HEADS
    ckv = hkv * HEAD_DIM
    outs, lses, new_t = [], [], []
    off = 0
    for g, (w, d) in enumerate(B_PATTERNS):
        kn_t = _new_rows_t(kn_ref[:, ckv * g:ckv * (g + 1)], s)
        vn_t = _new_rows_t(vn_ref[:, ckv * g:ckv * (g + 1)], s)
        new_t.append((kn_t, vn_t))
        _store_shifted(ko_ref, off, ck_ref[0, :, off:off + w], kn_t, s)
        _store_shifted(vo_ref, off, cv_ref[0, :, off:off + w], vn_t, s)
        off += w
    off = 0
    scores = []
    for g, (w, d) in enumerate(B_PATTERNS):
        qbd = _block_diag_q(q_ref, B_Q_HEADS * HEAD_DIM * g, hkv, grp, s)
        scores.append(_cached_scores(qbd, ck_ref[0, :, off:off + w], new_t[g][0]))
        off += w
    off = 0
    for g, (w, d) in enumerate(B_PATTERNS):
        o, lse = _cached_finish(*scores[g], cv_ref[0, :, off:off + w], new_t[g][1], d, s, None)
        outs.append(o)
        lses.append(lse)
        off += w
    mx = jnp.maximum(jnp.maximum(lses[0], lses[1]), lses[2])
    es = [jnp.exp(l - mx) for l in lses]
    den = es[0] + es[1] + es[2]
    comb = (es[0] / den) * outs[0] + (es[1] / den) * outs[1] + (es[2] / den) * outs[2]
    o_ref[...] = _diag_heads(comb, hkv, grp, s)


def _sample_b_call(q, kn, vn, ck_t, cv_t, *, s):
    nb, c, lb = ck_t.shape
    row = lambda w: pl.BlockSpec((s, w), lambda b: (b, 0))
    cache = pl.BlockSpec((1, c, lb), lambda b: (b, 0, 0))
    co = B_Q_HEADS * HEAD_DIM
    return pl.pallas_call(
        functools.partial(_sample_b_kernel, s=s),
        grid=(nb,),
        in_specs=[row(q.shape[1]), row(kn.shape[1]), row(vn.shape[1]), cache, cache],
        out_specs=[row(co), cache, cache],
        out_shape=[jax.ShapeDtypeStruct((q.shape[0], co), F32),
                   jax.ShapeDtypeStruct(ck_t.shape, F32),
                   jax.ShapeDtypeStruct(cv_t.shape, F32)],
        compiler_params=_params("parallel"),
        name="sample_attn_b",
    )(q, kn, vn, ck_t, cv_t)


def _rope_tables(pos):
    inv_freq = ROPE_THETA ** (-jnp.arange(HALF, dtype=F32) / HALF)
    ang = pos.astype(F32)[:, None] * inv_freq[None, :]
    cos, sin = jnp.cos(ang), jnp.sin(ang)
    return jnp.tile(cos, (1, 4)), jnp.tile(jnp.concatenate([-sin, sin], axis=1), (1, 2))


def _cache_t(cache):
    b, l, h, dh = cache.shape
    return jnp.transpose(cache, (0, 2, 3, 1)).reshape(b, h * dh, l)


def _cache_from_t(x, h):
    b, c, l = x.shape
    return jnp.transpose(x.reshape(b, h, c // h, l), (0, 3, 1, 2))[None]


def kernel(x_prompt, x_sample, cache_a_k, cache_a_v, cache_b_k, cache_b_v, state_ffn_conv,
           attn_norm, ffn_norm, a_w_qkv, a_q_norm, a_k_norm, a_sinks, a_w_o,
           b_w_qkv, b_q_norm, b_k_norm, b_w_o,
           ffn_w_gate, ffn_w_up, ffn_conv_w, ffn_conv_b, ffn_w_down):
    nb, seq, _ = x_prompt.shape
    ns, dec, _ = x_sample.shape
    mp, ms = nb * seq, ns * dec

    cos_p, sin_p = _rope_tables(jnp.tile(jnp.arange(seq, dtype=jnp.int32), nb))
    cos_s, sin_s = _rope_tables(jnp.tile(PAST_LEN + jnp.arange(dec, dtype=jnp.int32), ns))
    idx = jnp.arange(V7X_MXU_DIM, dtype=jnp.int32) // HEAD_DIM
    ones_blk = (idx[:, None] == idx[None, :]).astype(BF16)

    yp = x_prompt.reshape(mp, D_MODEL)
    ys = x_sample.reshape(ms, D_MODEL)
    row = lambda a: a.reshape(1, -1)

    def head_gain(gv, n_heads_per_tile):
        return jnp.tile(gv, n_heads_per_tile).reshape(1, -1)

    tn = a_w_qkv.shape[-1] // N_QKV_TILES
    qg, kg = head_gain(a_q_norm[0], tn // HEAD_DIM), head_gain(a_k_norm[0], tn // HEAD_DIM)
    g_attn = row(attn_norm[0])
    grp_a = A_Q_HEADS // A_KV_HEADS

    qs, ks, vs, wqkv = _qkv_sample_call(ys, g_attn, a_w_qkv, 0, cos_s, sin_s, qg, kg, ones_blk,
                                        n_q=A_Q_HEADS, n_kv=A_KV_HEADS)
    os_, ako, avo = _sample_a_call(qs, ks, vs, _cache_t(cache_a_k[0]), _cache_t(cache_a_v[0]),
                                   a_sinks[0], s=dec)
    ys, wo = _wo_cast_call(ys, os_, a_w_o, 0)
    a_k_sample = _cache_from_t(ako, A_KV_HEADS)
    a_v_sample = _cache_from_t(avo, A_KV_HEADS)

    q, k, v = _qkv_rows_call(yp, g_attn, wqkv, cos_p, sin_p, qg, kg, ones_blk, n_q=A_Q_HEADS,
                             n_kv=A_KV_HEADS, tm=TM_QKV, q_dtype=BF16, lane_major=False)
    k3, v3 = k.reshape(nb, seq, -1), v.reshape(nb, seq, -1)
    yp = _band_wo_call(q.reshape(nb, seq, -1), k3, v3, a_sinks[0], yp.reshape(nb, seq, -1), wo,
                       hkv=A_KV_HEADS, grp=grp_a).reshape(mp, D_MODEL)
    keep = min(A_WINDOW, seq)
    ca = A_KV_HEADS * HEAD_DIM
    a_k_prompt = _cache_from_t(_keep_t_call(k3, [keep], ca), A_KV_HEADS)
    a_v_prompt = _cache_from_t(_keep_t_call(v3, [keep], ca), A_KV_HEADS)

    conv_p, conv_s = [], []

    def ffn(layer, yp, ys):
        g_ffn = row(ffn_norm[layer])
        cw, cb = ffn_conv_w[layer], row(ffn_conv_b[layer])
        st = state_ffn_conv[layer]
        s0 = jnp.repeat(st[:, 0, :], dec, axis=0)
        s1 = jnp.repeat(st[:, 1, :], dec, axis=0)
        ys, gate_s, wg, wu, wd = _ffn_sample_call(ys, g_ffn, ffn_w_gate, ffn_w_up, ffn_w_down,
                                                  layer, cw, cb, s0, s1, seq=dec)
        conv_s.append(gate_s.reshape(ns, dec, D_FF)[:, dec - (CONV_W - 1):, :])
        yp, tail = _ffn_prompt_call(yp, g_ffn, wg, wu, wd, cw, cb, seq=seq)
        tiles = seq // TM_FFN
        conv_p.append(tail[tiles - 1::tiles, V7X_SUBLANES - (CONV_W - 1):, :])
        return yp, ys

    yp, ys = ffn(0, yp, ys)

    tn = b_w_qkv.shape[-1] // N_QKV_TILES
    qg, kg = head_gain(b_q_norm[0], tn // HEAD_DIM), head_gain(b_k_norm[0], tn // HEAD_DIM)
    g_attn = row(attn_norm[1])
    n_grp = len(B_PATTERNS)
    grp_b = B_Q_HEADS // B_KV_HEADS
    nqb, nkvb = n_grp * B_Q_HEADS, n_grp * B_KV_HEADS
    cb_ = B_KV_HEADS * HEAD_DIM

    qs, ks, vs, wqkv = _qkv_sample_call(ys, g_attn, b_w_qkv, 0, cos_s, sin_s, qg, kg, ones_blk,
                                        n_q=nqb, n_kv=nkvb)
    os_, bko, bvo = _sample_b_call(qs, ks, vs, _cache_t(cache_b_k[0]), _cache_t(cache_b_v[0]),
                                   s=dec)
    ys, wo = _wo_cast_call(ys, os_, b_w_o, 0)
    b_k_sample = _cache_from_t(bko, B_KV_HEADS)
    b_v_sample = _cache_from_t(bvo, B_KV_HEADS)

    q, k, v = _qkv_rows_call(yp, g_attn, wqkv, cos_p, sin_p, qg, kg, ones_blk, n_q=nqb,
                             n_kv=nkvb, tm=TM_QKV, q_dtype=F32, lane_major=True)
    outs, lses = [], []
    for g, (w, d) in enumerate(B_PATTERNS):
        assert w // d == BLOCK
        og, lg = _dil_call(q, k, v, nb=nb, hkv=B_KV_HEADS, grp=grp_b, d=d, group=g,
                           residue_major=(g == n_grp - 1))
        outs.append(og)
        lses.append(lg)
    yp = _wo_comb_call(yp, outs, lses, wo, tm=TM_WO_COMB, d_last=B_PATTERNS[-1][1])
    keeps = [min(w, seq) for w, _ in B_PATTERNS]
    k4 = k.reshape(k.shape[0], nb, seq, V7X_LANES)
    v4 = v.reshape(v.shape[0], nb, seq, V7X_LANES)
    b_k_prompt = _cache_from_t(_keep_t_call(k4, keeps, cb_, lane_major=True), B_KV_HEADS)
    b_v_prompt = _cache_from_t(_keep_t_call(v4, keeps, cb_, lane_major=True), B_KV_HEADS)

    yp, ys = ffn(1, yp, ys)

    return (yp.reshape(nb, seq, D_MODEL), ys.reshape(ns, dec, D_MODEL),
            a_k_prompt, a_v_prompt, a_k_sample, a_v_sample,
            b_k_prompt, b_v_prompt, b_k_sample, b_v_sample,
            jnp.stack(conv_p), jnp.stack(conv_s))
```

```python
import functools

import jax
import jax.numpy as jnp
from jax import lax
from jax.experimental import pallas as pl
from jax.experimental.pallas import tpu as pltpu

F32 = jnp.float32
BF16 = jnp.bfloat16

D_MODEL = 2048
HEAD_DIM = 64
HALF = HEAD_DIM // 2
ROPE_THETA = 10000.0
NORM_EPS = 1e-6
BLOCK = 128
PAST_LEN = 16384
A_WINDOW = 128
A_Q_HEADS = 32
A_KV_HEADS = 8
B_PATTERNS = ((128, 1), (512, 4), (2048, 16))
B_Q_HEADS = 16
B_KV_HEADS = 4
D_FF = 5632
CONV_W = 3
NEG_INF = -1e30
LOG2E = 1.4426950408889634
LN2 = 0.6931471805599453
Q_SCALE = HEAD_DIM ** -0.5 * LOG2E

V7X_MXU_DIM = 256
V7X_LANES = 128
V7X_SUBLANES = 8
VMEM_LIMIT = 56 * 1024 * 1024

TM_WO_COMB = 256
RM_GROUP_ROWS = TM_WO_COMB
TM_QKV = 512
TM_FFN = 1024
TF_FFN = 512
TN_WO_CAST = 512
PROJ_CHUNK_ROWS = 256
N_QKV_TILES = 6
DIL_MAX_ROWS = 1024
BAND_SUBBLOCKS = 4
DIL1_SUBBLOCKS = 8
ATTN_DEPTH = 2
DIL_TILES_PER_TRIP = 16


def _params(*sem):
    return pltpu.CompilerParams(dimension_semantics=sem, vmem_limit_bytes=VMEM_LIMIT)


def _rms(x, g):
    ms = jnp.mean(x * x, axis=-1, keepdims=True)
    return x * lax.rsqrt(ms + NORM_EPS) * g


def _head_norm_rope(a, gain, ones_blk, cos, sin):
    tn = a.shape[1]
    x2 = a * a
    hi = x2.astype(BF16)
    lo = (x2 - hi.astype(F32)).astype(BF16)
    parts = []
    for c in range(tn // V7X_MXU_DIM):
        sl = slice(V7X_MXU_DIM * c, V7X_MXU_DIM * (c + 1))
        parts.append(jnp.dot(hi[:, sl], ones_blk, preferred_element_type=F32)
                     + jnp.dot(lo[:, sl], ones_blk, preferred_element_type=F32))
    ss = jnp.concatenate(parts, axis=1)
    y = a * lax.rsqrt(ss * (1.0 / HEAD_DIM) + NORM_EPS) * gain
    lane = lax.broadcasted_iota(jnp.int32, y.shape, 1)
    first_half = (lane & (HEAD_DIM - 1)) < HALF
    partner = jnp.where(first_half, pltpu.roll(y, tn - HALF, 1), pltpu.roll(y, HALF, 1))
    reps = tn // V7X_LANES
    return y * jnp.tile(cos, (1, reps)) + partner * jnp.tile(sin, (1, reps))


def _qkv_sample_kernel(x_ref, g_ref, w_ref, cos_ref, sin_ref, qg_ref, kg_ref, ones_ref,
                       q_ref, k_ref, v_ref, w_out_ref, xn_ref, *, n_q_tiles):
    n = pl.program_id(0)
    w_out_ref[...] = w_ref[0].astype(BF16)

    @pl.when(n == 0)
    def _():
        xn_ref[...] = _rms(x_ref[...], g_ref[...]).astype(BF16)

    acc = jnp.dot(xn_ref[...], w_out_ref[...], preferred_element_type=F32)

    @pl.when(n < n_q_tiles)
    def _():
        q_ref[...] = _head_norm_rope(acc, qg_ref[...], ones_ref[...], cos_ref[...],
                                     sin_ref[...]) * Q_SCALE

    @pl.when(n == n_q_tiles)
    def _():
        k_ref[...] = _head_norm_rope(acc, kg_ref[...], ones_ref[...], cos_ref[...], sin_ref[...])

    @pl.when(n == n_q_tiles + 1)
    def _():
        v_ref[...] = acc


def _qkv_rows_kernel(x_ref, g_ref, w_ref, cos_ref, sin_ref, qg_ref, kg_ref, ones_ref,
                     q_ref, k_ref, v_ref, xn_ref, *, n_q_tiles, tn, lane_major, row_chunks):
    xn_ref[...] = _rms(x_ref[...], g_ref[...]).astype(BF16)
    rc = xn_ref.shape[0] // row_chunks
    chunk = lambda c: slice(rc * c, rc * (c + 1))
    jobs = [(n, c) for n in range(n_q_tiles + 2) for c in range(row_chunks)]

    def matmul(job):
        n, c = job
        return jnp.dot(xn_ref[chunk(c), :], w_ref[:, tn * n:tn * (n + 1)],
                       preferred_element_type=F32)

    def store(ref, rows, col0, val):
        if not lane_major:
            ref[rows, col0:col0 + tn] = val.astype(ref.dtype)
            return
        for t in range(tn // V7X_LANES):
            ref[col0 // V7X_LANES + t, rows, :] = (
                val[:, V7X_LANES * t:V7X_LANES * (t + 1)].astype(ref.dtype))

    def finish(job, acc):
        n, c = job
        rows = chunk(c)
        if n < n_q_tiles:
            r = _head_norm_rope(acc, qg_ref[...], ones_ref[...], cos_ref[rows, :], sin_ref[rows, :])
            store(q_ref, rows, tn * n, r * Q_SCALE)
        elif n == n_q_tiles:
            store(k_ref, rows, 0, _head_norm_rope(acc, kg_ref[...], ones_ref[...],
                                                  cos_ref[rows, :], sin_ref[rows, :]))
        else:
            store(v_ref, rows, 0, acc)

    acc_next = matmul(jobs[0])
    for i, job in enumerate(jobs):
        acc = acc_next
        if i + 1 < len(jobs):
            acc_next = matmul(jobs[i + 1])
        finish(job, acc)


def _qkv_rows_call(x, g, w, cos, sin, qg, kg, ones_blk, *, n_q, n_kv, tm, q_dtype, lane_major):
    m = x.shape[0]
    ncols = w.shape[1]
    tn = ncols // N_QKV_TILES
    n_q_tiles = (n_q * HEAD_DIM) // tn
    assert n_q_tiles * tn == n_q * HEAD_DIM and n_kv * HEAD_DIM == tn and tn % V7X_MXU_DIM == 0
    assert m % tm == 0 and n_q_tiles + 2 == N_QKV_TILES
    if lane_major:
        spec = lambda cols: pl.BlockSpec((cols // V7X_LANES, tm, V7X_LANES), lambda i: (0, i, 0))
        shape = lambda cols, dt: jax.ShapeDtypeStruct((cols // V7X_LANES, m, V7X_LANES), dt)
    else:
        spec = lambda cols: pl.BlockSpec((tm, cols), lambda i: (i, 0))
        shape = lambda cols, dt: jax.ShapeDtypeStruct((m, cols), dt)
    const = lambda shp: pl.BlockSpec(shp, lambda i: (0, 0))
    return pl.pallas_call(
        functools.partial(_qkv_rows_kernel, n_q_tiles=n_q_tiles, tn=tn, lane_major=lane_major,
                          row_chunks=max(1, tm // PROJ_CHUNK_ROWS)),
        grid=(m // tm,),
        in_specs=[
            pl.BlockSpec((tm, D_MODEL), lambda i: (i, 0)),
            const((1, D_MODEL)),
            pl.BlockSpec((D_MODEL, ncols), lambda i: (0, 0), pipeline_mode=pl.Buffered(1)),
            pl.BlockSpec((tm, V7X_LANES), lambda i: (i, 0)),
            pl.BlockSpec((tm, V7X_LANES), lambda i: (i, 0)),
            const((1, tn)),
            const((1, tn)),
            const((V7X_MXU_DIM, V7X_MXU_DIM)),
        ],
        out_specs=[spec(n_q * HEAD_DIM), spec(tn), spec(tn)],
        out_shape=[shape(n_q * HEAD_DIM, q_dtype), shape(tn, F32), shape(tn, F32)],
        scratch_shapes=[pltpu.VMEM((tm, D_MODEL), BF16)],
        compiler_params=_params("parallel"),
        name="qkv_proj_rows",
    )(x, g, w, cos, sin, qg, kg, ones_blk)


def _qkv_sample_call(x, g, w, layer, cos, sin, qg, kg, ones_blk, *, n_q, n_kv):
    m = x.shape[0]
    ncols = w.shape[-1]
    tn = ncols // N_QKV_TILES
    n_q_tiles = (n_q * HEAD_DIM) // tn
    assert n_q_tiles * tn == n_q * HEAD_DIM and n_kv * HEAD_DIM == tn and tn % V7X_MXU_DIM == 0
    last_q = n_q_tiles - 1
    const = lambda shp: pl.BlockSpec(shp, lambda n: (0,) * len(shp))
    return pl.pallas_call(
        functools.partial(_qkv_sample_kernel, n_q_tiles=n_q_tiles),
        grid=(N_QKV_TILES,),
        in_specs=[
            const((m, D_MODEL)),
            const((1, D_MODEL)),
            pl.BlockSpec((1, D_MODEL, tn), lambda n: (layer, 0, n)),
            const((m, V7X_LANES)),
            const((m, V7X_LANES)),
            const((1, tn)),
            const((1, tn)),
            const((V7X_MXU_DIM, V7X_MXU_DIM)),
        ],
        out_specs=[
            pl.BlockSpec((m, tn), lambda n: (0, jnp.minimum(n, last_q))),
            const((m, tn)),
            const((m, tn)),
            pl.BlockSpec((D_MODEL, tn), lambda n: (0, n)),
        ],
        out_shape=[
            jax.ShapeDtypeStruct((m, n_q * HEAD_DIM), F32),
            jax.ShapeDtypeStruct((m, tn), F32),
            jax.ShapeDtypeStruct((m, tn), F32),
            jax.ShapeDtypeStruct((D_MODEL, ncols), BF16),
        ],
        scratch_shapes=[pltpu.VMEM((m, D_MODEL), BF16)],
        compiler_params=_params("arbitrary"),
        name="qkv_proj_sample",
    )(x, g, w, cos, sin, qg, kg, ones_blk)


GRP = 4


def _band_mask_t(first):
    shape = (2 * BLOCK, BLOCK)
    kj = lax.broadcasted_iota(jnp.int32, shape, 0)
    qi = lax.broadcasted_iota(jnp.int32, shape, 1)
    seen = (kj >= qi) & (kj <= qi + BLOCK) & ((kj >= BLOCK) | jnp.logical_not(first))
    return jnp.where(seen, 0.0, NEG_INF)


def _kv_tile_forms(kt, vt):
    lo = lax.broadcasted_iota(jnp.int32, kt.shape, 1) < HEAD_DIM
    kt_sw = pltpu.roll(kt, HEAD_DIM, 1)
    k_forms = [(jnp.where(lo, kt, 0.0).astype(BF16), jnp.where(lo, 0.0, kt_sw).astype(BF16)),
               (jnp.where(lo, kt_sw, 0.0).astype(BF16), jnp.where(lo, 0.0, kt).astype(BF16))]
    return k_forms, vt.T.astype(BF16)


def _head_scores(k_lo, k_hi, q_a, q_b):
    nt = (((1,), (1,)), ((), ()))
    qpair = jnp.concatenate([q_a, q_b], axis=0)
    return jnp.concatenate([lax.dot_general(k_lo, qpair, nt, preferred_element_type=F32),
                            lax.dot_general(k_hi, qpair, nt, preferred_element_type=F32)], axis=1)


def _head_softmax_pv(s, mask4, v_t, sinks, want_lse):
    s = s + mask4
    m = jnp.max(s, axis=0, keepdims=True)
    if sinks is not None:
        sk = jnp.concatenate([jnp.full((1, BLOCK), sinks[j] * LOG2E, F32)
                              for j in (0, 2, 1, 3)], axis=1)
        m = jnp.maximum(m, sk)
    p = jnp.exp2(s - m)
    l = jnp.sum(p, axis=0, keepdims=True)
    if sinks is not None:
        l = l + jnp.exp2(sk - m)
    o_t = jnp.dot(v_t, p.astype(BF16), preferred_element_type=F32) / l
    lse_t = jnp.broadcast_to((m + jnp.log2(l)) * LN2, o_t.shape) if want_lse else None
    o_tiles, l_tiles = [], []
    for u in range(2):
        c0, c1 = slice(BLOCK * u, BLOCK * (u + 1)), slice(BLOCK * (2 + u), BLOCK * (3 + u))
        o_tiles.append(jnp.concatenate([o_t[:, c0], o_t[:, c1]], axis=0).T)
        if want_lse:
            l_tiles.append(jnp.concatenate([lse_t[:, c0], lse_t[:, c1]], axis=0).T)
    return o_tiles, l_tiles


def _attend_tiles(tiles, want_lse, fillers=None):
    jobs = [(ti, e) for ti in range(len(tiles)) for e in range(2)]
    forms = {}

    def scores(job):
        ti, e = job
        load_kv, load_q = tiles[ti][:2]
        if ti not in forms:
            forms[ti] = _kv_tile_forms(*load_kv())
        k_lo, k_hi = forms[ti][0][e]
        return _head_scores(k_lo, k_hi, load_q(2 * e), load_q(2 * e + 1))

    pending = [scores(job) for job in jobs[:ATTN_DEPTH]]
    for idx, (ti, e) in enumerate(jobs):
        s = pending.pop(0)
        if idx + ATTN_DEPTH < len(jobs):
            pending.append(scores(jobs[idx + ATTN_DEPTH]))
        sink_of, store, mask4 = tiles[ti][2:]
        v_t = forms[ti][1][HEAD_DIM * e:HEAD_DIM * (e + 1)]
        sinks = None if sink_of is None else [sink_of(GRP * e + j) for j in range(GRP)]
        o_tiles, l_tiles = _head_softmax_pv(s, mask4, v_t, sinks, want_lse)
        for u in range(2):
            store(2 * e + u, o_tiles[u], l_tiles[u] if want_lse else None)
        if fillers and idx in fillers:
            fillers[idx]()


def _band_wo_kernel(q_ref, kc_ref, kp_ref, vc_ref, vp_ref, sink_ref, y_ref, wo_ref, out_ref,
                    o_scr, *, kv_tiles, nsub):
    mask_first = jnp.tile(_band_mask_t(pl.program_id(1) == 0), (1, GRP))
    mask_rest = jnp.tile(_band_mask_t(False), (1, GRP))
    lanes = lambda t: slice(V7X_LANES * t, V7X_LANES * (t + 1))
    rows = lambda j: slice(BLOCK * j, BLOCK * (j + 1))

    def tile(j, t):
        def load_kv():
            if j == 0:
                kp, vp = kp_ref[0, :, lanes(t)], vp_ref[0, :, lanes(t)]
            else:
                kp, vp = kc_ref[0, rows(j - 1), lanes(t)], vc_ref[0, rows(j - 1), lanes(t)]
            return (jnp.concatenate([kp, kc_ref[0, rows(j), lanes(t)]], axis=0),
                    jnp.concatenate([vp, vc_ref[0, rows(j), lanes(t)]], axis=0))

        def store(u, o_tile, _):
            o_scr[rows(j), lanes(GRP * t + u)] = o_tile.astype(o_scr.dtype)

        return (load_kv, lambda u: q_ref[0, rows(j), lanes(GRP * t + u)],
                lambda h: sink_ref[2 * GRP * t + h], store, mask_first if j == 0 else mask_rest)

    jobs_per_block = 2 * kv_tiles
    n_out = out_ref.shape[2]
    piece = n_out // jobs_per_block

    def project(j, c):
        cols = slice(piece * c, piece * (c + 1))
        out_ref[0, rows(j), cols] = y_ref[0, rows(j), cols] + jnp.dot(
            o_scr[rows(j), :], wo_ref[:, cols], preferred_element_type=F32)

    fillers = {jobs_per_block * j + c: functools.partial(project, j - 1, c)
               for j in range(1, nsub) for c in range(jobs_per_block)}
    _attend_tiles([tile(j, t) for j in range(nsub) for t in range(kv_tiles)], False, fillers)
    for c in range(jobs_per_block):
        project(nsub - 1, c)


def _band_wo_call(q, k, v, sinks, y, wo, *, hkv, grp):
    b, seq, cq = q.shape
    ck = hkv * HEAD_DIM
    nsub = BAND_SUBBLOCKS
    rows = BLOCK * nsub
    assert grp == GRP and hkv % 2 == 0 and seq % rows == 0
    cur = lambda bi, c: (bi, c, 0)
    prev = lambda bi, c: (bi, jnp.maximum(c * nsub - 1, 0), 0)
    return pl.pallas_call(
        functools.partial(_band_wo_kernel, kv_tiles=hkv // 2, nsub=nsub),
        grid=(b, seq // rows),
        in_specs=[
            pl.BlockSpec((1, rows, cq), cur),
            pl.BlockSpec((1, rows, ck), cur),
            pl.BlockSpec((1, BLOCK, ck), prev),
            pl.BlockSpec((1, rows, ck), cur),
            pl.BlockSpec((1, BLOCK, ck), prev),
            pl.BlockSpec(memory_space=pltpu.SMEM),
            pl.BlockSpec((1, rows, D_MODEL), cur),
            pl.BlockSpec((cq, D_MODEL), lambda bi, c: (0, 0)),
        ],
        out_specs=pl.BlockSpec((1, rows, D_MODEL), cur),
        out_shape=jax.ShapeDtypeStruct((b, seq, D_MODEL), F32),
        scratch_shapes=[pltpu.VMEM((rows, cq), BF16)],
        compiler_params=_params("parallel", "arbitrary"),
        name="band_attn_wo",
    )(q, k, k, v, v, sinks, y, wo)


def _dil_kernel(q_ref, kc_ref, kp_ref, vc_ref, vp_ref, o_ref, lse_ref, *, kv_tiles, grp, d, nsub,
                residue_major):
    mask_first = jnp.tile(_band_mask_t(pl.program_id(1) == 0), (1, GRP))

    if d == 1:
        mask_rest = jnp.tile(_band_mask_t(False), (1, GRP))
        rows = lambda j: slice(BLOCK * j, BLOCK * (j + 1))

        def sub_tile(j, t):
            def load_kv():
                if j == 0:
                    kp, vp = kp_ref[t], vp_ref[t]
                else:
                    kp, vp = kc_ref[t, rows(j - 1), :], vc_ref[t, rows(j - 1), :]
                return (jnp.concatenate([kp, kc_ref[t, rows(j), :]], axis=0),
                        jnp.concatenate([vp, vc_ref[t, rows(j), :]], axis=0))

            def store(u, o_tile, l_tile):
                o_ref[grp * t + u, rows(j), :] = o_tile
                lse_ref[grp * t + u, rows(j), :] = l_tile

            return (load_kv, lambda u: q_ref[grp * t + u, rows(j), :].astype(BF16), None, store,
                    mask_first if j == 0 else mask_rest)

        _attend_tiles([sub_tile(j, t) for j in range(nsub) for t in range(kv_tiles)], True)
        return

    def tile(r, t):
        rs = pl.ds(r, BLOCK, stride=d)

        def load_kv():
            return (jnp.concatenate([kp_ref[t, rs, :], kc_ref[t, rs, :]], axis=0),
                    jnp.concatenate([vp_ref[t, rs, :], vc_ref[t, rs, :]], axis=0))

        def store(u, o_tile, l_tile):
            if not residue_major:
                o_ref[grp * t + u, rs, :] = o_tile
                lse_ref[grp * t + u, rs, :] = l_tile
                return
            per = RM_GROUP_ROWS // d
            for gq in range(BLOCK // per):
                ws = pl.ds(gq * RM_GROUP_ROWS + pl.multiple_of(r * per, per), per)
                o_ref[grp * t + u, ws, :] = o_tile[per * gq:per * (gq + 1)]
                lse_ref[grp * t + u, ws, :] = l_tile[per * gq:per * (gq + 1)]

        return (load_kv, lambda u: q_ref[grp * t + u, rs, :].astype(BF16), None, store,
                mask_first)

    unroll = min(d, max(1, DIL_TILES_PER_TRIP // kv_tiles))

    def body(i, carry):
        _attend_tiles([tile(i * unroll + rr, t) for rr in range(unroll) for t in range(kv_tiles)],
                      True)
        return carry

    if d > unroll:
        lax.fori_loop(0, d // unroll, body, 0)
    else:
        body(0, 0)


def _dil_call(q, k, v, *, nb, hkv, grp, d, group, residue_major=False):
    m = q.shape[1]
    seq = m // nb
    nsub = DIL1_SUBBLOCKS if d == 1 else 1
    rows = BLOCK * d * nsub
    hsplit = max(1, rows // DIL_MAX_ROWS)
    kv_tiles = hkv // 2 // hsplit
    q_tiles = kv_tiles * grp
    nchunk = seq // rows
    assert seq % rows == 0 and kv_tiles * 2 * hsplit == hkv and grp % 2 == 0
    cur = lambda bi, c, hp: (group * hsplit + hp, bi * nchunk + c, 0)
    if d == 1:
        prev_rows = BLOCK
        prev = lambda bi, c, hp: (group * hsplit + hp,
                                  jnp.maximum((bi * nchunk + c) * nsub - 1, 0), 0)
    else:
        prev_rows = rows
        prev = lambda bi, c, hp: (group * hsplit + hp, bi * nchunk + jnp.maximum(c - 1, 0), 0)
    o_spec = pl.BlockSpec((q_tiles, rows, V7X_LANES), lambda bi, c, hp: (hp, bi * nchunk + c, 0))
    o_shape = jax.ShapeDtypeStruct((q_tiles * hsplit, m, V7X_LANES), F32)
    return pl.pallas_call(
        functools.partial(_dil_kernel, kv_tiles=kv_tiles, grp=grp, d=d, nsub=nsub,
                          residue_major=residue_major and d > 1),
        grid=(nb, nchunk, hsplit),
        in_specs=[
            pl.BlockSpec((q_tiles, rows, V7X_LANES), cur),
            pl.BlockSpec((kv_tiles, rows, V7X_LANES), cur),
            pl.BlockSpec((kv_tiles, prev_rows, V7X_LANES), prev),
            pl.BlockSpec((kv_tiles, rows, V7X_LANES), cur),
            pl.BlockSpec((kv_tiles, prev_rows, V7X_LANES), prev),
        ],
        out_specs=[o_spec, o_spec],
        out_shape=[o_shape, o_shape],
        compiler_params=_params("parallel", "arbitrary", "arbitrary"),
        name="dilated_attn",
    )(q, k, k, v, v)


def _keep_t_kernel(x_ref, o_ref):
    o_ref[0] = x_ref[0].T


def _keep_t_lane_major_kernel(x_ref, o_ref, *, keeps):
    g = pl.program_id(2)
    seq = x_ref.shape[2]
    off = 0
    for k, keep in enumerate(keeps):
        @pl.when(g == k)
        def _(keep=keep, off=off):
            o_ref[0, :, off:off + keep] = x_ref[0, 0, seq - keep:, :].T
        off += keep


def _keep_t_call(x, keeps, c, *, lane_major=False):
    b, seq = x.shape[1:3] if lane_major else x.shape[:2]
    starts, first_rb = [], []
    n = 0
    for keep in keeps:
        assert keep % BLOCK == 0 and seq % BLOCK == 0
        starts.append(n)
        first_rb.append((seq - keep) // BLOCK)
        n += keep // BLOCK

    def src_block(j):
        rb = jnp.int32(0)
        cb = jnp.int32(0)
        for g in range(len(keeps)):
            inside = j >= starts[g]
            rb = jnp.where(inside, first_rb[g] + j - starts[g], rb)
            cb = jnp.where(inside, g, cb)
        return rb, cb

    if lane_major:
        ct = c // V7X_LANES
        grid = (b, ct, len(keeps))
        in_spec = pl.BlockSpec((1, 1, seq, V7X_LANES), lambda bi, t, g: (g * ct + t, bi, 0, 0))
        out_spec = pl.BlockSpec((1, V7X_LANES, n * BLOCK), lambda bi, t, g: (bi, t, 0))
        kern = functools.partial(_keep_t_lane_major_kernel, keeps=tuple(keeps))
        sem = ("parallel", "parallel", "arbitrary")
    else:
        grid = (b, n)

        def src(bi, j):
            rb, cb = src_block(j)
            return bi, rb, cb

        in_spec = pl.BlockSpec((1, BLOCK, c), src)
        out_spec = pl.BlockSpec((1, c, BLOCK), lambda bi, j: (bi, 0, j))
        kern = _keep_t_kernel
        sem = ("parallel", "parallel")
    return pl.pallas_call(
        kern,
        grid=grid,
        in_specs=[in_spec],
        out_specs=out_spec,
        out_shape=jax.ShapeDtypeStruct((b, c, n * BLOCK), F32),
        compiler_params=_params(*sem),
        name="keep_rows_t",
    )(x)


def _wo_comb_kernel(y_ref, o0_ref, o1_ref, o2_ref, l0_ref, l1_ref, l2_ref, w_ref, out_ref, *,
                    d_last):
    per = y_ref.shape[0] // d_last

    def natural(ref, t):
        return jnp.concatenate([ref[t, pl.ds(n, d_last, stride=per), :] for n in range(per)],
                               axis=0)

    def combined(t):
        l0, l1, l2 = l0_ref[t], l1_ref[t], natural(l2_ref, t)
        mx = jnp.maximum(jnp.maximum(l0, l1), l2)
        e0, e1, e2 = jnp.exp(l0 - mx), jnp.exp(l1 - mx), jnp.exp(l2 - mx)
        den = e0 + e1 + e2
        comb = ((e0 / den) * o0_ref[t] + (e1 / den) * o1_ref[t]
                + (e2 / den) * natural(o2_ref, t))
        return comb.astype(BF16)

    kp = 2 * V7X_LANES
    for p in range(o0_ref.shape[0] // 2):
        comb = jnp.concatenate([combined(2 * p), combined(2 * p + 1)], axis=1)
        part = jnp.dot(comb, w_ref[kp * p:kp * (p + 1), :], preferred_element_type=F32)
        if p == 0:
            out_ref[...] = y_ref[...] + part
        else:
            out_ref[...] += part


def _wo_cast_kernel(y_ref, o_ref, w_ref, out_ref, w_out_ref):
    w_out_ref[...] = w_ref[0].astype(BF16)
    out_ref[...] = y_ref[...] + jnp.dot(o_ref[...].astype(BF16), w_out_ref[...],
                                        preferred_element_type=F32)


def _wo_cast_call(y, o, w, layer):
    m = y.shape[0]
    c = o.shape[1]
    tn = TN_WO_CAST
    return pl.pallas_call(
        _wo_cast_kernel,
        grid=(D_MODEL // tn,),
        in_specs=[
            pl.BlockSpec((m, tn), lambda n: (0, n)),
            pl.BlockSpec((m, c), lambda n: (0, 0)),
            pl.BlockSpec((1, c, tn), lambda n: (layer, 0, n)),
        ],
        out_specs=[pl.BlockSpec((m, tn), lambda n: (0, n)),
                   pl.BlockSpec((c, tn), lambda n: (0, n))],
        out_shape=[jax.ShapeDtypeStruct((m, D_MODEL), F32),
                   jax.ShapeDtypeStruct((c, D_MODEL), BF16)],
        compiler_params=_params("parallel"),
        name="wo_proj_cast",
    )(y, o, w)


def _wo_comb_call(y, os_, ls_, w, *, tm, d_last):
    m = y.shape[0]
    c = w.shape[0]
    nt = c // V7X_LANES
    assert tm == RM_GROUP_ROWS and tm % d_last == 0 and (tm // d_last) % V7X_SUBLANES == 0
    blk = pl.BlockSpec((nt, tm, V7X_LANES), lambda i: (0, i, 0))
    return pl.pallas_call(
        functools.partial(_wo_comb_kernel, d_last=d_last),
        grid=(m // tm,),
        in_specs=[pl.BlockSpec((tm, D_MODEL), lambda i: (i, 0))] + [blk] * 6
                 + [pl.BlockSpec((c, D_MODEL), lambda i: (0, 0))],
        out_specs=pl.BlockSpec((tm, D_MODEL), lambda i: (i, 0)),
        out_shape=jax.ShapeDtypeStruct((m, D_MODEL), F32),
        compiler_params=_params("parallel"),
        name="wo_comb_proj",
    )(y, *os_, *ls_, w)


def _ffn_tail(gate, g1, g2, up, cw_ref, cb_ref, wd_ref):
    conv = cb_ref[...] + cw_ref[0:1, :] * g2 + cw_ref[1:2, :] * g1 + cw_ref[2:3, :] * gate
    h = conv * jax.nn.sigmoid(conv) * up
    return jnp.dot(h.astype(BF16), wd_ref[...], preferred_element_type=F32)


def _ffn_prompt_kernel(y_ref, g_ref, wg_ref, wu_ref, wd_ref, cw_ref, cb_ref,
                       out_ref, tail_ref, xn_ref, carry_ref, *, tiles_per_seq):
    m = pl.program_id(0)
    f = pl.program_id(1)

    @pl.when(f == 0)
    def _():
        x = y_ref[...]
        xn_ref[...] = _rms(x, g_ref[...]).astype(BF16)
        out_ref[...] = x

    @pl.when(m % tiles_per_seq == 0)
    def _():
        carry_ref[f] = jnp.zeros(carry_ref.shape[1:], F32)

    xn = xn_ref[...]
    gate = jnp.dot(xn, wg_ref[...], preferred_element_type=F32)
    up = jnp.dot(xn, wu_ref[...], preferred_element_type=F32)
    c = carry_ref[f]
    row = lax.broadcasted_iota(jnp.int32, gate.shape, 0)
    g1 = jnp.where(row == 0, c[7:8, :], pltpu.roll(gate, 1, 0))
    g2 = jnp.where(row == 0, c[6:7, :], jnp.where(row == 1, c[7:8, :], pltpu.roll(gate, 2, 0)))
    out_ref[...] += _ffn_tail(gate, g1, g2, up, cw_ref, cb_ref, wd_ref)
    last = gate[gate.shape[0] - V7X_SUBLANES:, :]
    carry_ref[f] = last
    tail_ref[0] = last


def _ffn_prompt_call(y, g, wg, wu, wd, cw, cb, *, seq):
    m = y.shape[0]
    tm, tf = TM_FFN, TF_FFN
    nf = D_FF // tf
    assert m % tm == 0 and seq % tm == 0 and D_FF % tf == 0
    return pl.pallas_call(
        functools.partial(_ffn_prompt_kernel, tiles_per_seq=seq // tm),
        grid=(m // tm, nf),
        in_specs=[
            pl.BlockSpec((tm, D_MODEL), lambda i, f: (i, 0)),
            pl.BlockSpec((1, D_MODEL), lambda i, f: (0, 0)),
            pl.BlockSpec((D_MODEL, tf), lambda i, f: (0, f)),
            pl.BlockSpec((D_MODEL, tf), lambda i, f: (0, f)),
            pl.BlockSpec((tf, D_MODEL), lambda i, f: (f, 0)),
            pl.BlockSpec((CONV_W, tf), lambda i, f: (0, f)),
            pl.BlockSpec((1, tf), lambda i, f: (0, f)),
        ],
        out_specs=[
            pl.BlockSpec((tm, D_MODEL), lambda i, f: (i, 0)),
            pl.BlockSpec((1, V7X_SUBLANES, tf), lambda i, f: (i, 0, f)),
        ],
        out_shape=[
            jax.ShapeDtypeStruct((m, D_MODEL), F32),
            jax.ShapeDtypeStruct((m // tm, V7X_SUBLANES, D_FF), F32),
        ],
        scratch_shapes=[pltpu.VMEM((tm, D_MODEL), BF16),
                        pltpu.VMEM((nf, V7X_SUBLANES, tf), F32)],
        compiler_params=_params("arbitrary", "arbitrary"),
        name="conv_ffn_prompt",
    )(y, g, wg, wu, wd, cw, cb)


def _ffn_sample_kernel(y_ref, g_ref, wg_ref, wu_ref, wd_ref, cw_ref, cb_ref, s0_ref, s1_ref,
                       out_ref, gate_ref, wg_out_ref, wu_out_ref, wd_out_ref, xn_ref, *, seq):
    f = pl.program_id(0)

    @pl.when(f == 0)
    def _():
        x = y_ref[...]
        xn_ref[...] = _rms(x, g_ref[...]).astype(BF16)
        out_ref[...] = x

    wg_out_ref[...] = wg_ref[0].astype(BF16)
    wu_out_ref[...] = wu_ref[0].astype(BF16)
    wd_out_ref[...] = wd_ref[0].astype(BF16)
    xn = xn_ref[...]
    gate = jnp.dot(xn, wg_out_ref[...], preferred_element_type=F32)
    up = jnp.dot(xn, wu_out_ref[...], preferred_element_type=F32)
    t = lax.broadcasted_iota(jnp.int32, gate.shape, 0) & (seq - 1)
    s0, s1 = s0_ref[...], s1_ref[...]
    g1 = jnp.where(t == 0, s1, pltpu.roll(gate, 1, 0))
    g2 = jnp.where(t == 0, s0, jnp.where(t == 1, s1, pltpu.roll(gate, 2, 0)))
    out_ref[...] += _ffn_tail(gate, g1, g2, up, cw_ref, cb_ref, wd_out_ref)
    gate_ref[...] = gate


def _ffn_sample_call(y, g, wg, wu, wd, layer, cw, cb, s0, s1, *, seq):
    m = y.shape[0]
    tf = TF_FFN
    nf = D_FF // tf
    full = pl.BlockSpec((m, D_MODEL), lambda f: (0, 0))
    col = pl.BlockSpec((m, tf), lambda f: (0, f))
    return pl.pallas_call(
        functools.partial(_ffn_sample_kernel, seq=seq),
        grid=(nf,),
        in_specs=[
            full,
            pl.BlockSpec((1, D_MODEL), lambda f: (0, 0)),
            pl.BlockSpec((1, D_MODEL, tf), lambda f: (layer, 0, f)),
            pl.BlockSpec((1, D_MODEL, tf), lambda f: (layer, 0, f)),
            pl.BlockSpec((1, tf, D_MODEL), lambda f: (layer, f, 0)),
            pl.BlockSpec((CONV_W, tf), lambda f: (0, f)),
            pl.BlockSpec((1, tf), lambda f: (0, f)),
            col, col,
        ],
        out_specs=[full, col,
                   pl.BlockSpec((D_MODEL, tf), lambda f: (0, f)),
                   pl.BlockSpec((D_MODEL, tf), lambda f: (0, f)),
                   pl.BlockSpec((tf, D_MODEL), lambda f: (f, 0))],
        out_shape=[jax.ShapeDtypeStruct((m, D_MODEL), F32),
                   jax.ShapeDtypeStruct((m, D_FF), F32),
                   jax.ShapeDtypeStruct((D_MODEL, D_FF), BF16),
                   jax.ShapeDtypeStruct((D_MODEL, D_FF), BF16),
                   jax.ShapeDtypeStruct((D_FF, D_MODEL), BF16)],
        scratch_shapes=[pltpu.VMEM((m, D_MODEL), BF16)],
        compiler_params=_params("arbitrary"),
        name="conv_ffn_sample",
    )(y, g, wg, wu, wd, cw, cb, s0, s1)


def _block_diag_q(q_ref, col0, hkv, grp, s):
    blocks = []
    for h in range(hkv):
        qh = jnp.concatenate(
            [q_ref[:, col0 + HEAD_DIM * (grp * h + j):col0 + HEAD_DIM * (grp * h + j + 1)]
             for j in range(grp)], axis=0)
        pieces = []
        if h > 0:
            pieces.append(jnp.zeros((grp * s, HEAD_DIM * h), F32))
        pieces.append(qh)
        if h < hkv - 1:
            pieces.append(jnp.zeros((grp * s, HEAD_DIM * (hkv - 1 - h)), F32))
        blocks.append(jnp.concatenate(pieces, axis=1) if len(pieces) > 1 else qh)
    return jnp.concatenate(blocks, axis=0).astype(BF16)


def _new_rows_t(x, s):
    pad = jnp.zeros((V7X_LANES - s, x.shape[1]), F32)
    return jnp.concatenate([pad, x], axis=0).T


def _cached_attend(qbd, ck_t, kn_t, cv_t, vn_t, d, s, sink_col):
    return _cached_finish(*_cached_scores(qbd, ck_t, kn_t), cv_t, vn_t, d, s, sink_col)


def _cached_scores(qbd, ck_t, kn_t):
    return (jnp.dot(qbd, ck_t.astype(BF16), preferred_element_type=F32),
            jnp.dot(qbd, kn_t.astype(BF16), preferred_element_type=F32))


def _cached_finish(sc, sn, cv_t, vn_t, d, s, sink_col):
    r_, lc = sc.shape
    ic = lax.broadcasted_iota(jnp.int32, (r_, lc), 0) & (s - 1)
    c = lax.broadcasted_iota(jnp.int32, (r_, lc), 1)
    i_n = lax.broadcasted_iota(jnp.int32, (r_, V7X_LANES), 0) & (s - 1)
    j = lax.broadcasted_iota(jnp.int32, (r_, V7X_LANES), 1) - (V7X_LANES - s)
    valid_c = c >= ic
    valid_n = (j >= 0) & (j <= i_n)
    if d > 1:
        valid_c = valid_c & ((c & (d - 1)) == (ic & (d - 1)))
        valid_n = valid_n & ((j & (d - 1)) == (i_n & (d - 1)))
    sc = jnp.where(valid_c, sc, NEG_INF)
    sn = jnp.where(valid_n, sn, NEG_INF)
    m = jnp.maximum(jnp.max(sc, axis=1, keepdims=True), jnp.max(sn, axis=1, keepdims=True))
    if sink_col is not None:
        m = jnp.maximum(m, sink_col)
    pc = jnp.exp2(sc - m)
    pn = jnp.exp2(sn - m)
    l = jnp.sum(pc, axis=1, keepdims=True) + jnp.sum(pn, axis=1, keepdims=True)
    if sink_col is not None:
        l = l + jnp.exp2(sink_col - m)
    nt = (((1,), (1,)), ((), ()))
    o = (lax.dot_general(pc.astype(BF16), cv_t.astype(BF16), nt, preferred_element_type=F32)
         + lax.dot_general(pn.astype(BF16), vn_t.astype(BF16), nt, preferred_element_type=F32)) / l
    return o, (m + jnp.log2(l)) * LN2


def _diag_heads(o, hkv, grp, s):
    pieces = []
    for h in range(hkv):
        for j in range(grp):
            r0 = (h * grp + j) * s
            pieces.append(o[r0:r0 + s, HEAD_DIM * h:HEAD_DIM * (h + 1)])
    return jnp.concatenate(pieces, axis=1)


def _store_shifted(out_ref, off, c_t, n_t, s):
    lc = c_t.shape[1]
    rolled = pltpu.roll(c_t, lc - s, 1)
    lane = lax.broadcasted_iota(jnp.int32, n_t.shape, 1)
    if lc > V7X_LANES:
        out_ref[0, :, off:off + lc - V7X_LANES] = rolled[:, :lc - V7X_LANES]
    out_ref[0, :, off + lc - V7X_LANES:off + lc] = jnp.where(
        lane < V7X_LANES - s, rolled[:, lc - V7X_LANES:], n_t)


def _sample_a_kernel(q_ref, kn_ref, vn_ref, ck_ref, cv_ref, sink_ref, o_ref, ko_ref, vo_ref, *, s):
    hkv, grp = A_KV_HEADS, A_Q_HEADS // A_KV_HEADS
    kn_t, vn_t = _new_rows_t(kn_ref[...], s), _new_rows_t(vn_ref[...], s)
    ck_t, cv_t = ck_ref[0], cv_ref[0]
    _store_shifted(ko_ref, 0, ck_t, kn_t, s)
    _store_shifted(vo_ref, 0, cv_t, vn_t, s)
    qbd = _block_diag_q(q_ref, 0, hkv, grp, s)
    sink_col = jnp.concatenate(
        [jnp.full((s, 1), sink_ref[hq] * LOG2E, F32) for hq in range(hkv * grp)], axis=0)
    o, _ = _cached_attend(qbd, ck_t, kn_t, cv_t, vn_t, 1, s, sink_col)
    o_ref[...] = _diag_heads(o, hkv, grp, s)


def _sample_a_call(q, kn, vn, ck_t, cv_t, sinks, *, s):
    nb, c, lc = ck_t.shape
    row = lambda w: pl.BlockSpec((s, w), lambda b: (b, 0))
    cache = pl.BlockSpec((1, c, lc), lambda b: (b, 0, 0))
    return pl.pallas_call(
        functools.partial(_sample_a_kernel, s=s),
        grid=(nb,),
        in_specs=[row(q.shape[1]), row(c), row(c), cache, cache,
                  pl.BlockSpec(memory_space=pltpu.SMEM)],
        out_specs=[row(q.shape[1]), cache, cache],
        out_shape=[jax.ShapeDtypeStruct(q.shape, F32),
                   jax.ShapeDtypeStruct(ck_t.shape, F32),
                   jax.ShapeDtypeStruct(cv_t.shape, F32)],
        compiler_params=_params("parallel"),
        name="sample_attn_a",
    )(q, kn, vn, ck_t, cv_t, sinks)


def _sample_b_kernel(q_ref, kn_ref, vn_ref, ck_ref, cv_ref, o_ref, ko_ref, vo_ref, *, s):
    hkv, grp = B_KV_HEADS, B_Q_HEADS // B_KV_HEADS
    ckv = hkv * HEAD_DIM
    outs, lses, new_t = [], [], []
    off = 0
    for g, (w, d) in enumerate(B_PATTERNS):
        kn_t = _new_rows_t(kn_ref[:, ckv * g:ckv * (g + 1)], s)
        vn_t = _new_rows_t(vn_ref[:, ckv * g:ckv * (g + 1)], s)
        new_t.append((kn_t, vn_t))
        _store_shifted(ko_ref, off, ck_ref[0, :, off:off + w], kn_t, s)
        _store_shifted(vo_ref, off, cv_ref[0, :, off:off + w], vn_t, s)
        off += w
    off = 0
    scores = []
    for g, (w, d) in enumerate(B_PATTERNS):
        qbd = _block_diag_q(q_ref, B_Q_HEADS * HEAD_DIM * g, hkv, grp, s)
        scores.append(_cached_scores(qbd, ck_ref[0, :, off:off + w], new_t[g][0]))
        off += w
    off = 0
    for g, (w, d) in enumerate(B_PATTERNS):
        o, lse = _cached_finish(*scores[g], cv_ref[0, :, off:off + w], new_t[g][1], d, s, None)
        outs.append(o)
        lses.append(lse)
        off += w
    mx = jnp.maximum(jnp.maximum(lses[0], lses[1]), lses[2])
    es = [jnp.exp(l - mx) for l in lses]
    den = es[0] + es[1] + es[2]
    comb = (es[0] / den) * outs[0] + (es[1] / den) * outs[1] + (es[2] / den) * outs[2]
    o_ref[...] = _diag_heads(comb, hkv, grp, s)


def _sample_b_call(q, kn, vn, ck_t, cv_t, *, s):
    nb, c, lb = ck_t.shape
    row = lambda w: pl.BlockSpec((s, w), lambda b: (b, 0))
    cache = pl.BlockSpec((1, c, lb), lambda b: (b, 0, 0))
    co = B_Q_HEADS * HEAD_DIM
    return pl.pallas_call(
        functools.partial(_sample_b_kernel, s=s),
        grid=(nb,),
        in_specs=[row(q.shape[1]), row(kn.shape[1]), row(vn.shape[1]), cache, cache],
        out_specs=[row(co), cache, cache],
        out_shape=[jax.ShapeDtypeStruct((q.shape[0], co), F32),
                   jax.ShapeDtypeStruct(ck_t.shape, F32),
                   jax.ShapeDtypeStruct(cv_t.shape, F32)],
        compiler_params=_params("parallel"),
        name="sample_attn_b",
    )(q, kn, vn, ck_t, cv_t)


def _rope_tables(pos):
    inv_freq = ROPE_THETA ** (-jnp.arange(HALF, dtype=F32) / HALF)
    ang = pos.astype(F32)[:, None] * inv_freq[None, :]
    cos, sin = jnp.cos(ang), jnp.sin(ang)
    return jnp.tile(cos, (1, 4)), jnp.tile(jnp.concatenate([-sin, sin], axis=1), (1, 2))


def _cache_t(cache):
    b, l, h, dh = cache.shape
    return jnp.transpose(cache, (0, 2, 3, 1)).reshape(b, h * dh, l)


def _cache_from_t(x, h):
    b, c, l = x.shape
    return jnp.transpose(x.reshape(b, h, c // h, l), (0, 3, 1, 2))[None]


def kernel(x_prompt, x_sample, cache_a_k, cache_a_v, cache_b_k, cache_b_v, state_ffn_conv,
           attn_norm, ffn_norm, a_w_qkv, a_q_norm, a_k_norm, a_sinks, a_w_o,
           b_w_qkv, b_q_norm, b_k_norm, b_w_o,
           ffn_w_gate, ffn_w_up, ffn_conv_w, ffn_conv_b, ffn_w_down):
    nb, seq, _ = x_prompt.shape
    ns, dec, _ = x_sample.shape
    mp, ms = nb * seq, ns * dec

    cos_p, sin_p = _rope_tables(jnp.tile(jnp.arange(seq, dtype=jnp.int32), nb))
    cos_s, sin_s = _rope_tables(jnp.tile(PAST_LEN + jnp.arange(dec, dtype=jnp.int32), ns))
    idx = jnp.arange(V7X_MXU_DIM, dtype=jnp.int32) // HEAD_DIM
    ones_blk = (idx[:, None] == idx[None, :]).astype(BF16)

    yp = x_prompt.reshape(mp, D_MODEL)
    ys = x_sample.reshape(ms, D_MODEL)
    row = lambda a: a.reshape(1, -1)

    def head_gain(gv, n_heads_per_tile):
        return jnp.tile(gv, n_heads_per_tile).reshape(1, -1)

    tn = a_w_qkv.shape[-1] // N_QKV_TILES
    qg, kg = head_gain(a_q_norm[0], tn // HEAD_DIM), head_gain(a_k_norm[0], tn // HEAD_DIM)
    g_attn = row(attn_norm[0])
    grp_a = A_Q_HEADS // A_KV_HEADS

    qs, ks, vs, wqkv = _qkv_sample_call(ys, g_attn, a_w_qkv, 0, cos_s, sin_s, qg, kg, ones_blk,
                                        n_q=A_Q_HEADS, n_kv=A_KV_HEADS)
    os_, ako, avo = _sample_a_call(qs, ks, vs, _cache_t(cache_a_k[0]), _cache_t(cache_a_v[0]),
                                   a_sinks[0], s=dec)
    ys, wo = _wo_cast_call(ys, os_, a_w_o, 0)
    a_k_sample = _cache_from_t(ako, A_KV_HEADS)
    a_v_sample = _cache_from_t(avo, A_KV_HEADS)

    q, k, v = _qkv_rows_call(yp, g_attn, wqkv, cos_p, sin_p, qg, kg, ones_blk, n_q=A_Q_HEADS,
                             n_kv=A_KV_HEADS, tm=TM_QKV, q_dtype=BF16, lane_major=False)
    k3, v3 = k.reshape(nb, seq, -1), v.reshape(nb, seq, -1)
    yp = _band_wo_call(q.reshape(nb, seq, -1), k3, v3, a_sinks[0], yp.reshape(nb, seq, -1), wo,
                       hkv=A_KV_HEADS, grp=grp_a).reshape(mp, D_MODEL)
    keep = min(A_WINDOW, seq)
    ca = A_KV_HEADS * HEAD_DIM
    a_k_prompt = _cache_from_t(_keep_t_call(k3, [keep], ca), A_KV_HEADS)
    a_v_prompt = _cache_from_t(_keep_t_call(v3, [keep], ca), A_KV_HEADS)

    conv_p, conv_s = [], []

    def ffn(layer, yp, ys):
        g_ffn = row(ffn_norm[layer])
        cw, cb = ffn_conv_w[layer], row(ffn_conv_b[layer])
        st = state_ffn_conv[layer]
        s0 = jnp.repeat(st[:, 0, :], dec, axis=0)
        s1 = jnp.repeat(st[:, 1, :], dec, axis=0)
        ys, gate_s, wg, wu, wd = _ffn_sample_call(ys, g_ffn, ffn_w_gate, ffn_w_up, ffn_w_down,
                                                  layer, cw, cb, s0, s1, seq=dec)
        conv_s.append(gate_s.reshape(ns, dec, D_FF)[:, dec - (CONV_W - 1):, :])
        yp, tail = _ffn_prompt_call(yp, g_ffn, wg, wu, wd, cw, cb, seq=seq)
        tiles = seq // TM_FFN
        conv_p.append(tail[tiles - 1::tiles, V7X_SUBLANES - (CONV_W - 1):, :])
        return yp, ys

    yp, ys = ffn(0, yp, ys)

    tn = b_w_qkv.shape[-1] // N_QKV_TILES
    qg, kg = head_gain(b_q_norm[0], tn // HEAD_DIM), head_gain(b_k_norm[0], tn // HEAD_DIM)
    g_attn = row(attn_norm[1])
    n_grp = len(B_PATTERNS)
    grp_b = B_Q_HEADS // B_KV_HEADS
    nqb, nkvb = n_grp * B_Q_HEADS, n_grp * B_KV_HEADS
    cb_ = B_KV_HEADS * HEAD_DIM

    qs, ks, vs, wqkv = _qkv_sample_call(ys, g_attn, b_w_qkv, 0, cos_s, sin_s, qg, kg, ones_blk,
                                        n_q=nqb, n_kv=nkvb)
    os_, bko, bvo = _sample_b_call(qs, ks, vs, _cache_t(cache_b_k[0]), _cache_t(cache_b_v[0]),
                                   s=dec)
    ys, wo = _wo_cast_call(ys, os_, b_w_o, 0)
    b_k_sample = _cache_from_t(bko, B_KV_HEADS)
    b_v_sample = _cache_from_t(bvo, B_KV_HEADS)

    q, k, v = _qkv_rows_call(yp, g_attn, wqkv, cos_p, sin_p, qg, kg, ones_blk, n_q=nqb,
                             n_kv=nkvb, tm=TM_QKV, q_dtype=F32, lane_major=True)
    outs, lses = [], []
    for g, (w, d) in enumerate(B_PATTERNS):
        assert w // d == BLOCK
        og, lg = _dil_call(q, k, v, nb=nb, hkv=B_KV_HEADS, grp=grp_b, d=d, group=g,
                           residue_major=(g == n_grp - 1))
        outs.append(og)
        lses.append(lg)
    yp = _wo_comb_call(yp, outs, lses, wo, tm=TM_WO_COMB, d_last=B_PATTERNS[-1][1])
    keeps = [min(w, seq) for w, _ in B_PATTERNS]
    k4 = k.reshape(k.shape[0], nb, seq, V7X_LANES)
    v4 = v.reshape(v.shape[0], nb, seq, V7X_LANES)
    b_k_prompt = _cache_from_t(_keep_t_call(k4, keeps, cb_, lane_major=True), B_KV_HEADS)
    b_v_prompt = _cache_from_t(_keep_t_call(v4, keeps, cb_, lane_major=True), B_KV_HEADS)

    yp, ys = ffn(1, yp, ys)

    return (yp.reshape(nb, seq, D_MODEL), ys.reshape(ns, dec, D_MODEL),
            a_k_prompt, a_v_prompt, a_k_sample, a_v_sample,
            b_k_prompt, b_v_prompt, b_k_sample, b_v_sample,
            jnp.stack(conv_p), jnp.stack(conv_s))
```

```python
import functools

import jax
import jax.numpy as jnp
from jax import lax
from jax.experimental import pallas as pl
from jax.experimental.pallas import tpu as pltpu

F32 = jnp.float32
BF16 = jnp.bfloat16

D_MODEL = 2048
HEAD_DIM = 64
HALF = HEAD_DIM // 2
ROPE_THETA = 10000.0
NORM_EPS = 1e-6
BLOCK = 128
PAST_LEN = 16384
A_WINDOW = 128
A_Q_HEADS = 32
A_KV_HEADS = 8
B_PATTERNS = ((128, 1), (512, 4), (2048, 16))
B_Q_HEADS = 16
B_KV_HEADS = 4
D_FF = 5632
CONV_W = 3
NEG_INF = -1e30
LOG2E = 1.4426950408889634
LN2 = 0.6931471805599453
Q_SCALE = HEAD_DIM ** -0.5 * LOG2E

V7X_MXU_DIM = 256
V7X_LANES = 128
V7X_SUBLANES = 8
VMEM_LIMIT = 56 * 1024 * 1024

TM_WO_COMB = 256
RM_GROUP_ROWS = TM_WO_COMB
TM_QKV = 512
TM_FFN = 1024
TF_FFN = 512
TN_WO_CAST = 512
PROJ_CHUNK_ROWS = 256
N_QKV_TILES = 6
DIL_MAX_ROWS = 1024
BAND_SUBBLOCKS = 4
ATTN_DEPTH = 2
DIL_TILES_PER_TRIP = 16


def _params(*sem):
    return pltpu.CompilerParams(dimension_semantics=sem, vmem_limit_bytes=VMEM_LIMIT)


def _rms(x, g):
    ms = jnp.mean(x * x, axis=-1, keepdims=True)
    return x * lax.rsqrt(ms + NORM_EPS) * g


def _head_norm_rope(a, gain, ones_blk, cos, sin):
    tn = a.shape[1]
    x2 = a * a
    hi = x2.astype(BF16)
    lo = (x2 - hi.astype(F32)).astype(BF16)
    parts = []
    for c in range(tn // V7X_MXU_DIM):
        sl = slice(V7X_MXU_DIM * c, V7X_MXU_DIM * (c + 1))
        parts.append(jnp.dot(hi[:, sl], ones_blk, preferred_element_type=F32)
                     + jnp.dot(lo[:, sl], ones_blk, preferred_element_type=F32))
    ss = jnp.concatenate(parts, axis=1)
    y = a * lax.rsqrt(ss * (1.0 / HEAD_DIM) + NORM_EPS) * gain
    lane = lax.broadcasted_iota(jnp.int32, y.shape, 1)
    first_half = (lane & (HEAD_DIM - 1)) < HALF
    partner = jnp.where(first_half, pltpu.roll(y, tn - HALF, 1), pltpu.roll(y, HALF, 1))
    reps = tn // V7X_LANES
    return y * jnp.tile(cos, (1, reps)) + partner * jnp.tile(sin, (1, reps))


def _qkv_sample_kernel(x_ref, g_ref, w_ref, cos_ref, sin_ref, qg_ref, kg_ref, ones_ref,
                       q_ref, k_ref, v_ref, w_out_ref, xn_ref, *, n_q_tiles):
    n = pl.program_id(0)
    w_out_ref[...] = w_ref[0].astype(BF16)

    @pl.when(n == 0)
    def _():
        xn_ref[...] = _rms(x_ref[...], g_ref[...]).astype(BF16)

    acc = jnp.dot(xn_ref[...], w_out_ref[...], preferred_element_type=F32)

    @pl.when(n < n_q_tiles)
    def _():
        q_ref[...] = _head_norm_rope(acc, qg_ref[...], ones_ref[...], cos_ref[...],
                                     sin_ref[...]) * Q_SCALE

    @pl.when(n == n_q_tiles)
    def _():
        k_ref[...] = _head_norm_rope(acc, kg_ref[...], ones_ref[...], cos_ref[...], sin_ref[...])

    @pl.when(n == n_q_tiles + 1)
    def _():
        v_ref[...] = acc


def _qkv_rows_kernel(x_ref, g_ref, w_ref, cos_ref, sin_ref, qg_ref, kg_ref, ones_ref,
                     q_ref, k_ref, v_ref, xn_ref, *, n_q_tiles, tn, lane_major, row_chunks):
    xn_ref[...] = _rms(x_ref[...], g_ref[...]).astype(BF16)
    rc = xn_ref.shape[0] // row_chunks
    chunk = lambda c: slice(rc * c, rc * (c + 1))
    jobs = [(n, c) for n in range(n_q_tiles + 2) for c in range(row_chunks)]

    def matmul(job):
        n, c = job
        return jnp.dot(xn_ref[chunk(c), :], w_ref[:, tn * n:tn * (n + 1)],
                       preferred_element_type=F32)

    def store(ref, rows, col0, val):
        if not lane_major:
            ref[rows, col0:col0 + tn] = val.astype(ref.dtype)
            return
        for t in range(tn // V7X_LANES):
            ref[col0 // V7X_LANES + t, rows, :] = (
                val[:, V7X_LANES * t:V7X_LANES * (t + 1)].astype(ref.dtype))

    def finish(job, acc):
        n, c = job
        rows = chunk(c)
        if n < n_q_tiles:
            r = _head_norm_rope(acc, qg_ref[...], ones_ref[...], cos_ref[rows, :], sin_ref[rows, :])
            store(q_ref, rows, tn * n, r * Q_SCALE)
        elif n == n_q_tiles:
            store(k_ref, rows, 0, _head_norm_rope(acc, kg_ref[...], ones_ref[...],
                                                  cos_ref[rows, :], sin_ref[rows, :]))
        else:
            store(v_ref, rows, 0, acc)

    acc_next = matmul(jobs[0])
    for i, job in enumerate(jobs):
        acc = acc_next
        if i + 1 < len(jobs):
            acc_next = matmul(jobs[i + 1])
        finish(job, acc)


def _qkv_rows_call(x, g, w, cos, sin, qg, kg, ones_blk, *, n_q, n_kv, tm, q_dtype, lane_major):
    m = x.shape[0]
    ncols = w.shape[1]
    tn = ncols // N_QKV_TILES
    n_q_tiles = (n_q * HEAD_DIM) // tn
    assert n_q_tiles * tn == n_q * HEAD_DIM and n_kv * HEAD_DIM == tn and tn % V7X_MXU_DIM == 0
    assert m % tm == 0 and n_q_tiles + 2 == N_QKV_TILES
    if lane_major:
        spec = lambda cols: pl.BlockSpec((cols // V7X_LANES, tm, V7X_LANES), lambda i: (0, i, 0))
        shape = lambda cols, dt: jax.ShapeDtypeStruct((cols // V7X_LANES, m, V7X_LANES), dt)
    else:
        spec = lambda cols: pl.BlockSpec((tm, cols), lambda i: (i, 0))
        shape = lambda cols, dt: jax.ShapeDtypeStruct((m, cols), dt)
    const = lambda shp: pl.BlockSpec(shp, lambda i: (0, 0))
    return pl.pallas_call(
        functools.partial(_qkv_rows_kernel, n_q_tiles=n_q_tiles, tn=tn, lane_major=lane_major,
                          row_chunks=max(1, tm // PROJ_CHUNK_ROWS)),
        grid=(m // tm,),
        in_specs=[
            pl.BlockSpec((tm, D_MODEL), lambda i: (i, 0)),
            const((1, D_MODEL)),
            pl.BlockSpec((D_MODEL, ncols), lambda i: (0, 0), pipeline_mode=pl.Buffered(1)),
            pl.BlockSpec((tm, V7X_LANES), lambda i: (i, 0)),
            pl.BlockSpec((tm, V7X_LANES), lambda i: (i, 0)),
            const((1, tn)),
            const((1, tn)),
            const((V7X_MXU_DIM, V7X_MXU_DIM)),
        ],
        out_specs=[spec(n_q * HEAD_DIM), spec(tn), spec(tn)],
        out_shape=[shape(n_q * HEAD_DIM, q_dtype), shape(tn, F32), shape(tn, F32)],
        scratch_shapes=[pltpu.VMEM((tm, D_MODEL), BF16)],
        compiler_params=_params("parallel"),
        name="qkv_proj_rows",
    )(x, g, w, cos, sin, qg, kg, ones_blk)


def _qkv_sample_call(x, g, w, layer, cos, sin, qg, kg, ones_blk, *, n_q, n_kv):
    m = x.shape[0]
    ncols = w.shape[-1]
    tn = ncols // N_QKV_TILES
    n_q_tiles = (n_q * HEAD_DIM) // tn
    assert n_q_tiles * tn == n_q * HEAD_DIM and n_kv * HEAD_DIM == tn and tn % V7X_MXU_DIM == 0
    last_q = n_q_tiles - 1
    const = lambda shp: pl.BlockSpec(shp, lambda n: (0,) * len(shp))
    return pl.pallas_call(
        functools.partial(_qkv_sample_kernel, n_q_tiles=n_q_tiles),
        grid=(N_QKV_TILES,),
        in_specs=[
            const((m, D_MODEL)),
            const((1, D_MODEL)),
            pl.BlockSpec((1, D_MODEL, tn), lambda n: (layer, 0, n)),
            const((m, V7X_LANES)),
            const((m, V7X_LANES)),
            const((1, tn)),
            const((1, tn)),
            const((V7X_MXU_DIM, V7X_MXU_DIM)),
        ],
        out_specs=[
            pl.BlockSpec((m, tn), lambda n: (0, jnp.minimum(n, last_q))),
            const((m, tn)),
            const((m, tn)),
            pl.BlockSpec((D_MODEL, tn), lambda n: (0, n)),
        ],
        out_shape=[
            jax.ShapeDtypeStruct((m, n_q * HEAD_DIM), F32),
            jax.ShapeDtypeStruct((m, tn), F32),
            jax.ShapeDtypeStruct((m, tn), F32),
            jax.ShapeDtypeStruct((D_MODEL, ncols), BF16),
        ],
        scratch_shapes=[pltpu.VMEM((m, D_MODEL), BF16)],
        compiler_params=_params("arbitrary"),
        name="qkv_proj_sample",
    )(x, g, w, cos, sin, qg, kg, ones_blk)


GRP = 4


def _band_mask_t(first):
    shape = (2 * BLOCK, BLOCK)
    kj = lax.broadcasted_iota(jnp.int32, shape, 0)
    qi = lax.broadcasted_iota(jnp.int32, shape, 1)
    seen = (kj >= qi) & (kj <= qi + BLOCK) & ((kj >= BLOCK) | jnp.logical_not(first))
    return jnp.where(seen, 0.0, NEG_INF)


def _kv_tile_forms(kt, vt):
    lo = lax.broadcasted_iota(jnp.int32, kt.shape, 1) < HEAD_DIM
    kt_sw = pltpu.roll(kt, HEAD_DIM, 1)
    k_forms = [(jnp.where(lo, kt, 0.0).astype(BF16), jnp.where(lo, 0.0, kt_sw).astype(BF16)),
               (jnp.where(lo, kt_sw, 0.0).astype(BF16), jnp.where(lo, 0.0, kt).astype(BF16))]
    return k_forms, vt.T.astype(BF16)


def _head_scores(k_lo, k_hi, q_a, q_b):
    nt = (((1,), (1,)), ((), ()))
    qpair = jnp.concatenate([q_a, q_b], axis=0)
    return jnp.concatenate([lax.dot_general(k_lo, qpair, nt, preferred_element_type=F32),
                            lax.dot_general(k_hi, qpair, nt, preferred_element_type=F32)], axis=1)


def _head_softmax_pv(s, mask4, v_t, sinks, want_lse):
    s = s + mask4
    m = jnp.max(s, axis=0, keepdims=True)
    if sinks is not None:
        sk = jnp.concatenate([jnp.full((1, BLOCK), sinks[j] * LOG2E, F32)
                              for j in (0, 2, 1, 3)], axis=1)
        m = jnp.maximum(m, sk)
    p = jnp.exp2(s - m)
    l = jnp.sum(p, axis=0, keepdims=True)
    if sinks is not None:
        l = l + jnp.exp2(sk - m)
    o_t = jnp.dot(v_t, p.astype(BF16), preferred_element_type=F32) / l
    lse_t = jnp.broadcast_to((m + jnp.log2(l)) * LN2, o_t.shape) if want_lse else None
    o_tiles, l_tiles = [], []
    for u in range(2):
        c0, c1 = slice(BLOCK * u, BLOCK * (u + 1)), slice(BLOCK * (2 + u), BLOCK * (3 + u))
        o_tiles.append(jnp.concatenate([o_t[:, c0], o_t[:, c1]], axis=0).T)
        if want_lse:
            l_tiles.append(jnp.concatenate([lse_t[:, c0], lse_t[:, c1]], axis=0).T)
    return o_tiles, l_tiles


def _attend_tiles(tiles, want_lse, fillers=None):
    jobs = [(ti, e) for ti in range(len(tiles)) for e in range(2)]
    forms = {}

    def scores(job):
        ti, e = job
        load_kv, load_q = tiles[ti][:2]
        if ti not in forms:
            forms[ti] = _kv_tile_forms(*load_kv())
        k_lo, k_hi = forms[ti][0][e]
        return _head_scores(k_lo, k_hi, load_q(2 * e), load_q(2 * e + 1))

    pending = [scores(job) for job in jobs[:ATTN_DEPTH]]
    for idx, (ti, e) in enumerate(jobs):
        s = pending.pop(0)
        if idx + ATTN_DEPTH < len(jobs):
            pending.append(scores(jobs[idx + ATTN_DEPTH]))
        sink_of, store, mask4 = tiles[ti][2:]
        v_t = forms[ti][1][HEAD_DIM * e:HEAD_DIM * (e + 1)]
        sinks = None if sink_of is None else [sink_of(GRP * e + j) for j in range(GRP)]
        o_tiles, l_tiles = _head_softmax_pv(s, mask4, v_t, sinks, want_lse)
        for u in range(2):
            store(2 * e + u, o_tiles[u], l_tiles[u] if want_lse else None)
        if fillers and idx in fillers:
            fillers[idx]()


def _band_wo_kernel(q_ref, kc_ref, kp_ref, vc_ref, vp_ref, sink_ref, y_ref, wo_ref, out_ref,
                    o_scr, *, kv_tiles, nsub):
    mask_first = jnp.tile(_band_mask_t(pl.program_id(1) == 0), (1, GRP))
    mask_rest = jnp.tile(_band_mask_t(False), (1, GRP))
    lanes = lambda t: slice(V7X_LANES * t, V7X_LANES * (t + 1))
    rows = lambda j: slice(BLOCK * j, BLOCK * (j + 1))

    def tile(j, t):
        def load_kv():
            if j == 0:
                kp, vp = kp_ref[0, :, lanes(t)], vp_ref[0, :, lanes(t)]
            else:
                kp, vp = kc_ref[0, rows(j - 1), lanes(t)], vc_ref[0, rows(j - 1), lanes(t)]
            return (jnp.concatenate([kp, kc_ref[0, rows(j), lanes(t)]], axis=0),
                    jnp.concatenate([vp, vc_ref[0, rows(j), lanes(t)]], axis=0))

        def store(u, o_tile, _):
            o_scr[rows(j), lanes(GRP * t + u)] = o_tile.astype(o_scr.dtype)

        return (load_kv, lambda u: q_ref[0, rows(j), lanes(GRP * t + u)],
                lambda h: sink_ref[2 * GRP * t + h], store, mask_first if j == 0 else mask_rest)

    jobs_per_block = 2 * kv_tiles
    n_out = out_ref.shape[2]
    piece = n_out // jobs_per_block

    def project(j, c):
        cols = slice(piece * c, piece * (c + 1))
        out_ref[0, rows(j), cols] = y_ref[0, rows(j), cols] + jnp.dot(
            o_scr[rows(j), :], wo_ref[:, cols], preferred_element_type=F32)

    fillers = {jobs_per_block * j + c: functools.partial(project, j - 1, c)
               for j in range(1, nsub) for c in range(jobs_per_block)}
    _attend_tiles([tile(j, t) for j in range(nsub) for t in range(kv_tiles)], False, fillers)
    for c in range(jobs_per_block):
        project(nsub - 1, c)


def _band_wo_call(q, k, v, sinks, y, wo, *, hkv, grp):
    b, seq, cq = q.shape
    ck = hkv * HEAD_DIM
    nsub = BAND_SUBBLOCKS
    rows = BLOCK * nsub
    assert grp == GRP and hkv % 2 == 0 and seq % rows == 0
    cur = lambda bi, c: (bi, c, 0)
    prev = lambda bi, c: (bi, jnp.maximum(c * nsub - 1, 0), 0)
    return pl.pallas_call(
        functools.partial(_band_wo_kernel, kv_tiles=hkv // 2, nsub=nsub),
        grid=(b, seq // rows),
        in_specs=[
            pl.BlockSpec((1, rows, cq), cur),
            pl.BlockSpec((1, rows, ck), cur),
            pl.BlockSpec((1, BLOCK, ck), prev),
            pl.BlockSpec((1, rows, ck), cur),
            pl.BlockSpec((1, BLOCK, ck), prev),
            pl.BlockSpec(memory_space=pltpu.SMEM),
            pl.BlockSpec((1, rows, D_MODEL), cur),
            pl.BlockSpec((cq, D_MODEL), lambda bi, c: (0, 0), pipeline_mode=pl.Buffered(1)),
        ],
        out_specs=pl.BlockSpec((1, rows, D_MODEL), cur),
        out_shape=jax.ShapeDtypeStruct((b, seq, D_MODEL), F32),
        scratch_shapes=[pltpu.VMEM((rows, cq), BF16)],
        compiler_params=_params("parallel", "arbitrary"),
        name="band_attn_wo",
    )(q, k, k, v, v, sinks, y, wo)


def _dil_kernel(q_ref, kc_ref, kp_ref, vc_ref, vp_ref, o_ref, lse_ref, *, kv_tiles, grp, d, nsub,
                residue_major):
    mask_first = jnp.tile(_band_mask_t(pl.program_id(1) == 0), (1, GRP))

    if d == 1:
        mask_rest = jnp.tile(_band_mask_t(False), (1, GRP))
        rows = lambda j: slice(BLOCK * j, BLOCK * (j + 1))

        def sub_tile(j, t):
            def load_kv():
                if j == 0:
                    kp, vp = kp_ref[t], vp_ref[t]
                else:
                    kp, vp = kc_ref[t, rows(j - 1), :], vc_ref[t, rows(j - 1), :]
                return (jnp.concatenate([kp, kc_ref[t, rows(j), :]], axis=0),
                        jnp.concatenate([vp, vc_ref[t, rows(j), :]], axis=0))

            def store(u, o_tile, l_tile):
                o_ref[grp * t + u, rows(j), :] = o_tile
                lse_ref[grp * t + u, rows(j), :] = l_tile

            return (load_kv, lambda u: q_ref[grp * t + u, rows(j), :].astype(BF16), None, store,
                    mask_first if j == 0 else mask_rest)

        _attend_tiles([sub_tile(j, t) for j in range(nsub) for t in range(kv_tiles)], True)
        return

    def tile(r, t):
        rs = pl.ds(r, BLOCK, stride=d)

        def load_kv():
            return (jnp.concatenate([kp_ref[t, rs, :], kc_ref[t, rs, :]], axis=0),
                    jnp.concatenate([vp_ref[t, rs, :], vc_ref[t, rs, :]], axis=0))

        def store(u, o_tile, l_tile):
            if not residue_major:
                o_ref[grp * t + u, rs, :] = o_tile
                lse_ref[grp * t + u, rs, :] = l_tile
                return
            per = RM_GROUP_ROWS // d
            for gq in range(BLOCK // per):
                ws = pl.ds(gq * RM_GROUP_ROWS + pl.multiple_of(r * per, per), per)
                o_ref[grp * t + u, ws, :] = o_tile[per * gq:per * (gq + 1)]
                lse_ref[grp * t + u, ws, :] = l_tile[per * gq:per * (gq + 1)]

        return (load_kv, lambda u: q_ref[grp * t + u, rs, :].astype(BF16), None, store,
                mask_first)

    unroll = min(d, max(1, DIL_TILES_PER_TRIP // kv_tiles))

    def body(i, carry):
        _attend_tiles([tile(i * unroll + rr, t) for rr in range(unroll) for t in range(kv_tiles)],
                      True)
        return carry

    if d > unroll:
        lax.fori_loop(0, d // unroll, body, 0)
    else:
        body(0, 0)


def _dil_call(q, k, v, *, nb, hkv, grp, d, group, residue_major=False):
    m = q.shape[1]
    seq = m // nb
    nsub = BAND_SUBBLOCKS if d == 1 else 1
    rows = BLOCK * d * nsub
    hsplit = max(1, rows // DIL_MAX_ROWS)
    kv_tiles = hkv // 2 // hsplit
    q_tiles = kv_tiles * grp
    nchunk = seq // rows
    assert seq % rows == 0 and kv_tiles * 2 * hsplit == hkv and grp % 2 == 0
    cur = lambda bi, c, hp: (group * hsplit + hp, bi * nchunk + c, 0)
    if d == 1:
        prev_rows = BLOCK
        prev = lambda bi, c, hp: (group * hsplit + hp,
                                  jnp.maximum((bi * nchunk + c) * nsub - 1, 0), 0)
    else:
        prev_rows = rows
        prev = lambda bi, c, hp: (group * hsplit + hp, bi * nchunk + jnp.maximum(c - 1, 0), 0)
    o_spec = pl.BlockSpec((q_tiles, rows, V7X_LANES), lambda bi, c, hp: (hp, bi * nchunk + c, 0))
    o_shape = jax.ShapeDtypeStruct((q_tiles * hsplit, m, V7X_LANES), F32)
    return pl.pallas_call(
        functools.partial(_dil_kernel, kv_tiles=kv_tiles, grp=grp, d=d, nsub=nsub,
                          residue_major=residue_major and d > 1),
        grid=(nb, nchunk, hsplit),
        in_specs=[
            pl.BlockSpec((q_tiles, rows, V7X_LANES), cur),
            pl.BlockSpec((kv_tiles, rows, V7X_LANES), cur),
            pl.BlockSpec((kv_tiles, prev_rows, V7X_LANES), prev),
            pl.BlockSpec((kv_tiles, rows, V7X_LANES), cur),
            pl.BlockSpec((kv_tiles, prev_rows, V7X_LANES), prev),
        ],
        out_specs=[o_spec, o_spec],
        out_shape=[o_shape, o_shape],
        compiler_params=_params("parallel", "arbitrary", "arbitrary"),
        name="dilated_attn",
    )(q, k, k, v, v)


def _keep_t_kernel(x_ref, o_ref):
    o_ref[0] = x_ref[0].T


def _keep_t_lane_major_kernel(x_ref, o_ref, *, keeps):
    g = pl.program_id(2)
    seq = x_ref.shape[2]
    off = 0
    for k, keep in enumerate(keeps):
        @pl.when(g == k)
        def _(keep=keep, off=off):
            o_ref[0, :, off:off + keep] = x_ref[0, 0, seq - keep:, :].T
        off += keep


def _keep_t_call(x, keeps, c, *, lane_major=False):
    b, seq = x.shape[1:3] if lane_major else x.shape[:2]
    starts, first_rb = [], []
    n = 0
    for keep in keeps:
        assert keep % BLOCK == 0 and seq % BLOCK == 0
        starts.append(n)
        first_rb.append((seq - keep) // BLOCK)
        n += keep // BLOCK

    def src_block(j):
        rb = jnp.int32(0)
        cb = jnp.int32(0)
        for g in range(len(keeps)):
            inside = j >= starts[g]
            rb = jnp.where(inside, first_rb[g] + j - starts[g], rb)
            cb = jnp.where(inside, g, cb)
        return rb, cb

    if lane_major:
        ct = c // V7X_LANES
        grid = (b, ct, len(keeps))
        in_spec = pl.BlockSpec((1, 1, seq, V7X_LANES), lambda bi, t, g: (g * ct + t, bi, 0, 0))
        out_spec = pl.BlockSpec((1, V7X_LANES, n * BLOCK), lambda bi, t, g: (bi, t, 0))
        kern = functools.partial(_keep_t_lane_major_kernel, keeps=tuple(keeps))
        sem = ("parallel", "parallel", "arbitrary")
    else:
        grid = (b, n)

        def src(bi, j):
            rb, cb = src_block(j)
            return bi, rb, cb

        in_spec = pl.BlockSpec((1, BLOCK, c), src)
        out_spec = pl.BlockSpec((1, c, BLOCK), lambda bi, j: (bi, 0, j))
        kern = _keep_t_kernel
        sem = ("parallel", "parallel")
    return pl.pallas_call(
        kern,
        grid=grid,
        in_specs=[in_spec],
        out_specs=out_spec,
        out_shape=jax.ShapeDtypeStruct((b, c, n * BLOCK), F32),
        compiler_params=_params(*sem),
        name="keep_rows_t",
    )(x)


def _wo_comb_kernel(y_ref, o0_ref, o1_ref, o2_ref, l0_ref, l1_ref, l2_ref, w_ref, out_ref, *,
                    d_last):
    per = y_ref.shape[0] // d_last

    def natural(ref, t):
        return jnp.concatenate([ref[t, pl.ds(n, d_last, stride=per), :] for n in range(per)],
                               axis=0)

    def combined(t):
        l0, l1, l2 = l0_ref[t], l1_ref[t], natural(l2_ref, t)
        mx = jnp.maximum(jnp.maximum(l0, l1), l2)
        e0, e1, e2 = jnp.exp(l0 - mx), jnp.exp(l1 - mx), jnp.exp(l2 - mx)
        den = e0 + e1 + e2
        comb = ((e0 / den) * o0_ref[t] + (e1 / den) * o1_ref[t]
                + (e2 / den) * natural(o2_ref, t))
        return comb.astype(BF16)

    kp = 2 * V7X_LANES
    for p in range(o0_ref.shape[0] // 2):
        comb = jnp.concatenate([combined(2 * p), combined(2 * p + 1)], axis=1)
        part = jnp.dot(comb, w_ref[kp * p:kp * (p + 1), :], preferred_element_type=F32)
        if p == 0:
            out_ref[...] = y_ref[...] + part
        else:
            out_ref[...] += part


def _wo_cast_kernel(y_ref, o_ref, w_ref, out_ref, w_out_ref):
    w_out_ref[...] = w_ref[0].astype(BF16)
    out_ref[...] = y_ref[...] + jnp.dot(o_ref[...].astype(BF16), w_out_ref[...],
                                        preferred_element_type=F32)


def _wo_cast_call(y, o, w, layer):
    m = y.shape[0]
    c = o.shape[1]
    tn = TN_WO_CAST
    return pl.pallas_call(
        _wo_cast_kernel,
        grid=(D_MODEL // tn,),
        in_specs=[
            pl.BlockSpec((m, tn), lambda n: (0, n)),
            pl.BlockSpec((m, c), lambda n: (0, 0)),
            pl.BlockSpec((1, c, tn), lambda n: (layer, 0, n)),
        ],
        out_specs=[pl.BlockSpec((m, tn), lambda n: (0, n)),
                   pl.BlockSpec((c, tn), lambda n: (0, n))],
        out_shape=[jax.ShapeDtypeStruct((m, D_MODEL), F32),
                   jax.ShapeDtypeStruct((c, D_MODEL), BF16)],
        compiler_params=_params("parallel"),
        name="wo_proj_cast",
    )(y, o, w)


def _wo_comb_call(y, os_, ls_, w, *, tm, d_last):
    m = y.shape[0]
    c = w.shape[0]
    nt = c // V7X_LANES
    assert tm == RM_GROUP_ROWS and tm % d_last == 0 and (tm // d_last) % V7X_SUBLANES == 0
    blk = pl.BlockSpec((nt, tm, V7X_LANES), lambda i: (0, i, 0))
    return pl.pallas_call(
        functools.partial(_wo_comb_kernel, d_last=d_last),
        grid=(m // tm,),
        in_specs=[pl.BlockSpec((tm, D_MODEL), lambda i: (i, 0))] + [blk] * 6
                 + [pl.BlockSpec((c, D_MODEL), lambda i: (0, 0), pipeline_mode=pl.Buffered(1))],
        out_specs=pl.BlockSpec((tm, D_MODEL), lambda i: (i, 0)),
        out_shape=jax.ShapeDtypeStruct((m, D_MODEL), F32),
        compiler_params=_params("parallel"),
        name="wo_comb_proj",
    )(y, *os_, *ls_, w)


def _ffn_tail(gate, g1, g2, up, cw_ref, cb_ref, wd_ref):
    conv = cb_ref[...] + cw_ref[0:1, :] * g2 + cw_ref[1:2, :] * g1 + cw_ref[2:3, :] * gate
    h = conv * jax.nn.sigmoid(conv) * up
    return jnp.dot(h.astype(BF16), wd_ref[...], preferred_element_type=F32)


def _ffn_prompt_kernel(y_ref, g_ref, wg_ref, wu_ref, wd_ref, cw_ref, cb_ref,
                       out_ref, tail_ref, xn_ref, carry_ref, *, tiles_per_seq):
    m = pl.program_id(0)
    f = pl.program_id(1)

    @pl.when(f == 0)
    def _():
        x = y_ref[...]
        xn_ref[...] = _rms(x, g_ref[...]).astype(BF16)
        out_ref[...] = x

    @pl.when(m % tiles_per_seq == 0)
    def _():
        carry_ref[f] = jnp.zeros(carry_ref.shape[1:], F32)

    xn = xn_ref[...]
    gate = jnp.dot(xn, wg_ref[...], preferred_element_type=F32)
    up = jnp.dot(xn, wu_ref[...], preferred_element_type=F32)
    c = carry_ref[f]
    row = lax.broadcasted_iota(jnp.int32, gate.shape, 0)
    g1 = jnp.where(row == 0, c[7:8, :], pltpu.roll(gate, 1, 0))
    g2 = jnp.where(row == 0, c[6:7, :], jnp.where(row == 1, c[7:8, :], pltpu.roll(gate, 2, 0)))
    out_ref[...] += _ffn_tail(gate, g1, g2, up, cw_ref, cb_ref, wd_ref)
    last = gate[gate.shape[0] - V7X_SUBLANES:, :]
    carry_ref[f] = last
    tail_ref[0] = last


def _ffn_prompt_call(y, g, wg, wu, wd, cw, cb, *, seq):
    m = y.shape[0]
    tm, tf = TM_FFN, TF_FFN
    nf = D_FF // tf
    assert m % tm == 0 and seq % tm == 0 and D_FF % tf == 0
    return pl.pallas_call(
        functools.partial(_ffn_prompt_kernel, tiles_per_seq=seq // tm),
        grid=(m // tm, nf),
        in_specs=[
            pl.BlockSpec((tm, D_MODEL), lambda i, f: (i, 0)),
            pl.BlockSpec((1, D_MODEL), lambda i, f: (0, 0)),
            pl.BlockSpec((D_MODEL, tf), lambda i, f: (0, f)),
            pl.BlockSpec((D_MODEL, tf), lambda i, f: (0, f)),
            pl.BlockSpec((tf, D_MODEL), lambda i, f: (f, 0)),
            pl.BlockSpec((CONV_W, tf), lambda i, f: (0, f)),
            pl.BlockSpec((1, tf), lambda i, f: (0, f)),
        ],
        out_specs=[
            pl.BlockSpec((tm, D_MODEL), lambda i, f: (i, 0)),
            pl.BlockSpec((1, V7X_SUBLANES, tf), lambda i, f: (i, 0, f)),
        ],
        out_shape=[
            jax.ShapeDtypeStruct((m, D_MODEL), F32),
            jax.ShapeDtypeStruct((m // tm, V7X_SUBLANES, D_FF), F32),
        ],
        scratch_shapes=[pltpu.VMEM((tm, D_MODEL), BF16),
                        pltpu.VMEM((nf, V7X_SUBLANES, tf), F32)],
        compiler_params=_params("arbitrary", "arbitrary"),
        name="conv_ffn_prompt",
    )(y, g, wg, wu, wd, cw, cb)


def _ffn_sample_kernel(y_ref, g_ref, wg_ref, wu_ref, wd_ref, cw_ref, cb_ref, s0_ref, s1_ref,
                       out_ref, gate_ref, wg_out_ref, wu_out_ref, wd_out_ref, xn_ref, *, seq):
    f = pl.program_id(0)

    @pl.when(f == 0)
    def _():
        x = y_ref[...]
        xn_ref[...] = _rms(x, g_ref[...]).astype(BF16)
        out_ref[...] = x

    wg_out_ref[...] = wg_ref[0].astype(BF16)
    wu_out_ref[...] = wu_ref[0].astype(BF16)
    wd_out_ref[...] = wd_ref[0].astype(BF16)
    xn = xn_ref[...]
    gate = jnp.dot(xn, wg_out_ref[...], preferred_element_type=F32)
    up = jnp.dot(xn, wu_out_ref[...], preferred_element_type=F32)
    t = lax.broadcasted_iota(jnp.int32, gate.shape, 0) & (seq - 1)
    s0, s1 = s0_ref[...], s1_ref[...]
    g1 = jnp.where(t == 0, s1, pltpu.roll(gate, 1, 0))
    g2 = jnp.where(t == 0, s0, jnp.where(t == 1, s1, pltpu.roll(gate, 2, 0)))
    out_ref[...] += _ffn_tail(gate, g1, g2, up, cw_ref, cb_ref, wd_out_ref)
    gate_ref[...] = gate


def _ffn_sample_call(y, g, wg, wu, wd, layer, cw, cb, s0, s1, *, seq):
    m = y.shape[0]
    tf = TF_FFN
    nf = D_FF // tf
    full = pl.BlockSpec((m, D_MODEL), lambda f: (0, 0))
    col = pl.BlockSpec((m, tf), lambda f: (0, f))
    return pl.pallas_call(
        functools.partial(_ffn_sample_kernel, seq=seq),
        grid=(nf,),
        in_specs=[
            full,
            pl.BlockSpec((1, D_MODEL), lambda f: (0, 0)),
            pl.BlockSpec((1, D_MODEL, tf), lambda f: (layer, 0, f)),
            pl.BlockSpec((1, D_MODEL, tf), lambda f: (layer, 0, f)),
            pl.BlockSpec((1, tf, D_MODEL), lambda f: (layer, f, 0)),
            pl.BlockSpec((CONV_W, tf), lambda f: (0, f)),
            pl.BlockSpec((1, tf), lambda f: (0, f)),
            col, col,
        ],
        out_specs=[full, col,
                   pl.BlockSpec((D_MODEL, tf), lambda f: (0, f)),
                   pl.BlockSpec((D_MODEL, tf), lambda f: (0, f)),
                   pl.BlockSpec((tf, D_MODEL), lambda f: (f, 0))],
        out_shape=[jax.ShapeDtypeStruct((m, D_MODEL), F32),
                   jax.ShapeDtypeStruct((m, D_FF), F32),
                   jax.ShapeDtypeStruct((D_MODEL, D_FF), BF16),
                   jax.ShapeDtypeStruct((D_MODEL, D_FF), BF16),
                   jax.ShapeDtypeStruct((D_FF, D_MODEL), BF16)],
        scratch_shapes=[pltpu.VMEM((m, D_MODEL), BF16)],
        compiler_params=_params("arbitrary"),
        name="conv_ffn_sample",
    )(y, g, wg, wu, wd, cw, cb, s0, s1)


def _block_diag_q(q_ref, col0, hkv, grp, s):
    blocks = []
    for h in range(hkv):
        qh = jnp.concatenate(
            [q_ref[:, col0 + HEAD_DIM * (grp * h + j):col0 + HEAD_DIM * (grp * h + j + 1)]
             for j in range(grp)], axis=0)
        pieces = []
        if h > 0:
            pieces.append(jnp.zeros((grp * s, HEAD_DIM * h), F32))
        pieces.append(qh)
        if h < hkv - 1:
            pieces.append(jnp.zeros((grp * s, HEAD_DIM * (hkv - 1 - h)), F32))
        blocks.append(jnp.concatenate(pieces, axis=1) if len(pieces) > 1 else qh)
    return jnp.concatenate(blocks, axis=0).astype(BF16)


def _new_rows_t(x, s):
    pad = jnp.zeros((V7X_LANES - s, x.shape[1]), F32)
    return jnp.concatenate([pad, x], axis=0).T


def _cached_attend(qbd, ck_t, kn_t, cv_t, vn_t, d, s, sink_col):
    return _cached_finish(*_cached_scores(qbd, ck_t, kn_t), cv_t, vn_t, d, s, sink_col)


def _cached_scores(qbd, ck_t, kn_t):
    return (jnp.dot(qbd, ck_t.astype(BF16), preferred_element_type=F32),
            jnp.dot(qbd, kn_t.astype(BF16), preferred_element_type=F32))


def _cached_finish(sc, sn, cv_t, vn_t, d, s, sink_col):
    r_, lc = sc.shape
    ic = lax.broadcasted_iota(jnp.int32, (r_, lc), 0) & (s - 1)
    c = lax.broadcasted_iota(jnp.int32, (r_, lc), 1)
    i_n = lax.broadcasted_iota(jnp.int32, (r_, V7X_LANES), 0) & (s - 1)
    j = lax.broadcasted_iota(jnp.int32, (r_, V7X_LANES), 1) - (V7X_LANES - s)
    valid_c = c >= ic
    valid_n = (j >= 0) & (j <= i_n)
    if d > 1:
        valid_c = valid_c & ((c & (d - 1)) == (ic & (d - 1)))
        valid_n = valid_n & ((j & (d - 1)) == (i_n & (d - 1)))
    sc = jnp.where(valid_c, sc, NEG_INF)
    sn = jnp.where(valid_n, sn, NEG_INF)
    m = jnp.maximum(jnp.max(sc, axis=1, keepdims=True), jnp.max(sn, axis=1, keepdims=True))
    if sink_col is not None:
        m = jnp.maximum(m, sink_col)
    pc = jnp.exp2(sc - m)
    pn = jnp.exp2(sn - m)
    l = jnp.sum(pc, axis=1, keepdims=True) + jnp.sum(pn, axis=1, keepdims=True)
    if sink_col is not None:
        l = l + jnp.exp2(sink_col - m)
    nt = (((1,), (1,)), ((), ()))
    o = (lax.dot_general(pc.astype(BF16), cv_t.astype(BF16), nt, preferred_element_type=F32)
         + lax.dot_general(pn.astype(BF16), vn_t.astype(BF16), nt, preferred_element_type=F32)) / l
    return o, (m + jnp.log2(l)) * LN2


def _diag_heads(o, hkv, grp, s):
    pieces = []
    for h in range(hkv):
        for j in range(grp):
            r0 = (h * grp + j) * s
            pieces.append(o[r0:r0 + s, HEAD_DIM * h:HEAD_DIM * (h + 1)])
    return jnp.concatenate(pieces, axis=1)


def _store_shifted(out_ref, off, c_t, n_t, s):
    lc = c_t.shape[1]
    rolled = pltpu.roll(c_t, lc - s, 1)
    lane = lax.broadcasted_iota(jnp.int32, n_t.shape, 1)
    if lc > V7X_LANES:
        out_ref[0, :, off:off + lc - V7X_LANES] = rolled[:, :lc - V7X_LANES]
    out_ref[0, :, off + lc - V7X_LANES:off + lc] = jnp.where(
        lane < V7X_LANES - s, rolled[:, lc - V7X_LANES:], n_t)


def _sample_a_kernel(q_ref, kn_ref, vn_ref, ck_ref, cv_ref, sink_ref, o_ref, ko_ref, vo_ref, *, s):
    hkv, grp = A_KV_HEADS, A_Q_HEADS // A_KV_HEADS
    kn_t, vn_t = _new_rows_t(kn_ref[...], s), _new_rows_t(vn_ref[...], s)
    ck_t, cv_t = ck_ref[0], cv_ref[0]
    _store_shifted(ko_ref, 0, ck_t, kn_t, s)
    _store_shifted(vo_ref, 0, cv_t, vn_t, s)
    qbd = _block_diag_q(q_ref, 0, hkv, grp, s)
    sink_col = jnp.concatenate(
        [jnp.full((s, 1), sink_ref[hq] * LOG2E, F32) for hq in range(hkv * grp)], axis=0)
    o, _ = _cached_attend(qbd, ck_t, kn_t, cv_t, vn_t, 1, s, sink_col)
    o_ref[...] = _diag_heads(o, hkv, grp, s)


def _sample_a_call(q, kn, vn, ck_t, cv_t, sinks, *, s):
    nb, c, lc = ck_t.shape
    row = lambda w: pl.BlockSpec((s, w), lambda b: (b, 0))
    cache = pl.BlockSpec((1, c, lc), lambda b: (b, 0, 0))
    return pl.pallas_call(
        functools.partial(_sample_a_kernel, s=s),
        grid=(nb,),
        in_specs=[row(q.shape[1]), row(c), row(c), cache, cache,
                  pl.BlockSpec(memory_space=pltpu.SMEM)],
        out_specs=[row(q.shape[1]), cache, cache],
        out_shape=[jax.ShapeDtypeStruct(q.shape, F32),
                   jax.ShapeDtypeStruct(ck_t.shape, F32),
                   jax.ShapeDtypeStruct(cv_t.shape, F32)],
        compiler_params=_params("parallel"),
        name="sample_attn_a",
    )(q, kn, vn, ck_t, cv_t, sinks)


def _sample_b_kernel(q_ref, kn_ref, vn_ref, ck_ref, cv_ref, o_ref, ko_ref, vo_ref, *, s):
    hkv, grp = B_KV_HEADS, B_Q_HEADS // B_KV_HEADS
    ckv = hkv * HEAD_DIM
    outs, lses, new_t = [], [], []
    off = 0
    for g, (w, d) in enumerate(B_PATTERNS):
        kn_t = _new_rows_t(kn_ref[:, ckv * g:ckv * (g + 1)], s)
        vn_t = _new_rows_t(vn_ref[:, ckv * g:ckv * (g + 1)], s)
        new_t.append((kn_t, vn_t))
        _store_shifted(ko_ref, off, ck_ref[0, :, off:off + w], kn_t, s)
        _store_shifted(vo_ref, off, cv_ref[0, :, off:off + w], vn_t, s)
        off += w
    off = 0
    scores = []
    for g, (w, d) in enumerate(B_PATTERNS):
        qbd = _block_diag_q(q_ref, B_Q_HEADS * HEAD_DIM * g, hkv, grp, s)
        scores.append(_cached_scores(qbd, ck_ref[0, :, off:off + w], new_t[g][0]))
        off += w
    off = 0
    for g, (w, d) in enumerate(B_PATTERNS):
        o, lse = _cached_finish(*scores[g], cv_ref[0, :, off:off + w], new_t[g][1], d, s, None)
        outs.append(o)
        lses.append(lse)
        off += w
    mx = jnp.maximum(jnp.maximum(lses[0], lses[1]), lses[2])
    es = [jnp.exp(l - mx) for l in lses]
    den = es[0] + es[1] + es[2]
    comb = (es[0] / den) * outs[0] + (es[1] / den) * outs[1] + (es[2] / den) * outs[2]
    o_ref[...] = _diag_heads(comb, hkv, grp, s)


def _sample_b_call(q, kn, vn, ck_t, cv_t, *, s):
    nb, c, lb = ck_t.shape
    row = lambda w: pl.BlockSpec((s, w), lambda b: (b, 0))
    cache = pl.BlockSpec((1, c, lb), lambda b: (b, 0, 0))
    co = B_Q_HEADS * HEAD_DIM
    return pl.pallas_call(
        functools.partial(_sample_b_kernel, s=s),
        grid=(nb,),
        in_specs=[row(q.shape[1]), row(kn.shape[1]), row(vn.shape[1]), cache, cache],
        out_specs=[row(co), cache, cache],
        out_shape=[jax.ShapeDtypeStruct((q.shape[0], co), F32),
                   jax.ShapeDtypeStruct(ck_t.shape, F32),
                   jax.ShapeDtypeStruct(cv_t.shape, F32)],
        compiler_params=_params("parallel"),
        name="sample_attn_b",
    )(q, kn, vn, ck_t, cv_t)


def _rope_tables(pos):
    inv_freq = ROPE_THETA ** (-jnp.arange(HALF, dtype=F32) / HALF)
    ang = pos.astype(F32)[:, None] * inv_freq[None, :]
    cos, sin = jnp.cos(ang), jnp.sin(ang)
    return jnp.tile(cos, (1, 4)), jnp.tile(jnp.concatenate([-sin, sin], axis=1), (1, 2))


def _cache_t(cache):
    b, l, h, dh = cache.shape
    return jnp.transpose(cache, (0, 2, 3, 1)).reshape(b, h * dh, l)


def _cache_from_t(x, h):
    b, c, l = x.shape
    return jnp.transpose(x.reshape(b, h, c // h, l), (0, 3, 1, 2))[None]


def kernel(x_prompt, x_sample, cache_a_k, cache_a_v, cache_b_k, cache_b_v, state_ffn_conv,
           attn_norm, ffn_norm, a_w_qkv, a_q_norm, a_k_norm, a_sinks, a_w_o,
           b_w_qkv, b_q_norm, b_k_norm, b_w_o,
           ffn_w_gate, ffn_w_up, ffn_conv_w, ffn_conv_b, ffn_w_down):
    nb, seq, _ = x_prompt.shape
    ns, dec, _ = x_sample.shape
    mp, ms = nb * seq, ns * dec

    cos_p, sin_p = _rope_tables(jnp.tile(jnp.arange(seq, dtype=jnp.int32), nb))
    cos_s, sin_s = _rope_tables(jnp.tile(PAST_LEN + jnp.arange(dec, dtype=jnp.int32), ns))
    idx = jnp.arange(V7X_MXU_DIM, dtype=jnp.int32) // HEAD_DIM
    ones_blk = (idx[:, None] == idx[None, :]).astype(BF16)

    yp = x_prompt.reshape(mp, D_MODEL)
    ys = x_sample.reshape(ms, D_MODEL)
    row = lambda a: a.reshape(1, -1)

    def head_gain(gv, n_heads_per_tile):
        return jnp.tile(gv, n_heads_per_tile).reshape(1, -1)

    tn = a_w_qkv.shape[-1] // N_QKV_TILES
    qg, kg = head_gain(a_q_norm[0], tn // HEAD_DIM), head_gain(a_k_norm[0], tn // HEAD_DIM)
    g_attn = row(attn_norm[0])
    grp_a = A_Q_HEADS // A_KV_HEADS

    qs, ks, vs, wqkv = _qkv_sample_call(ys, g_attn, a_w_qkv, 0, cos_s, sin_s, qg, kg, ones_blk,
                                        n_q=A_Q_HEADS, n_kv=A_KV_HEADS)
    os_, ako, avo = _sample_a_call(qs, ks, vs, _cache_t(cache_a_k[0]), _cache_t(cache_a_v[0]),
                                   a_sinks[0], s=dec)
    ys, wo = _wo_cast_call(ys, os_, a_w_o, 0)
    a_k_sample = _cache_from_t(ako, A_KV_HEADS)
    a_v_sample = _cache_from_t(avo, A_KV_HEADS)

    q, k, v = _qkv_rows_call(yp, g_attn, wqkv, cos_p, sin_p, qg, kg, ones_blk, n_q=A_Q_HEADS,
                             n_kv=A_KV_HEADS, tm=TM_QKV, q_dtype=BF16, lane_major=False)
    k3, v3 = k.reshape(nb, seq, -1), v.reshape(nb, seq, -1)
    yp = _band_wo_call(q.reshape(nb, seq, -1), k3, v3, a_sinks[0], yp.reshape(nb, seq, -1), wo,
                       hkv=A_KV_HEADS, grp=grp_a).reshape(mp, D_MODEL)
    keep = min(A_WINDOW, seq)
    ca = A_KV_HEADS * HEAD_DIM
    a_k_prompt = _cache_from_t(_keep_t_call(k3, [keep], ca), A_KV_HEADS)
    a_v_prompt = _cache_from_t(_keep_t_call(v3, [keep], ca), A_KV_HEADS)

    conv_p, conv_s = [], []

    def ffn(layer, yp, ys):
        g_ffn = row(ffn_norm[layer])
        cw, cb = ffn_conv_w[layer], row(ffn_conv_b[layer])
        st = state_ffn_conv[layer]
        s0 = jnp.repeat(st[:, 0, :], dec, axis=0)
        s1 = jnp.repeat(st[:, 1, :], dec, axis=0)
        ys, gate_s, wg, wu, wd = _ffn_sample_call(ys, g_ffn, ffn_w_gate, ffn_w_up, ffn_w_down,
                                                  layer, cw, cb, s0, s1, seq=dec)
        conv_s.append(gate_s.reshape(ns, dec, D_FF)[:, dec - (CONV_W - 1):, :])
        yp, tail = _ffn_prompt_call(yp, g_ffn, wg, wu, wd, cw, cb, seq=seq)
        tiles = seq // TM_FFN
        conv_p.append(tail[tiles - 1::tiles, V7X_SUBLANES - (CONV_W - 1):, :])
        return yp, ys

    yp, ys = ffn(0, yp, ys)

    tn = b_w_qkv.shape[-1] // N_QKV_TILES
    qg, kg = head_gain(b_q_norm[0], tn // HEAD_DIM), head_gain(b_k_norm[0], tn // HEAD_DIM)
    g_attn = row(attn_norm[1])
    n_grp = len(B_PATTERNS)
    grp_b = B_Q_HEADS // B_KV_HEADS
    nqb, nkvb = n_grp * B_Q_HEADS, n_grp * B_KV_HEADS
    cb_ = B_KV_HEADS * HEAD_DIM

    qs, ks, vs, wqkv = _qkv_sample_call(ys, g_attn, b_w_qkv, 0, cos_s, sin_s, qg, kg, ones_blk,
                                        n_q=nqb, n_kv=nkvb)
    os_, bko, bvo = _sample_b_call(qs, ks, vs, _cache_t(cache_b_k[0]), _cache_t(cache_b_v[0]),
                                   s=dec)
    ys, wo = _wo_cast_call(ys, os_, b_w_o, 0)
    b_k_sample = _cache_from_t(bko, B_KV_HEADS)
    b_v_sample = _cache_from_t(bvo, B_KV_HEADS)

    q, k, v = _qkv_rows_call(yp, g_attn, wqkv, cos_p, sin_p, qg, kg, ones_blk, n_q=nqb,
                             n_kv=nkvb, tm=TM_QKV, q_dtype=F32, lane_major=True)
    outs, lses = [], []
    for g, (w, d) in enumerate(B_PATTERNS):
        assert w // d == BLOCK
        og, lg = _dil_call(q, k, v, nb=nb, hkv=B_KV_HEADS, grp=grp_b, d=d, group=g,
                           residue_major=(g == n_grp - 1))
        outs.append(og)
        lses.append(lg)
    yp = _wo_comb_call(yp, outs, lses, wo, tm=TM_WO_COMB, d_last=B_PATTERNS[-1][1])
    keeps = [min(w, seq) for w, _ in B_PATTERNS]
    k4 = k.reshape(k.shape[0], nb, seq, V7X_LANES)
    v4 = v.reshape(v.shape[0], nb, seq, V7X_LANES)
    b_k_prompt = _cache_from_t(_keep_t_call(k4, keeps, cb_, lane_major=True), B_KV_HEADS)
    b_v_prompt = _cache_from_t(_keep_t_call(v4, keeps, cb_, lane_major=True), B_KV_HEADS)

    yp, ys = ffn(1, yp, ys)

    return (yp.reshape(nb, seq, D_MODEL), ys.reshape(ns, dec, D_MODEL),
            a_k_prompt, a_v_prompt, a_k_sample, a_v_sample,
            b_k_prompt, b_v_prompt, b_k_sample, b_v_sample,
            jnp.stack(conv_p), jnp.stack(conv_s))
```
